```python
import math
import jax, jax.numpy as jnp
from jax import lax
import numpy as np

D_MODEL = 1024
BATCH = 32
SEQ = 256
DEPTH = 2
DEC_BATCH = 2
DEC_SEQ = 1024
PAST_LEN = 512

GRID_W = 64
H_A = 4
DH_A = 64
W_A = H_A * 2 * DH_A
H_B = 8
P_B = 64
G_B = 2
N_B = 64
DI_B = H_B * P_B
CONV_K = 5
CONV_DIM = DI_B + 2 * G_B * N_B
SSD_CHUNK = 128
H_C = 8
DH_C = 64
W_C = H_C * DH_C
NA_KH = 8
NA_KW = 16
N_BRANCH = 3
ROPE_BASE = 10000.0
Q_BLOCK = 128
EPS = 1e-6
IN_SIZES = (W_A, W_A, W_A, W_A,
            DI_B, CONV_DIM, 2 * H_B,
            W_C, W_C, W_C, W_C,
            N_BRANCH * D_MODEL)
IN_COLS = sum(IN_SIZES)

kernel_name = 'hybrid_diffattn_ssd_natten_prefix_step'


def _rmsnorm(x, g):
    xf = x.astype(jnp.float32)
    y = xf * lax.rsqrt(jnp.mean(xf * xf, axis=-1, keepdims=True) + EPS)
    return (y * g.astype(jnp.float32)).astype(x.dtype)


def _softmax32(s):
    return jax.nn.softmax(s.astype(jnp.float32), axis=-1)


def _query_blocks(fn, qs):
    b, t = qs[0].shape[:2]
    blk = math.gcd(t, Q_BLOCK)
    nb = t // blk
    split = lambda a: jnp.moveaxis(a.reshape((b, nb, blk) + a.shape[2:]), 1, 0)
    out = lax.map(fn, tuple(split(a) for a in qs))
    out = jnp.moveaxis(out, 0, 1)
    return out.reshape((b, t) + out.shape[3:])


def _axial_rope(x):
    t, dh = x.shape[1], x.shape[-1]
    half = dh // 2
    quarter = half // 2
    pos = jnp.arange(t)
    inv = ROPE_BASE ** (-jnp.arange(quarter, dtype=jnp.float32) / quarter)

    def rot(u, p):
        ang = p.astype(jnp.float32)[:, None] * inv[None, :]
        bshape = (1, t) + (1,) * (u.ndim - 3) + (quarter,)
        cos, sin = jnp.cos(ang).reshape(bshape), jnp.sin(ang).reshape(bshape)
        u1 = u[..., :quarter].astype(jnp.float32)
        u2 = u[..., quarter:].astype(jnp.float32)
        return jnp.concatenate([u1 * cos - u2 * sin, u2 * cos + u1 * sin], axis=-1)

    out = jnp.concatenate([rot(x[..., :half], pos // GRID_W), rot(x[..., half:], pos % GRID_W)], axis=-1)
    return out.astype(x.dtype)


def _diff_lambda(lq1, lk1, lq2, lk2, lam_init):
    f = lambda a, b: jnp.exp(jnp.sum(a.astype(jnp.float32) * b.astype(jnp.float32)))
    return f(lq1, lk1) - f(lq2, lk2) + lam_init


def _diff_attention(q, k, v, lam):
    scale = DH_A ** -0.5

    def block(args):
        (qb,) = args
        p = _softmax32(jnp.einsum('bqhmd,bkhmd->bhmqk', qb, k) * scale)
        p = p[:, :, 0] - lam * p[:, :, 1]
        return jnp.einsum('bhqk,bkhe->bqhe', p.astype(v.dtype), v)

    return _query_blocks(block, (q,))


def _diff_post(o, subln_g, lam_init):
    o = _rmsnorm(o, subln_g) * (1.0 - lam_init)
    return o.reshape(o.shape[:2] + (W_A,))


def _softmax_attention(q, k, v):
    scale = q.shape[-1] ** -0.5

    def block(args):
        (qb,) = args
        p = _softmax32(jnp.einsum('bqhd,bkhd->bhqk', qb, k) * scale)
        return jnp.einsum('bhqk,bkhd->bqhd', p.astype(v.dtype), v)

    return _query_blocks(block, (q,))


def _neighborhood_attention(q, k, v, ck, cv, rpb):
    b, t, h, d = q.shape
    rows = t // GRID_W
    kh, kw = min(NA_KH, rows), NA_KW
    scale = d ** -0.5
    grid = lambda a: a.reshape(b, rows, GRID_W, h, d)
    qg, kg, vg = grid(q), grid(k), grid(v)
    r = jnp.arange(rows)
    row_idx = jnp.clip(r - kh // 2, 0, rows - kh)[:, None] + jnp.arange(kh)[None, :]
    k_rows = jnp.take(kg, row_idx, axis=1)
    v_rows = jnp.take(vg, row_idx, axis=1)
    col = jnp.arange(GRID_W)
    col_start = jnp.clip(col - kw // 2, 0, GRID_W - kw)
    in_win = (col[None, :] >= col_start[:, None]) & (col[None, :] < col_start[:, None] + kw)
    dr = row_idx - r[:, None] + (NA_KH - 1)
    dc = jnp.clip(col[None, :] - col[:, None], -(kw - 1), kw - 1) + (kw - 1)
    bias = rpb[:, dr[:, None, :, None], dc[None, :, None, :]].astype(jnp.float32)
    s_win = jnp.einsum('brqhd,brikhd->bhrqik', qg, k_rows).astype(jnp.float32) * scale + bias[None]
    s_win = jnp.where(in_win[:, None, :], s_win, -jnp.inf)
    s_ctx = jnp.einsum('brqhd,bphd->bhrqp', qg, ck).astype(jnp.float32) * scale
    nwin = kh * GRID_W
    p = _softmax32(jnp.concatenate([s_win.reshape(b, h, rows, GRID_W, nwin), s_ctx], axis=-1))
    p_win = p[..., :nwin].reshape(b, h, rows, GRID_W, kh, GRID_W).astype(v.dtype)
    p_ctx = p[..., nwin:].astype(v.dtype)
    o = jnp.einsum('bhrqik,brikhd->brqhd', p_win, v_rows) + jnp.einsum('bhrqp,bphd->brqhd', p_ctx, cv)
    return o.reshape(b, t, h * d)


def _centred_depthwise_conv(u, w, bias):
    c = u.shape[-1]
    out = lax.conv_general_dilated(u, w[:, None, :].astype(u.dtype), window_strides=(1,),
                                   padding=[(CONV_K // 2, CONV_K // 2)],
                                   dimension_numbers=('NWC', 'WIO', 'NWC'), feature_group_count=c)
    return out + bias


def _ssd_scan(x, dt, a, bm, cm, h0):
    b, l, h, p = x.shape
    g, n = bm.shape[2], bm.shape[3]
    r = h // g
    q = math.gcd(l, SSD_CHUNK)
    nc = l // q
    la = (dt * a).reshape(b, nc, q, g, r)
    xdt = (x * dt[..., None]).reshape(b, nc, q, g, r, p)
    bm = bm.reshape(b, nc, q, g, n)
    cm = cm.reshape(b, nc, q, g, n)
    acum = jnp.cumsum(la, axis=2)
    causal = jnp.arange(q)[:, None] >= jnp.arange(q)[None, :]
    seg = acum[:, :, :, None] - acum[:, :, None, :]
    decay = jnp.exp(jnp.where(causal[:, :, None, None], seg, -jnp.inf))
    cb = jnp.einsum('bcign,bcjgn->bcijg', cm, bm)
    y_diag = jnp.einsum('bcijg,bcijgr,bcjgrp->bcigrp', cb, decay, xdt)
    to_end = jnp.exp(acum[:, :, -1:] - acum)
    states = jnp.einsum('bcjgn,bcjgr,bcjgrp->bcgrpn', bm, to_end, xdt)
    chunk_decay = jnp.exp(acum[:, :, -1])

    def step(hc, inp):
        s, dcy = inp
        return dcy[..., None, None] * hc + s, hc

    h_last, h_in = lax.scan(step, h0.reshape(b, g, r, p, n),
                            (jnp.moveaxis(states, 1, 0), jnp.moveaxis(chunk_decay, 1, 0)))
    h_in = jnp.moveaxis(h_in, 0, 1)
    y_off = jnp.einsum('bcign,bcgrpn,bcigr->bcigrp', cm, h_in, jnp.exp(acum))
    return (y_diag + y_off).reshape(b, l, h, p), h_last.reshape(b, h, p, n)


def _ssd_branch(z, xbc, dt_raw, conv_w, conv_b, dt_bias, a_log, d_skip, norm_g, h0):
    b, l, _ = z.shape
    f32 = jnp.float32
    xbc = jax.nn.silu(_centred_depthwise_conv(xbc, conv_w, conv_b)).astype(f32)
    xs = xbc[..., :DI_B].reshape(b, l, H_B, P_B)
    bm = xbc[..., DI_B:DI_B + G_B * N_B].reshape(b, l, G_B, N_B)
    cm = xbc[..., DI_B + G_B * N_B:].reshape(b, l, G_B, N_B)
    dt = jax.nn.softplus(dt_raw.astype(f32).reshape(b, l, 2, H_B) + dt_bias.astype(f32))
    a = -jnp.exp(a_log.astype(f32))
    h0 = h0.astype(f32)
    flip = lambda u: jnp.flip(u, axis=1)
    y_f, h_f = _ssd_scan(xs, dt[:, :, 0], a[0], bm, cm, h0[:, 0])
    y_b, h_b = _ssd_scan(flip(xs), flip(dt[:, :, 1]), a[1], flip(bm), flip(cm), h0[:, 1])
    y = y_f + flip(y_b) + xs * jnp.sum(d_skip.astype(f32), axis=0)[:, None]
    y = y.reshape(b, l, DI_B) * jax.nn.silu(z.astype(f32))
    y = _rmsnorm(y, norm_g).astype(z.dtype)
    return y, jnp.stack([h_f, h_b], axis=1).astype(z.dtype)


def _pre(x, cvec, norm_g, w_ada, b_ada, w_in):
    ada = (jax.nn.silu(cvec) @ w_ada + b_ada)[..., None, :]
    shift, scale, gate = jnp.split(ada, 3, axis=-1)
    hmod = _rmsnorm(x, norm_g) * (1.0 + scale) + shift
    parts = jnp.split(hmod @ w_in, np.cumsum(IN_SIZES)[:-1].tolist(), axis=-1)
    return parts, gate


def _post(x, gate, ya, ga, yb, yc, gc, merge_logits, w_br_a, w_br_b, w_br_c, w_out):
    ma, mb, mc = jnp.split(jax.nn.sigmoid(merge_logits), N_BRANCH, axis=-1)
    merged = (ma * ((ya * jax.nn.silu(ga)) @ w_br_a) + mb * (yb @ w_br_b)
              + mc * ((yc * jax.nn.silu(gc)) @ w_br_c))
    return x + gate * (merged @ w_out)


def _context_layer(x, c_ctx, lam_init, lw):
    (norm_g, w_ada, b_ada, w_in, lam_q1, lam_k1, lam_q2, lam_k2, subln_g, conv_w, conv_b,
     dt_bias, a_log, d_skip, ssd_norm_g, na_rpb, w_br_a, w_br_b, w_br_c, w_out) = lw
    b, l, _ = x.shape
    (qa, ka, va, ga, z, xbc, dt_raw, qc, kc, vc, gc, merge), gate = _pre(x, c_ctx, norm_g, w_ada, b_ada, w_in)
    qa = qa.reshape(b, l, H_A, 2, DH_A)
    ka = ka.reshape(b, l, H_A, 2, DH_A)
    va = va.reshape(b, l, H_A, 2 * DH_A)
    lam = _diff_lambda(lam_q1, lam_k1, lam_q2, lam_k2, lam_init)
    ya = _diff_post(_diff_attention(qa, ka, va, lam), subln_g, lam_init)
    h0 = jnp.zeros((b, 2, H_B, P_B, N_B), jnp.float32)
    yb, ssd_state = _ssd_branch(z, xbc, dt_raw, conv_w, conv_b, dt_bias, a_log, d_skip, ssd_norm_g, h0)
    qc, kc, vc = (u.reshape(b, l, H_C, DH_C) for u in (qc, kc, vc))
    yc = _softmax_attention(qc, kc, vc).reshape(b, l, W_C)
    x = _post(x, gate, ya, ga, yb, yc, gc, merge, w_br_a, w_br_b, w_br_c, w_out)
    return x, ka.reshape(b, l, H_A, 2 * DH_A), va, kc, vc, ssd_state


def _latent_layer(x, c, lam_init, ck_a, cv_a, ck_c, cv_c, h0, lw):
    (norm_g, w_ada, b_ada, w_in, lam_q1, lam_k1, lam_q2, lam_k2, subln_g, conv_w, conv_b,
     dt_bias, a_log, d_skip, ssd_norm_g, na_rpb, w_br_a, w_br_b, w_br_c, w_out) = lw
    b, l, _ = x.shape
    plen = ck_a.shape[1]
    (qa, ka, va, ga, z, xbc, dt_raw, qc, kc, vc, gc, merge), gate = _pre(x, c, norm_g, w_ada, b_ada, w_in)
    qa = _axial_rope(qa.reshape(b, l, H_A, 2, DH_A))
    ka = _axial_rope(ka.reshape(b, l, H_A, 2, DH_A))
    k_all = jnp.concatenate([ka, ck_a.reshape(b, plen, H_A, 2, DH_A)], axis=1)
    v_all = jnp.concatenate([va.reshape(b, l, H_A, 2 * DH_A), cv_a], axis=1)
    lam = _diff_lambda(lam_q1, lam_k1, lam_q2, lam_k2, lam_init)
    ya = _diff_post(_diff_attention(qa, k_all, v_all, lam), subln_g, lam_init)
    yb, _ = _ssd_branch(z, xbc, dt_raw, conv_w, conv_b, dt_bias, a_log, d_skip, ssd_norm_g, h0)
    qc, kc, vc = (u.reshape(b, l, H_C, DH_C) for u in (qc, kc, vc))
    yc = _neighborhood_attention(qc, kc, vc, ck_c, cv_c, na_rpb)
    return _post(x, gate, ya, ga, yb, yc, gc, merge, w_br_a, w_br_b, w_br_c, w_out)


def setup_inputs(seed: int = 0) -> dict:
    key = jax.random.key(seed)
    ks = iter(jax.random.split(key, 40))
    d = D_MODEL

    def nrm(shape, s):
        return jax.random.normal(next(ks), shape, jnp.float32) * s

    dt0 = jnp.exp(jax.random.uniform(next(ks), (DEPTH, 2, H_B), jnp.float32, math.log(1e-3), math.log(1e-1)))
    a0 = jax.random.uniform(next(ks), (DEPTH, 2, H_B), jnp.float32, 1.0, 16.0)
    return {
        'x_prompt': nrm((BATCH, SEQ, d), 1.0),
        'x_sample': nrm((DEC_BATCH, DEC_SEQ, d), 1.0),
        'cache_diff_k': nrm((DEC_BATCH, DEPTH, PAST_LEN, H_A, 2 * DH_A), 1.0),
        'cache_diff_v': nrm((DEC_BATCH, DEPTH, PAST_LEN, H_A, 2 * DH_A), 1.0),
        'cache_na_k': nrm((DEC_BATCH, DEPTH, PAST_LEN, H_C, DH_C), 1.0),
        'cache_na_v': nrm((DEC_BATCH, DEPTH, PAST_LEN, H_C, DH_C), 1.0),
        'state_ssd': nrm((DEC_BATCH, DEPTH, 2, H_B, P_B, N_B), 0.5),
        'c': nrm((DEC_BATCH, d), 1.0),
        'c_ctx': nrm((d,), 1.0),
        'norm_g': 1.0 + nrm((DEPTH, d), 0.01),
        'w_ada': nrm((DEPTH, d, 3 * d), 0.5 * d ** -0.5),
        'b_ada': nrm((DEPTH, 3 * d), 0.01),
        'w_in': nrm((DEPTH, d, IN_COLS), d ** -0.5),
        'lam_q1': nrm((DEPTH, DH_A), 0.1),
        'lam_k1': nrm((DEPTH, DH_A), 0.1),
        'lam_q2': nrm((DEPTH, DH_A), 0.1),
        'lam_k2': nrm((DEPTH, DH_A), 0.1),
        'diff_subln_g': 1.0 + nrm((DEPTH, 2 * DH_A), 0.01),
        'conv_w': nrm((DEPTH, CONV_K, CONV_DIM), CONV_K ** -0.5),
        'conv_b': nrm((DEPTH, CONV_DIM), 0.01),
        'dt_bias': dt0 + jnp.log(-jnp.expm1(-dt0)),
        'a_log': jnp.log(a0),
        'd_skip': 1.0 + nrm((DEPTH, 2, H_B), 0.1),
        'ssd_norm_g': 1.0 + nrm((DEPTH, DI_B), 0.01),
        'na_rpb': nrm((DEPTH, H_C, 2 * NA_KH - 1, 2 * NA_KW - 1), 0.02),
        'w_br_a': nrm((DEPTH, W_A, d), W_A ** -0.5),
        'w_br_b': nrm((DEPTH, DI_B, d), DI_B ** -0.5),
        'w_br_c': nrm((DEPTH, W_C, d), W_C ** -0.5),
        'w_out': nrm((DEPTH, d, d), d ** -0.5),
        'final_g': 1.0 + nrm((d,), 0.01),
    }


def reference(x_prompt, x_sample, cache_diff_k, cache_diff_v, cache_na_k, cache_na_v, state_ssd,
              c, c_ctx, norm_g, w_ada, b_ada, w_in, lam_q1, lam_k1, lam_q2, lam_k2, diff_subln_g,
              conv_w, conv_b, dt_bias, a_log, d_skip, ssd_norm_g, na_rpb, w_br_a, w_br_b, w_br_c,
              w_out, final_g):
    xp, xs = x_prompt, x_sample
    new_k_a, new_v_a, new_k_c, new_v_c, new_ssd = [], [], [], [], []
    for li in range(DEPTH):
        lam_init = 0.8 - 0.6 * math.exp(-0.3 * li)
        lw = (norm_g[li], w_ada[li], b_ada[li], w_in[li], lam_q1[li], lam_k1[li], lam_q2[li], lam_k2[li],
              diff_subln_g[li], conv_w[li], conv_b[li], dt_bias[li], a_log[li], d_skip[li], ssd_norm_g[li],
              na_rpb[li], w_br_a[li], w_br_b[li], w_br_c[li], w_out[li])
        xp, ka, va, kc, vc, hs = _context_layer(xp, c_ctx, lam_init, lw)
        new_k_a.append(ka)
        new_v_a.append(va)
        new_k_c.append(kc)
        new_v_c.append(vc)
        new_ssd.append(hs)
        xs = _latent_layer(xs, c, lam_init, cache_diff_k[:, li], cache_diff_v[:, li],
                           cache_na_k[:, li], cache_na_v[:, li], state_ssd[:, li], lw)
    y_prompt = _rmsnorm(xp, final_g)
    y_sample = _rmsnorm(xs, final_g)
    return (y_prompt, y_sample, jnp.stack(new_k_a, axis=1), jnp.stack(new_v_a, axis=1),
            jnp.stack(new_k_c, axis=1), jnp.stack(new_v_c, axis=1), jnp.stack(new_ssd, axis=1))
```

```python
import functools
import math

import jax
import jax.numpy as jnp
import numpy as np
from jax import lax
from jax.experimental import pallas as pl
from jax.experimental.pallas import tpu as pltpu

D_MODEL = 1024
BATCH = 32
SEQ = 256
DEPTH = 2
DEC_BATCH = 2
DEC_SEQ = 1024
PAST_LEN = 512
GRID_W = 64
GRID_ROWS = DEC_SEQ // GRID_W
H_A = 4
DH_A = 64
W_A = H_A * 2 * DH_A
H_B = 8
P_B = 64
G_B = 2
N_B = 64
DI_B = H_B * P_B
CONV_K = 5
CONV_DIM = DI_B + 2 * G_B * N_B
SSD_CHUNK = 128
H_C = 8
DH_C = 64
W_C = H_C * DH_C
NA_KH = 8
NA_KW = 16
N_BRANCH = 3
ROPE_BASE = 10000.0
EPS = 1e-6

LANES = 128
HALF = LANES // 2
DT_PAD = 256
VMEM_LIMIT = 56 * 1024 * 1024

COL_MERGE = 0
COL_QA = 3072
COL_KA = 3584
COL_VA = 4096
COL_GA = 4608
COL_DT = 5120
COL_XBC = 5376
COL_Z = 6144
COL_QC = 6656
COL_KC = 7168
COL_VC = 7680
COL_GC = 8192
PROJ_COLS = 8704
PROJ_TN = 512

NA_QROWS = 4
NA_WROWS = 12
NA_TILES = 2 * NA_KH
NEG_INF = float("-inf")
HI = lax.Precision.HIGHEST
F32 = jnp.float32
BF16 = jnp.bfloat16


def _dot(a, b, precision=None):
    return jnp.dot(a, b, preferred_element_type=F32, precision=precision)


def _dot_nt(a, b):
    return lax.dot_general(a, b, (((1,), (1,)), ((), ())), preferred_element_type=F32)


def _sigmoid(x):
    return 1.0 / (1.0 + jnp.exp(-x))


def _silu(x):
    return x * _sigmoid(x)


def _lane(shape):
    return lax.broadcasted_iota(jnp.int32, shape, len(shape) - 1)


def _params(*sem):
    return pltpu.CompilerParams(dimension_semantics=sem, vmem_limit_bytes=VMEM_LIMIT)


def _ada_kernel(cv_ref, w_ref, b_ref, o_ref):
    o_ref[...] = _dot(_silu(cv_ref[...]), w_ref[...], HI) + b_ref[...]


def _ada_call(cvecs, w_ada, b_ada):
    tn = 512
    return pl.pallas_call(
        _ada_kernel,
        grid=(DEPTH, 3 * D_MODEL // tn),
        in_specs=[
            pl.BlockSpec((8, D_MODEL), lambda l, j: (0, 0)),
            pl.BlockSpec((None, D_MODEL, tn), lambda l, j: (l, 0, j)),
            pl.BlockSpec((None, 1, tn), lambda l, j: (l, 0, j)),
        ],
        out_specs=pl.BlockSpec((None, 8, tn), lambda l, j: (l, 0, j)),
        out_shape=jax.ShapeDtypeStruct((DEPTH, 8, 3 * D_MODEL), F32),
        compiler_params=_params("arbitrary", "arbitrary"),
    )(cvecs, w_ada, b_ada.reshape(DEPTH, 1, 3 * D_MODEL))


def _inproj_kernel(x_ref, ada_ref, g_ref, w_ref, o_ref, h_s, *, tm, row_base, tokens_per_row):
    i = pl.program_id(0)

    @pl.when(pl.program_id(1) == 0)
    def _():
        x = x_ref[...]
        y = x * lax.rsqrt(jnp.mean(x * x, axis=-1, keepdims=True) + EPS) * g_ref[...]
        row = row_base + (i * tm) // tokens_per_row
        shift = ada_ref[pl.ds(row, 1), 0:D_MODEL]
        scale = ada_ref[pl.ds(row, 1), D_MODEL:2 * D_MODEL]
        h_s[...] = (y * (1.0 + scale) + shift).astype(BF16)

    o_ref[...] = _dot(h_s[...], w_ref[...])


def _inproj_call(x, ada, norm_g, w_in_p, *, tm, row_base, tokens_per_row):
    t = x.shape[0]
    kern = functools.partial(_inproj_kernel, tm=tm, row_base=row_base, tokens_per_row=tokens_per_row)
    return pl.pallas_call(
        kern,
        grid=(t // tm, PROJ_COLS // PROJ_TN),
        in_specs=[
            pl.BlockSpec((tm, D_MODEL), lambda i, j: (i, 0)),
            pl.BlockSpec((8, 3 * D_MODEL), lambda i, j: (0, 0)),
            pl.BlockSpec((1, D_MODEL), lambda i, j: (0, 0)),
            pl.BlockSpec((D_MODEL, PROJ_TN), lambda i, j: (0, j)),
        ],
        out_specs=pl.BlockSpec((tm, PROJ_TN), lambda i, j: (i, j)),
        out_shape=jax.ShapeDtypeStruct((t, PROJ_COLS), F32),
        scratch_shapes=[pltpu.VMEM((tm, D_MODEL), BF16)],
        compiler_params=_params("arbitrary", "arbitrary"),
    )(x, ada, norm_g.reshape(1, D_MODEL), w_in_p)


def _diff_lambda_in_kernel(lam_ref, lam_init):
    v = lam_ref[...]
    l1 = jnp.sum(v[0:1] * v[1:2], axis=-1, keepdims=True)
    l2 = jnp.sum(v[2:3] * v[3:4], axis=-1, keepdims=True)
    return jnp.exp(l1) - jnp.exp(l2) + lam_init


def _split_halves(x, scale):
    lo = _lane(x.shape) < HALF
    xs = x * scale
    return jnp.concatenate([jnp.where(lo, xs, 0.0), jnp.where(lo, 0.0, xs)], axis=0).astype(BF16)


def _diff_head_post(o, subln_g, lam_init, gate):
    o = o * lax.rsqrt(jnp.mean(o * o, axis=-1, keepdims=True) + EPS) * subln_g
    return (o * (1.0 - lam_init) * _silu(gate)).astype(BF16)


def _attn_a_ctx_kernel(q_ref, k_ref, v_ref, g_ref, lam_ref, sg_ref, o_ref, *, lam_init):
    t = q_ref.shape[0]
    lam = _diff_lambda_in_kernel(lam_ref, lam_init)
    for h in range(H_A):
        sl = slice(h * LANES, (h + 1) * LANES)
        qq = _split_halves(q_ref[:, sl], DH_A ** -0.5)
        s = _dot_nt(qq, k_ref[:, sl].astype(BF16))
        e = jnp.exp(s - jnp.max(s, axis=-1, keepdims=True))
        p = e * (1.0 / jnp.sum(e, axis=-1, keepdims=True))
        pd = (p[:t] - lam * p[t:]).astype(BF16)
        o = _dot(pd, v_ref[:, sl].astype(BF16))
        o_ref[:, sl] = _diff_head_post(o, sg_ref[...], lam_init, g_ref[:, sl])


def _attn_a_ctx_call(proj, lamvec, subln_g, lam_init):
    t = SEQ
    blk = lambda c: pl.BlockSpec((t, W_A), lambda b: (b, c // W_A))
    return pl.pallas_call(
        functools.partial(_attn_a_ctx_kernel, lam_init=lam_init),
        grid=(BATCH,),
        in_specs=[blk(COL_QA), blk(COL_KA), blk(COL_VA), blk(COL_GA),
                  pl.BlockSpec((4, LANES), lambda b: (0, 0)),
                  pl.BlockSpec((1, LANES), lambda b: (0, 0))],
        out_specs=pl.BlockSpec((t, W_A), lambda b: (b, 0)),
        out_shape=jax.ShapeDtypeStruct((BATCH * t, W_A), BF16),
        compiler_params=_params("arbitrary"),
    )(proj, proj, proj, proj, lamvec, subln_g)


def _rope(x, cos, sin_signed):
    first = (_lane(x.shape) % 32) < 16
    swapped = jnp.where(first, pltpu.roll(x, LANES - 16, 1), pltpu.roll(x, 16, 1))
    return x * cos + swapped * sin_signed


def _attn_a_lat_kernel(q_ref, k_ref, v_ref, g_ref, ck_ref, cv_ref, cosq_ref, sinq_ref, cosk_ref, sink_ref,
                       lam_ref, sg_ref, o_ref, kr_s, *, lam_init):
    tq = q_ref.shape[0]

    @pl.when(pl.program_id(1) == 0)
    def _():
        for h in range(H_A):
            sl = slice(h * LANES, (h + 1) * LANES)
            kr_s[:, sl] = _rope(k_ref[:, sl], cosk_ref[...], sink_ref[...]).astype(BF16)

    lam = _diff_lambda_in_kernel(lam_ref, lam_init)
    for h in range(H_A):
        sl = slice(h * LANES, (h + 1) * LANES)
        qq = _split_halves(_rope(q_ref[:, sl], cosq_ref[...], sinq_ref[...]), DH_A ** -0.5)
        s_lat = _dot_nt(qq, kr_s[:, sl])
        s_ctx = _dot_nt(qq, ck_ref[:, sl].astype(BF16))
        m = jnp.maximum(jnp.max(s_lat, axis=-1, keepdims=True), jnp.max(s_ctx, axis=-1, keepdims=True))
        e_lat = jnp.exp(s_lat - m)
        e_ctx = jnp.exp(s_ctx - m)
        r = 1.0 / (jnp.sum(e_lat, axis=-1, keepdims=True) + jnp.sum(e_ctx, axis=-1, keepdims=True))
        p_lat = e_lat * r
        p_ctx = e_ctx * r
        pd_lat = (p_lat[:tq] - lam * p_lat[tq:]).astype(BF16)
        pd_ctx = (p_ctx[:tq] - lam * p_ctx[tq:]).astype(BF16)
        o = _dot(pd_lat, v_ref[:, sl].astype(BF16)) + _dot(pd_ctx, cv_ref[:, sl].astype(BF16))
        o_ref[:, sl] = _diff_head_post(o, sg_ref[...], lam_init, g_ref[:, sl])


def _attn_a_lat_call(proj, cache_k, cache_v, li, cos_t, sin_t, lamvec, subln_g, lam_init):
    tq = 256
    nq = DEC_SEQ // tq
    qblk = lambda c: pl.BlockSpec((tq, W_A), lambda b, i: (b * nq + i, c // W_A))
    full = lambda c: pl.BlockSpec((DEC_SEQ, W_A), lambda b, i: (b, c // W_A))
    cache = pl.BlockSpec((None, None, PAST_LEN, W_A), lambda b, i: (b, li, 0, 0))
    return pl.pallas_call(
        functools.partial(_attn_a_lat_kernel, lam_init=lam_init),
        grid=(DEC_BATCH, nq),
        in_specs=[qblk(COL_QA), full(COL_KA), full(COL_VA), qblk(COL_GA), cache, cache,
                  pl.BlockSpec((tq, LANES), lambda b, i: (i, 0)),
                  pl.BlockSpec((tq, LANES), lambda b, i: (i, 0)),
                  pl.BlockSpec((DEC_SEQ, LANES), lambda b, i: (0, 0)),
                  pl.BlockSpec((DEC_SEQ, LANES), lambda b, i: (0, 0)),
                  pl.BlockSpec((4, LANES), lambda b, i: (0, 0)),
                  pl.BlockSpec((1, LANES), lambda b, i: (0, 0))],
        out_specs=pl.BlockSpec((tq, W_A), lambda b, i: (b * nq + i, 0)),
        out_shape=jax.ShapeDtypeStruct((DEC_BATCH * DEC_SEQ, W_A), BF16),
        scratch_shapes=[pltpu.VMEM((DEC_SEQ, W_A), BF16)],
        compiler_params=_params("arbitrary", "arbitrary"),
    )(proj, proj, proj, proj, cache_k, cache_v, cos_t, sin_t, cos_t, sin_t, lamvec, subln_g)


def _merge_halves(o, t):
    return jnp.where(_lane((t, LANES)) < HALF, o[:t], o[t:])


def _attn_c_ctx_kernel(q_ref, k_ref, v_ref, g_ref, o_ref):
    t = q_ref.shape[0]
    for j in range(H_C // 2):
        sl = slice(j * LANES, (j + 1) * LANES)
        qq = _split_halves(q_ref[:, sl], DH_C ** -0.5)
        s = _dot_nt(qq, k_ref[:, sl].astype(BF16))
        e = jnp.exp(s - jnp.max(s, axis=-1, keepdims=True))
        p = (e * (1.0 / jnp.sum(e, axis=-1, keepdims=True))).astype(BF16)
        o = _merge_halves(_dot(p, v_ref[:, sl].astype(BF16)), t)
        o_ref[:, sl] = (o * _silu(g_ref[:, sl])).astype(BF16)


def _attn_c_ctx_call(proj):
    t = SEQ
    blk = lambda c: pl.BlockSpec((t, W_C), lambda b: (b, c // W_C))
    return pl.pallas_call(
        _attn_c_ctx_kernel,
        grid=(BATCH,),
        in_specs=[blk(COL_QC), blk(COL_KC), blk(COL_VC), blk(COL_GC)],
        out_specs=pl.BlockSpec((t, W_C), lambda b: (b, 0)),
        out_shape=jax.ShapeDtypeStruct((BATCH * t, W_C), BF16),
        compiler_params=_params("arbitrary"),
    )(proj, proj, proj, proj)


def _rpb_kernel(rpb_ref, o_ref):
    h = pl.program_id(0)
    shape = (GRID_W, LANES)
    c = lax.broadcasted_iota(jnp.int32, shape, 0)
    cp = _lane(shape) % GRID_W
    d = jnp.clip(cp - c, -(NA_KW - 1), NA_KW - 1) + (NA_KW - 1)
    start = jnp.clip(c - NA_KW // 2, 0, GRID_W - NA_KW)
    in_win = (cp >= start) & (cp < start + NA_KW)
    o_ref[0] = jnp.full(shape, NEG_INF, F32)
    n_dc = 2 * NA_KW - 1
    for dr in range(2 * NA_KH - 1):
        acc = jnp.full(shape, NEG_INF, F32)
        for dc in range(n_dc):
            acc = jnp.where(d == dc, rpb_ref[(h * (2 * NA_KH - 1) + dr) * n_dc + dc], acc)
        o_ref[1 + dr] = jnp.where(in_win, acc, NEG_INF)


def _rpb_call(rpb):
    return pl.pallas_call(
        _rpb_kernel,
        grid=(H_C,),
        in_specs=[pl.BlockSpec(memory_space=pltpu.SMEM)],
        out_specs=pl.BlockSpec((None, NA_TILES, GRID_W, LANES), lambda h: (h, 0, 0, 0)),
        out_shape=jax.ShapeDtypeStruct((H_C, NA_TILES, GRID_W, LANES), F32),
        compiler_params=_params("arbitrary"),
    )(rpb.reshape(-1))


def _attn_c_lat_kernel(q_ref, k_ref, v_ref, g_ref, ck_ref, cv_ref, tile_ref, o_ref, bias_s):
    tq = q_ref.shape[0]
    nwin = NA_WROWS * GRID_W
    m = pl.program_id(1)
    w0 = jnp.where(m < (GRID_ROWS // NA_QROWS) // 2, 0, GRID_ROWS - NA_WROWS)
    k0 = pl.multiple_of(w0 * GRID_W, GRID_W)
    lo = _lane((GRID_W, LANES)) < HALF
    for j in range(H_C // 2):
        sl = slice(j * LANES, (j + 1) * LANES)
        for s in range(2):
            for i in range(NA_QROWS):
                r = m * NA_QROWS + i
                start = jnp.clip(r - NA_KH // 2, 0, GRID_ROWS - NA_KH)
                for jp in range(NA_WROWS // 2):
                    idx = []
                    for u in range(2):
                        rk = w0 + 2 * jp + u
                        valid = (rk >= start) & (rk < start + NA_KH)
                        idx.append(jnp.where(valid, rk - r + NA_KH, 0))
                    tile = jnp.where(lo, tile_ref[2 * j + s, idx[0]], tile_ref[2 * j + s, idx[1]])
                    bias_s[(s * NA_QROWS + i) * GRID_W:(s * NA_QROWS + i + 1) * GRID_W,
                           jp * LANES:(jp + 1) * LANES] = tile
        qq = _split_halves(q_ref[:, sl], DH_C ** -0.5)
        kw = k_ref[pl.ds(k0, nwin), sl].astype(BF16)
        vw = v_ref[pl.ds(k0, nwin), sl].astype(BF16)
        s_win = _dot_nt(qq, kw) + bias_s[...]
        s_ctx = _dot_nt(qq, ck_ref[:, sl].astype(BF16))
        mx = jnp.maximum(jnp.max(s_win, axis=-1, keepdims=True), jnp.max(s_ctx, axis=-1, keepdims=True))
        e_win = jnp.exp(s_win - mx)
        e_ctx = jnp.exp(s_ctx - mx)
        rs = 1.0 / (jnp.sum(e_win, axis=-1, keepdims=True) + jnp.sum(e_ctx, axis=-1, keepdims=True))
        o = _dot((e_win * rs).astype(BF16), vw) + _dot((e_ctx * rs).astype(BF16), cv_ref[:, sl].astype(BF16))
        o_ref[:, sl] = (_merge_halves(o, tq) * _silu(g_ref[:, sl])).astype(BF16)


def _attn_c_lat_call(proj, cache_k, cache_v, li, tiles):
    tq = NA_QROWS * GRID_W
    nq = DEC_SEQ // tq
    qblk = lambda c: pl.BlockSpec((tq, W_C), lambda b, i: (b * nq + i, c // W_C))
    full = lambda c: pl.BlockSpec((DEC_SEQ, W_C), lambda b, i: (b, c // W_C))
    cache = pl.BlockSpec((None, None, PAST_LEN, W_C), lambda b, i: (b, li, 0, 0))
    return pl.pallas_call(
        _attn_c_lat_kernel,
        grid=(DEC_BATCH, nq),
        in_specs=[qblk(COL_QC), full(COL_KC), full(COL_VC), qblk(COL_GC), cache, cache,
                  pl.BlockSpec((H_C, NA_TILES, GRID_W, LANES), lambda b, i: (0, 0, 0, 0))],
        out_specs=pl.BlockSpec((tq, W_C), lambda b, i: (b * nq + i, 0)),
        out_shape=jax.ShapeDtypeStruct((DEC_BATCH * DEC_SEQ, W_C), BF16),
        scratch_shapes=[pltpu.VMEM((2 * tq, NA_WROWS * GRID_W), F32)],
        compiler_params=_params("arbitrary", "arbitrary"),
    )(proj, proj, proj, proj, cache_k, cache_v, tiles)


def _ssd_kernel(*refs, seq, use_h0, want_state):
    refs = list(refs)
    dt_ref, xbc_ref, z_ref = refs[:3]
    pos = 3
    h0_ref = None
    if use_h0:
        h0_ref = refs[pos]
        pos += 1
    cw_ref, cb_ref, dtb_ref, alog_ref, dsk_ref, g_ref = refs[pos:pos + 6]
    pos += 6
    y_ref = refs[pos]
    pos += 1
    hs_ref = None
    if want_state:
        hs_ref = refs[pos]
        pos += 1
    xc_s, dtv_s, acum_s, yf_s, yb_s, st_s = refs[pos:]

    q = SSD_CHUNK
    nc = seq // q
    n_pair = H_B // 2

    u = xbc_ref[...]
    trow = lax.broadcasted_iota(jnp.int32, (seq, CONV_DIM), 0)
    acc = jnp.zeros((seq, CONV_DIM), F32) + cb_ref[...]
    for k in range(CONV_K):
        off = k - CONV_K // 2
        shifted = u if off == 0 else pltpu.roll(u, (-off) % seq, 0)
        ok = (trow + off >= 0) & (trow + off < seq)
        acc = acc + jnp.where(ok, shifted, 0.0) * cw_ref[k:k + 1, :]
    xc_s[...] = _silu(acc)

    xdt = dt_ref[:, 0:LANES] + dtb_ref[...]
    dtv_s[...] = jnp.maximum(xdt, 0.0) + jnp.log1p(jnp.exp(-jnp.abs(xdt)))
    a_row = -jnp.exp(alog_ref[...])

    ri = lax.broadcasted_iota(jnp.int32, (q, q), 0)
    ci = lax.broadcasted_iota(jnp.int32, (q, q), 1)
    ltri = (ri >= ci).astype(F32)
    for c in range(nc):
        acum_s[c * q:(c + 1) * q, :] = _dot(ltri, dtv_s[c * q:(c + 1) * q, :] * a_row, HI)

    if use_h0:
        st_s[...] = h0_ref[...].reshape(2, n_pair, N_B, LANES)
    else:
        st_s[...] = jnp.zeros_like(st_s)

    lane_q = _lane((q, LANES))
    lo = lane_q < HALF

    def pair_bcast(mat, c0):
        return jnp.where(lo, jnp.broadcast_to(mat[:, c0:c0 + 1], (q, LANES)),
                         jnp.broadcast_to(mat[:, c0 + 1:c0 + 2], (q, LANES)))

    def chunk(dirn, c, y_s):
        r0 = pl.multiple_of(c * q, q)
        rows = pl.ds(r0, q)
        dtv = dtv_s[rows, :]
        acum = acum_s[rows, :]
        tot = acum[q - 1:q, :]
        if dirn == 0:
            expo = acum
            yscale = jnp.exp(acum)
            wend = jnp.exp(tot - acum)
            tri = ri >= ci
        else:
            aex = acum - dtv * a_row
            expo = -aex
            yscale = jnp.exp(tot - aex)
            wend = jnp.exp(aex)
            tri = ci >= ri
        cdec = jnp.exp(tot)
        expo_t = expo.T
        bm = xc_s[rows, DI_B:DI_B + LANES]
        cm = xc_s[rows, DI_B + LANES:DI_B + 2 * LANES]
        bm_t = bm.T.astype(BF16)
        bm16 = bm.astype(BF16)
        for g in range(G_B):
            in_g = (lane_q >= g * N_B) & (lane_q < (g + 1) * N_B)
            cmg = jnp.where(in_g, cm, 0.0).astype(BF16)
            cb = _dot_nt(cmg, bm16)
            for k in range(g * n_pair // G_B, (g + 1) * n_pair // G_B):
                psl = slice(k * LANES, (k + 1) * LANES)
                col = dirn * H_B + 2 * k
                xdt_p = xc_s[rows, psl] * pair_bcast(dtv, col)
                xdt16 = xdt_p.astype(BF16)
                yd = []
                for s in range(2):
                    seg = expo[:, col + s:col + s + 1] - expo_t[col + s:col + s + 1, :]
                    dec = jnp.exp(jnp.where(tri, seg, NEG_INF))
                    yd.append(_dot((cb * dec).astype(BF16), xdt16))
                y_diag = jnp.where(lo, yd[0], yd[1])
                st = st_s[dirn, k]
                st2 = jnp.concatenate([st, st], axis=0).astype(BF16)
                y_off = _dot(cmg, st2) * pair_bcast(yscale, col)
                y_s[rows, psl] = y_diag + y_off
                xw16 = (xdt_p * pair_bcast(wend, col)).astype(BF16)
                cd = jnp.where(lo[0:1], jnp.broadcast_to(cdec[:, col:col + 1], (1, LANES)),
                               jnp.broadcast_to(cdec[:, col + 1:col + 2], (1, LANES)))
                st_s[dirn, k] = cd * st + _dot(bm_t[g * N_B:(g + 1) * N_B, :], xw16)

    def body(c, carry):
        chunk(0, c, yf_s)
        chunk(1, nc - 1 - c, yb_s)
        return carry

    lax.fori_loop(0, nc, body, 0)

    dsum = dsk_ref[0:1, :] + dsk_ref[1:2, :]
    y = yf_s[...] + yb_s[...] + xc_s[:, 0:DI_B] * dsum
    y = y * _silu(z_ref[...])
    y = y * lax.rsqrt(jnp.mean(y * y, axis=-1, keepdims=True) + EPS) * g_ref[...]
    y_ref[...] = y.astype(BF16)
    if want_state:
        hs_ref[...] = st_s[...].reshape(2, n_pair * N_B, LANES)


def _ssd_call(proj, h0t, conv_w, conv_b, dtb, alog, dskx, norm_g, *, nb, seq, want_state):
    use_h0 = h0t is not None
    n_pair = H_B // 2
    in_specs = [pl.BlockSpec((seq, DT_PAD), lambda b: (b, COL_DT // DT_PAD)),
                pl.BlockSpec((seq, CONV_DIM), lambda b: (b, COL_XBC // CONV_DIM)),
                pl.BlockSpec((seq, DI_B), lambda b: (b, COL_Z // DI_B))]
    args = [proj, proj, proj]
    if use_h0:
        in_specs.append(pl.BlockSpec((None, 2, n_pair * N_B, LANES), lambda b: (b, 0, 0, 0)))
        args.append(h0t)
    const = lambda shape: pl.BlockSpec(shape, lambda b: (0,) * len(shape))
    in_specs += [const((CONV_K, CONV_DIM)), const((1, CONV_DIM)), const((1, LANES)), const((1, LANES)),
                 const((2, DI_B)), const((1, DI_B))]
    args += [conv_w, conv_b, dtb, alog, dskx, norm_g]
    out_specs = [pl.BlockSpec((seq, DI_B), lambda b: (b, 0))]
    out_shape = [jax.ShapeDtypeStruct((nb * seq, DI_B), BF16)]
    if want_state:
        out_specs.append(pl.BlockSpec((None, 2, n_pair * N_B, LANES), lambda b: (b, 0, 0, 0)))
        out_shape.append(jax.ShapeDtypeStruct((nb, 2, n_pair * N_B, LANES), F32))
    scratch = [pltpu.VMEM((seq, CONV_DIM), F32), pltpu.VMEM((seq, LANES), F32), pltpu.VMEM((seq, LANES), F32),
               pltpu.VMEM((seq, DI_B), F32), pltpu.VMEM((seq, DI_B), F32),
               pltpu.VMEM((2, n_pair, N_B, LANES), F32)]
    return pl.pallas_call(
        functools.partial(_ssd_kernel, seq=seq, use_h0=use_h0, want_state=want_state),
        grid=(nb,),
        in_specs=in_specs,
        out_specs=out_specs,
        out_shape=out_shape,
        scratch_shapes=scratch,
        compiler_params=_params("arbitrary"),
    )(*args)


def _post_kernel(x_ref, ml_ref, ya_ref, yb_ref, yc_ref, ada_ref, wa_ref, wb_ref, wc_ref, wo_ref, fg_ref, o_ref,
                 *, tm, row_base, tokens_per_row, final):
    row = row_base + (pl.program_id(0) * tm) // tokens_per_row
    gate = ada_ref[pl.ds(row, 1), 2 * D_MODEL:3 * D_MODEL]
    d = D_MODEL
    merged = (_sigmoid(ml_ref[:, 0:d]) * _dot(ya_ref[...], wa_ref[...])
              + _sigmoid(ml_ref[:, d:2 * d]) * _dot(yb_ref[...], wb_ref[...])
              + _sigmoid(ml_ref[:, 2 * d:3 * d]) * _dot(yc_ref[...], wc_ref[...]))
    x = x_ref[...] + gate * _dot(merged.astype(BF16), wo_ref[...])
    if final:
        x = x * lax.rsqrt(jnp.mean(x * x, axis=-1, keepdims=True) + EPS) * fg_ref[...]
    o_ref[...] = x


def _post_call(x, proj, ya, yb, yc, ada, wa, wb, wc, wo, final_g, *, tm, row_base, tokens_per_row, final):
    t = x.shape[0]
    tok = lambda w: pl.BlockSpec((tm, w), lambda i: (i, 0))
    const = lambda shape: pl.BlockSpec(shape, lambda i: (0,) * len(shape))
    kern = functools.partial(_post_kernel, tm=tm, row_base=row_base, tokens_per_row=tokens_per_row, final=final)
    return pl.pallas_call(
        kern,
        grid=(t // tm,),
        in_specs=[tok(D_MODEL), tok(N_BRANCH * D_MODEL), tok(W_A), tok(DI_B), tok(W_C),
                  const((8, 3 * D_MODEL)), const((W_A, D_MODEL)), const((DI_B, D_MODEL)), const((W_C, D_MODEL)),
                  const((D_MODEL, D_MODEL)), const((1, D_MODEL))],
        out_specs=tok(D_MODEL),
        out_shape=jax.ShapeDtypeStruct((t, D_MODEL), F32),
        compiler_params=_params("arbitrary"),
    )(x, proj, ya, yb, yc, ada, wa, wb, wc, wo, final_g)


def _rope_tables():
    pos = np.arange(DEC_SEQ)
    lane = np.arange(LANES)
    l64 = lane % (2 * (DH_A // 2))
    quarter = DH_A // 4
    p = jnp.where((l64 < DH_A // 2)[None, :], (pos // GRID_W)[:, None], (pos % GRID_W)[:, None])
    inv = ROPE_BASE ** (-jnp.arange(quarter, dtype=F32) / quarter)
    ang = p.astype(F32) * inv[l64 % quarter][None, :]
    sign = jnp.where((lane % (2 * quarter)) < quarter, -1.0, 1.0).astype(F32)
    return jnp.cos(ang), jnp.sin(ang) * sign[None, :]


def _pad_lanes(v, width=LANES):
    v = v.reshape(1, -1).astype(F32)
    return jnp.pad(v, ((0, 0), (0, width - v.shape[1])))


def kernel(x_prompt, x_sample, cache_diff_k, cache_diff_v, cache_na_k, cache_na_v, state_ssd, c, c_ctx,
           norm_g, w_ada, b_ada, w_in, lam_q1, lam_k1, lam_q2, lam_k2, diff_subln_g, conv_w, conv_b,
           dt_bias, a_log, d_skip, ssd_norm_g, na_rpb, w_br_a, w_br_b, w_br_c, w_out, final_g):
    assert x_prompt.shape == (BATCH, SEQ, D_MODEL) and x_sample.shape == (DEC_BATCH, DEC_SEQ, D_MODEL)
    o = np.cumsum((0,) + (W_A,) * 4 + (DI_B, CONV_DIM, 2 * H_B) + (W_C,) * 4 + (N_BRANCH * D_MODEL,))
    seg = lambda a, b: w_in[:, :, o[a]:o[b]]
    w_in_p = jnp.concatenate(
        [seg(11, 12), seg(0, 4), seg(6, 7), jnp.zeros((DEPTH, D_MODEL, DT_PAD - 2 * H_B), F32),
         seg(5, 6), seg(4, 5), seg(7, 11)], axis=-1).astype(BF16)
    assert w_in_p.shape[-1] == PROJ_COLS
    wa16, wb16, wc16, wo16 = (w.astype(BF16) for w in (w_br_a, w_br_b, w_br_c, w_out))

    cvecs = jnp.concatenate([c_ctx[None, :], c, jnp.zeros((8 - 1 - DEC_BATCH, D_MODEL), F32)], axis=0)
    ada = _ada_call(cvecs, w_ada, b_ada)
    cos_t, sin_t = _rope_tables()

    ck_a = cache_diff_k.reshape(DEC_BATCH, DEPTH, PAST_LEN, W_A)
    cv_a = cache_diff_v.reshape(DEC_BATCH, DEPTH, PAST_LEN, W_A)
    ck_c = cache_na_k.reshape(DEC_BATCH, DEPTH, PAST_LEN, W_C)
    cv_c = cache_na_v.reshape(DEC_BATCH, DEPTH, PAST_LEN, W_C)
    h0t = state_ssd.transpose(0, 1, 2, 5, 3, 4).reshape(DEC_BATCH, DEPTH, 2, N_B, DI_B)
    h0t = h0t.reshape(DEC_BATCH, DEPTH, 2, N_B, H_B // 2, LANES).transpose(0, 1, 2, 4, 3, 5)
    h0t = h0t.reshape(DEC_BATCH, DEPTH, 2, (H_B // 2) * N_B, LANES)

    xp = x_prompt.reshape(BATCH * SEQ, D_MODEL)
    xs = x_sample.reshape(DEC_BATCH * DEC_SEQ, D_MODEL)
    fg = final_g.reshape(1, D_MODEL)
    new_k_a, new_v_a, new_k_c, new_v_c, new_ssd = [], [], [], [], []
    for li in range(DEPTH):
        lam_init = 0.8 - 0.6 * math.exp(-0.3 * li)
        final = li == DEPTH - 1
        lamvec = jnp.concatenate([_pad_lanes(v[li]) for v in (lam_q1, lam_k1, lam_q2, lam_k2)], axis=0)
        subln = diff_subln_g[li].reshape(1, LANES)
        dtb = _pad_lanes(dt_bias[li])
        alog = _pad_lanes(a_log[li])
        dskx = jnp.repeat(d_skip[li], P_B, axis=-1)
        ssd_w = (conv_w[li], conv_b[li].reshape(1, CONV_DIM), dtb, alog, dskx, ssd_norm_g[li].reshape(1, DI_B))
        post_w = (wa16[li], wb16[li], wc16[li], wo16[li], fg)

        proj = _inproj_call(xp, ada[li], norm_g[li], w_in_p[li], tm=1024, row_base=0,
                            tokens_per_row=BATCH * SEQ)
        ya = _attn_a_ctx_call(proj, lamvec, subln, lam_init)
        yb, hs = _ssd_call(proj, None, *ssd_w, nb=BATCH, seq=SEQ, want_state=True)
        yc = _attn_c_ctx_call(proj)
        xp = _post_call(xp, proj, ya, yb, yc, ada[li], *post_w, tm=512, row_base=0,
                        tokens_per_row=BATCH * SEQ, final=final)
        p3 = proj.reshape(BATCH, SEQ, PROJ_COLS)
        new_k_a.append(p3[:, :, COL_KA:COL_KA + W_A].reshape(BATCH, SEQ, H_A, 2 * DH_A))
        new_v_a.append(p3[:, :, COL_VA:COL_VA + W_A].reshape(BATCH, SEQ, H_A, 2 * DH_A))
        new_k_c.append(p3[:, :, COL_KC:COL_KC + W_C].reshape(BATCH, SEQ, H_C, DH_C))
        new_v_c.append(p3[:, :, COL_VC:COL_VC + W_C].reshape(BATCH, SEQ, H_C, DH_C))
        hs = hs.reshape(BATCH, 2, H_B // 2, N_B, 2, P_B).transpose(0, 1, 2, 4, 5, 3)
        new_ssd.append(hs.reshape(BATCH, 2, H_B, P_B, N_B))

        proj = _inproj_call(xs, ada[li], norm_g[li], w_in_p[li], tm=512, row_base=1, tokens_per_row=DEC_SEQ)
        ya = _attn_a_lat_call(proj, ck_a, cv_a, li, cos_t, sin_t, lamvec, subln, lam_init)
        (yb,) = _ssd_call(proj, h0t[:, li], *ssd_w, nb=DEC_BATCH, seq=DEC_SEQ, want_state=False)
        yc = _attn_c_lat_call(proj, ck_c, cv_c, li, _rpb_call(na_rpb[li]))
        xs = _post_call(xs, proj, ya, yb, yc, ada[li], *post_w, tm=512, row_base=1,
                        tokens_per_row=DEC_SEQ, final=final)

    return (xp.reshape(BATCH, SEQ, D_MODEL), xs.reshape(DEC_BATCH, DEC_SEQ, D_MODEL),
            jnp.stack(new_k_a, axis=1), jnp.stack(new_v_a, axis=1),
            jnp.stack(new_k_c, axis=1), jnp.stack(new_v_c, axis=1), jnp.stack(new_ssd, axis=1))
```

```python
import functools
import math

import jax
import jax.numpy as jnp
import numpy as np
from jax import lax
from jax.experimental import pallas as pl
from jax.experimental.pallas import tpu as pltpu

D_MODEL = 1024
BATCH = 32
SEQ = 256
DEPTH = 2
DEC_BATCH = 2
DEC_SEQ = 1024
PAST_LEN = 512
GRID_W = 64
GRID_ROWS = DEC_SEQ // GRID_W
H_A = 4
DH_A = 64
W_A = H_A * 2 * DH_A
H_B = 8
P_B = 64
G_B = 2
N_B = 64
DI_B = H_B * P_B
CONV_K = 5
CONV_DIM = DI_B + 2 * G_B * N_B
SSD_CHUNK = 128
H_C = 8
DH_C = 64
W_C = H_C * DH_C
NA_KH = 8
NA_KW = 16
N_BRANCH = 3
ROPE_BASE = 10000.0
EPS = 1e-6

LANES = 128
HALF = LANES // 2
DT_PAD = 256
VMEM_LIMIT = 56 * 1024 * 1024

COL_MERGE = 0
COL_QA = 3072
COL_KA = 3584
COL_VA = 4096
COL_GA = 4608
COL_DT = 5120
COL_XBC = 5376
COL_Z = 6144
COL_QC = 6656
COL_KC = 7168
COL_VC = 7680
COL_GC = 8192
PROJ_COLS = 8704
PROJ_TN = 512

NA_QROWS = 4
NA_WROWS = 12
NA_TILES = 2 * NA_KH
NEG_INF = float("-inf")
HI = lax.Precision.HIGHEST
F32 = jnp.float32
BF16 = jnp.bfloat16


def _dot(a, b, precision=None):
    return jnp.dot(a, b, preferred_element_type=F32, precision=precision)


def _dot_nt(a, b):
    return lax.dot_general(a, b, (((1,), (1,)), ((), ())), preferred_element_type=F32)


def _sigmoid(x):
    return 1.0 / (1.0 + jnp.exp(-x))


def _silu(x):
    return x * _sigmoid(x)


def _lane(shape):
    return lax.broadcasted_iota(jnp.int32, shape, len(shape) - 1)


def _params(*sem):
    return pltpu.CompilerParams(dimension_semantics=sem, vmem_limit_bytes=VMEM_LIMIT)


def _ada_kernel(cv_ref, w_ref, b_ref, o_ref):
    o_ref[...] = _dot(_silu(cv_ref[...]), w_ref[...], HI) + b_ref[...]


def _ada_call(cvecs, w_ada, b_ada):
    tn = 512
    return pl.pallas_call(
        _ada_kernel,
        grid=(DEPTH, 3 * D_MODEL // tn),
        in_specs=[
            pl.BlockSpec((8, D_MODEL), lambda l, j: (0, 0)),
            pl.BlockSpec((None, D_MODEL, tn), lambda l, j: (l, 0, j)),
            pl.BlockSpec((None, 1, tn), lambda l, j: (l, 0, j)),
        ],
        out_specs=pl.BlockSpec((None, 8, tn), lambda l, j: (l, 0, j)),
        out_shape=jax.ShapeDtypeStruct((DEPTH, 8, 3 * D_MODEL), F32),
        compiler_params=_params("arbitrary", "arbitrary"),
        name="ada",
    )(cvecs, w_ada, b_ada.reshape(DEPTH, 1, 3 * D_MODEL))


def _inproj_kernel(*refs, tm, row_base, tokens_per_row, n_carry, emit_cache):
    x_ref, ada_ref, g_ref, w_ref = refs[:4]
    o_ref = refs[4 + n_carry]
    h_s = refs[-1]
    i = pl.program_id(0)
    j = pl.program_id(1)

    @pl.when(j == 0)
    def _():
        x = x_ref[...]
        y = x * lax.rsqrt(jnp.mean(x * x, axis=-1, keepdims=True) + EPS) * g_ref[...]
        row = row_base + (i * tm) // tokens_per_row
        shift = ada_ref[pl.ds(row, 1), 0:D_MODEL]
        scale = ada_ref[pl.ds(row, 1), D_MODEL:2 * D_MODEL]
        h_s[...] = (y * (1.0 + scale) + shift).astype(BF16)

    acc = _dot(h_s[...], w_ref[...])
    o_ref[...] = acc
    if emit_cache:
        ka_ref, va_ref, kc_ref, vc_ref = refs[5 + n_carry:9 + n_carry]
        nb = tm // SEQ

        def per_token(dst):
            for b in range(nb):
                for h in range(H_A):
                    dst[b, :, h, :] = acc[b * SEQ:(b + 1) * SEQ, h * LANES:(h + 1) * LANES]

        def per_channel(dst):
            for b in range(nb):
                dst[b] = acc[b * SEQ:(b + 1) * SEQ, :].T.reshape(H_C, DH_C, SEQ)

        pl.when(j == COL_KA // PROJ_TN)(lambda: per_token(ka_ref))
        pl.when(j == COL_VA // PROJ_TN)(lambda: per_token(va_ref))
        pl.when(j == COL_KC // PROJ_TN)(lambda: per_channel(kc_ref))
        pl.when(j == COL_VC // PROJ_TN)(lambda: per_channel(vc_ref))


def _inproj_call(x, ada, norm_g, w_in_p, *, tm, row_base, tokens_per_row, li=0, carry=None, emit_cache=False):
    t = x.shape[0]
    n_carry = 0 if carry is None else len(carry)
    kern = functools.partial(_inproj_kernel, tm=tm, row_base=row_base, tokens_per_row=tokens_per_row,
                             n_carry=n_carry, emit_cache=emit_cache)
    in_specs = [
        pl.BlockSpec((tm, D_MODEL), lambda i, j: (i, 0)),
        pl.BlockSpec((8, 3 * D_MODEL), lambda i, j: (0, 0)),
        pl.BlockSpec((1, D_MODEL), lambda i, j: (0, 0)),
        pl.BlockSpec((D_MODEL, PROJ_TN), lambda i, j: (0, j)),
    ]
    args = [x, ada, norm_g.reshape(1, D_MODEL), w_in_p]
    out_specs = [pl.BlockSpec((tm, PROJ_TN), lambda i, j: (i, j))]
    out_shape = [jax.ShapeDtypeStruct((t, PROJ_COLS), F32)]
    aliases = {}
    if emit_cache:
        nb = tm // SEQ
        out_specs += [pl.BlockSpec((nb, None, SEQ, H_A, 2 * DH_A), lambda i, j: (i, li, 0, 0, 0))] * 2
        out_specs += [pl.BlockSpec((nb, None, H_C, DH_C, SEQ), lambda i, j: (i, li, 0, 0, 0))] * 2
        out_shape += [jax.ShapeDtypeStruct((BATCH, DEPTH, SEQ, H_A, 2 * DH_A), F32)] * 2
        out_shape += [jax.ShapeDtypeStruct((BATCH, DEPTH, H_C, DH_C, SEQ), F32)] * 2
        if carry is not None:
            in_specs += [pl.BlockSpec(memory_space=pl.ANY)] * n_carry
            args += list(carry)
            aliases = {4 + k: 1 + k for k in range(n_carry)}
    return pl.pallas_call(
        kern,
        grid=(t // tm, PROJ_COLS // PROJ_TN),
        in_specs=in_specs,
        out_specs=out_specs,
        out_shape=out_shape,
        input_output_aliases=aliases,
        scratch_shapes=[pltpu.VMEM((tm, D_MODEL), BF16)],
        compiler_params=_params("arbitrary", "arbitrary"),
        name="inproj_ctx" if emit_cache else "inproj_lat",
    )(*args)


def _diff_lambda_in_kernel(lam_ref, lam_init):
    v = lam_ref[...]
    l1 = jnp.sum(v[0:1] * v[1:2], axis=-1, keepdims=True)
    l2 = jnp.sum(v[2:3] * v[3:4], axis=-1, keepdims=True)
    return jnp.exp(l1) - jnp.exp(l2) + lam_init


def _split_halves(x, scale):
    lo = _lane(x.shape) < HALF
    xs = x * scale
    return jnp.concatenate([jnp.where(lo, xs, 0.0), jnp.where(lo, 0.0, xs)], axis=0).astype(BF16)


def _diff_head_post(o, subln_g, lam_init, gate):
    o = o * lax.rsqrt(jnp.mean(o * o, axis=-1, keepdims=True) + EPS) * subln_g
    return (o * (1.0 - lam_init) * _silu(gate)).astype(BF16)


def _attn_a_ctx_kernel(q_ref, k_ref, v_ref, g_ref, lam_ref, sg_ref, o_ref, *, lam_init):
    t = q_ref.shape[0]
    lam = _diff_lambda_in_kernel(lam_ref, lam_init)
    for h in range(H_A):
        sl = slice(h * LANES, (h + 1) * LANES)
        qq = _split_halves(q_ref[:, sl], DH_A ** -0.5)
        s = _dot_nt(qq, k_ref[:, sl].astype(BF16))
        e = jnp.exp(s - jnp.max(s, axis=-1, keepdims=True))
        p = e * (1.0 / jnp.sum(e, axis=-1, keepdims=True))
        pd = (p[:t] - lam * p[t:]).astype(BF16)
        o = _dot(pd, v_ref[:, sl].astype(BF16))
        o_ref[:, sl] = _diff_head_post(o, sg_ref[...], lam_init, g_ref[:, sl])


def _attn_a_ctx_call(proj, lamvec, subln_g, lam_init):
    t = SEQ
    blk = lambda c: pl.BlockSpec((t, W_A), lambda b: (b, c // W_A))
    return pl.pallas_call(
        functools.partial(_attn_a_ctx_kernel, lam_init=lam_init),
        grid=(BATCH,),
        in_specs=[blk(COL_QA), blk(COL_KA), blk(COL_VA), blk(COL_GA),
                  pl.BlockSpec((4, LANES), lambda b: (0, 0)),
                  pl.BlockSpec((1, LANES), lambda b: (0, 0))],
        out_specs=pl.BlockSpec((t, W_A), lambda b: (b, 0)),
        out_shape=jax.ShapeDtypeStruct((BATCH * t, W_A), BF16),
        compiler_params=_params("arbitrary"),
        name="attn_a_ctx",
    )(proj, proj, proj, proj, lamvec, subln_g)


def _rope(x, cos, sin_signed):
    first = (_lane(x.shape) % 32) < 16
    swapped = jnp.where(first, pltpu.roll(x, LANES - 16, 1), pltpu.roll(x, 16, 1))
    return x * cos + swapped * sin_signed


def _attn_a_lat_kernel(q_ref, k_ref, v_ref, g_ref, ck_ref, cv_ref, cosq_ref, sinq_ref, cosk_ref, sink_ref,
                       lam_ref, sg_ref, o_ref, kr_s, *, lam_init):
    tq = q_ref.shape[0]

    @pl.when(pl.program_id(1) == 0)
    def _():
        for h in range(H_A):
            sl = slice(h * LANES, (h + 1) * LANES)
            kr_s[:, sl] = _rope(k_ref[:, sl], cosk_ref[...], sink_ref[...]).astype(BF16)

    lam = _diff_lambda_in_kernel(lam_ref, lam_init)
    for h in range(H_A):
        sl = slice(h * LANES, (h + 1) * LANES)
        qq = _split_halves(_rope(q_ref[:, sl], cosq_ref[...], sinq_ref[...]), DH_A ** -0.5)
        s_lat = _dot_nt(qq, kr_s[:, sl])
        s_ctx = _dot_nt(qq, ck_ref[:, sl].astype(BF16))
        m = jnp.maximum(jnp.max(s_lat, axis=-1, keepdims=True), jnp.max(s_ctx, axis=-1, keepdims=True))
        e_lat = jnp.exp(s_lat - m)
        e_ctx = jnp.exp(s_ctx - m)
        r = 1.0 / (jnp.sum(e_lat, axis=-1, keepdims=True) + jnp.sum(e_ctx, axis=-1, keepdims=True))
        p_lat = e_lat * r
        p_ctx = e_ctx * r
        pd_lat = (p_lat[:tq] - lam * p_lat[tq:]).astype(BF16)
        pd_ctx = (p_ctx[:tq] - lam * p_ctx[tq:]).astype(BF16)
        o = _dot(pd_lat, v_ref[:, sl].astype(BF16)) + _dot(pd_ctx, cv_ref[:, sl].astype(BF16))
        o_ref[:, sl] = _diff_head_post(o, sg_ref[...], lam_init, g_ref[:, sl])


def _attn_a_lat_call(proj, cache_k, cache_v, li, cos_t, sin_t, lamvec, subln_g, lam_init):
    tq = 256
    nq = DEC_SEQ // tq
    qblk = lambda c: pl.BlockSpec((tq, W_A), lambda b, i: (b * nq + i, c // W_A))
    full = lambda c: pl.BlockSpec((DEC_SEQ, W_A), lambda b, i: (b, c // W_A))
    cache = pl.BlockSpec((None, None, PAST_LEN, W_A), lambda b, i: (b, li, 0, 0))
    return pl.pallas_call(
        functools.partial(_attn_a_lat_kernel, lam_init=lam_init),
        grid=(DEC_BATCH, nq),
        in_specs=[qblk(COL_QA), full(COL_KA), full(COL_VA), qblk(COL_GA), cache, cache,
                  pl.BlockSpec((tq, LANES), lambda b, i: (i, 0)),
                  pl.BlockSpec((tq, LANES), lambda b, i: (i, 0)),
                  pl.BlockSpec((DEC_SEQ, LANES), lambda b, i: (0, 0)),
                  pl.BlockSpec((DEC_SEQ, LANES), lambda b, i: (0, 0)),
                  pl.BlockSpec((4, LANES), lambda b, i: (0, 0)),
                  pl.BlockSpec((1, LANES), lambda b, i: (0, 0))],
        out_specs=pl.BlockSpec((tq, W_A), lambda b, i: (b * nq + i, 0)),
        out_shape=jax.ShapeDtypeStruct((DEC_BATCH * DEC_SEQ, W_A), BF16),
        scratch_shapes=[pltpu.VMEM((DEC_SEQ, W_A), BF16)],
        compiler_params=_params("arbitrary", "arbitrary"),
        name="attn_a_lat",
    )(proj, proj, proj, proj, cache_k, cache_v, cos_t, sin_t, cos_t, sin_t, lamvec, subln_g)


def _merge_halves(o, t):
    return jnp.where(_lane((t, LANES)) < HALF, o[:t], o[t:])


def _attn_c_ctx_kernel(q_ref, k_ref, v_ref, g_ref, o_ref):
    t = q_ref.shape[0]
    for j in range(H_C // 2):
        sl = slice(j * LANES, (j + 1) * LANES)
        qq = _split_halves(q_ref[:, sl], DH_C ** -0.5)
        s = _dot_nt(qq, k_ref[:, sl].astype(BF16))
        e = jnp.exp(s - jnp.max(s, axis=-1, keepdims=True))
        p = (e * (1.0 / jnp.sum(e, axis=-1, keepdims=True))).astype(BF16)
        o = _merge_halves(_dot(p, v_ref[:, sl].astype(BF16)), t)
        o_ref[:, sl] = (o * _silu(g_ref[:, sl])).astype(BF16)


def _attn_c_ctx_call(proj):
    t = SEQ
    blk = lambda c: pl.BlockSpec((t, W_C), lambda b: (b, c // W_C))
    return pl.pallas_call(
        _attn_c_ctx_kernel,
        grid=(BATCH,),
        in_specs=[blk(COL_QC), blk(COL_KC), blk(COL_VC), blk(COL_GC)],
        out_specs=pl.BlockSpec((t, W_C), lambda b: (b, 0)),
        out_shape=jax.ShapeDtypeStruct((BATCH * t, W_C), BF16),
        compiler_params=_params("arbitrary"),
        name="attn_c_ctx",
    )(proj, proj, proj, proj)


def _rpb_kernel(rpb_ref, o_ref):
    h = pl.program_id(0)
    shape = (GRID_W, LANES)
    c = lax.broadcasted_iota(jnp.int32, shape, 0)
    cp = _lane(shape) % GRID_W
    d = jnp.clip(cp - c, -(NA_KW - 1), NA_KW - 1) + (NA_KW - 1)
    start = jnp.clip(c - NA_KW // 2, 0, GRID_W - NA_KW)
    in_win = (cp >= start) & (cp < start + NA_KW)
    o_ref[0] = jnp.full(shape, NEG_INF, F32)
    n_dc = 2 * NA_KW - 1
    for dr in range(2 * NA_KH - 1):
        acc = jnp.full(shape, NEG_INF, F32)
        for dc in range(n_dc):
            acc = jnp.where(d == dc, rpb_ref[(h * (2 * NA_KH - 1) + dr) * n_dc + dc], acc)
        o_ref[1 + dr] = jnp.where(in_win, acc, NEG_INF)


def _rpb_call(rpb):
    return pl.pallas_call(
        _rpb_kernel,
        grid=(H_C,),
        in_specs=[pl.BlockSpec(memory_space=pltpu.SMEM)],
        out_specs=pl.BlockSpec((None, NA_TILES, GRID_W, LANES), lambda h: (h, 0, 0, 0)),
        out_shape=jax.ShapeDtypeStruct((H_C, NA_TILES, GRID_W, LANES), F32),
        compiler_params=_params("arbitrary"),
        name="rpb_tiles",
    )(rpb.reshape(-1))


def _attn_c_lat_kernel(q_ref, k_ref, v_ref, g_ref, ck_ref, cv_ref, tile_ref, o_ref, bias_s):
    tq = q_ref.shape[0]
    nwin = NA_WROWS * GRID_W
    m = pl.program_id(1)
    w0 = jnp.where(m < (GRID_ROWS // NA_QROWS) // 2, 0, GRID_ROWS - NA_WROWS)
    k0 = pl.multiple_of(w0 * GRID_W, GRID_W)
    lo = _lane((GRID_W, LANES)) < HALF
    for j in range(H_C // 2):
        sl = slice(j * LANES, (j + 1) * LANES)
        for s in range(2):
            for i in range(NA_QROWS):
                r = m * NA_QROWS + i
                start = jnp.clip(r - NA_KH // 2, 0, GRID_ROWS - NA_KH)
                for jp in range(NA_WROWS // 2):
                    idx = []
                    for u in range(2):
                        rk = w0 + 2 * jp + u
                        valid = (rk >= start) & (rk < start + NA_KH)
                        idx.append(jnp.where(valid, rk - r + NA_KH, 0))
                    tile = jnp.where(lo, tile_ref[2 * j + s, idx[0]], tile_ref[2 * j + s, idx[1]])
                    bias_s[(s * NA_QROWS + i) * GRID_W:(s * NA_QROWS + i + 1) * GRID_W,
                           jp * LANES:(jp + 1) * LANES] = tile
        qq = _split_halves(q_ref[:, sl], DH_C ** -0.5)
        kw = k_ref[pl.ds(k0, nwin), sl].astype(BF16)
        vw = v_ref[pl.ds(k0, nwin), sl].astype(BF16)
        s_win = _dot_nt(qq, kw) + bias_s[...]
        s_ctx = _dot_nt(qq, ck_ref[:, sl].astype(BF16))
        mx = jnp.maximum(jnp.max(s_win, axis=-1, keepdims=True), jnp.max(s_ctx, axis=-1, keepdims=True))
        e_win = jnp.exp(s_win - mx)
        e_ctx = jnp.exp(s_ctx - mx)
        rs = 1.0 / (jnp.sum(e_win, axis=-1, keepdims=True) + jnp.sum(e_ctx, axis=-1, keepdims=True))
        o = _dot((e_win * rs).astype(BF16), vw) + _dot((e_ctx * rs).astype(BF16), cv_ref[:, sl].astype(BF16))
        o_ref[:, sl] = (_merge_halves(o, tq) * _silu(g_ref[:, sl])).astype(BF16)


def _attn_c_lat_call(proj, cache_k, cache_v, li, tiles):
    tq = NA_QROWS * GRID_W
    nq = DEC_SEQ // tq
    qblk = lambda c: pl.BlockSpec((tq, W_C), lambda b, i: (b * nq + i, c // W_C))
    full = lambda c: pl.BlockSpec((DEC_SEQ, W_C), lambda b, i: (b, c // W_C))
    cache = pl.BlockSpec((None, None, PAST_LEN, W_C), lambda b, i: (b, li, 0, 0))
    return pl.pallas_call(
        _attn_c_lat_kernel,
        grid=(DEC_BATCH, nq),
        in_specs=[qblk(COL_QC), full(COL_KC), full(COL_VC), qblk(COL_GC), cache, cache,
                  pl.BlockSpec((H_C, NA_TILES, GRID_W, LANES), lambda b, i: (0, 0, 0, 0))],
        out_specs=pl.BlockSpec((tq, W_C), lambda b, i: (b * nq + i, 0)),
        out_shape=jax.ShapeDtypeStruct((DEC_BATCH * DEC_SEQ, W_C), BF16),
        scratch_shapes=[pltpu.VMEM((2 * tq, NA_WROWS * GRID_W), F32)],
        compiler_params=_params("arbitrary", "arbitrary"),
        name="attn_c_lat",
    )(proj, proj, proj, proj, cache_k, cache_v, tiles)


def _ssd_kernel(*refs, seq, use_h0, want_state, has_carry):
    refs = list(refs)
    dt_ref, xbc_ref, z_ref = refs[:3]
    pos = 3
    h0_ref = None
    if use_h0:
        h0_ref = refs[pos]
        pos += 1
    cw_ref, cb_ref, dtb_ref, alog_ref, dsk_ref, g_ref = refs[pos:pos + 6]
    pos += 6 + int(has_carry)
    y_ref = refs[pos]
    pos += 1
    hs_ref = None
    if want_state:
        hs_ref = refs[pos]
        pos += 1
    xc_s, dtv_s, acum_s, yf_s, yb_s, st_s = refs[pos:]

    q = SSD_CHUNK
    nc = seq // q
    n_pair = H_B // 2

    u = xbc_ref[...]
    trow = lax.broadcasted_iota(jnp.int32, (seq, CONV_DIM), 0)
    acc = jnp.zeros((seq, CONV_DIM), F32) + cb_ref[...]
    for k in range(CONV_K):
        off = k - CONV_K // 2
        shifted = u if off == 0 else pltpu.roll(u, (-off) % seq, 0)
        ok = (trow + off >= 0) & (trow + off < seq)
        acc = acc + jnp.where(ok, shifted, 0.0) * cw_ref[k:k + 1, :]
    xc_s[...] = _silu(acc)

    xdt = dt_ref[:, 0:LANES] + dtb_ref[...]
    dtv_s[...] = jnp.maximum(xdt, 0.0) + jnp.log1p(jnp.exp(-jnp.abs(xdt)))
    a_row = -jnp.exp(alog_ref[...])

    ri = lax.broadcasted_iota(jnp.int32, (q, q), 0)
    ci = lax.broadcasted_iota(jnp.int32, (q, q), 1)
    ltri = (ri >= ci).astype(F32)
    for c in range(nc):
        acum_s[c * q:(c + 1) * q, :] = _dot(ltri, dtv_s[c * q:(c + 1) * q, :] * a_row, HI)

    if use_h0:
        st_s[...] = h0_ref[...].reshape(2, n_pair, N_B, LANES)
    else:
        st_s[...] = jnp.zeros_like(st_s)

    lane_q = _lane((q, LANES))
    lo = lane_q < HALF

    def pair_bcast(mat, c0):
        return jnp.where(lo, jnp.broadcast_to(mat[:, c0:c0 + 1], (q, LANES)),
                         jnp.broadcast_to(mat[:, c0 + 1:c0 + 2], (q, LANES)))

    def chunk(dirn, c, y_s):
        r0 = pl.multiple_of(c * q, q)
        rows = pl.ds(r0, q)
        dtv = dtv_s[rows, :]
        acum = acum_s[rows, :]
        tot = acum[q - 1:q, :]
        if dirn == 0:
            expo = acum
            yscale = jnp.exp(acum)
            wend = jnp.exp(tot - acum)
            tri = ri >= ci
        else:
            aex = acum - dtv * a_row
            expo = -aex
            yscale = jnp.exp(tot - aex)
            wend = jnp.exp(aex)
            tri = ci >= ri
        cdec = jnp.exp(tot)
        expo_t = expo.T
        bm = xc_s[rows, DI_B:DI_B + LANES]
        cm = xc_s[rows, DI_B + LANES:DI_B + 2 * LANES]
        bm_t = bm.T.astype(BF16)
        bm16 = bm.astype(BF16)
        for g in range(G_B):
            in_g = (lane_q >= g * N_B) & (lane_q < (g + 1) * N_B)
            cmg = jnp.where(in_g, cm, 0.0).astype(BF16)
            cb = _dot_nt(cmg, bm16)
            for k in range(g * n_pair // G_B, (g + 1) * n_pair // G_B):
                psl = slice(k * LANES, (k + 1) * LANES)
                col = dirn * H_B + 2 * k
                xdt_p = xc_s[rows, psl] * pair_bcast(dtv, col)
                xdt16 = xdt_p.astype(BF16)
                yd = []
                for s in range(2):
                    seg = expo[:, col + s:col + s + 1] - expo_t[col + s:col + s + 1, :]
                    dec = jnp.exp(jnp.where(tri, seg, NEG_INF))
                    yd.append(_dot((cb * dec).astype(BF16), xdt16))
                y_diag = jnp.where(lo, yd[0], yd[1])
                st = st_s[dirn, k]
                st2 = jnp.concatenate([st, st], axis=0).astype(BF16)
                y_off = _dot(cmg, st2) * pair_bcast(yscale, col)
                y_s[rows, psl] = y_diag + y_off
                xw16 = (xdt_p * pair_bcast(wend, col)).astype(BF16)
                cd = jnp.where(lo[0:1], jnp.broadcast_to(cdec[:, col:col + 1], (1, LANES)),
                               jnp.broadcast_to(cdec[:, col + 1:col + 2], (1, LANES)))
                st_s[dirn, k] = cd * st + _dot(bm_t[g * N_B:(g + 1) * N_B, :], xw16)

    def body(c, carry):
        chunk(0, c, yf_s)
        chunk(1, nc - 1 - c, yb_s)
        return carry

    lax.fori_loop(0, nc, body, 0)

    dsum = dsk_ref[0:1, :] + dsk_ref[1:2, :]
    y = yf_s[...] + yb_s[...] + xc_s[:, 0:DI_B] * dsum
    y = y * _silu(z_ref[...])
    y = y * lax.rsqrt(jnp.mean(y * y, axis=-1, keepdims=True) + EPS) * g_ref[...]
    y_ref[...] = y.astype(BF16)
    if want_state:
        for dirn in range(2):
            for k in range(n_pair):
                st = st_s[dirn, k]
                st_t = jnp.concatenate([st, st], axis=0).T
                for s in range(2):
                    hs_ref[dirn, 2 * k + s] = st_t[s * P_B:(s + 1) * P_B, 0:N_B]


def _ssd_call(proj, h0t, conv_w, conv_b, dtb, alog, dskx, norm_g, *, nb, seq, want_state, li=0, carry=None):
    use_h0 = h0t is not None
    n_pair = H_B // 2
    in_specs = [pl.BlockSpec((seq, DT_PAD), lambda b: (b, COL_DT // DT_PAD)),
                pl.BlockSpec((seq, CONV_DIM), lambda b: (b, COL_XBC // CONV_DIM)),
                pl.BlockSpec((seq, DI_B), lambda b: (b, COL_Z // DI_B))]
    args = [proj, proj, proj]
    if use_h0:
        in_specs.append(pl.BlockSpec((None, 2, n_pair * N_B, LANES), lambda b: (b, 0, 0, 0)))
        args.append(h0t)
    const = lambda shape: pl.BlockSpec(shape, lambda b: (0,) * len(shape))
    in_specs += [const((CONV_K, CONV_DIM)), const((1, CONV_DIM)), const((1, LANES)), const((1, LANES)),
                 const((2, DI_B)), const((1, DI_B))]
    args += [conv_w, conv_b, dtb, alog, dskx, norm_g]
    out_specs = [pl.BlockSpec((seq, DI_B), lambda b: (b, 0))]
    out_shape = [jax.ShapeDtypeStruct((nb * seq, DI_B), BF16)]
    aliases = {}
    if want_state:
        out_specs.append(pl.BlockSpec((None, None, 2, H_B, P_B, N_B), lambda b: (b, li, 0, 0, 0, 0)))
        out_shape.append(jax.ShapeDtypeStruct((nb, DEPTH, 2, H_B, P_B, N_B), F32))
        if carry is not None:
            aliases = {len(args): 1}
            in_specs.append(pl.BlockSpec(memory_space=pl.ANY))
            args.append(carry)
    scratch =[pltpu.VMEM((seq, CONV_DIM), F32), pltpu.VMEM((seq, LANES), F32), pltpu.VMEM((seq, LANES), F32),
               pltpu.VMEM((seq, DI_B), F32), pltpu.VMEM((seq, DI_B), F32),
               pltpu.VMEM((2, n_pair, N_B, LANES), F32)]
    return pl.pallas_call(
        functools.partial(_ssd_kernel, seq=seq, use_h0=use_h0, want_state=want_state,
                          has_carry=carry is not None),
        grid=(nb,),
        in_specs=in_specs,
        out_specs=out_specs,
        out_shape=out_shape,
        input_output_aliases=aliases,
        scratch_shapes=scratch,
        compiler_params=_params("arbitrary"),
        name="ssd_ctx" if want_state else "ssd_lat",
    )(*args)


def _post_kernel(x_ref, ml_ref, ya_ref, yb_ref, yc_ref, ada_ref, wa_ref, wb_ref, wc_ref, wo_ref, fg_ref, o_ref,
                 *, tm, row_base, tokens_per_row, final):
    row = row_base + (pl.program_id(0) * tm) // tokens_per_row
    gate = ada_ref[pl.ds(row, 1), 2 * D_MODEL:3 * D_MODEL]
    d = D_MODEL
    merged = (_sigmoid(ml_ref[:, 0:d]) * _dot(ya_ref[...], wa_ref[...])
              + _sigmoid(ml_ref[:, d:2 * d]) * _dot(yb_ref[...], wb_ref[...])
              + _sigmoid(ml_ref[:, 2 * d:3 * d]) * _dot(yc_ref[...], wc_ref[...]))
    x = x_ref[...] + gate * _dot(merged.astype(BF16), wo_ref[...])
    if final:
        x = x * lax.rsqrt(jnp.mean(x * x, axis=-1, keepdims=True) + EPS) * fg_ref[...]
    o_ref[...] = x


def _post_call(x, proj, ya, yb, yc, ada, wa, wb, wc, wo, final_g, *, tm, row_base, tokens_per_row, final):
    t = x.shape[0]
    tok = lambda w: pl.BlockSpec((tm, w), lambda i: (i, 0))
    const = lambda shape: pl.BlockSpec(shape, lambda i: (0,) * len(shape))
    kern = functools.partial(_post_kernel, tm=tm, row_base=row_base, tokens_per_row=tokens_per_row, final=final)
    return pl.pallas_call(
        kern,
        grid=(t // tm,),
        in_specs=[tok(D_MODEL), tok(N_BRANCH * D_MODEL), tok(W_A), tok(DI_B), tok(W_C),
                  const((8, 3 * D_MODEL)), const((W_A, D_MODEL)), const((DI_B, D_MODEL)), const((W_C, D_MODEL)),
                  const((D_MODEL, D_MODEL)), const((1, D_MODEL))],
        out_specs=tok(D_MODEL),
        out_shape=jax.ShapeDtypeStruct((t, D_MODEL), F32),
        compiler_params=_params("arbitrary"),
        name="post_final" if final else "post",
    )(x, proj, ya, yb, yc, ada, wa, wb, wc, wo, final_g)


def _rope_tables():
    pos = np.arange(DEC_SEQ)
    lane = np.arange(LANES)
    l64 = lane % (2 * (DH_A // 2))
    quarter = DH_A // 4
    p = jnp.where((l64 < DH_A // 2)[None, :], (pos // GRID_W)[:, None], (pos % GRID_W)[:, None])
    inv = ROPE_BASE ** (-jnp.arange(quarter, dtype=F32) / quarter)
    ang = p.astype(F32) * inv[l64 % quarter][None, :]
    sign = jnp.where((lane % (2 * quarter)) < quarter, -1.0, 1.0).astype(F32)
    return jnp.cos(ang), jnp.sin(ang) * sign[None, :]


def _pad_lanes(v, width=LANES):
    v = v.reshape(1, -1).astype(F32)
    return jnp.pad(v, ((0, 0), (0, width - v.shape[1])))


def kernel(x_prompt, x_sample, cache_diff_k, cache_diff_v, cache_na_k, cache_na_v, state_ssd, c, c_ctx,
           norm_g, w_ada, b_ada, w_in, lam_q1, lam_k1, lam_q2, lam_k2, diff_subln_g, conv_w, conv_b,
           dt_bias, a_log, d_skip, ssd_norm_g, na_rpb, w_br_a, w_br_b, w_br_c, w_out, final_g):
    assert x_prompt.shape == (BATCH, SEQ, D_MODEL) and x_sample.shape == (DEC_BATCH, DEC_SEQ, D_MODEL)
    o = np.cumsum((0,) + (W_A,) * 4 + (DI_B, CONV_DIM, 2 * H_B) + (W_C,) * 4 + (N_BRANCH * D_MODEL,))
    w_in_p = []
    for li in range(DEPTH):
        seg = lambda a, b: w_in[li, :, o[a]:o[b]].astype(BF16)
        w_in_p.append(jnp.concatenate(
            [seg(11, 12), seg(0, 4), seg(6, 7), jnp.zeros((D_MODEL, DT_PAD - 2 * H_B), BF16),
             seg(5, 6), seg(4, 5), seg(7, 11)], axis=-1))
        assert w_in_p[li].shape[-1] == PROJ_COLS
    wa16, wb16, wc16, wo16 = (w.astype(BF16) for w in (w_br_a, w_br_b, w_br_c, w_out))

    cvecs = jnp.concatenate([c_ctx[None, :], c, jnp.zeros((8 - 1 - DEC_BATCH, D_MODEL), F32)], axis=0)
    ada = _ada_call(cvecs, w_ada, b_ada)
    cos_t, sin_t = _rope_tables()

    ck_a = cache_diff_k.reshape(DEC_BATCH, DEPTH, PAST_LEN, W_A)
    cv_a = cache_diff_v.reshape(DEC_BATCH, DEPTH, PAST_LEN, W_A)
    ck_c = cache_na_k.reshape(DEC_BATCH, DEPTH, PAST_LEN, W_C)
    cv_c = cache_na_v.reshape(DEC_BATCH, DEPTH, PAST_LEN, W_C)
    h0t = state_ssd.transpose(0, 1, 2, 5, 3, 4).reshape(DEC_BATCH, DEPTH, 2, N_B, DI_B)
    h0t = h0t.reshape(DEC_BATCH, DEPTH, 2, N_B, H_B // 2, LANES).transpose(0, 1, 2, 4, 3, 5)
    h0t = h0t.reshape(DEC_BATCH, DEPTH, 2, (H_B // 2) * N_B, LANES)

    xp = x_prompt.reshape(BATCH * SEQ, D_MODEL)
    xs = x_sample.reshape(DEC_BATCH * DEC_SEQ, D_MODEL)
    fg = final_g.reshape(1, D_MODEL)
    caches = None
    new_ssd = None
    for li in range(DEPTH):
        lam_init = 0.8 - 0.6 * math.exp(-0.3 * li)
        final = li == DEPTH - 1
        lamvec = jnp.concatenate([_pad_lanes(v[li]) for v in (lam_q1, lam_k1, lam_q2, lam_k2)], axis=0)
        subln = diff_subln_g[li].reshape(1, LANES)
        dtb = _pad_lanes(dt_bias[li])
        alog = _pad_lanes(a_log[li])
        dskx = jnp.repeat(d_skip[li], P_B, axis=-1)
        ssd_w = (conv_w[li], conv_b[li].reshape(1, CONV_DIM), dtb, alog, dskx, ssd_norm_g[li].reshape(1, DI_B))
        post_w = (wa16[li], wb16[li], wc16[li], wo16[li], fg)

        proj, *caches = _inproj_call(xp, ada[li], norm_g[li], w_in_p[li], tm=1024, row_base=0,
                                     tokens_per_row=BATCH * SEQ, li=li, carry=caches, emit_cache=True)
        ya = _attn_a_ctx_call(proj, lamvec, subln, lam_init)
        yb, new_ssd = _ssd_call(proj, None, *ssd_w, nb=BATCH, seq=SEQ, want_state=True, li=li, carry=new_ssd)
        yc = _attn_c_ctx_call(proj)
        xp = _post_call(xp, proj, ya, yb, yc, ada[li], *post_w, tm=512, row_base=0,
                        tokens_per_row=BATCH * SEQ, final=final)

        (proj,) = _inproj_call(xs, ada[li], norm_g[li], w_in_p[li], tm=512, row_base=1, tokens_per_row=DEC_SEQ)
        ya = _attn_a_lat_call(proj, ck_a, cv_a, li, cos_t, sin_t, lamvec, subln, lam_init)
        (yb,) = _ssd_call(proj, h0t[:, li], *ssd_w, nb=DEC_BATCH, seq=DEC_SEQ, want_state=False)
        yc = _attn_c_lat_call(proj, ck_c, cv_c, li, _rpb_call(na_rpb[li]))
        xs = _post_call(xs, proj, ya, yb, yc, ada[li], *post_w, tm=512, row_base=1,
                        tokens_per_row=DEC_SEQ, final=final)

    new_k_a, new_v_a, new_k_c_t, new_v_c_t = caches
    to_token_major = lambda a: a.transpose(0, 1, 4, 2, 3)
    return (xp.reshape(BATCH, SEQ, D_MODEL), xs.reshape(DEC_BATCH, DEC_SEQ, D_MODEL),
            new_k_a, new_v_a, to_token_major(new_k_c_t), to_token_major(new_v_c_t), new_ssd)
```

```python
import functools
import math

import jax
import jax.numpy as jnp
import numpy as np
from jax import lax
from jax.experimental import pallas as pl
from jax.experimental.pallas import tpu as pltpu

D_MODEL = 1024
BATCH = 32
SEQ = 256
DEPTH = 2
DEC_BATCH = 2
DEC_SEQ = 1024
PAST_LEN = 512
GRID_W = 64
GRID_ROWS = DEC_SEQ // GRID_W
H_A = 4
DH_A = 64
W_A = H_A * 2 * DH_A
H_B = 8
P_B = 64
G_B = 2
N_B = 64
DI_B = H_B * P_B
CONV_K = 5
CONV_DIM = DI_B + 2 * G_B * N_B
SSD_CHUNK = 128
H_C = 8
DH_C = 64
W_C = H_C * DH_C
NA_KH = 8
NA_KW = 16
N_BRANCH = 3
ROPE_BASE = 10000.0
EPS = 1e-6

LANES = 128
HALF = LANES // 2
DT_PAD = 256
VMEM_LIMIT = 56 * 1024 * 1024

COL_MERGE = 0
COL_QA = 3072
COL_GA = 3584
COL_XS = 4096
COL_Z = 4608
COL_QC = 5120
COL_DT = 5632
COL_BC = 5888
COL_KA = 6144
COL_VA = 6656
COL_GC = 7168
PROJ_COLS = 7680
PROJ_TN = 1536
BC_DIM = CONV_DIM - DI_B
KVC_COLS = 2 * W_C
_SRC = dict(qa=0, ka=512, va=1024, ga=1536, z=2048, xs=2560, bc=3072, dt=3328, qc=3344, kc=3856, vc=4368,
            gc=4880, merge=5392)
MAIN_SEGMENTS = ((COL_MERGE, _SRC["merge"], N_BRANCH * D_MODEL), (COL_QA, _SRC["qa"], W_A),
                 (COL_GA, _SRC["ga"], W_A), (COL_XS, _SRC["xs"], DI_B), (COL_Z, _SRC["z"], DI_B),
                 (COL_QC, _SRC["qc"], W_C), (COL_DT, _SRC["dt"], 2 * H_B), (COL_BC, _SRC["bc"], BC_DIM),
                 (COL_KA, _SRC["ka"], W_A), (COL_VA, _SRC["va"], W_A), (COL_GC, _SRC["gc"], W_C))
KVC_SEGMENTS = ((0, _SRC["kc"], W_C), (W_C, _SRC["vc"], W_C))

NA_QROWS = 4
NA_WROWS = 12
NA_TILES = 2 * NA_KH
NEG_INF = float("-inf")
HI = lax.Precision.HIGHEST
F32 = jnp.float32
BF16 = jnp.bfloat16


def _dot(a, b, precision=None):
    return jnp.dot(a, b, preferred_element_type=F32, precision=precision)


def _dot_nt(a, b):
    return lax.dot_general(a, b, (((1,), (1,)), ((), ())), preferred_element_type=F32)


def _sigmoid(x):
    return 1.0 / (1.0 + jnp.exp(-x))


def _silu(x):
    return x * _sigmoid(x)


def _lane(shape):
    return lax.broadcasted_iota(jnp.int32, shape, len(shape) - 1)


def _params(*sem):
    return pltpu.CompilerParams(dimension_semantics=sem, vmem_limit_bytes=VMEM_LIMIT)


def _ada_kernel(cv_ref, w_ref, b_ref, o_ref):
    o_ref[...] = _dot(_silu(cv_ref[...]), w_ref[...], HI) + b_ref[...]


def _ada_call(cvecs, w_ada, b_ada):
    tn = 512
    return pl.pallas_call(
        _ada_kernel,
        grid=(DEPTH, 3 * D_MODEL // tn),
        in_specs=[
            pl.BlockSpec((8, D_MODEL), lambda l, j: (0, 0)),
            pl.BlockSpec((None, D_MODEL, tn), lambda l, j: (l, 0, j)),
            pl.BlockSpec((None, 1, tn), lambda l, j: (l, 0, j)),
        ],
        out_specs=pl.BlockSpec((None, 8, tn), lambda l, j: (l, 0, j)),
        out_shape=jax.ShapeDtypeStruct((DEPTH, 8, 3 * D_MODEL), F32),
        compiler_params=_params("arbitrary", "arbitrary"),
        name="ada",
    )(cvecs, w_ada, b_ada.reshape(DEPTH, 1, 3 * D_MODEL))


def _wprep_kernel(w_ref, main_ref, kvc_ref):
    for dst, segments in ((main_ref, MAIN_SEGMENTS), (kvc_ref, KVC_SEGMENTS)):
        for d0, s0, n in segments:
            dst[:, d0:d0 + n] = w_ref[:, s0:s0 + n].astype(BF16)
    d0 = COL_DT + 2 * H_B
    main_ref[:, d0:COL_DT + DT_PAD] = jnp.zeros((main_ref.shape[0], DT_PAD - 2 * H_B), BF16)


def _wprep_call(w_in):
    tk = 256
    in_cols = w_in.shape[-1]
    return pl.pallas_call(
        _wprep_kernel,
        grid=(DEPTH, D_MODEL // tk),
        in_specs=[pl.BlockSpec((None, tk, in_cols), lambda l, i: (l, i, 0))],
        out_specs=[pl.BlockSpec((None, tk, PROJ_COLS), lambda l, i: (l, i, 0)),
                   pl.BlockSpec((None, tk, KVC_COLS), lambda l, i: (l, i, 0))],
        out_shape=[jax.ShapeDtypeStruct((DEPTH, D_MODEL, PROJ_COLS), BF16),
                   jax.ShapeDtypeStruct((DEPTH, D_MODEL, KVC_COLS), BF16)],
        compiler_params=_params("arbitrary", "arbitrary"),
        name="wprep",
    )(w_in)


def _inproj_kernel(*refs, tm, row_base, tokens_per_row, n_carry, ctx):
    x_ref, ada_ref, g_ref, w_ref, wkv_ref = refs[:5]
    outs = refs[5 + n_carry:]
    o_ref, h_s = outs[0], outs[-1]
    i = pl.program_id(0)
    j = pl.program_id(1)
    nb = tm // SEQ

    @pl.when(j == 0)
    def _():
        x = x_ref[...]
        y = x * lax.rsqrt(jnp.mean(x * x, axis=-1, keepdims=True) + EPS) * g_ref[...]
        row = row_base + (i * tm) // tokens_per_row
        shift = ada_ref[pl.ds(row, 1), 0:D_MODEL]
        scale = ada_ref[pl.ds(row, 1), D_MODEL:2 * D_MODEL]
        h_s[...] = (y * (1.0 + scale) + shift).astype(BF16)
        if ctx:
            kc_ref, vc_ref = outs[3:5]
            for b in range(nb):
                kv = _dot(h_s[b * SEQ:(b + 1) * SEQ, :], wkv_ref[...])
                kc_ref[b] = kv[:, 0:W_C].T.reshape(H_C, DH_C, SEQ)
                vc_ref[b] = kv[:, W_C:2 * W_C].T.reshape(H_C, DH_C, SEQ)
        else:
            outs[1][...] = _dot(h_s[...], wkv_ref[...])

    acc = _dot(h_s[...], w_ref[...])
    o_ref[...] = acc
    if ctx:
        @pl.when(j == COL_KA // PROJ_TN)
        def _():
            for dst, c0 in ((outs[1], COL_KA % PROJ_TN), (outs[2], COL_VA % PROJ_TN)):
                for b in range(nb):
                    for h in range(H_A):
                        dst[b, :, h, :] = acc[b * SEQ:(b + 1) * SEQ, c0 + h * LANES:c0 + (h + 1) * LANES]


def _inproj_call(x, ada, norm_g, w_main, w_kvc, *, tm, row_base, tokens_per_row, ctx, li=0, carry=None):
    t = x.shape[0]
    n_carry = 0 if carry is None else len(carry)
    kern = functools.partial(_inproj_kernel, tm=tm, row_base=row_base, tokens_per_row=tokens_per_row,
                             n_carry=n_carry, ctx=ctx)
    in_specs = [
        pl.BlockSpec((tm, D_MODEL), lambda i, j: (i, 0)),
        pl.BlockSpec((8, 3 * D_MODEL), lambda i, j: (0, 0)),
        pl.BlockSpec((1, D_MODEL), lambda i, j: (0, 0)),
        pl.BlockSpec((D_MODEL, PROJ_TN), lambda i, j: (0, j)),
        pl.BlockSpec((D_MODEL, KVC_COLS), lambda i, j: (0, 0)),
    ]
    args = [x, ada, norm_g.reshape(1, D_MODEL), w_main, w_kvc]
    out_specs = [pl.BlockSpec((tm, PROJ_TN), lambda i, j: (i, j))]
    out_shape = [jax.ShapeDtypeStruct((t, PROJ_COLS), F32)]
    aliases = {}
    if ctx:
        nb = tm // SEQ
        out_specs += [pl.BlockSpec((nb, None, SEQ, H_A, 2 * DH_A), lambda i, j: (i, li, 0, 0, 0))] * 2
        out_specs += [pl.BlockSpec((nb, None, H_C, DH_C, SEQ), lambda i, j: (i, li, 0, 0, 0))] * 2
        out_shape += [jax.ShapeDtypeStruct((BATCH, DEPTH, SEQ, H_A, 2 * DH_A), F32)] * 2
        out_shape += [jax.ShapeDtypeStruct((BATCH, DEPTH, H_C, DH_C, SEQ), F32)] * 2
        if carry is not None:
            in_specs += [pl.BlockSpec(memory_space=pl.ANY)] * n_carry
            args += list(carry)
            aliases = {5 + k: 1 + k for k in range(n_carry)}
    else:
        out_specs.append(pl.BlockSpec((tm, KVC_COLS), lambda i, j: (i, 0)))
        out_shape.append(jax.ShapeDtypeStruct((t, KVC_COLS), F32))
    return pl.pallas_call(
        kern,
        grid=(t // tm, PROJ_COLS // PROJ_TN),
        in_specs=in_specs,
        out_specs=out_specs,
        out_shape=out_shape,
        input_output_aliases=aliases,
        scratch_shapes=[pltpu.VMEM((tm, D_MODEL), BF16)],
        compiler_params=_params("arbitrary", "arbitrary"),
        name="inproj_ctx" if ctx else "inproj_lat",
    )(*args)


def _diff_lambda_in_kernel(lam_ref, lam_init):
    v = lam_ref[...]
    l1 = jnp.sum(v[0:1] * v[1:2], axis=-1, keepdims=True)
    l2 = jnp.sum(v[2:3] * v[3:4], axis=-1, keepdims=True)
    return jnp.exp(l1) - jnp.exp(l2) + lam_init


def _split_halves(x, scale):
    lo = _lane(x.shape) < HALF
    xs = x * scale
    return jnp.concatenate([jnp.where(lo, xs, 0.0), jnp.where(lo, 0.0, xs)], axis=0).astype(BF16)


def _diff_head_post(o, subln_g, lam_init, gate):
    o = o * lax.rsqrt(jnp.mean(o * o, axis=-1, keepdims=True) + EPS) * subln_g
    return (o * (1.0 - lam_init) * _silu(gate)).astype(BF16)


def _attn_a_ctx_kernel(q_ref, k_ref, v_ref, g_ref, lam_ref, sg_ref, o_ref, *, lam_init):
    t = q_ref.shape[0]
    lam = _diff_lambda_in_kernel(lam_ref, lam_init)
    for h in range(H_A):
        sl = slice(h * LANES, (h + 1) * LANES)
        qq = _split_halves(q_ref[:, sl], DH_A ** -0.5)
        s = _dot_nt(qq, k_ref[:, sl].astype(BF16))
        e = jnp.exp(s - jnp.max(s, axis=-1, keepdims=True))
        p = e * (1.0 / jnp.sum(e, axis=-1, keepdims=True))
        pd = (p[:t] - lam * p[t:]).astype(BF16)
        o = _dot(pd, v_ref[:, sl].astype(BF16))
        o_ref[:, sl] = _diff_head_post(o, sg_ref[...], lam_init, g_ref[:, sl])


def _attn_a_ctx_call(proj, lamvec, subln_g, lam_init):
    t = SEQ
    blk = lambda c: pl.BlockSpec((t, W_A), lambda b: (b, c // W_A))
    return pl.pallas_call(
        functools.partial(_attn_a_ctx_kernel, lam_init=lam_init),
        grid=(BATCH,),
        in_specs=[blk(COL_QA), blk(COL_KA), blk(COL_VA), blk(COL_GA),
                  pl.BlockSpec((4, LANES), lambda b: (0, 0)),
                  pl.BlockSpec((1, LANES), lambda b: (0, 0))],
        out_specs=pl.BlockSpec((t, W_A), lambda b: (b, 0)),
        out_shape=jax.ShapeDtypeStruct((BATCH * t, W_A), BF16),
        compiler_params=_params("arbitrary"),
        name="attn_a_ctx",
    )(proj, proj, proj, proj, lamvec, subln_g)


def _rope(x, cos, sin_signed):
    first = (_lane(x.shape) % 32) < 16
    swapped = jnp.where(first, pltpu.roll(x, LANES - 16, 1), pltpu.roll(x, 16, 1))
    return x * cos + swapped * sin_signed


def _attn_a_lat_kernel(q_ref, k_ref, v_ref, g_ref, ck_ref, cv_ref, cosq_ref, sinq_ref, cosk_ref, sink_ref,
                       lam_ref, sg_ref, o_ref, kr_s, *, lam_init):
    tq = q_ref.shape[0]

    @pl.when(pl.program_id(1) == 0)
    def _():
        for h in range(H_A):
            sl = slice(h * LANES, (h + 1) * LANES)
            kr_s[:, sl] = _rope(k_ref[:, sl], cosk_ref[...], sink_ref[...]).astype(BF16)

    lam = _diff_lambda_in_kernel(lam_ref, lam_init)
    for h in range(H_A):
        sl = slice(h * LANES, (h + 1) * LANES)
        qq = _split_halves(_rope(q_ref[:, sl], cosq_ref[...], sinq_ref[...]), DH_A ** -0.5)
        s_lat = _dot_nt(qq, kr_s[:, sl])
        s_ctx = _dot_nt(qq, ck_ref[:, sl].astype(BF16))
        m = jnp.maximum(jnp.max(s_lat, axis=-1, keepdims=True), jnp.max(s_ctx, axis=-1, keepdims=True))
        e_lat = jnp.exp(s_lat - m)
        e_ctx = jnp.exp(s_ctx - m)
        r = 1.0 / (jnp.sum(e_lat, axis=-1, keepdims=True) + jnp.sum(e_ctx, axis=-1, keepdims=True))
        p_lat = e_lat * r
        p_ctx = e_ctx * r
        pd_lat = (p_lat[:tq] - lam * p_lat[tq:]).astype(BF16)
        pd_ctx = (p_ctx[:tq] - lam * p_ctx[tq:]).astype(BF16)
        o = _dot(pd_lat, v_ref[:, sl].astype(BF16)) + _dot(pd_ctx, cv_ref[:, sl].astype(BF16))
        o_ref[:, sl] = _diff_head_post(o, sg_ref[...], lam_init, g_ref[:, sl])


def _attn_a_lat_call(proj, cache_k, cache_v, li, cos_t, sin_t, lamvec, subln_g, lam_init):
    tq = 256
    nq = DEC_SEQ // tq
    qblk = lambda c: pl.BlockSpec((tq, W_A), lambda b, i: (b * nq + i, c // W_A))
    full = lambda c: pl.BlockSpec((DEC_SEQ, W_A), lambda b, i: (b, c // W_A))
    cache = pl.BlockSpec((None, None, PAST_LEN, W_A), lambda b, i: (b, li, 0, 0))
    return pl.pallas_call(
        functools.partial(_attn_a_lat_kernel, lam_init=lam_init),
        grid=(DEC_BATCH, nq),
        in_specs=[qblk(COL_QA), full(COL_KA), full(COL_VA), qblk(COL_GA), cache, cache,
                  pl.BlockSpec((tq, LANES), lambda b, i: (i, 0)),
                  pl.BlockSpec((tq, LANES), lambda b, i: (i, 0)),
                  pl.BlockSpec((DEC_SEQ, LANES), lambda b, i: (0, 0)),
                  pl.BlockSpec((DEC_SEQ, LANES), lambda b, i: (0, 0)),
                  pl.BlockSpec((4, LANES), lambda b, i: (0, 0)),
                  pl.BlockSpec((1, LANES), lambda b, i: (0, 0))],
        out_specs=pl.BlockSpec((tq, W_A), lambda b, i: (b * nq + i, 0)),
        out_shape=jax.ShapeDtypeStruct((DEC_BATCH * DEC_SEQ, W_A), BF16),
        scratch_shapes=[pltpu.VMEM((DEC_SEQ, W_A), BF16)],
        compiler_params=_params("arbitrary", "arbitrary"),
        name="attn_a_lat",
    )(proj, proj, proj, proj, cache_k, cache_v, cos_t, sin_t, cos_t, sin_t, lamvec, subln_g)


def _merge_halves(o, t):
    return jnp.where(_lane((t, LANES)) < HALF, o[:t], o[t:])


def _attn_c_ctx_kernel(q_ref, kt_ref, vt_ref, g_ref, o_ref):
    t = q_ref.shape[0]
    for j in range(H_C // 2):
        sl = slice(j * LANES, (j + 1) * LANES)
        qq = _split_halves(q_ref[:, sl], DH_C ** -0.5)
        kt = kt_ref[2 * j:2 * j + 2].reshape(LANES, t).astype(BF16)
        vt = vt_ref[2 * j:2 * j + 2].reshape(LANES, t).astype(BF16)
        s = _dot(qq, kt)
        e = jnp.exp(s - jnp.max(s, axis=-1, keepdims=True))
        p = (e * (1.0 / jnp.sum(e, axis=-1, keepdims=True))).astype(BF16)
        o = _merge_halves(_dot_nt(p, vt), t)
        o_ref[:, sl] = (o * _silu(g_ref[:, sl])).astype(BF16)


def _attn_c_ctx_call(proj, kc_t, vc_t, li):
    t = SEQ
    blk = lambda c: pl.BlockSpec((t, W_C), lambda b: (b, c // W_C))
    cache = pl.BlockSpec((None, None, H_C, DH_C, t), lambda b: (b, li, 0, 0, 0))
    return pl.pallas_call(
        _attn_c_ctx_kernel,
        grid=(BATCH,),
        in_specs=[blk(COL_QC), cache, cache, blk(COL_GC)],
        out_specs=pl.BlockSpec((t, W_C), lambda b: (b, 0)),
        out_shape=jax.ShapeDtypeStruct((BATCH * t, W_C), BF16),
        compiler_params=_params("arbitrary"),
        name="attn_c_ctx",
    )(proj, kc_t, vc_t, proj)


def _rpb_kernel(rpb_ref, o_ref):
    h = pl.program_id(0)
    shape = (GRID_W, LANES)
    c = lax.broadcasted_iota(jnp.int32, shape, 0)
    cp = _lane(shape) % GRID_W
    d = jnp.clip(cp - c, -(NA_KW - 1), NA_KW - 1) + (NA_KW - 1)
    start = jnp.clip(c - NA_KW // 2, 0, GRID_W - NA_KW)
    in_win = (cp >= start) & (cp < start + NA_KW)
    o_ref[0] = jnp.full(shape, NEG_INF, F32)
    n_dc = 2 * NA_KW - 1
    for dr in range(2 * NA_KH - 1):
        acc = jnp.full(shape, NEG_INF, F32)
        for dc in range(n_dc):
            acc = jnp.where(d == dc, rpb_ref[(h * (2 * NA_KH - 1) + dr) * n_dc + dc], acc)
        o_ref[1 + dr] = jnp.where(in_win, acc, NEG_INF)


def _rpb_call(rpb):
    return pl.pallas_call(
        _rpb_kernel,
        grid=(H_C,),
        in_specs=[pl.BlockSpec(memory_space=pltpu.SMEM)],
        out_specs=pl.BlockSpec((None, NA_TILES, GRID_W, LANES), lambda h: (h, 0, 0, 0)),
        out_shape=jax.ShapeDtypeStruct((H_C, NA_TILES, GRID_W, LANES), F32),
        compiler_params=_params("arbitrary"),
        name="rpb_tiles",
    )(rpb.reshape(-1))


def _attn_c_lat_kernel(q_ref, k_ref, v_ref, g_ref, ck_ref, cv_ref, tile_ref, o_ref, bias_s):
    tq = q_ref.shape[0]
    nwin = NA_WROWS * GRID_W
    m = pl.program_id(1)
    w0 = jnp.where(m < (GRID_ROWS // NA_QROWS) // 2, 0, GRID_ROWS - NA_WROWS)
    k0 = pl.multiple_of(w0 * GRID_W, GRID_W)
    lo = _lane((GRID_W, LANES)) < HALF
    for j in range(H_C // 2):
        sl = slice(j * LANES, (j + 1) * LANES)
        for s in range(2):
            for i in range(NA_QROWS):
                r = m * NA_QROWS + i
                start = jnp.clip(r - NA_KH // 2, 0, GRID_ROWS - NA_KH)
                for jp in range(NA_WROWS // 2):
                    idx = []
                    for u in range(2):
                        rk = w0 + 2 * jp + u
                        valid = (rk >= start) & (rk < start + NA_KH)
                        idx.append(jnp.where(valid, rk - r + NA_KH, 0))
                    tile = jnp.where(lo, tile_ref[2 * j + s, idx[0]], tile_ref[2 * j + s, idx[1]])
                    bias_s[(s * NA_QROWS + i) * GRID_W:(s * NA_QROWS + i + 1) * GRID_W,
                           jp * LANES:(jp + 1) * LANES] = tile
        qq = _split_halves(q_ref[:, sl], DH_C ** -0.5)
        kw = k_ref[pl.ds(k0, nwin), sl].astype(BF16)
        vw = v_ref[pl.ds(k0, nwin), sl].astype(BF16)
        s_win = _dot_nt(qq, kw) + bias_s[...]
        s_ctx = _dot_nt(qq, ck_ref[:, sl].astype(BF16))
        mx = jnp.maximum(jnp.max(s_win, axis=-1, keepdims=True), jnp.max(s_ctx, axis=-1, keepdims=True))
        e_win = jnp.exp(s_win - mx)
        e_ctx = jnp.exp(s_ctx - mx)
        rs = 1.0 / (jnp.sum(e_win, axis=-1, keepdims=True) + jnp.sum(e_ctx, axis=-1, keepdims=True))
        o = _dot((e_win * rs).astype(BF16), vw) + _dot((e_ctx * rs).astype(BF16), cv_ref[:, sl].astype(BF16))
        o_ref[:, sl] = (_merge_halves(o, tq) * _silu(g_ref[:, sl])).astype(BF16)


def _attn_c_lat_call(proj, kv, cache_k, cache_v, li, tiles):
    tq = NA_QROWS * GRID_W
    nq = DEC_SEQ // tq
    qblk = lambda c: pl.BlockSpec((tq, W_C), lambda b, i: (b * nq + i, c // W_C))
    full = lambda c: pl.BlockSpec((DEC_SEQ, W_C), lambda b, i: (b, c // W_C))
    cache = pl.BlockSpec((None, None, PAST_LEN, W_C), lambda b, i: (b, li, 0, 0))
    return pl.pallas_call(
        _attn_c_lat_kernel,
        grid=(DEC_BATCH, nq),
        in_specs=[qblk(COL_QC), full(0), full(W_C), qblk(COL_GC), cache, cache,
                  pl.BlockSpec((H_C, NA_TILES, GRID_W, LANES), lambda b, i: (0, 0, 0, 0))],
        out_specs=pl.BlockSpec((tq, W_C), lambda b, i: (b * nq + i, 0)),
        out_shape=jax.ShapeDtypeStruct((DEC_BATCH * DEC_SEQ, W_C), BF16),
        scratch_shapes=[pltpu.VMEM((2 * tq, NA_WROWS * GRID_W), F32)],
        compiler_params=_params("arbitrary", "arbitrary"),
        name="attn_c_lat",
    )(proj, kv, kv, proj, cache_k, cache_v, tiles)


def _ssd_kernel(*refs, seq, use_h0, want_state, has_carry):
    refs = list(refs)
    dt_ref, xs_ref, bc_ref, z_ref = refs[:4]
    pos = 4
    h0_ref = None
    if use_h0:
        h0_ref = refs[pos]
        pos += 1
    cw_ref, cb_ref, dtb_ref, alog_ref, dsk_ref, g_ref = refs[pos:pos + 6]
    pos += 6 + int(has_carry)
    y_ref = refs[pos]
    pos += 1
    hs_ref = None
    if want_state:
        hs_ref = refs[pos]
        pos += 1
    upad_s, xc_s, expo_s, expot_s, dtt_s, tot_s, bmt_s, yf_s, yb_s, st_s = refs[pos:]

    q = SSD_CHUNK
    nc = seq // q
    n_pair = H_B // 2
    n_hd = 2 * H_B
    pad = 8

    upad_s[0:pad, :] = jnp.zeros((pad, CONV_DIM), F32)
    upad_s[pad + seq:2 * pad + seq, :] = jnp.zeros((pad, CONV_DIM), F32)
    upad_s[pad:pad + seq, 0:DI_B] = xs_ref[...]
    upad_s[pad:pad + seq, DI_B:CONV_DIM] = bc_ref[...]

    for c in range(nc):
        for cb_ in range(CONV_DIM // LANES):
            csl = slice(cb_ * LANES, (cb_ + 1) * LANES)
            acc = jnp.zeros((q, LANES), F32) + cb_ref[:, csl]
            for k in range(CONV_K):
                r0 = c * q + pad - CONV_K // 2 + k
                acc = acc + upad_s[r0:r0 + q, csl] * cw_ref[k:k + 1, csl]
            xc_s[c * q:(c + 1) * q, csl] = _silu(acc)

    a_row = -jnp.exp(alog_ref[...])
    a_col = jnp.broadcast_to(a_row, (LANES, LANES)).T[0:n_hd, 0:1]
    ri = lax.broadcasted_iota(jnp.int32, (q, q), 0)
    ci = lax.broadcasted_iota(jnp.int32, (q, q), 1)
    ltri = (ri >= ci).astype(F32)
    fwd_lane = _lane((q, LANES)) < H_B
    fwd_row = lax.broadcasted_iota(jnp.int32, (n_hd, q), 0) < H_B

    def prep_body(c, carry):
        rows = pl.ds(pl.multiple_of(c * q, q), q)
        xdt = dt_ref[rows, 0:LANES] + dtb_ref[...]
        dtv = jnp.maximum(xdt, 0.0) + jnp.log1p(jnp.exp(-jnp.abs(xdt)))
        la = dtv * a_row
        acum = _dot(ltri, la, HI)
        expo_s[rows, :] = jnp.where(fwd_lane, acum, la - acum)
        acum_t = acum.T[0:n_hd, :]
        dt_t = dtv.T[0:n_hd, :]
        expot_s[c] = jnp.where(fwd_row, acum_t, dt_t * a_col - acum_t)
        dtt_s[c] = dt_t
        tot_s[c] = jnp.broadcast_to(acum_t[:, q - 1:q], (n_hd, q))
        bmt_s[c] = xc_s[rows, DI_B:DI_B + LANES].T
        return carry

    lax.fori_loop(0, nc, prep_body, 0, unroll=2)

    if use_h0:
        st_s[...] = h0_ref[...].reshape(2, n_pair, N_B, LANES)
    else:
        st_s[...] = jnp.zeros_like(st_s)

    lane_q = _lane((q, LANES))
    lo = lane_q < HALF
    lo_st = _lane((N_B, LANES)) < HALF

    def chunk(dirn, c, y_s):
        rows = pl.ds(pl.multiple_of(c * q, q), q)
        tri = (ri >= ci) if dirn == 0 else (ci >= ri)
        bm16 = xc_s[rows, DI_B:DI_B + LANES].astype(BF16)
        cm = xc_s[rows, DI_B + LANES:DI_B + 2 * LANES]
        for g in range(G_B):
            in_g = (lane_q >= g * N_B) & (lane_q < (g + 1) * N_B)
            cmg = jnp.where(in_g, cm, 0.0).astype(BF16)
            cb = _dot_nt(cmg, bm16)
            bmt_g = bmt_s[c, g * N_B:(g + 1) * N_B, :]
            for k in range(g * n_pair // G_B, (g + 1) * n_pair // G_B):
                psl = slice(k * LANES, (k + 1) * LANES)
                x16 = xc_s[rows, psl].astype(BF16)
                mats, lhs, ysc, cdec = [], [], [], []
                for s in range(2):
                    col = dirn * H_B + 2 * k + s
                    e_col = jnp.broadcast_to(expo_s[rows, col:col + 1], (q, q))
                    e_row = expot_s[c, col:col + 1, :]
                    dt_row = dtt_s[c, col:col + 1, :]
                    tot = tot_s[c, col:col + 1, :]
                    dec = jnp.exp(jnp.where(tri, e_col - e_row, NEG_INF))
                    mats.append((cb * dec * dt_row).astype(BF16))
                    if dirn == 0:
                        ysc.append(jnp.exp(e_col))
                        w_row = jnp.exp(tot - e_row)
                    else:
                        ysc.append(jnp.exp(e_col + tot))
                        w_row = jnp.exp(-e_row)
                    lhs.append((bmt_g * (w_row * dt_row)).astype(BF16))
                    cdec.append(jnp.exp(tot[:, 0:LANES]))
                yd = _dot(jnp.concatenate(mats, axis=0), x16)
                st = st_s[dirn, k]
                y_off = _dot(cmg, jnp.concatenate([st, st], axis=0).astype(BF16))
                y_s[rows, psl] = jnp.where(lo, yd[:q] + ysc[0] * y_off, yd[q:] + ysc[1] * y_off)
                ds = _dot(jnp.concatenate(lhs, axis=0), x16)
                st_s[dirn, k] = jnp.where(lo_st, cdec[0] * st + ds[:N_B], cdec[1] * st + ds[N_B:])

    def body(c, carry):
        chunk(0, c, yf_s)
        chunk(1, nc - 1 - c, yb_s)
        return carry

    lax.fori_loop(0, nc, body, 0, unroll=2)

    dsum = dsk_ref[0:1, :] + dsk_ref[1:2, :]

    def out_body(c, carry):
        rows = pl.ds(pl.multiple_of(c * q, q), q)
        y = yf_s[rows, :] + yb_s[rows, :] + xc_s[rows, 0:DI_B] * dsum
        y = y * _silu(z_ref[rows, :])
        y = y * lax.rsqrt(jnp.mean(y * y, axis=-1, keepdims=True) + EPS) * g_ref[...]
        y_ref[rows, :] = y.astype(BF16)
        return carry

    lax.fori_loop(0, nc, out_body, 0)
    if want_state:
        for dirn in range(2):
            for k in range(n_pair):
                st = st_s[dirn, k]
                st_t = jnp.concatenate([st, st], axis=0).T
                for s in range(2):
                    hs_ref[dirn, 2 * k + s] = st_t[s * P_B:(s + 1) * P_B, 0:N_B]


def _ssd_call(proj, h0t, conv_w, conv_b, dtb, alog, dskx, norm_g, *, nb, seq, want_state, li=0, carry=None):
    use_h0 = h0t is not None
    n_pair = H_B // 2
    in_specs = [pl.BlockSpec((seq, DT_PAD), lambda b: (b, COL_DT // DT_PAD)),
                pl.BlockSpec((seq, DI_B), lambda b: (b, COL_XS // DI_B)),
                pl.BlockSpec((seq, BC_DIM), lambda b: (b, COL_BC // BC_DIM)),
                pl.BlockSpec((seq, DI_B), lambda b: (b, COL_Z // DI_B))]
    args = [proj, proj, proj, proj]
    if use_h0:
        in_specs.append(pl.BlockSpec((None, 2, n_pair * N_B, LANES), lambda b: (b, 0, 0, 0)))
        args.append(h0t)
    const = lambda shape: pl.BlockSpec(shape, lambda b: (0,) * len(shape))
    in_specs += [const((CONV_K, CONV_DIM)), const((1, CONV_DIM)), const((1, LANES)), const((1, LANES)),
                 const((2, DI_B)), const((1, DI_B))]
    args += [conv_w, conv_b, dtb, alog, dskx, norm_g]
    out_specs = [pl.BlockSpec((seq, DI_B), lambda b: (b, 0))]
    out_shape = [jax.ShapeDtypeStruct((nb * seq, DI_B), BF16)]
    aliases = {}
    if want_state:
        out_specs.append(pl.BlockSpec((None, None, 2, H_B, P_B, N_B), lambda b: (b, li, 0, 0, 0, 0)))
        out_shape.append(jax.ShapeDtypeStruct((nb, DEPTH, 2, H_B, P_B, N_B), F32))
        if carry is not None:
            aliases = {len(args): 1}
            in_specs.append(pl.BlockSpec(memory_space=pl.ANY))
            args.append(carry)
    nc = seq // SSD_CHUNK
    per_chunk_rows = pltpu.VMEM((nc, 2 * H_B, SSD_CHUNK), F32)
    scratch = [pltpu.VMEM((seq + 16, CONV_DIM), F32), pltpu.VMEM((seq, CONV_DIM), F32),
               pltpu.VMEM((seq, LANES), F32), per_chunk_rows, per_chunk_rows, per_chunk_rows,
               pltpu.VMEM((nc, LANES, SSD_CHUNK), F32),
               pltpu.VMEM((seq, DI_B), F32), pltpu.VMEM((seq, DI_B), F32),
               pltpu.VMEM((2, n_pair, N_B, LANES), F32)]
    return pl.pallas_call(
        functools.partial(_ssd_kernel, seq=seq, use_h0=use_h0, want_state=want_state,
                          has_carry=carry is not None),
        grid=(nb,),
        in_specs=in_specs,
        out_specs=out_specs,
        out_shape=out_shape,
        input_output_aliases=aliases,
        scratch_shapes=scratch,
        compiler_params=_params("arbitrary"),
        name="ssd_ctx" if want_state else "ssd_lat",
    )(*args)


def _post_kernel(x_ref, ml_ref, ya_ref, yb_ref, yc_ref, ada_ref, wa_ref, wb_ref, wc_ref, wo_ref, fg_ref, o_ref,
                 *, tm, row_base, tokens_per_row, final):
    row = row_base + (pl.program_id(0) * tm) // tokens_per_row
    gate = ada_ref[pl.ds(row, 1), 2 * D_MODEL:3 * D_MODEL]
    d = D_MODEL
    merged = (_sigmoid(ml_ref[:, 0:d]) * _dot(ya_ref[...], wa_ref[...])
              + _sigmoid(ml_ref[:, d:2 * d]) * _dot(yb_ref[...], wb_ref[...])
              + _sigmoid(ml_ref[:, 2 * d:3 * d]) * _dot(yc_ref[...], wc_ref[...]))
    x = x_ref[...] + gate * _dot(merged.astype(BF16), wo_ref[...])
    if final:
        x = x * lax.rsqrt(jnp.mean(x * x, axis=-1, keepdims=True) + EPS) * fg_ref[...]
    o_ref[...] = x


def _post_call(x, proj, ya, yb, yc, ada, wa, wb, wc, wo, final_g, *, tm, row_base, tokens_per_row, final):
    t = x.shape[0]
    tok = lambda w: pl.BlockSpec((tm, w), lambda i: (i, 0))
    const = lambda shape: pl.BlockSpec(shape, lambda i: (0,) * len(shape))
    kern = functools.partial(_post_kernel, tm=tm, row_base=row_base, tokens_per_row=tokens_per_row, final=final)
    return pl.pallas_call(
        kern,
        grid=(t // tm,),
        in_specs=[tok(D_MODEL), tok(N_BRANCH * D_MODEL), tok(W_A), tok(DI_B), tok(W_C),
                  const((8, 3 * D_MODEL)), const((W_A, D_MODEL)), const((DI_B, D_MODEL)), const((W_C, D_MODEL)),
                  const((D_MODEL, D_MODEL)), const((1, D_MODEL))],
        out_specs=tok(D_MODEL),
        out_shape=jax.ShapeDtypeStruct((t, D_MODEL), F32),
        compiler_params=_params("arbitrary"),
        name="post_final" if final else "post",
    )(x, proj, ya, yb, yc, ada, wa, wb, wc, wo, final_g)


def _rope_tables():
    pos = np.arange(DEC_SEQ)
    lane = np.arange(LANES)
    l64 = lane % (2 * (DH_A // 2))
    quarter = DH_A // 4
    p = jnp.where((l64 < DH_A // 2)[None, :], (pos // GRID_W)[:, None], (pos % GRID_W)[:, None])
    inv = ROPE_BASE ** (-jnp.arange(quarter, dtype=F32) / quarter)
    ang = p.astype(F32) * inv[l64 % quarter][None, :]
    sign = jnp.where((lane % (2 * quarter)) < quarter, -1.0, 1.0).astype(F32)
    return jnp.cos(ang), jnp.sin(ang) * sign[None, :]


def _pad_lanes(v, width=LANES):
    v = v.reshape(1, -1).astype(F32)
    return jnp.pad(v, ((0, 0), (0, width - v.shape[1])))


def kernel(x_prompt, x_sample, cache_diff_k, cache_diff_v, cache_na_k, cache_na_v, state_ssd, c, c_ctx,
           norm_g, w_ada, b_ada, w_in, lam_q1, lam_k1, lam_q2, lam_k2, diff_subln_g, conv_w, conv_b,
           dt_bias, a_log, d_skip, ssd_norm_g, na_rpb, w_br_a, w_br_b, w_br_c, w_out, final_g):
    assert x_prompt.shape == (BATCH, SEQ, D_MODEL) and x_sample.shape == (DEC_BATCH, DEC_SEQ, D_MODEL)
    assert w_in.shape == (DEPTH, D_MODEL, _SRC["merge"] + N_BRANCH * D_MODEL)
    w_main, w_kvc = _wprep_call(w_in)
    wa16, wb16, wc16, wo16 = (w.astype(BF16) for w in (w_br_a, w_br_b, w_br_c, w_out))

    cvecs = jnp.concatenate([c_ctx[None, :], c, jnp.zeros((8 - 1 - DEC_BATCH, D_MODEL), F32)], axis=0)
    ada = _ada_call(cvecs, w_ada, b_ada)
    cos_t, sin_t = _rope_tables()

    ck_a = cache_diff_k.reshape(DEC_BATCH, DEPTH, PAST_LEN, W_A)
    cv_a = cache_diff_v.reshape(DEC_BATCH, DEPTH, PAST_LEN, W_A)
    ck_c = cache_na_k.reshape(DEC_BATCH, DEPTH, PAST_LEN, W_C)
    cv_c = cache_na_v.reshape(DEC_BATCH, DEPTH, PAST_LEN, W_C)
    h0t = state_ssd.transpose(0, 1, 2, 5, 3, 4).reshape(DEC_BATCH, DEPTH, 2, N_B, DI_B)
    h0t = h0t.reshape(DEC_BATCH, DEPTH, 2, N_B, H_B // 2, LANES).transpose(0, 1, 2, 4, 3, 5)
    h0t = h0t.reshape(DEC_BATCH, DEPTH, 2, (H_B // 2) * N_B, LANES)

    xp = x_prompt.reshape(BATCH * SEQ, D_MODEL)
    xs = x_sample.reshape(DEC_BATCH * DEC_SEQ, D_MODEL)
    fg = final_g.reshape(1, D_MODEL)
    caches = None
    new_ssd = None
    for li in range(DEPTH):
        lam_init = 0.8 - 0.6 * math.exp(-0.3 * li)
        final = li == DEPTH - 1
        lamvec = jnp.concatenate([_pad_lanes(v[li]) for v in (lam_q1, lam_k1, lam_q2, lam_k2)], axis=0)
        subln = diff_subln_g[li].reshape(1, LANES)
        dtb = _pad_lanes(dt_bias[li])
        alog = _pad_lanes(a_log[li])
        dskx = jnp.repeat(d_skip[li], P_B, axis=-1)
        ssd_w = (conv_w[li], conv_b[li].reshape(1, CONV_DIM), dtb, alog, dskx, ssd_norm_g[li].reshape(1, DI_B))
        post_w = (wa16[li], wb16[li], wc16[li], wo16[li], fg)

        proj, *caches = _inproj_call(xp, ada[li], norm_g[li], w_main[li], w_kvc[li], tm=1024, row_base=0,
                                     tokens_per_row=BATCH * SEQ, ctx=True, li=li, carry=caches)
        ya = _attn_a_ctx_call(proj, lamvec, subln, lam_init)
        yb, new_ssd = _ssd_call(proj, None, *ssd_w, nb=BATCH, seq=SEQ, want_state=True, li=li, carry=new_ssd)
        yc = _attn_c_ctx_call(proj, caches[2], caches[3], li)
        xp = _post_call(xp, proj, ya, yb, yc, ada[li], *post_w, tm=512, row_base=0,
                        tokens_per_row=BATCH * SEQ, final=final)

        proj, kv = _inproj_call(xs, ada[li], norm_g[li], w_main[li], w_kvc[li], tm=1024, row_base=1,
                                tokens_per_row=DEC_SEQ, ctx=False)
        ya = _attn_a_lat_call(proj, ck_a, cv_a, li, cos_t, sin_t, lamvec, subln, lam_init)
        (yb,) = _ssd_call(proj, h0t[:, li], *ssd_w, nb=DEC_BATCH, seq=DEC_SEQ, want_state=False)
        yc = _attn_c_lat_call(proj, kv, ck_c, cv_c, li, _rpb_call(na_rpb[li]))
        xs = _post_call(xs, proj, ya, yb, yc, ada[li], *post_w, tm=512, row_base=1,
                        tokens_per_row=DEC_SEQ, final=final)

    new_k_a, new_v_a, new_k_c_t, new_v_c_t = caches
    to_token_major = lambda a: a.transpose(0, 1, 4, 2, 3)
    return (xp.reshape(BATCH, SEQ, D_MODEL), xs.reshape(DEC_BATCH, DEC_SEQ, D_MODEL),
            new_k_a, new_v_a, to_token_major(new_k_c_t), to_token_major(new_v_c_t), new_ssd)
```

```python
import functools
import math

import jax
import jax.numpy as jnp
import numpy as np
from jax import lax
from jax.experimental import pallas as pl
from jax.experimental.pallas import tpu as pltpu

D_MODEL = 1024
BATCH = 32
SEQ = 256
DEPTH = 2
DEC_BATCH = 2
DEC_SEQ = 1024
PAST_LEN = 512
GRID_W = 64
GRID_ROWS = DEC_SEQ // GRID_W
H_A = 4
DH_A = 64
W_A = H_A * 2 * DH_A
H_B = 8
P_B = 64
G_B = 2
N_B = 64
DI_B = H_B * P_B
CONV_K = 5
CONV_DIM = DI_B + 2 * G_B * N_B
SSD_CHUNK = 128
H_C = 8
DH_C = 64
W_C = H_C * DH_C
NA_KH = 8
NA_KW = 16
N_BRANCH = 3
ROPE_BASE = 10000.0
EPS = 1e-6

LANES = 128
HALF = LANES // 2
DT_PAD = 256
VMEM_LIMIT = 56 * 1024 * 1024

COL_QA = 0
COL_GA = 512
COL_XS = 1024
COL_Z = 1536
COL_QC = 2048
COL_DT = 2560
COL_BC = 2816
COL_KA = 3072
COL_VA = 3584
COL_GC = 4096
PROJ_COLS = 4608
PROJ_TN = 1536
BC_DIM = CONV_DIM - DI_B
KVC_COLS = 2 * W_C
MERGE_COLS = N_BRANCH * D_MODEL
_SRC = dict(qa=0, ka=512, va=1024, ga=1536, z=2048, xs=2560, bc=3072, dt=3328, qc=3344, kc=3856, vc=4368,
            gc=4880, merge=5392)
MAIN_SEGMENTS = ((_SRC["qa"], W_A), (_SRC["ga"], W_A), (_SRC["xs"], DI_B), (_SRC["z"], DI_B),
                 (_SRC["qc"], W_C), (_SRC["dt"], 2 * H_B), (None, DT_PAD - 2 * H_B), (_SRC["bc"], BC_DIM),
                 (_SRC["ka"], W_A), (_SRC["va"], W_A), (_SRC["gc"], W_C))

NA_QROWS = 4
NA_WROWS = 12
NA_TILES = 2 * NA_KH
NEG_INF = float("-inf")
HI = lax.Precision.HIGHEST
F32 = jnp.float32
BF16 = jnp.bfloat16


def _dot(a, b, precision=None):
    return jnp.dot(a, b, preferred_element_type=F32, precision=precision)


def _dot_nt(a, b):
    return lax.dot_general(a, b, (((1,), (1,)), ((), ())), preferred_element_type=F32)


def _sigmoid(x):
    return 1.0 / (1.0 + jnp.exp(-x))


def _silu(x):
    return x * _sigmoid(x)


def _lane(shape):
    return lax.broadcasted_iota(jnp.int32, shape, len(shape) - 1)


def _params(*sem):
    return pltpu.CompilerParams(dimension_semantics=sem, vmem_limit_bytes=VMEM_LIMIT)


def _ada_kernel(cv_ref, w_ref, b_ref, o_ref):
    o_ref[...] = _dot(_silu(cv_ref[...]), w_ref[...], HI) + b_ref[...]


def _ada_call(cvecs, w_ada, b_ada):
    tn = 512
    return pl.pallas_call(
        _ada_kernel,
        grid=(DEPTH, 3 * D_MODEL // tn),
        in_specs=[
            pl.BlockSpec((8, D_MODEL), lambda l, j: (0, 0)),
            pl.BlockSpec((None, D_MODEL, tn), lambda l, j: (l, 0, j)),
            pl.BlockSpec((None, 1, tn), lambda l, j: (l, 0, j)),
        ],
        out_specs=pl.BlockSpec((None, 8, tn), lambda l, j: (l, 0, j)),
        out_shape=jax.ShapeDtypeStruct((DEPTH, 8, 3 * D_MODEL), F32),
        compiler_params=_params("arbitrary", "arbitrary"),
        name="ada",
    )(cvecs, w_ada, b_ada.reshape(DEPTH, 1, 3 * D_MODEL))


def _modulated_norm(x, g_ref, ada_ref, row):
    y = x * lax.rsqrt(jnp.mean(x * x, axis=-1, keepdims=True) + EPS) * g_ref[...]
    shift = ada_ref[pl.ds(row, 1), 0:D_MODEL]
    scale = ada_ref[pl.ds(row, 1), D_MODEL:2 * D_MODEL]
    return (y * (1.0 + scale) + shift).astype(BF16)


def _inproj_kernel(*refs, tm, row_base, tokens_per_row, n_carry, ctx):
    x_ref, ada_ref, g_ref, w_ref, wkv_ref = refs[:5]
    outs = refs[5 + n_carry:]
    o_ref, h_s = outs[0], outs[-1]
    i = pl.program_id(0)
    j = pl.program_id(1)
    nb = tm // SEQ

    @pl.when(j == 0)
    def _():
        h_s[...] = _modulated_norm(x_ref[...], g_ref, ada_ref, row_base + (i * tm) // tokens_per_row)
        if ctx:
            kc_ref, vc_ref = outs[3:5]
            for b in range(nb):
                kv_t = _dot_nt(wkv_ref[...], h_s[b * SEQ:(b + 1) * SEQ, :])
                kc_ref[b] = kv_t[0:W_C].reshape(H_C, DH_C, SEQ)
                vc_ref[b] = kv_t[W_C:2 * W_C].reshape(H_C, DH_C, SEQ)
        else:
            outs[1][...] = _dot_nt(h_s[...], wkv_ref[...])

    acc = _dot_nt(h_s[...], w_ref[...])
    o_ref[...] = acc
    if ctx:
        @pl.when(j == COL_KA // PROJ_TN)
        def _():
            for dst, c0 in ((outs[1], COL_KA % PROJ_TN), (outs[2], COL_VA % PROJ_TN)):
                for b in range(nb):
                    for h in range(H_A):
                        dst[b, :, h, :] = acc[b * SEQ:(b + 1) * SEQ, c0 + h * LANES:c0 + (h + 1) * LANES]


def _inproj_call(x, ada, norm_g, w_main, w_kvc, *, tm, row_base, tokens_per_row, ctx, li=0, carry=None):
    t = x.shape[0]
    n_carry = 0 if carry is None else len(carry)
    kern = functools.partial(_inproj_kernel, tm=tm, row_base=row_base, tokens_per_row=tokens_per_row,
                             n_carry=n_carry, ctx=ctx)
    in_specs = [
        pl.BlockSpec((tm, D_MODEL), lambda i, j: (i, 0)),
        pl.BlockSpec((8, 3 * D_MODEL), lambda i, j: (0, 0)),
        pl.BlockSpec((1, D_MODEL), lambda i, j: (0, 0)),
        pl.BlockSpec((None, PROJ_TN, D_MODEL), lambda i, j: (li, j, 0)),
        pl.BlockSpec((None, KVC_COLS, D_MODEL), lambda i, j: (li, 0, 0)),
    ]
    args = [x, ada, norm_g.reshape(1, D_MODEL), w_main, w_kvc]
    out_specs = [pl.BlockSpec((tm, PROJ_TN), lambda i, j: (i, j))]
    out_shape = [jax.ShapeDtypeStruct((t, PROJ_COLS), F32)]
    aliases = {}
    if ctx:
        nb = tm // SEQ
        out_specs += [pl.BlockSpec((nb, None, SEQ, H_A, 2 * DH_A), lambda i, j: (i, li, 0, 0, 0))] * 2
        out_specs += [pl.BlockSpec((nb, None, H_C, DH_C, SEQ), lambda i, j: (i, li, 0, 0, 0))] * 2
        out_shape += [jax.ShapeDtypeStruct((BATCH, DEPTH, SEQ, H_A, 2 * DH_A), F32)] * 2
        out_shape += [jax.ShapeDtypeStruct((BATCH, DEPTH, H_C, DH_C, SEQ), F32)] * 2
        if carry is not None:
            in_specs += [pl.BlockSpec(memory_space=pl.ANY)] * n_carry
            args += list(carry)
            aliases = {5 + k: 1 + k for k in range(n_carry)}
    else:
        out_specs.append(pl.BlockSpec((tm, KVC_COLS), lambda i, j: (i, 0)))
        out_shape.append(jax.ShapeDtypeStruct((t, KVC_COLS), F32))
    return pl.pallas_call(
        kern,
        grid=(t // tm, PROJ_COLS // PROJ_TN),
        in_specs=in_specs,
        out_specs=out_specs,
        out_shape=out_shape,
        input_output_aliases=aliases,
        scratch_shapes=[pltpu.VMEM((tm, D_MODEL), BF16)],
        compiler_params=_params("arbitrary", "arbitrary"),
        name="inproj_ctx" if ctx else "inproj_lat",
    )(*args)


def _diff_lambda_in_kernel(lam_ref, lam_init):
    v = lam_ref[...]
    l1 = jnp.sum(v[0:1] * v[1:2], axis=-1, keepdims=True)
    l2 = jnp.sum(v[2:3] * v[3:4], axis=-1, keepdims=True)
    return jnp.exp(l1) - jnp.exp(l2) + lam_init


def _split_halves(x, scale):
    lo = _lane(x.shape) < HALF
    xs = x * scale
    return jnp.concatenate([jnp.where(lo, xs, 0.0), jnp.where(lo, 0.0, xs)], axis=0).astype(BF16)


def _diff_head_post(o, subln_g, lam_init, gate):
    o = o * lax.rsqrt(jnp.mean(o * o, axis=-1, keepdims=True) + EPS) * subln_g
    return (o * (1.0 - lam_init) * _silu(gate)).astype(BF16)


def _attn_a_ctx_kernel(q_ref, k_ref, v_ref, g_ref, lam_ref, sg_ref, o_ref, *, lam_init):
    t = q_ref.shape[0]
    lam = _diff_lambda_in_kernel(lam_ref, lam_init)
    for h in range(H_A):
        sl = slice(h * LANES, (h + 1) * LANES)
        qq = _split_halves(q_ref[:, sl], DH_A ** -0.5)
        s = _dot_nt(qq, k_ref[:, sl].astype(BF16))
        e = jnp.exp(s - jnp.max(s, axis=-1, keepdims=True))
        p = e * (1.0 / jnp.sum(e, axis=-1, keepdims=True))
        pd = (p[:t] - lam * p[t:]).astype(BF16)
        o = _dot(pd, v_ref[:, sl].astype(BF16))
        o_ref[:, sl] = _diff_head_post(o, sg_ref[...], lam_init, g_ref[:, sl])


def _attn_a_ctx_call(proj, lamvec, subln_g, lam_init):
    t = SEQ
    blk = lambda c: pl.BlockSpec((t, W_A), lambda b: (b, c // W_A))
    return pl.pallas_call(
        functools.partial(_attn_a_ctx_kernel, lam_init=lam_init),
        grid=(BATCH,),
        in_specs=[blk(COL_QA), blk(COL_KA), blk(COL_VA), blk(COL_GA),
                  pl.BlockSpec((4, LANES), lambda b: (0, 0)),
                  pl.BlockSpec((1, LANES), lambda b: (0, 0))],
        out_specs=pl.BlockSpec((t, W_A), lambda b: (b, 0)),
        out_shape=jax.ShapeDtypeStruct((BATCH * t, W_A), BF16),
        compiler_params=_params("arbitrary"),
        name="attn_a_ctx",
    )(proj, proj, proj, proj, lamvec, subln_g)


def _rope(x, cos, sin_signed):
    first = (_lane(x.shape) % 32) < 16
    swapped = jnp.where(first, pltpu.roll(x, LANES - 16, 1), pltpu.roll(x, 16, 1))
    return x * cos + swapped * sin_signed


def _attn_a_lat_kernel(q_ref, k_ref, v_ref, g_ref, ck_ref, cv_ref, cosq_ref, sinq_ref, cosk_ref, sink_ref,
                       lam_ref, sg_ref, o_ref, kr_s, *, lam_init):
    tq = q_ref.shape[0]

    @pl.when(pl.program_id(1) == 0)
    def _():
        for h in range(H_A):
            sl = slice(h * LANES, (h + 1) * LANES)
            kr_s[:, sl] = _rope(k_ref[:, sl], cosk_ref[...], sink_ref[...]).astype(BF16)

    lam = _diff_lambda_in_kernel(lam_ref, lam_init)
    for h in range(H_A):
        sl = slice(h * LANES, (h + 1) * LANES)
        qq = _split_halves(_rope(q_ref[:, sl], cosq_ref[...], sinq_ref[...]), DH_A ** -0.5)
        s_lat = _dot_nt(qq, kr_s[:, sl])
        s_ctx = _dot_nt(qq, ck_ref[:, sl].astype(BF16))
        m = jnp.maximum(jnp.max(s_lat, axis=-1, keepdims=True), jnp.max(s_ctx, axis=-1, keepdims=True))
        e_lat = jnp.exp(s_lat - m)
        e_ctx = jnp.exp(s_ctx - m)
        r = 1.0 / (jnp.sum(e_lat, axis=-1, keepdims=True) + jnp.sum(e_ctx, axis=-1, keepdims=True))
        p_lat = e_lat * r
        p_ctx = e_ctx * r
        pd_lat = (p_lat[:tq] - lam * p_lat[tq:]).astype(BF16)
        pd_ctx = (p_ctx[:tq] - lam * p_ctx[tq:]).astype(BF16)
        o = _dot(pd_lat, v_ref[:, sl].astype(BF16)) + _dot(pd_ctx, cv_ref[:, sl].astype(BF16))
        o_ref[:, sl] = _diff_head_post(o, sg_ref[...], lam_init, g_ref[:, sl])


def _attn_a_lat_call(proj, cache_k, cache_v, li, cos_t, sin_t, lamvec, subln_g, lam_init):
    tq = 256
    nq = DEC_SEQ // tq
    qblk = lambda c: pl.BlockSpec((tq, W_A), lambda b, i: (b * nq + i, c // W_A))
    full = lambda c: pl.BlockSpec((DEC_SEQ, W_A), lambda b, i: (b, c // W_A))
    cache = pl.BlockSpec((None, None, PAST_LEN, W_A), lambda b, i: (b, li, 0, 0))
    return pl.pallas_call(
        functools.partial(_attn_a_lat_kernel, lam_init=lam_init),
        grid=(DEC_BATCH, nq),
        in_specs=[qblk(COL_QA), full(COL_KA), full(COL_VA), qblk(COL_GA), cache, cache,
                  pl.BlockSpec((tq, LANES), lambda b, i: (i, 0)),
                  pl.BlockSpec((tq, LANES), lambda b, i: (i, 0)),
                  pl.BlockSpec((DEC_SEQ, LANES), lambda b, i: (0, 0)),
                  pl.BlockSpec((DEC_SEQ, LANES), lambda b, i: (0, 0)),
                  pl.BlockSpec((4, LANES), lambda b, i: (0, 0)),
                  pl.BlockSpec((1, LANES), lambda b, i: (0, 0))],
        out_specs=pl.BlockSpec((tq, W_A), lambda b, i: (b * nq + i, 0)),
        out_shape=jax.ShapeDtypeStruct((DEC_BATCH * DEC_SEQ, W_A), BF16),
        scratch_shapes=[pltpu.VMEM((DEC_SEQ, W_A), BF16)],
        compiler_params=_params("arbitrary", "arbitrary"),
        name="attn_a_lat",
    )(proj, proj, proj, proj, cache_k, cache_v, cos_t, sin_t, cos_t, sin_t, lamvec, subln_g)


def _merge_halves(o, t):
    return jnp.where(_lane((t, LANES)) < HALF, o[:t], o[t:])


def _attn_c_ctx_kernel(q_ref, kt_ref, vt_ref, g_ref, o_ref):
    t = q_ref.shape[0]
    for j in range(H_C // 2):
        sl = slice(j * LANES, (j + 1) * LANES)
        qq = _split_halves(q_ref[:, sl], DH_C ** -0.5)
        kt = kt_ref[2 * j:2 * j + 2].reshape(LANES, t).astype(BF16)
        vt = vt_ref[2 * j:2 * j + 2].reshape(LANES, t).astype(BF16)
        s = _dot(qq, kt)
        e = jnp.exp(s - jnp.max(s, axis=-1, keepdims=True))
        p = (e * (1.0 / jnp.sum(e, axis=-1, keepdims=True))).astype(BF16)
        o = _merge_halves(_dot_nt(p, vt), t)
        o_ref[:, sl] = (o * _silu(g_ref[:, sl])).astype(BF16)


def _attn_c_ctx_call(proj, kc_t, vc_t, li):
    t = SEQ
    blk = lambda c: pl.BlockSpec((t, W_C), lambda b: (b, c // W_C))
    cache = pl.BlockSpec((None, None, H_C, DH_C, t), lambda b: (b, li, 0, 0, 0))
    return pl.pallas_call(
        _attn_c_ctx_kernel,
        grid=(BATCH,),
        in_specs=[blk(COL_QC), cache, cache, blk(COL_GC)],
        out_specs=pl.BlockSpec((t, W_C), lambda b: (b, 0)),
        out_shape=jax.ShapeDtypeStruct((BATCH * t, W_C), BF16),
        compiler_params=_params("arbitrary"),
        name="attn_c_ctx",
    )(proj, kc_t, vc_t, proj)


def _rpb_kernel(rpb_ref, o_ref):
    h = pl.program_id(0)
    shape = (GRID_W, LANES)
    c = lax.broadcasted_iota(jnp.int32, shape, 0)
    cp = _lane(shape) % GRID_W
    d = jnp.clip(cp - c, -(NA_KW - 1), NA_KW - 1) + (NA_KW - 1)
    start = jnp.clip(c - NA_KW // 2, 0, GRID_W - NA_KW)
    in_win = (cp >= start) & (cp < start + NA_KW)
    o_ref[0] = jnp.full(shape, NEG_INF, F32)
    n_dc = 2 * NA_KW - 1
    for dr in range(2 * NA_KH - 1):
        acc = jnp.full(shape, NEG_INF, F32)
        for dc in range(n_dc):
            acc = jnp.where(d == dc, rpb_ref[(h * (2 * NA_KH - 1) + dr) * n_dc + dc], acc)
        o_ref[1 + dr] = jnp.where(in_win, acc, NEG_INF)


def _rpb_call(rpb):
    return pl.pallas_call(
        _rpb_kernel,
        grid=(H_C,),
        in_specs=[pl.BlockSpec(memory_space=pltpu.SMEM)],
        out_specs=pl.BlockSpec((None, NA_TILES, GRID_W, LANES), lambda h: (h, 0, 0, 0)),
        out_shape=jax.ShapeDtypeStruct((H_C, NA_TILES, GRID_W, LANES), F32),
        compiler_params=_params("arbitrary"),
        name="rpb_tiles",
    )(rpb.reshape(-1))


def _attn_c_lat_kernel(q_ref, k_ref, v_ref, g_ref, ck_ref, cv_ref, tile_ref, o_ref, bias_s):
    tq = q_ref.shape[0]
    nwin = NA_WROWS * GRID_W
    m = pl.program_id(1)
    w0 = jnp.where(m < (GRID_ROWS // NA_QROWS) // 2, 0, GRID_ROWS - NA_WROWS)
    k0 = pl.multiple_of(w0 * GRID_W, GRID_W)
    lo = _lane((GRID_W, LANES)) < HALF
    for j in range(H_C // 2):
        sl = slice(j * LANES, (j + 1) * LANES)
        for s in range(2):
            for i in range(NA_QROWS):
                r = m * NA_QROWS + i
                start = jnp.clip(r - NA_KH // 2, 0, GRID_ROWS - NA_KH)
                for jp in range(NA_WROWS // 2):
                    idx = []
                    for u in range(2):
                        rk = w0 + 2 * jp + u
                        valid = (rk >= start) & (rk < start + NA_KH)
                        idx.append(jnp.where(valid, rk - r + NA_KH, 0))
                    tile = jnp.where(lo, tile_ref[2 * j + s, idx[0]], tile_ref[2 * j + s, idx[1]])
                    bias_s[(s * NA_QROWS + i) * GRID_W:(s * NA_QROWS + i + 1) * GRID_W,
                           jp * LANES:(jp + 1) * LANES] = tile
        qq = _split_halves(q_ref[:, sl], DH_C ** -0.5)
        kw = k_ref[pl.ds(k0, nwin), sl].astype(BF16)
        vw = v_ref[pl.ds(k0, nwin), sl].astype(BF16)
        s_win = _dot_nt(qq, kw) + bias_s[...]
        s_ctx = _dot_nt(qq, ck_ref[:, sl].astype(BF16))
        mx = jnp.maximum(jnp.max(s_win, axis=-1, keepdims=True), jnp.max(s_ctx, axis=-1, keepdims=True))
        e_win = jnp.exp(s_win - mx)
        e_ctx = jnp.exp(s_ctx - mx)
        rs = 1.0 / (jnp.sum(e_win, axis=-1, keepdims=True) + jnp.sum(e_ctx, axis=-1, keepdims=True))
        o = _dot((e_win * rs).astype(BF16), vw) + _dot((e_ctx * rs).astype(BF16), cv_ref[:, sl].astype(BF16))
        o_ref[:, sl] = (_merge_halves(o, tq) * _silu(g_ref[:, sl])).astype(BF16)


def _attn_c_lat_call(proj, kv, cache_k, cache_v, li, tiles):
    tq = NA_QROWS * GRID_W
    nq = DEC_SEQ // tq
    qblk = lambda c: pl.BlockSpec((tq, W_C), lambda b, i: (b * nq + i, c // W_C))
    full = lambda c: pl.BlockSpec((DEC_SEQ, W_C), lambda b, i: (b, c // W_C))
    cache = pl.BlockSpec((None, None, PAST_LEN, W_C), lambda b, i: (b, li, 0, 0))
    return pl.pallas_call(
        _attn_c_lat_kernel,
        grid=(DEC_BATCH, nq),
        in_specs=[qblk(COL_QC), full(0), full(W_C), qblk(COL_GC), cache, cache,
                  pl.BlockSpec((H_C, NA_TILES, GRID_W, LANES), lambda b, i: (0, 0, 0, 0))],
        out_specs=pl.BlockSpec((tq, W_C), lambda b, i: (b * nq + i, 0)),
        out_shape=jax.ShapeDtypeStruct((DEC_BATCH * DEC_SEQ, W_C), BF16),
        scratch_shapes=[pltpu.VMEM((2 * tq, NA_WROWS * GRID_W), F32)],
        compiler_params=_params("arbitrary", "arbitrary"),
        name="attn_c_lat",
    )(proj, kv, kv, proj, cache_k, cache_v, tiles)


def _ssd_kernel(*refs, seq, use_h0, want_state, has_carry):
    refs = list(refs)
    dt_ref, xs_ref, bc_ref, z_ref = refs[:4]
    pos = 4
    h0_ref = None
    if use_h0:
        h0_ref = refs[pos]
        pos += 1
    cw_ref, cb_ref, dtb_ref, alog_ref, dsk_ref, g_ref = refs[pos:pos + 6]
    pos += 6 + int(has_carry)
    y_ref = refs[pos]
    pos += 1
    hs_ref = None
    if want_state:
        hs_ref = refs[pos]
        pos += 1
    upad_s, xc_s, expo_s, expot_s, dtt_s, tot_s, bmt_s, yf_s, yb_s, st_s = refs[pos:]

    q = SSD_CHUNK
    nc = seq // q
    n_pair = H_B // 2
    n_hd = 2 * H_B
    pad = 8

    upad_s[0:pad, :] = jnp.zeros((pad, CONV_DIM), F32)
    upad_s[pad + seq:2 * pad + seq, :] = jnp.zeros((pad, CONV_DIM), F32)
    upad_s[pad:pad + seq, 0:DI_B] = xs_ref[...]
    upad_s[pad:pad + seq, DI_B:CONV_DIM] = bc_ref[...]

    for c in range(nc):
        for cb_ in range(CONV_DIM // LANES):
            csl = slice(cb_ * LANES, (cb_ + 1) * LANES)
            acc = jnp.zeros((q, LANES), F32) + cb_ref[:, csl]
            for k in range(CONV_K):
                r0 = c * q + pad - CONV_K // 2 + k
                acc = acc + upad_s[r0:r0 + q, csl] * cw_ref[k:k + 1, csl]
            xc_s[c * q:(c + 1) * q, csl] = _silu(acc)

    a_row = -jnp.exp(alog_ref[...])
    a_col = jnp.broadcast_to(a_row, (LANES, LANES)).T[0:n_hd, 0:1]
    ri = lax.broadcasted_iota(jnp.int32, (q, q), 0)
    ci = lax.broadcasted_iota(jnp.int32, (q, q), 1)
    ltri = (ri >= ci).astype(F32)
    fwd_lane = _lane((q, LANES)) < H_B
    fwd_row = lax.broadcasted_iota(jnp.int32, (n_hd, q), 0) < H_B

    def prep_body(c, carry):
        rows = pl.ds(pl.multiple_of(c * q, q), q)
        xdt = dt_ref[rows, 0:LANES] + dtb_ref[...]
        dtv = jnp.maximum(xdt, 0.0) + jnp.log1p(jnp.exp(-jnp.abs(xdt)))
        la = dtv * a_row
        acum = _dot(ltri, la, HI)
        expo_s[rows, :] = jnp.where(fwd_lane, acum, la - acum)
        acum_t = acum.T[0:n_hd, :]
        dt_t = dtv.T[0:n_hd, :]
        expot_s[c] = jnp.where(fwd_row, acum_t, dt_t * a_col - acum_t)
        dtt_s[c] = dt_t
        tot_s[c] = jnp.broadcast_to(acum_t[:, q - 1:q], (n_hd, q))
        bmt_s[c] = xc_s[rows, DI_B:DI_B + LANES].T
        return carry

    lax.fori_loop(0, nc, prep_body, 0, unroll=2)

    if use_h0:
        st_s[...] = h0_ref[...].reshape(2, n_pair, N_B, LANES)
    else:
        st_s[...] = jnp.zeros_like(st_s)

    lane_q = _lane((q, LANES))
    lo = lane_q < HALF
    lo_st = _lane((N_B, LANES)) < HALF

    def chunk(dirn, c, y_s):
        rows = pl.ds(pl.multiple_of(c * q, q), q)
        tri = (ri >= ci) if dirn == 0 else (ci >= ri)
        bm16 = xc_s[rows, DI_B:DI_B + LANES].astype(BF16)
        cm = xc_s[rows, DI_B + LANES:DI_B + 2 * LANES]
        for g in range(G_B):
            in_g = (lane_q >= g * N_B) & (lane_q < (g + 1) * N_B)
            cmg = jnp.where(in_g, cm, 0.0).astype(BF16)
            cb = _dot_nt(cmg, bm16)
            bmt_g = bmt_s[c, g * N_B:(g + 1) * N_B, :]
            for k in range(g * n_pair // G_B, (g + 1) * n_pair // G_B):
                psl = slice(k * LANES, (k + 1) * LANES)
                x16 = xc_s[rows, psl].astype(BF16)
                mats, lhs, ysc, cdec = [], [], [], []
                for s in range(2):
                    col = dirn * H_B + 2 * k + s
                    e_col = jnp.broadcast_to(expo_s[rows, col:col + 1], (q, q))
                    e_row = expot_s[c, col:col + 1, :]
                    dt_row = dtt_s[c, col:col + 1, :]
                    tot = tot_s[c, col:col + 1, :]
                    dec = jnp.exp(jnp.where(tri, e_col - e_row, NEG_INF))
                    mats.append((cb * dec * dt_row).astype(BF16))
                    if dirn == 0:
                        ysc.append(jnp.exp(e_col))
                        w_row = jnp.exp(tot - e_row)
                    else:
                        ysc.append(jnp.exp(e_col + tot))
                        w_row = jnp.exp(-e_row)
                    lhs.append((bmt_g * (w_row * dt_row)).astype(BF16))
                    cdec.append(jnp.exp(tot[:, 0:LANES]))
                yd = _dot(jnp.concatenate(mats, axis=0), x16)
                st = st_s[dirn, k]
                y_off = _dot(cmg, jnp.concatenate([st, st], axis=0).astype(BF16))
                y_s[rows, psl] = jnp.where(lo, yd[:q] + ysc[0] * y_off, yd[q:] + ysc[1] * y_off)
                ds = _dot(jnp.concatenate(lhs, axis=0), x16)
                st_s[dirn, k] = jnp.where(lo_st, cdec[0] * st + ds[:N_B], cdec[1] * st + ds[N_B:])

    def body(c, carry):
        chunk(0, c, yf_s)
        chunk(1, nc - 1 - c, yb_s)
        return carry

    lax.fori_loop(0, nc, body, 0, unroll=2)

    dsum = dsk_ref[0:1, :] + dsk_ref[1:2, :]

    def out_body(c, carry):
        rows = pl.ds(pl.multiple_of(c * q, q), q)
        y = yf_s[rows, :] + yb_s[rows, :] + xc_s[rows, 0:DI_B] * dsum
        y = y * _silu(z_ref[rows, :])
        y = y * lax.rsqrt(jnp.mean(y * y, axis=-1, keepdims=True) + EPS) * g_ref[...]
        y_ref[rows, :] = y.astype(BF16)
        return carry

    lax.fori_loop(0, nc, out_body, 0)
    if want_state:
        for dirn in range(2):
            for k in range(n_pair):
                st = st_s[dirn, k]
                st_t = jnp.concatenate([st, st], axis=0).T
                for s in range(2):
                    hs_ref[dirn, 2 * k + s] = st_t[s * P_B:(s + 1) * P_B, 0:N_B]


def _ssd_call(proj, h0t, conv_w, conv_b, dtb, alog, dskx, norm_g, *, nb, seq, want_state, li=0, carry=None):
    use_h0 = h0t is not None
    n_pair = H_B // 2
    in_specs = [pl.BlockSpec((seq, DT_PAD), lambda b: (b, COL_DT // DT_PAD)),
                pl.BlockSpec((seq, DI_B), lambda b: (b, COL_XS // DI_B)),
                pl.BlockSpec((seq, BC_DIM), lambda b: (b, COL_BC // BC_DIM)),
                pl.BlockSpec((seq, DI_B), lambda b: (b, COL_Z // DI_B))]
    args = [proj, proj, proj, proj]
    if use_h0:
        in_specs.append(pl.BlockSpec((None, 2, n_pair * N_B, LANES), lambda b: (b, 0, 0, 0)))
        args.append(h0t)
    const = lambda shape: pl.BlockSpec(shape, lambda b: (0,) * len(shape))
    in_specs += [const((CONV_K, CONV_DIM)), const((1, CONV_DIM)), const((1, LANES)), const((1, LANES)),
                 const((2, DI_B)), const((1, DI_B))]
    args += [conv_w, conv_b, dtb, alog, dskx, norm_g]
    out_specs = [pl.BlockSpec((seq, DI_B), lambda b: (b, 0))]
    out_shape = [jax.ShapeDtypeStruct((nb * seq, DI_B), BF16)]
    aliases = {}
    if want_state:
        out_specs.append(pl.BlockSpec((None, None, 2, H_B, P_B, N_B), lambda b: (b, li, 0, 0, 0, 0)))
        out_shape.append(jax.ShapeDtypeStruct((nb, DEPTH, 2, H_B, P_B, N_B), F32))
        if carry is not None:
            aliases = {len(args): 1}
            in_specs.append(pl.BlockSpec(memory_space=pl.ANY))
            args.append(carry)
    nc = seq // SSD_CHUNK
    per_chunk_rows = pltpu.VMEM((nc, 2 * H_B, SSD_CHUNK), F32)
    scratch = [pltpu.VMEM((seq + 16, CONV_DIM), F32), pltpu.VMEM((seq, CONV_DIM), F32),
               pltpu.VMEM((seq, LANES), F32), per_chunk_rows, per_chunk_rows, per_chunk_rows,
               pltpu.VMEM((nc, LANES, SSD_CHUNK), F32),
               pltpu.VMEM((seq, DI_B), F32), pltpu.VMEM((seq, DI_B), F32),
               pltpu.VMEM((2, n_pair, N_B, LANES), F32)]
    return pl.pallas_call(
        functools.partial(_ssd_kernel, seq=seq, use_h0=use_h0, want_state=want_state,
                          has_carry=carry is not None),
        grid=(nb,),
        in_specs=in_specs,
        out_specs=out_specs,
        out_shape=out_shape,
        input_output_aliases=aliases,
        scratch_shapes=scratch,
        compiler_params=_params("arbitrary"),
        name="ssd_ctx" if want_state else "ssd_lat",
    )(*args)


def _post_kernel(x_ref, ya_ref, yb_ref, yc_ref, ada_ref, g_ref, wm_ref, wa_ref, wb_ref, wc_ref, wo_ref, fg_ref,
                 o_ref, *, tm, row_base, tokens_per_row, final):
    row = row_base + (pl.program_id(0) * tm) // tokens_per_row
    gate = ada_ref[pl.ds(row, 1), 2 * D_MODEL:3 * D_MODEL]
    d = D_MODEL
    x = x_ref[...]
    h = _modulated_norm(x, g_ref, ada_ref, row)
    merged = None
    for n, (y_ref, w_ref) in enumerate(((ya_ref, wa_ref), (yb_ref, wb_ref), (yc_ref, wc_ref))):
        logits = _dot_nt(h, wm_ref[n * d:(n + 1) * d, :])
        term = _sigmoid(logits) * _dot(y_ref[...], w_ref[...])
        merged = term if merged is None else merged + term
    x = x + gate * _dot(merged.astype(BF16), wo_ref[...])
    if final:
        x = x * lax.rsqrt(jnp.mean(x * x, axis=-1, keepdims=True) + EPS) * fg_ref[...]
    o_ref[...] = x


def _post_call(x, ya, yb, yc, ada, norm_g, w_merge_t, li, wa, wb, wc, wo, final_g, *, tm, row_base,
               tokens_per_row, final):
    t = x.shape[0]
    tok = lambda w: pl.BlockSpec((tm, w), lambda i: (i, 0))
    const = lambda shape: pl.BlockSpec(shape, lambda i: (0,) * len(shape))
    kern = functools.partial(_post_kernel, tm=tm, row_base=row_base, tokens_per_row=tokens_per_row, final=final)
    return pl.pallas_call(
        kern,
        grid=(t // tm,),
        in_specs=[tok(D_MODEL), tok(W_A), tok(DI_B), tok(W_C),
                  const((8, 3 * D_MODEL)), const((1, D_MODEL)),
                  pl.BlockSpec((None, MERGE_COLS, D_MODEL), lambda i: (li, 0, 0)),
                  const((W_A, D_MODEL)), const((DI_B, D_MODEL)), const((W_C, D_MODEL)),
                  const((D_MODEL, D_MODEL)), const((1, D_MODEL))],
        out_specs=tok(D_MODEL),
        out_shape=jax.ShapeDtypeStruct((t, D_MODEL), F32),
        compiler_params=_params("arbitrary"),
        name="post_final" if final else "post",
    )(x, ya, yb, yc, ada, norm_g.reshape(1, D_MODEL), w_merge_t, wa, wb, wc, wo, final_g)


def _rope_tables():
    pos = np.arange(DEC_SEQ)
    lane = np.arange(LANES)
    l64 = lane % (2 * (DH_A // 2))
    quarter = DH_A // 4
    p = np.where((l64 < DH_A // 2)[None, :], (pos // GRID_W)[:, None], (pos % GRID_W)[:, None])
    inv = ROPE_BASE ** (-np.arange(quarter, dtype=np.float64) / quarter)
    ang = p.astype(np.float64) * inv[l64 % quarter][None, :]
    sign = np.where((lane % (2 * quarter)) < quarter, -1.0, 1.0)
    return jnp.asarray(np.cos(ang), F32), jnp.asarray(np.sin(ang) * sign[None, :], F32)


def _pad_lanes(v, width=LANES):
    v = v.reshape(1, -1).astype(F32)
    return jnp.pad(v, ((0, 0), (0, width - v.shape[1])))


def kernel(x_prompt, x_sample, cache_diff_k, cache_diff_v, cache_na_k, cache_na_v, state_ssd, c, c_ctx,
           norm_g, w_ada, b_ada, w_in, lam_q1, lam_k1, lam_q2, lam_k2, diff_subln_g, conv_w, conv_b,
           dt_bias, a_log, d_skip, ssd_norm_g, na_rpb, w_br_a, w_br_b, w_br_c, w_out, final_g):
    assert x_prompt.shape == (BATCH, SEQ, D_MODEL) and x_sample.shape == (DEC_BATCH, DEC_SEQ, D_MODEL)
    assert w_in.shape == (DEPTH, D_MODEL, _SRC["merge"] + MERGE_COLS)
    w_t = jnp.swapaxes(w_in, 1, 2)
    rows = lambda s0, n: w_t[:, s0:s0 + n, :].astype(BF16)
    w_main = jnp.concatenate([jnp.zeros((DEPTH, n, D_MODEL), BF16) if s0 is None else rows(s0, n)
                              for s0, n in MAIN_SEGMENTS], axis=1)
    assert w_main.shape == (DEPTH, PROJ_COLS, D_MODEL)
    w_kvc = rows(_SRC["kc"], KVC_COLS)
    w_merge = rows(_SRC["merge"], MERGE_COLS)
    wa16, wb16, wc16, wo16 = (w.astype(BF16) for w in (w_br_a, w_br_b, w_br_c, w_out))

    cvecs = jnp.concatenate([c_ctx[None, :], c, jnp.zeros((8 - 1 - DEC_BATCH, D_MODEL), F32)], axis=0)
    ada = _ada_call(cvecs, w_ada, b_ada)
    cos_t, sin_t = _rope_tables()

    ck_a = cache_diff_k.reshape(DEC_BATCH, DEPTH, PAST_LEN, W_A)
    cv_a = cache_diff_v.reshape(DEC_BATCH, DEPTH, PAST_LEN, W_A)
    ck_c = cache_na_k.reshape(DEC_BATCH, DEPTH, PAST_LEN, W_C)
    cv_c = cache_na_v.reshape(DEC_BATCH, DEPTH, PAST_LEN, W_C)
    h0t = state_ssd.transpose(0, 1, 2, 5, 3, 4).reshape(DEC_BATCH, DEPTH, 2, N_B, DI_B)
    h0t = h0t.reshape(DEC_BATCH, DEPTH, 2, N_B, H_B // 2, LANES).transpose(0, 1, 2, 4, 3, 5)
    h0t = h0t.reshape(DEC_BATCH, DEPTH, 2, (H_B // 2) * N_B, LANES)

    xp = x_prompt.reshape(BATCH * SEQ, D_MODEL)
    xs = x_sample.reshape(DEC_BATCH * DEC_SEQ, D_MODEL)
    fg = final_g.reshape(1, D_MODEL)
    caches = None
    new_ssd = None
    for li in range(DEPTH):
        lam_init = 0.8 - 0.6 * math.exp(-0.3 * li)
        final = li == DEPTH - 1
        lamvec = jnp.concatenate([_pad_lanes(v[li]) for v in (lam_q1, lam_k1, lam_q2, lam_k2)], axis=0)
        subln = diff_subln_g[li].reshape(1, LANES)
        dtb = _pad_lanes(dt_bias[li])
        alog = _pad_lanes(a_log[li])
        dskx = jnp.repeat(d_skip[li], P_B, axis=-1)
        ssd_w = (conv_w[li], conv_b[li].reshape(1, CONV_DIM), dtb, alog, dskx, ssd_norm_g[li].reshape(1, DI_B))
        post_w = (wa16[li], wb16[li], wc16[li], wo16[li], fg)

        proj, *caches = _inproj_call(xp, ada[li], norm_g[li], w_main, w_kvc, tm=1024, row_base=0,
                                     tokens_per_row=BATCH * SEQ, ctx=True, li=li, carry=caches)
        ya = _attn_a_ctx_call(proj, lamvec, subln, lam_init)
        yb, new_ssd = _ssd_call(proj, None, *ssd_w, nb=BATCH, seq=SEQ, want_state=True, li=li, carry=new_ssd)
        yc = _attn_c_ctx_call(proj, caches[2], caches[3], li)
        xp = _post_call(xp, ya, yb, yc, ada[li], norm_g[li], w_merge, li, *post_w, tm=512, row_base=0,
                        tokens_per_row=BATCH * SEQ, final=final)

        proj, kv = _inproj_call(xs, ada[li], norm_g[li], w_main, w_kvc, tm=1024, row_base=1,
                                tokens_per_row=DEC_SEQ, ctx=False, li=li)
        ya = _attn_a_lat_call(proj, ck_a, cv_a, li, cos_t, sin_t, lamvec, subln, lam_init)
        (yb,) = _ssd_call(proj, h0t[:, li], *ssd_w, nb=DEC_BATCH, seq=DEC_SEQ, want_state=False)
        yc = _attn_c_lat_call(proj, kv, ck_c, cv_c, li, _rpb_call(na_rpb[li]))
        xs = _post_call(xs, ya, yb, yc, ada[li], norm_g[li], w_merge, li, *post_w, tm=512, row_base=1,
                        tokens_per_row=DEC_SEQ, final=final)

    new_k_a, new_v_a, new_k_c_t, new_v_c_t = caches
    to_token_major = lambda a: a.transpose(0, 1, 4, 2, 3)
    return (xp.reshape(BATCH, SEQ, D_MODEL), xs.reshape(DEC_BATCH, DEC_SEQ, D_MODEL),
            new_k_a, new_v_a, to_token_major(new_k_c_t), to_token_major(new_v_c_t), new_ssd)
```

```python
import functools
import math

import jax
import jax.numpy as jnp
import numpy as np
from jax import lax
from jax.experimental import pallas as pl
from jax.experimental.pallas import tpu as pltpu

D_MODEL = 1024
BATCH = 32
SEQ = 256
DEPTH = 2
DEC_BATCH = 2
DEC_SEQ = 1024
PAST_LEN = 512
GRID_W = 64
GRID_ROWS = DEC_SEQ // GRID_W
H_A = 4
DH_A = 64
W_A = H_A * 2 * DH_A
H_B = 8
P_B = 64
G_B = 2
N_B = 64
DI_B = H_B * P_B
CONV_K = 5
CONV_DIM = DI_B + 2 * G_B * N_B
SSD_CHUNK = 128
H_C = 8
DH_C = 64
W_C = H_C * DH_C
NA_KH = 8
NA_KW = 16
N_BRANCH = 3
ROPE_BASE = 10000.0
EPS = 1e-6

LANES = 128
HALF = LANES // 2
DT_PAD = 256
VMEM_LIMIT = 56 * 1024 * 1024

COL_QA = 0
COL_GA = 512
COL_XS = 1024
COL_Z = 1536
COL_QC = 2048
COL_DT = 2560
COL_BC = 2816
COL_KA = 3072
COL_VA = 3584
COL_GC = 4096
PROJ_COLS = 4608
PROJ_TN = 1536
BC_DIM = CONV_DIM - DI_B
KVC_COLS = 2 * W_C
MERGE_COLS = N_BRANCH * D_MODEL
_SRC = dict(qa=0, ka=512, va=1024, ga=1536, z=2048, xs=2560, bc=3072, dt=3328, qc=3344, kc=3856, vc=4368,
            gc=4880, merge=5392)
MAIN_SEGMENTS = ((_SRC["qa"], W_A), (_SRC["ga"], W_A), (_SRC["xs"], DI_B), (_SRC["z"], DI_B),
                 (_SRC["qc"], W_C), (_SRC["dt"], 2 * H_B), (None, DT_PAD - 2 * H_B), (_SRC["bc"], BC_DIM),
                 (_SRC["ka"], W_A), (_SRC["va"], W_A), (_SRC["gc"], W_C))

NA_QROWS = 4
NA_WROWS = 12
NA_TILES = 2 * NA_KH
NEG_INF = float("-inf")
LOG2E = math.log2(math.e)
ATTN_ROWS = 256
ATTN_AHEAD = 2
HI = lax.Precision.HIGHEST
F32 = jnp.float32
BF16 = jnp.bfloat16


def _dot(a, b, precision=None):
    return jnp.dot(a, b, preferred_element_type=F32, precision=precision)


def _dot_nt(a, b):
    return lax.dot_general(a, b, (((1,), (1,)), ((), ())), preferred_element_type=F32)


def _sigmoid(x):
    return 1.0 / (1.0 + jnp.exp(-x))


def _silu(x):
    return x * _sigmoid(x)


def _lane(shape):
    return lax.broadcasted_iota(jnp.int32, shape, len(shape) - 1)


def _params(*sem):
    return pltpu.CompilerParams(dimension_semantics=sem, vmem_limit_bytes=VMEM_LIMIT)


def _ada_kernel(cv_ref, w_ref, b_ref, o_ref):
    o_ref[...] = _dot(_silu(cv_ref[...]), w_ref[...], HI) + b_ref[...]


def _ada_call(cvecs, w_ada, b_ada):
    tn = 512
    return pl.pallas_call(
        _ada_kernel,
        grid=(DEPTH, 3 * D_MODEL // tn),
        in_specs=[
            pl.BlockSpec((8, D_MODEL), lambda l, j: (0, 0)),
            pl.BlockSpec((None, D_MODEL, tn), lambda l, j: (l, 0, j)),
            pl.BlockSpec((None, 1, tn), lambda l, j: (l, 0, j)),
        ],
        out_specs=pl.BlockSpec((None, 8, tn), lambda l, j: (l, 0, j)),
        out_shape=jax.ShapeDtypeStruct((DEPTH, 8, 3 * D_MODEL), F32),
        compiler_params=_params("arbitrary", "arbitrary"),
        name="ada",
    )(cvecs, w_ada, b_ada.reshape(DEPTH, 1, 3 * D_MODEL))


def _modulated_norm(x, g_ref, ada_ref, row):
    y = x * lax.rsqrt(jnp.mean(x * x, axis=-1, keepdims=True) + EPS) * g_ref[...]
    shift = ada_ref[pl.ds(row, 1), 0:D_MODEL]
    scale = ada_ref[pl.ds(row, 1), D_MODEL:2 * D_MODEL]
    return (y * (1.0 + scale) + shift).astype(BF16)


def _inproj_kernel(*refs, tm, row_base, tokens_per_row, n_carry, ctx):
    x_ref, ada_ref, g_ref, w_ref, wkv_ref = refs[:5]
    outs = refs[5 + n_carry:]
    o_ref, h_s = outs[0], outs[-1]
    i = pl.program_id(0)
    j = pl.program_id(1)
    nb = tm // SEQ

    @pl.when(j == 0)
    def _():
        h_s[...] = _modulated_norm(x_ref[...], g_ref, ada_ref, row_base + (i * tm) // tokens_per_row)
        if ctx:
            kc_ref, vc_ref = outs[3:5]
            for b in range(nb):
                kv_t = _dot_nt(wkv_ref[...], h_s[b * SEQ:(b + 1) * SEQ, :])
                kc_ref[b] = kv_t[0:W_C].reshape(H_C, DH_C, SEQ)
                vc_ref[b] = kv_t[W_C:2 * W_C].reshape(H_C, DH_C, SEQ)
        else:
            outs[1][...] = _dot_nt(h_s[...], wkv_ref[...])

    acc = _dot_nt(h_s[...], w_ref[...])
    o_ref[...] = acc
    if ctx:
        @pl.when(j == COL_KA // PROJ_TN)
        def _():
            for dst, c0 in ((outs[1], COL_KA % PROJ_TN), (outs[2], COL_VA % PROJ_TN)):
                for b in range(nb):
                    for h in range(H_A):
                        dst[b, :, h, :] = acc[b * SEQ:(b + 1) * SEQ, c0 + h * LANES:c0 + (h + 1) * LANES]


def _inproj_call(x, ada, norm_g, w_main, w_kvc, *, tm, row_base, tokens_per_row, ctx, li=0, carry=None):
    t = x.shape[0]
    n_carry = 0 if carry is None else len(carry)
    kern = functools.partial(_inproj_kernel, tm=tm, row_base=row_base, tokens_per_row=tokens_per_row,
                             n_carry=n_carry, ctx=ctx)
    in_specs = [
        pl.BlockSpec((tm, D_MODEL), lambda i, j: (i, 0)),
        pl.BlockSpec((8, 3 * D_MODEL), lambda i, j: (0, 0)),
        pl.BlockSpec((1, D_MODEL), lambda i, j: (0, 0)),
        pl.BlockSpec((None, PROJ_TN, D_MODEL), lambda i, j: (li, j, 0)),
        pl.BlockSpec((None, KVC_COLS, D_MODEL), lambda i, j: (li, 0, 0)),
    ]
    args = [x, ada, norm_g.reshape(1, D_MODEL), w_main, w_kvc]
    out_specs = [pl.BlockSpec((tm, PROJ_TN), lambda i, j: (i, j))]
    out_shape = [jax.ShapeDtypeStruct((t, PROJ_COLS), F32)]
    aliases = {}
    if ctx:
        nb = tm // SEQ
        out_specs += [pl.BlockSpec((nb, None, SEQ, H_A, 2 * DH_A), lambda i, j: (i, li, 0, 0, 0))] * 2
        out_specs += [pl.BlockSpec((nb, None, H_C, DH_C, SEQ), lambda i, j: (i, li, 0, 0, 0))] * 2
        out_shape += [jax.ShapeDtypeStruct((BATCH, DEPTH, SEQ, H_A, 2 * DH_A), F32)] * 2
        out_shape += [jax.ShapeDtypeStruct((BATCH, DEPTH, H_C, DH_C, SEQ), F32)] * 2
        if carry is not None:
            in_specs += [pl.BlockSpec(memory_space=pl.ANY)] * n_carry
            args += list(carry)
            aliases = {5 + k: 1 + k for k in range(n_carry)}
    else:
        out_specs.append(pl.BlockSpec((tm, KVC_COLS), lambda i, j: (i, 0)))
        out_shape.append(jax.ShapeDtypeStruct((t, KVC_COLS), F32))
    return pl.pallas_call(
        kern,
        grid=(t // tm, PROJ_COLS // PROJ_TN),
        in_specs=in_specs,
        out_specs=out_specs,
        out_shape=out_shape,
        input_output_aliases=aliases,
        scratch_shapes=[pltpu.VMEM((tm, D_MODEL), BF16)],
        compiler_params=_params("arbitrary", "arbitrary"),
        name="inproj_ctx" if ctx else "inproj_lat",
    )(*args)


def _diff_lambda_in_kernel(lam_ref, lam_init):
    v = lam_ref[...]
    l1 = jnp.sum(v[0:1] * v[1:2], axis=-1, keepdims=True)
    l2 = jnp.sum(v[2:3] * v[3:4], axis=-1, keepdims=True)
    return jnp.exp(l1) - jnp.exp(l2) + lam_init


def _split_halves(x, scale):
    lo = _lane(x.shape) < HALF
    xs = x * (scale * LOG2E)
    return jnp.concatenate([jnp.where(lo, xs, 0.0), jnp.where(lo, 0.0, xs)], axis=0).astype(BF16)


def _diff_combine(o2, rsum, lam, t):
    return o2[:t] * rsum[:t] - (lam * rsum[t:]) * o2[t:]


def _diff_head_post(o, subln_g, lam_init, gate):
    o = o * lax.rsqrt(jnp.mean(o * o, axis=-1, keepdims=True) + EPS) * (subln_g * (1.0 - lam_init))
    return (o * _silu(gate)).astype(BF16)


def _attn_a_ctx_kernel(q_ref, k_ref, v_ref, g_ref, lam_ref, sg_ref, o_ref, *, lam_init):
    t = q_ref.shape[0]
    rb = ATTN_ROWS
    lam = _diff_lambda_in_kernel(lam_ref, lam_init)
    ones = jnp.ones((t, LANES), BF16)
    k16 = [k_ref[:, h * LANES:(h + 1) * LANES].astype(BF16) for h in range(H_A)]
    v16 = [v_ref[:, h * LANES:(h + 1) * LANES].astype(BF16) for h in range(H_A)]
    blocks = [(h, i0) for h in range(H_A) for i0 in range(0, t, rb)]

    def scores(h, i0):
        qq = _split_halves(q_ref[i0:i0 + rb, h * LANES:(h + 1) * LANES], DH_A ** -0.5)
        return _dot_nt(qq, k16[h])

    pending = [scores(*blk) for blk in blocks[:ATTN_AHEAD]]
    for n, (h, i0) in enumerate(blocks):
        s = pending.pop(0)
        if n + ATTN_AHEAD < len(blocks):
            pending.append(scores(*blocks[n + ATTN_AHEAD]))
        rows, sl = slice(i0, i0 + rb), slice(h * LANES, (h + 1) * LANES)
        e = jnp.exp2(s - jnp.max(s, axis=-1, keepdims=True)).astype(BF16)
        rsum = 1.0 / _dot(e, ones)
        o = _diff_combine(_dot(e, v16[h]), rsum, lam, rb)
        o_ref[rows, sl] = _diff_head_post(o, sg_ref[...], lam_init, g_ref[rows, sl])


def _attn_a_ctx_call(proj, lamvec, subln_g, lam_init):
    t = SEQ
    blk = lambda c: pl.BlockSpec((t, W_A), lambda b: (b, c // W_A))
    return pl.pallas_call(
        functools.partial(_attn_a_ctx_kernel, lam_init=lam_init),
        grid=(BATCH,),
        in_specs=[blk(COL_QA), blk(COL_KA), blk(COL_VA), blk(COL_GA),
                  pl.BlockSpec((4, LANES), lambda b: (0, 0)),
                  pl.BlockSpec((1, LANES), lambda b: (0, 0))],
        out_specs=pl.BlockSpec((t, W_A), lambda b: (b, 0)),
        out_shape=jax.ShapeDtypeStruct((BATCH * t, W_A), BF16),
        compiler_params=_params("arbitrary"),
        name="attn_a_ctx",
    )(proj, proj, proj, proj, lamvec, subln_g)


def _rope(x, cos, sin_signed):
    first = (_lane(x.shape) % 32) < 16
    swapped = jnp.where(first, pltpu.roll(x, LANES - 16, 1), pltpu.roll(x, 16, 1))
    return x * cos + swapped * sin_signed


def _attn_a_lat_kernel(q_ref, k_ref, v_ref, g_ref, ck_ref, cv_ref, cosq_ref, sinq_ref, cosk_ref, sink_ref,
                       lam_ref, sg_ref, o_ref, kr_s, *, lam_init):
    tq = q_ref.shape[0]

    @pl.when(pl.program_id(1) == 0)
    def _():
        for h in range(H_A):
            sl = slice(h * LANES, (h + 1) * LANES)
            kr_s[:, sl] = _rope(k_ref[:, sl], cosk_ref[...], sink_ref[...]).astype(BF16)

    lam = _diff_lambda_in_kernel(lam_ref, lam_init)

    def scores(h):
        sl = slice(h * LANES, (h + 1) * LANES)
        qq = _split_halves(_rope(q_ref[:, sl], cosq_ref[...], sinq_ref[...]), DH_A ** -0.5)
        return _dot_nt(qq, kr_s[:, sl]), _dot_nt(qq, ck_ref[:, sl].astype(BF16))

    pending = [scores(h) for h in range(ATTN_AHEAD)]
    for h in range(H_A):
        sl = slice(h * LANES, (h + 1) * LANES)
        s_lat, s_ctx = pending.pop(0)
        if h + ATTN_AHEAD < H_A:
            pending.append(scores(h + ATTN_AHEAD))
        m = jnp.maximum(jnp.max(s_lat, axis=-1, keepdims=True), jnp.max(s_ctx, axis=-1, keepdims=True))
        e_lat = jnp.exp2(s_lat - m)
        e_ctx = jnp.exp2(s_ctx - m)
        rsum = 1.0 / (jnp.sum(e_lat, axis=-1, keepdims=True) + jnp.sum(e_ctx, axis=-1, keepdims=True))
        o2 = _dot(e_lat.astype(BF16), v_ref[:, sl].astype(BF16)) + _dot(e_ctx.astype(BF16),
                                                                         cv_ref[:, sl].astype(BF16))
        o = _diff_combine(o2, rsum, lam, tq)
        o_ref[:, sl] = _diff_head_post(o, sg_ref[...], lam_init, g_ref[:, sl])


def _attn_a_lat_call(proj, cache_k, cache_v, li, cos_t, sin_t, lamvec, subln_g, lam_init):
    tq = 256
    nq = DEC_SEQ // tq
    qblk = lambda c: pl.BlockSpec((tq, W_A), lambda b, i: (b * nq + i, c // W_A))
    full = lambda c: pl.BlockSpec((DEC_SEQ, W_A), lambda b, i: (b, c // W_A))
    cache = pl.BlockSpec((None, None, PAST_LEN, W_A), lambda b, i: (b, li, 0, 0))
    return pl.pallas_call(
        functools.partial(_attn_a_lat_kernel, lam_init=lam_init),
        grid=(DEC_BATCH, nq),
        in_specs=[qblk(COL_QA), full(COL_KA), full(COL_VA), qblk(COL_GA), cache, cache,
                  pl.BlockSpec((tq, LANES), lambda b, i: (i, 0)),
                  pl.BlockSpec((tq, LANES), lambda b, i: (i, 0)),
                  pl.BlockSpec((DEC_SEQ, LANES), lambda b, i: (0, 0)),
                  pl.BlockSpec((DEC_SEQ, LANES), lambda b, i: (0, 0)),
                  pl.BlockSpec((4, LANES), lambda b, i: (0, 0)),
                  pl.BlockSpec((1, LANES), lambda b, i: (0, 0))],
        out_specs=pl.BlockSpec((tq, W_A), lambda b, i: (b * nq + i, 0)),
        out_shape=jax.ShapeDtypeStruct((DEC_BATCH * DEC_SEQ, W_A), BF16),
        scratch_shapes=[pltpu.VMEM((DEC_SEQ, W_A), BF16)],
        compiler_params=_params("arbitrary", "arbitrary"),
        name="attn_a_lat",
    )(proj, proj, proj, proj, cache_k, cache_v, cos_t, sin_t, cos_t, sin_t, lamvec, subln_g)


def _merge_halves(o, t):
    return jnp.where(_lane((t, LANES)) < HALF, o[:t], o[t:])


def _attn_c_ctx_kernel(q_ref, kt_ref, vt_ref, g_ref, o_ref):
    t = q_ref.shape[0]
    n_pair = H_C // 2

    def scores(j):
        qq = _split_halves(q_ref[:, j * LANES:(j + 1) * LANES], DH_C ** -0.5)
        return _dot(qq, kt_ref[2 * j:2 * j + 2].reshape(LANES, t).astype(BF16))

    pending = [scores(j) for j in range(ATTN_AHEAD)]
    for j in range(n_pair):
        sl = slice(j * LANES, (j + 1) * LANES)
        s = pending.pop(0)
        if j + ATTN_AHEAD < n_pair:
            pending.append(scores(j + ATTN_AHEAD))
        vt = vt_ref[2 * j:2 * j + 2].reshape(LANES, t).astype(BF16)
        e = jnp.exp2(s - jnp.max(s, axis=-1, keepdims=True))
        rsum = 1.0 / jnp.sum(e, axis=-1, keepdims=True)
        o = _merge_halves(_dot_nt(e.astype(BF16), vt) * rsum, t)
        o_ref[:, sl] = (o * _silu(g_ref[:, sl])).astype(BF16)


def _attn_c_ctx_call(proj, kc_t, vc_t, li):
    t = SEQ
    blk = lambda c: pl.BlockSpec((t, W_C), lambda b: (b, c // W_C))
    cache = pl.BlockSpec((None, None, H_C, DH_C, t), lambda b: (b, li, 0, 0, 0))
    return pl.pallas_call(
        _attn_c_ctx_kernel,
        grid=(BATCH,),
        in_specs=[blk(COL_QC), cache, cache, blk(COL_GC)],
        out_specs=pl.BlockSpec((t, W_C), lambda b: (b, 0)),
        out_shape=jax.ShapeDtypeStruct((BATCH * t, W_C), BF16),
        compiler_params=_params("arbitrary"),
        name="attn_c_ctx",
    )(proj, kc_t, vc_t, proj)


def _rpb_kernel(rpb_ref, o_ref):
    h = pl.program_id(0)
    shape = (GRID_W, LANES)
    c = lax.broadcasted_iota(jnp.int32, shape, 0)
    cp = _lane(shape) % GRID_W
    d = jnp.clip(cp - c, -(NA_KW - 1), NA_KW - 1) + (NA_KW - 1)
    start = jnp.clip(c - NA_KW // 2, 0, GRID_W - NA_KW)
    in_win = (cp >= start) & (cp < start + NA_KW)
    o_ref[0] = jnp.full(shape, NEG_INF, F32)
    n_dc = 2 * NA_KW - 1
    for dr in range(2 * NA_KH - 1):
        acc = jnp.full(shape, NEG_INF, F32)
        for dc in range(n_dc):
            acc = jnp.where(d == dc, rpb_ref[(h * (2 * NA_KH - 1) + dr) * n_dc + dc], acc)
        o_ref[1 + dr] = jnp.where(in_win, acc * LOG2E, NEG_INF)


def _rpb_call(rpb):
    return pl.pallas_call(
        _rpb_kernel,
        grid=(H_C,),
        in_specs=[pl.BlockSpec(memory_space=pltpu.SMEM)],
        out_specs=pl.BlockSpec((None, NA_TILES, GRID_W, LANES), lambda h: (h, 0, 0, 0)),
        out_shape=jax.ShapeDtypeStruct((H_C, NA_TILES, GRID_W, LANES), F32),
        compiler_params=_params("arbitrary"),
        name="rpb_tiles",
    )(rpb.reshape(-1))


def _attn_c_lat_kernel(q_ref, k_ref, v_ref, g_ref, ck_ref, cv_ref, tile_ref, o_ref, bias_s):
    tq = q_ref.shape[0]
    nwin = NA_WROWS * GRID_W
    m = pl.program_id(1)
    w0 = jnp.where(m < (GRID_ROWS // NA_QROWS) // 2, 0, GRID_ROWS - NA_WROWS)
    k0 = pl.multiple_of(w0 * GRID_W, GRID_W)
    lo = _lane((GRID_W, LANES)) < HALF
    n_pair = H_C // 2

    def scores(j):
        sl = slice(j * LANES, (j + 1) * LANES)
        for s in range(2):
            for i in range(NA_QROWS):
                r = m * NA_QROWS + i
                start = jnp.clip(r - NA_KH // 2, 0, GRID_ROWS - NA_KH)
                for jp in range(NA_WROWS // 2):
                    idx = []
                    for u in range(2):
                        rk = w0 + 2 * jp + u
                        valid = (rk >= start) & (rk < start + NA_KH)
                        idx.append(jnp.where(valid, rk - r + NA_KH, 0))
                    tile = jnp.where(lo, tile_ref[2 * j + s, idx[0]], tile_ref[2 * j + s, idx[1]])
                    bias_s[(s * NA_QROWS + i) * GRID_W:(s * NA_QROWS + i + 1) * GRID_W,
                           jp * LANES:(jp + 1) * LANES] = tile
        qq = _split_halves(q_ref[:, sl], DH_C ** -0.5)
        kw = k_ref[pl.ds(k0, nwin), sl].astype(BF16)
        s_win = _dot_nt(qq, kw) + bias_s[...]
        return s_win, _dot_nt(qq, ck_ref[:, sl].astype(BF16))

    pending = [scores(j) for j in range(ATTN_AHEAD)]
    for j in range(n_pair):
        sl = slice(j * LANES, (j + 1) * LANES)
        s_win, s_ctx = pending.pop(0)
        if j + ATTN_AHEAD < n_pair:
            pending.append(scores(j + ATTN_AHEAD))
        vw = v_ref[pl.ds(k0, nwin), sl].astype(BF16)
        mx = jnp.maximum(jnp.max(s_win, axis=-1, keepdims=True), jnp.max(s_ctx, axis=-1, keepdims=True))
        e_win = jnp.exp2(s_win - mx)
        e_ctx = jnp.exp2(s_ctx - mx)
        rs = 1.0 / (jnp.sum(e_win, axis=-1, keepdims=True) + jnp.sum(e_ctx, axis=-1, keepdims=True))
        o = (_dot(e_win.astype(BF16), vw) + _dot(e_ctx.astype(BF16), cv_ref[:, sl].astype(BF16))) * rs
        o_ref[:, sl] = (_merge_halves(o, tq) * _silu(g_ref[:, sl])).astype(BF16)


def _attn_c_lat_call(proj, kv, cache_k, cache_v, li, tiles):
    tq = NA_QROWS * GRID_W
    nq = DEC_SEQ // tq
    qblk = lambda c: pl.BlockSpec((tq, W_C), lambda b, i: (b * nq + i, c // W_C))
    full = lambda c: pl.BlockSpec((DEC_SEQ, W_C), lambda b, i: (b, c // W_C))
    cache = pl.BlockSpec((None, None, PAST_LEN, W_C), lambda b, i: (b, li, 0, 0))
    return pl.pallas_call(
        _attn_c_lat_kernel,
        grid=(DEC_BATCH, nq),
        in_specs=[qblk(COL_QC), full(0), full(W_C), qblk(COL_GC), cache, cache,
                  pl.BlockSpec((H_C, NA_TILES, GRID_W, LANES), lambda b, i: (0, 0, 0, 0))],
        out_specs=pl.BlockSpec((tq, W_C), lambda b, i: (b * nq + i, 0)),
        out_shape=jax.ShapeDtypeStruct((DEC_BATCH * DEC_SEQ, W_C), BF16),
        scratch_shapes=[pltpu.VMEM((2 * tq, NA_WROWS * GRID_W), F32)],
        compiler_params=_params("arbitrary", "arbitrary"),
        name="attn_c_lat",
    )(proj, kv, kv, proj, cache_k, cache_v, tiles)


def _ssd_kernel(*refs, seq, use_h0, want_state, has_carry):
    refs = list(refs)
    dt_ref, xs_ref, bc_ref, z_ref = refs[:4]
    pos = 4
    h0_ref = None
    if use_h0:
        h0_ref = refs[pos]
        pos += 1
    cw_ref, cb_ref, dtb_ref, alog_ref, dsk_ref, g_ref = refs[pos:pos + 6]
    pos += 6 + int(has_carry)
    y_ref = refs[pos]
    pos += 1
    hs_ref = None
    if want_state:
        hs_ref = refs[pos]
        pos += 1
    upad_s, xc_s, expo_s, expot_s, dtt_s, tot_s, bmt_s, yf_s, yb_s, st_s = refs[pos:]

    q = SSD_CHUNK
    nc = seq // q
    n_pair = H_B // 2
    n_hd = 2 * H_B
    pad = 8

    upad_s[0:pad, :] = jnp.zeros((pad, CONV_DIM), F32)
    upad_s[pad + seq:2 * pad + seq, :] = jnp.zeros((pad, CONV_DIM), F32)
    upad_s[pad:pad + seq, 0:DI_B] = xs_ref[...]
    upad_s[pad:pad + seq, DI_B:CONV_DIM] = bc_ref[...]

    for c in range(nc):
        for cb_ in range(CONV_DIM // LANES):
            csl = slice(cb_ * LANES, (cb_ + 1) * LANES)
            acc = jnp.zeros((q, LANES), F32) + cb_ref[:, csl]
            for k in range(CONV_K):
                r0 = c * q + pad - CONV_K // 2 + k
                acc = acc + upad_s[r0:r0 + q, csl] * cw_ref[k:k + 1, csl]
            xc_s[c * q:(c + 1) * q, csl] = _silu(acc)

    a_row = -jnp.exp(alog_ref[...]) * LOG2E
    a_col = jnp.broadcast_to(a_row, (LANES, LANES)).T[0:n_hd, 0:1]
    ri = lax.broadcasted_iota(jnp.int32, (q, q), 0)
    ci = lax.broadcasted_iota(jnp.int32, (q, q), 1)
    ltri = (ri >= ci).astype(F32)
    fwd_lane = _lane((q, LANES)) < H_B
    fwd_row = lax.broadcasted_iota(jnp.int32, (n_hd, q), 0) < H_B

    def prep_body(c, carry):
        rows = pl.ds(pl.multiple_of(c * q, q), q)
        xdt = dt_ref[rows, 0:LANES] + dtb_ref[...]
        dtv = jnp.maximum(xdt, 0.0) + jnp.log1p(jnp.exp(-jnp.abs(xdt)))
        la = dtv * a_row
        acum = _dot(ltri, la, HI)
        expo_s[rows, :] = jnp.where(fwd_lane, acum, la - acum)
        acum_t = acum.T[0:n_hd, :]
        dt_t = dtv.T[0:n_hd, :]
        expot_s[c] = jnp.where(fwd_row, acum_t, dt_t * a_col - acum_t)
        dtt_s[c] = dt_t
        tot_s[c] = jnp.broadcast_to(acum_t[:, q - 1:q], (n_hd, q))
        bmt_s[c] = xc_s[rows, DI_B:DI_B + LANES].T
        return carry

    lax.fori_loop(0, nc, prep_body, 0, unroll=2)

    if use_h0:
        st_s[...] = h0_ref[...].reshape(2, n_pair, N_B, LANES)
    else:
        st_s[...] = jnp.zeros_like(st_s)

    lane_q = _lane((q, LANES))
    lo = lane_q < HALF
    lo_st = _lane((N_B, LANES)) < HALF

    def chunk(dirn, c, y_s):
        rows = pl.ds(pl.multiple_of(c * q, q), q)
        tri = (ri >= ci) if dirn == 0 else (ci >= ri)
        bm16 = xc_s[rows, DI_B:DI_B + LANES].astype(BF16)
        cm = xc_s[rows, DI_B + LANES:DI_B + 2 * LANES]
        for g in range(G_B):
            in_g = (lane_q >= g * N_B) & (lane_q < (g + 1) * N_B)
            cmg = jnp.where(in_g, cm, 0.0).astype(BF16)
            cb = _dot_nt(cmg, bm16)
            bmt_g = bmt_s[c, g * N_B:(g + 1) * N_B, :]
            for k in range(g * n_pair // G_B, (g + 1) * n_pair // G_B):
                psl = slice(k * LANES, (k + 1) * LANES)
                x16 = xc_s[rows, psl].astype(BF16)
                mats, lhs, ysc, cdec = [], [], [], []
                for s in range(2):
                    col = dirn * H_B + 2 * k + s
                    e_col = jnp.broadcast_to(expo_s[rows, col:col + 1], (q, q))
                    e_row = expot_s[c, col:col + 1, :]
                    dt_row = dtt_s[c, col:col + 1, :]
                    tot = tot_s[c, col:col + 1, :]
                    dec = jnp.exp2(jnp.where(tri, e_col - e_row, NEG_INF))
                    mats.append((cb * dec * dt_row).astype(BF16))
                    if dirn == 0:
                        ysc.append(jnp.exp2(e_col))
                        w_row = jnp.exp2(tot - e_row)
                    else:
                        ysc.append(jnp.exp2(e_col + tot))
                        w_row = jnp.exp2(-e_row)
                    lhs.append((bmt_g * (w_row * dt_row)).astype(BF16))
                    cdec.append(jnp.exp2(tot[:, 0:LANES]))
                yd = _dot(jnp.concatenate(mats, axis=0), x16)
                st = st_s[dirn, k]
                y_off = _dot(cmg, jnp.concatenate([st, st], axis=0).astype(BF16))
                y_s[rows, psl] = jnp.where(lo, yd[:q] + ysc[0] * y_off, yd[q:] + ysc[1] * y_off)
                ds = _dot(jnp.concatenate(lhs, axis=0), x16)
                st_s[dirn, k] = jnp.where(lo_st, cdec[0] * st + ds[:N_B], cdec[1] * st + ds[N_B:])

    def body(c, carry):
        chunk(0, c, yf_s)
        chunk(1, nc - 1 - c, yb_s)
        return carry

    lax.fori_loop(0, nc, body, 0, unroll=2)

    dsum = dsk_ref[0:1, :] + dsk_ref[1:2, :]

    def out_body(c, carry):
        rows = pl.ds(pl.multiple_of(c * q, q), q)
        y = yf_s[rows, :] + yb_s[rows, :] + xc_s[rows, 0:DI_B] * dsum
        y = y * _silu(z_ref[rows, :])
        y = y * lax.rsqrt(jnp.mean(y * y, axis=-1, keepdims=True) + EPS) * g_ref[...]
        y_ref[rows, :] = y.astype(BF16)
        return carry

    lax.fori_loop(0, nc, out_body, 0)
    if want_state:
        for dirn in range(2):
            for k in range(n_pair):
                st = st_s[dirn, k]
                st_t = jnp.concatenate([st, st], axis=0).T
                for s in range(2):
                    hs_ref[dirn, 2 * k + s] = st_t[s * P_B:(s + 1) * P_B, 0:N_B]


def _ssd_call(proj, h0t, conv_w, conv_b, dtb, alog, dskx, norm_g, *, nb, seq, want_state, li=0, carry=None):
    use_h0 = h0t is not None
    n_pair = H_B // 2
    in_specs = [pl.BlockSpec((seq, DT_PAD), lambda b: (b, COL_DT // DT_PAD)),
                pl.BlockSpec((seq, DI_B), lambda b: (b, COL_XS // DI_B)),
                pl.BlockSpec((seq, BC_DIM), lambda b: (b, COL_BC // BC_DIM)),
                pl.BlockSpec((seq, DI_B), lambda b: (b, COL_Z // DI_B))]
    args = [proj, proj, proj, proj]
    if use_h0:
        in_specs.append(pl.BlockSpec((None, 2, n_pair * N_B, LANES), lambda b: (b, 0, 0, 0)))
        args.append(h0t)
    const = lambda shape: pl.BlockSpec(shape, lambda b: (0,) * len(shape))
    in_specs += [const((CONV_K, CONV_DIM)), const((1, CONV_DIM)), const((1, LANES)), const((1, LANES)),
                 const((2, DI_B)), const((1, DI_B))]
    args += [conv_w, conv_b, dtb, alog, dskx, norm_g]
    out_specs = [pl.BlockSpec((seq, DI_B), lambda b: (b, 0))]
    out_shape = [jax.ShapeDtypeStruct((nb * seq, DI_B), BF16)]
    aliases = {}
    if want_state:
        out_specs.append(pl.BlockSpec((None, None, 2, H_B, P_B, N_B), lambda b: (b, li, 0, 0, 0, 0)))
        out_shape.append(jax.ShapeDtypeStruct((nb, DEPTH, 2, H_B, P_B, N_B), F32))
        if carry is not None:
            aliases = {len(args): 1}
            in_specs.append(pl.BlockSpec(memory_space=pl.ANY))
            args.append(carry)
    nc = seq // SSD_CHUNK
    per_chunk_rows = pltpu.VMEM((nc, 2 * H_B, SSD_CHUNK), F32)
    scratch = [pltpu.VMEM((seq + 16, CONV_DIM), F32), pltpu.VMEM((seq, CONV_DIM), F32),
               pltpu.VMEM((seq, LANES), F32), per_chunk_rows, per_chunk_rows, per_chunk_rows,
               pltpu.VMEM((nc, LANES, SSD_CHUNK), F32),
               pltpu.VMEM((seq, DI_B), F32), pltpu.VMEM((seq, DI_B), F32),
               pltpu.VMEM((2, n_pair, N_B, LANES), F32)]
    return pl.pallas_call(
        functools.partial(_ssd_kernel, seq=seq, use_h0=use_h0, want_state=want_state,
                          has_carry=carry is not None),
        grid=(nb,),
        in_specs=in_specs,
        out_specs=out_specs,
        out_shape=out_shape,
        input_output_aliases=aliases,
        scratch_shapes=scratch,
        compiler_params=_params("arbitrary"),
        name="ssd_ctx" if want_state else "ssd_lat",
    )(*args)


def _post_kernel(x_ref, ya_ref, yb_ref, yc_ref, ada_ref, g_ref, wm_ref, wa_ref, wb_ref, wc_ref, wo_ref, fg_ref,
                 o_ref, *, tm, row_base, tokens_per_row, final):
    row = row_base + (pl.program_id(0) * tm) // tokens_per_row
    gate = ada_ref[pl.ds(row, 1), 2 * D_MODEL:3 * D_MODEL]
    d = D_MODEL
    x = x_ref[...]
    h = _modulated_norm(x, g_ref, ada_ref, row)
    merged = None
    for n, (y_ref, w_ref) in enumerate(((ya_ref, wa_ref), (yb_ref, wb_ref), (yc_ref, wc_ref))):
        logits = _dot_nt(h, wm_ref[n * d:(n + 1) * d, :])
        term = _sigmoid(logits) * _dot(y_ref[...], w_ref[...])
        merged = term if merged is None else merged + term
    x = x + gate * _dot(merged.astype(BF16), wo_ref[...])
    if final:
        x = x * lax.rsqrt(jnp.mean(x * x, axis=-1, keepdims=True) + EPS) * fg_ref[...]
    o_ref[...] = x


def _post_call(x, ya, yb, yc, ada, norm_g, w_merge_t, li, wa, wb, wc, wo, final_g, *, tm, row_base,
               tokens_per_row, final):
    t = x.shape[0]
    tok = lambda w: pl.BlockSpec((tm, w), lambda i: (i, 0))
    const = lambda shape: pl.BlockSpec(shape, lambda i: (0,) * len(shape))
    kern = functools.partial(_post_kernel, tm=tm, row_base=row_base, tokens_per_row=tokens_per_row, final=final)
    return pl.pallas_call(
        kern,
        grid=(t // tm,),
        in_specs=[tok(D_MODEL), tok(W_A), tok(DI_B), tok(W_C),
                  const((8, 3 * D_MODEL)), const((1, D_MODEL)),
                  pl.BlockSpec((None, MERGE_COLS, D_MODEL), lambda i: (li, 0, 0)),
                  const((W_A, D_MODEL)), const((DI_B, D_MODEL)), const((W_C, D_MODEL)),
                  const((D_MODEL, D_MODEL)), const((1, D_MODEL))],
        out_specs=tok(D_MODEL),
        out_shape=jax.ShapeDtypeStruct((t, D_MODEL), F32),
        compiler_params=_params("arbitrary"),
        name="post_final" if final else "post",
    )(x, ya, yb, yc, ada, norm_g.reshape(1, D_MODEL), w_merge_t, wa, wb, wc, wo, final_g)


def _rope_tables():
    pos = np.arange(DEC_SEQ)
    lane = np.arange(LANES)
    l64 = lane % (2 * (DH_A // 2))
    quarter = DH_A // 4
    p = np.where((l64 < DH_A // 2)[None, :], (pos // GRID_W)[:, None], (pos % GRID_W)[:, None])
    inv = ROPE_BASE ** (-np.arange(quarter, dtype=np.float64) / quarter)
    ang = p.astype(np.float64) * inv[l64 % quarter][None, :]
    sign = np.where((lane % (2 * quarter)) < quarter, -1.0, 1.0)
    return jnp.asarray(np.cos(ang), F32), jnp.asarray(np.sin(ang) * sign[None, :], F32)


def _pad_lanes(v, width=LANES):
    v = v.reshape(1, -1).astype(F32)
    return jnp.pad(v, ((0, 0), (0, width - v.shape[1])))


def kernel(x_prompt, x_sample, cache_diff_k, cache_diff_v, cache_na_k, cache_na_v, state_ssd, c, c_ctx,
           norm_g, w_ada, b_ada, w_in, lam_q1, lam_k1, lam_q2, lam_k2, diff_subln_g, conv_w, conv_b,
           dt_bias, a_log, d_skip, ssd_norm_g, na_rpb, w_br_a, w_br_b, w_br_c, w_out, final_g):
    assert x_prompt.shape == (BATCH, SEQ, D_MODEL) and x_sample.shape == (DEC_BATCH, DEC_SEQ, D_MODEL)
    assert w_in.shape == (DEPTH, D_MODEL, _SRC["merge"] + MERGE_COLS)
    w_t = jnp.swapaxes(w_in, 1, 2)
    rows = lambda s0, n: w_t[:, s0:s0 + n, :].astype(BF16)
    w_main = jnp.concatenate([jnp.zeros((DEPTH, n, D_MODEL), BF16) if s0 is None else rows(s0, n)
                              for s0, n in MAIN_SEGMENTS], axis=1)
    assert w_main.shape == (DEPTH, PROJ_COLS, D_MODEL)
    w_kvc = rows(_SRC["kc"], KVC_COLS)
    w_merge = rows(_SRC["merge"], MERGE_COLS)
    wa16, wb16, wc16, wo16 = (w.astype(BF16) for w in (w_br_a, w_br_b, w_br_c, w_out))

    cvecs = jnp.concatenate([c_ctx[None, :], c, jnp.zeros((8 - 1 - DEC_BATCH, D_MODEL), F32)], axis=0)
    ada = _ada_call(cvecs, w_ada, b_ada)
    cos_t, sin_t = _rope_tables()

    ck_a = cache_diff_k.reshape(DEC_BATCH, DEPTH, PAST_LEN, W_A)
    cv_a = cache_diff_v.reshape(DEC_BATCH, DEPTH, PAST_LEN, W_A)
    ck_c = cache_na_k.reshape(DEC_BATCH, DEPTH, PAST_LEN, W_C)
    cv_c = cache_na_v.reshape(DEC_BATCH, DEPTH, PAST_LEN, W_C)
    h0t = state_ssd.transpose(0, 1, 2, 5, 3, 4).reshape(DEC_BATCH, DEPTH, 2, N_B, DI_B)
    h0t = h0t.reshape(DEC_BATCH, DEPTH, 2, N_B, H_B // 2, LANES).transpose(0, 1, 2, 4, 3, 5)
    h0t = h0t.reshape(DEC_BATCH, DEPTH, 2, (H_B // 2) * N_B, LANES)

    xp = x_prompt.reshape(BATCH * SEQ, D_MODEL)
    xs = x_sample.reshape(DEC_BATCH * DEC_SEQ, D_MODEL)
    fg = final_g.reshape(1, D_MODEL)
    caches = None
    new_ssd = None
    for li in range(DEPTH):
        lam_init = 0.8 - 0.6 * math.exp(-0.3 * li)
        final = li == DEPTH - 1
        lamvec = jnp.concatenate([_pad_lanes(v[li]) for v in (lam_q1, lam_k1, lam_q2, lam_k2)], axis=0)
        subln = diff_subln_g[li].reshape(1, LANES)
        dtb = _pad_lanes(dt_bias[li])
        alog = _pad_lanes(a_log[li])
        dskx = jnp.repeat(d_skip[li], P_B, axis=-1)
        ssd_w = (conv_w[li], conv_b[li].reshape(1, CONV_DIM), dtb, alog, dskx, ssd_norm_g[li].reshape(1, DI_B))
        post_w = (wa16[li], wb16[li], wc16[li], wo16[li], fg)

        proj, *caches = _inproj_call(xp, ada[li], norm_g[li], w_main, w_kvc, tm=1024, row_base=0,
                                     tokens_per_row=BATCH * SEQ, ctx=True, li=li, carry=caches)
        ya = _attn_a_ctx_call(proj, lamvec, subln, lam_init)
        yb, new_ssd = _ssd_call(proj, None, *ssd_w, nb=BATCH, seq=SEQ, want_state=True, li=li, carry=new_ssd)
        yc = _attn_c_ctx_call(proj, caches[2], caches[3], li)
        xp = _post_call(xp, ya, yb, yc, ada[li], norm_g[li], w_merge, li, *post_w, tm=512, row_base=0,
                        tokens_per_row=BATCH * SEQ, final=final)

        proj, kv = _inproj_call(xs, ada[li], norm_g[li], w_main, w_kvc, tm=1024, row_base=1,
                                tokens_per_row=DEC_SEQ, ctx=False, li=li)
        ya = _attn_a_lat_call(proj, ck_a, cv_a, li, cos_t, sin_t, lamvec, subln, lam_init)
        (yb,) = _ssd_call(proj, h0t[:, li], *ssd_w, nb=DEC_BATCH, seq=DEC_SEQ, want_state=False)
        yc = _attn_c_lat_call(proj, kv, ck_c, cv_c, li, _rpb_call(na_rpb[li]))
        xs = _post_call(xs, ya, yb, yc, ada[li], norm_g[li], w_merge, li, *post_w, tm=512, row_base=1,
                        tokens_per_row=DEC_SEQ, final=final)

    new_k_a, new_v_a, new_k_c_t, new_v_c_t = caches
    to_token_major = lambda a: a.transpose(0, 1, 4, 2, 3)
    return (xp.reshape(BATCH, SEQ, D_MODEL), xs.reshape(DEC_BATCH, DEC_SEQ, D_MODEL),
            new_k_a, new_v_a, to_token_major(new_k_c_t), to_token_major(new_v_c_t), new_ssd)
```

```python
import functools
import math

import jax
import jax.numpy as jnp
import numpy as np
from jax import lax
from jax.experimental import pallas as pl
from jax.experimental.pallas import tpu as pltpu

D_MODEL = 1024
BATCH = 32
SEQ = 256
DEPTH = 2
DEC_BATCH = 2
DEC_SEQ = 1024
PAST_LEN = 512
GRID_W = 64
GRID_ROWS = DEC_SEQ // GRID_W
H_A = 4
DH_A = 64
W_A = H_A * 2 * DH_A
H_B = 8
P_B = 64
G_B = 2
N_B = 64
DI_B = H_B * P_B
CONV_K = 5
CONV_DIM = DI_B + 2 * G_B * N_B
SSD_CHUNK = 128
H_C = 8
DH_C = 64
W_C = H_C * DH_C
NA_KH = 8
NA_KW = 16
N_BRANCH = 3
ROPE_BASE = 10000.0
EPS = 1e-6

LANES = 128
HALF = LANES // 2
DT_PAD = 256
VMEM_LIMIT = 56 * 1024 * 1024

COL_QA = 0
COL_GA = 512
COL_XS = 1024
COL_Z = 1536
COL_QC = 2048
COL_DT = 2560
COL_BC = 2816
COL_KA = 3072
COL_VA = 3584
COL_GC = 4096
PROJ_COLS = 4608
PROJ_TN = 1536
BC_DIM = CONV_DIM - DI_B
KVC_COLS = 2 * W_C
MERGE_COLS = N_BRANCH * D_MODEL
_SRC = dict(qa=0, ka=512, va=1024, ga=1536, z=2048, xs=2560, bc=3072, dt=3328, qc=3344, kc=3856, vc=4368,
            gc=4880, merge=5392)
MAIN_SEGMENTS = ((_SRC["qa"], W_A), (_SRC["ga"], W_A), (_SRC["xs"], DI_B), (_SRC["z"], DI_B),
                 (_SRC["qc"], W_C), (_SRC["dt"], 2 * H_B), (None, DT_PAD - 2 * H_B), (_SRC["bc"], BC_DIM),
                 (_SRC["ka"], W_A), (_SRC["va"], W_A), (_SRC["gc"], W_C))

NA_QROWS = 4
NA_WROWS = 12
NA_TILES = 2 * NA_KH
NEG_INF = float("-inf")
LOG2E = math.log2(math.e)
ATTN_AHEAD = 2
CTX_BATCHES_PER_STEP = 2
HI = lax.Precision.HIGHEST
F32 = jnp.float32
BF16 = jnp.bfloat16


def _dot(a, b, precision=None):
    return jnp.dot(a, b, preferred_element_type=F32, precision=precision)


def _dot_nt(a, b):
    return lax.dot_general(a, b, (((1,), (1,)), ((), ())), preferred_element_type=F32)


def _sigmoid(x):
    return 1.0 / (1.0 + jnp.exp(-x))


def _silu(x):
    return x * _sigmoid(x)


def _lane(shape):
    return lax.broadcasted_iota(jnp.int32, shape, len(shape) - 1)


def _params(*sem):
    return pltpu.CompilerParams(dimension_semantics=sem, vmem_limit_bytes=VMEM_LIMIT)


def _ada_kernel(cv_ref, w_ref, b_ref, o_ref):
    o_ref[...] = _dot(_silu(cv_ref[...]), w_ref[...], HI) + b_ref[...]


def _ada_call(cvecs, w_ada, b_ada):
    tn = 512
    return pl.pallas_call(
        _ada_kernel,
        grid=(DEPTH, 3 * D_MODEL // tn),
        in_specs=[
            pl.BlockSpec((8, D_MODEL), lambda l, j: (0, 0)),
            pl.BlockSpec((None, D_MODEL, tn), lambda l, j: (l, 0, j)),
            pl.BlockSpec((None, 1, tn), lambda l, j: (l, 0, j)),
        ],
        out_specs=pl.BlockSpec((None, 8, tn), lambda l, j: (l, 0, j)),
        out_shape=jax.ShapeDtypeStruct((DEPTH, 8, 3 * D_MODEL), F32),
        compiler_params=_params("arbitrary", "arbitrary"),
        name="ada",
    )(cvecs, w_ada, b_ada.reshape(DEPTH, 1, 3 * D_MODEL))


def _modulated_norm(x, g_ref, ada_ref, row):
    y = x * lax.rsqrt(jnp.mean(x * x, axis=-1, keepdims=True) + EPS) * g_ref[...]
    shift = ada_ref[pl.ds(row, 1), 0:D_MODEL]
    scale = ada_ref[pl.ds(row, 1), D_MODEL:2 * D_MODEL]
    return (y * (1.0 + scale) + shift).astype(BF16)


def _inproj_kernel(*refs, tm, row_base, tokens_per_row, n_carry, ctx):
    x_ref, ada_ref, g_ref, w_ref, wkv_ref = refs[:5]
    outs = refs[5 + n_carry:]
    o_ref, h_s = outs[0], outs[-1]
    i = pl.program_id(0)
    j = pl.program_id(1)
    nb = tm // SEQ

    @pl.when(j == 0)
    def _():
        h_s[...] = _modulated_norm(x_ref[...], g_ref, ada_ref, row_base + (i * tm) // tokens_per_row)
        if ctx:
            kc_ref, vc_ref = outs[3:5]
            for b in range(nb):
                kv_t = _dot_nt(wkv_ref[...], h_s[b * SEQ:(b + 1) * SEQ, :])
                kc_ref[b] = kv_t[0:W_C].reshape(H_C, DH_C, SEQ)
                vc_ref[b] = kv_t[W_C:2 * W_C].reshape(H_C, DH_C, SEQ)
        else:
            outs[1][...] = _dot_nt(h_s[...], wkv_ref[...])

    acc = _dot_nt(h_s[...], w_ref[...])
    o_ref[...] = acc
    if ctx:
        @pl.when(j == COL_KA // PROJ_TN)
        def _():
            for dst, c0 in ((outs[1], COL_KA % PROJ_TN), (outs[2], COL_VA % PROJ_TN)):
                for b in range(nb):
                    for h in range(H_A):
                        dst[b, :, h, :] = acc[b * SEQ:(b + 1) * SEQ, c0 + h * LANES:c0 + (h + 1) * LANES]


def _inproj_call(x, ada, norm_g, w_main, w_kvc, *, tm, row_base, tokens_per_row, ctx, li=0, carry=None):
    t = x.shape[0]
    n_carry = 0 if carry is None else len(carry)
    kern = functools.partial(_inproj_kernel, tm=tm, row_base=row_base, tokens_per_row=tokens_per_row,
                             n_carry=n_carry, ctx=ctx)
    in_specs = [
        pl.BlockSpec((tm, D_MODEL), lambda i, j: (i, 0)),
        pl.BlockSpec((8, 3 * D_MODEL), lambda i, j: (0, 0)),
        pl.BlockSpec((1, D_MODEL), lambda i, j: (0, 0)),
        pl.BlockSpec((None, PROJ_TN, D_MODEL), lambda i, j: (li, j, 0)),
        pl.BlockSpec((None, KVC_COLS, D_MODEL), lambda i, j: (li, 0, 0)),
    ]
    args = [x, ada, norm_g.reshape(1, D_MODEL), w_main, w_kvc]
    out_specs = [pl.BlockSpec((tm, PROJ_TN), lambda i, j: (i, j))]
    out_shape = [jax.ShapeDtypeStruct((t, PROJ_COLS), F32)]
    aliases = {}
    if ctx:
        nb = tm // SEQ
        out_specs += [pl.BlockSpec((nb, None, SEQ, H_A, 2 * DH_A), lambda i, j: (i, li, 0, 0, 0))] * 2
        out_specs += [pl.BlockSpec((nb, None, H_C, DH_C, SEQ), lambda i, j: (i, li, 0, 0, 0))] * 2
        out_shape += [jax.ShapeDtypeStruct((BATCH, DEPTH, SEQ, H_A, 2 * DH_A), F32)] * 2
        out_shape += [jax.ShapeDtypeStruct((BATCH, DEPTH, H_C, DH_C, SEQ), F32)] * 2
        if carry is not None:
            in_specs += [pl.BlockSpec(memory_space=pl.ANY)] * n_carry
            args += list(carry)
            aliases = {5 + k: 1 + k for k in range(n_carry)}
    else:
        out_specs.append(pl.BlockSpec((tm, KVC_COLS), lambda i, j: (i, 0)))
        out_shape.append(jax.ShapeDtypeStruct((t, KVC_COLS), F32))
    return pl.pallas_call(
        kern,
        grid=(t // tm, PROJ_COLS // PROJ_TN),
        in_specs=in_specs,
        out_specs=out_specs,
        out_shape=out_shape,
        input_output_aliases=aliases,
        scratch_shapes=[pltpu.VMEM((tm, D_MODEL), BF16)],
        compiler_params=_params("arbitrary", "arbitrary"),
        name="inproj_ctx" if ctx else "inproj_lat",
    )(*args)


def _diff_lambda_in_kernel(lam_ref, lam_init):
    v = lam_ref[...]
    l1 = jnp.sum(v[0:1] * v[1:2], axis=-1, keepdims=True)
    l2 = jnp.sum(v[2:3] * v[3:4], axis=-1, keepdims=True)
    return jnp.exp(l1) - jnp.exp(l2) + lam_init


def _split_halves(x, scale):
    lo = _lane(x.shape) < HALF
    xs = x * (scale * LOG2E)
    return jnp.concatenate([jnp.where(lo, xs, 0.0), jnp.where(lo, 0.0, xs)], axis=0).astype(BF16)


def _diff_combine(o2, rsum, lam, t):
    return o2[:t] * rsum[:t] - (lam * rsum[t:]) * o2[t:]


def _diff_head_post(o, subln_g, lam_init, gate):
    o = o * lax.rsqrt(jnp.mean(o * o, axis=-1, keepdims=True) + EPS) * (subln_g * (1.0 - lam_init))
    return (o * _silu(gate)).astype(BF16)


def _attn_a_ctx_kernel(q_ref, k_ref, v_ref, g_ref, lam_ref, sg_ref, o_ref, *, lam_init):
    t = SEQ
    lam = _diff_lambda_in_kernel(lam_ref, lam_init)
    ones = jnp.ones((t, LANES), BF16)
    blocks = [(b, h) for b in range(q_ref.shape[0] // t) for h in range(H_A)]
    where = lambda b, h: (slice(b * t, (b + 1) * t), slice(h * LANES, (h + 1) * LANES))

    def scores(b, h):
        qq = _split_halves(q_ref[where(b, h)], DH_A ** -0.5)
        return _dot_nt(qq, k_ref[where(b, h)].astype(BF16))

    pending = [scores(*blk) for blk in blocks[:ATTN_AHEAD]]
    for n, blk in enumerate(blocks):
        s = pending.pop(0)
        if n + ATTN_AHEAD < len(blocks):
            pending.append(scores(*blocks[n + ATTN_AHEAD]))
        e = jnp.exp2(s - jnp.max(s, axis=-1, keepdims=True)).astype(BF16)
        rsum = 1.0 / _dot(e, ones)
        o = _diff_combine(_dot(e, v_ref[where(*blk)].astype(BF16)), rsum, lam, t)
        o_ref[where(*blk)] = _diff_head_post(o, sg_ref[...], lam_init, g_ref[where(*blk)])


def _attn_a_ctx_call(proj, lamvec, subln_g, lam_init):
    rows = CTX_BATCHES_PER_STEP * SEQ
    blk = lambda c: pl.BlockSpec((rows, W_A), lambda b: (b, c // W_A))
    return pl.pallas_call(
        functools.partial(_attn_a_ctx_kernel, lam_init=lam_init),
        grid=(BATCH // CTX_BATCHES_PER_STEP,),
        in_specs=[blk(COL_QA), blk(COL_KA), blk(COL_VA), blk(COL_GA),
                  pl.BlockSpec((4, LANES), lambda b: (0, 0)),
                  pl.BlockSpec((1, LANES), lambda b: (0, 0))],
        out_specs=pl.BlockSpec((rows, W_A), lambda b: (b, 0)),
        out_shape=jax.ShapeDtypeStruct((BATCH * SEQ, W_A), BF16),
        compiler_params=_params("arbitrary"),
        name="attn_a_ctx",
    )(proj, proj, proj, proj, lamvec, subln_g)


def _rope(x, cos, sin_signed):
    first = (_lane(x.shape) % 32) < 16
    swapped = jnp.where(first, pltpu.roll(x, LANES - 16, 1), pltpu.roll(x, 16, 1))
    return x * cos + swapped * sin_signed


def _attn_a_lat_kernel(q_ref, k_ref, v_ref, g_ref, ck_ref, cv_ref, cosq_ref, sinq_ref, cosk_ref, sink_ref,
                       lam_ref, sg_ref, o_ref, kr_s, *, lam_init):
    tq = q_ref.shape[0]

    @pl.when(pl.program_id(1) == 0)
    def _():
        for h in range(H_A):
            sl = slice(h * LANES, (h + 1) * LANES)
            kr_s[:, sl] = _rope(k_ref[:, sl], cosk_ref[...], sink_ref[...]).astype(BF16)

    lam = _diff_lambda_in_kernel(lam_ref, lam_init)

    def scores(h):
        sl = slice(h * LANES, (h + 1) * LANES)
        qq = _split_halves(_rope(q_ref[:, sl], cosq_ref[...], sinq_ref[...]), DH_A ** -0.5)
        return _dot_nt(qq, kr_s[:, sl]), _dot_nt(qq, ck_ref[:, sl].astype(BF16))

    pending = [scores(h) for h in range(ATTN_AHEAD)]
    for h in range(H_A):
        sl = slice(h * LANES, (h + 1) * LANES)
        s_lat, s_ctx = pending.pop(0)
        if h + ATTN_AHEAD < H_A:
            pending.append(scores(h + ATTN_AHEAD))
        m = jnp.maximum(jnp.max(s_lat, axis=-1, keepdims=True), jnp.max(s_ctx, axis=-1, keepdims=True))
        e_lat = jnp.exp2(s_lat - m)
        e_ctx = jnp.exp2(s_ctx - m)
        rsum = 1.0 / (jnp.sum(e_lat, axis=-1, keepdims=True) + jnp.sum(e_ctx, axis=-1, keepdims=True))
        o2 = _dot(e_lat.astype(BF16), v_ref[:, sl].astype(BF16)) + _dot(e_ctx.astype(BF16),
                                                                         cv_ref[:, sl].astype(BF16))
        o = _diff_combine(o2, rsum, lam, tq)
        o_ref[:, sl] = _diff_head_post(o, sg_ref[...], lam_init, g_ref[:, sl])


def _attn_a_lat_call(proj, cache_k, cache_v, li, cos_t, sin_t, lamvec, subln_g, lam_init):
    tq = 256
    nq = DEC_SEQ // tq
    qblk = lambda c: pl.BlockSpec((tq, W_A), lambda b, i: (b * nq + i, c // W_A))
    full = lambda c: pl.BlockSpec((DEC_SEQ, W_A), lambda b, i: (b, c // W_A))
    cache = pl.BlockSpec((None, None, PAST_LEN, W_A), lambda b, i: (b, li, 0, 0))
    return pl.pallas_call(
        functools.partial(_attn_a_lat_kernel, lam_init=lam_init),
        grid=(DEC_BATCH, nq),
        in_specs=[qblk(COL_QA), full(COL_KA), full(COL_VA), qblk(COL_GA), cache, cache,
                  pl.BlockSpec((tq, LANES), lambda b, i: (i, 0)),
                  pl.BlockSpec((tq, LANES), lambda b, i: (i, 0)),
                  pl.BlockSpec((DEC_SEQ, LANES), lambda b, i: (0, 0)),
                  pl.BlockSpec((DEC_SEQ, LANES), lambda b, i: (0, 0)),
                  pl.BlockSpec((4, LANES), lambda b, i: (0, 0)),
                  pl.BlockSpec((1, LANES), lambda b, i: (0, 0))],
        out_specs=pl.BlockSpec((tq, W_A), lambda b, i: (b * nq + i, 0)),
        out_shape=jax.ShapeDtypeStruct((DEC_BATCH * DEC_SEQ, W_A), BF16),
        scratch_shapes=[pltpu.VMEM((DEC_SEQ, W_A), BF16)],
        compiler_params=_params("arbitrary", "arbitrary"),
        name="attn_a_lat",
    )(proj, proj, proj, proj, cache_k, cache_v, cos_t, sin_t, cos_t, sin_t, lamvec, subln_g)


def _merge_halves(o, t):
    return jnp.where(_lane((t, LANES)) < HALF, o[:t], o[t:])


def _attn_c_ctx_kernel(q_ref, kt_ref, vt_ref, g_ref, o_ref):
    t = SEQ
    blocks = [(b, j) for b in range(q_ref.shape[0] // t) for j in range(H_C // 2)]
    where = lambda b, j: (slice(b * t, (b + 1) * t), slice(j * LANES, (j + 1) * LANES))
    pair_t = lambda ref, b, j: ref[b, 2 * j:2 * j + 2].reshape(LANES, t).astype(BF16)

    def scores(b, j):
        return _dot(_split_halves(q_ref[where(b, j)], DH_C ** -0.5), pair_t(kt_ref, b, j))

    pending = [scores(*blk) for blk in blocks[:ATTN_AHEAD]]
    for n, blk in enumerate(blocks):
        s = pending.pop(0)
        if n + ATTN_AHEAD < len(blocks):
            pending.append(scores(*blocks[n + ATTN_AHEAD]))
        e = jnp.exp2(s - jnp.max(s, axis=-1, keepdims=True))
        rsum = 1.0 / jnp.sum(e, axis=-1, keepdims=True)
        o = _merge_halves(_dot_nt(e.astype(BF16), pair_t(vt_ref, *blk)) * rsum, t)
        o_ref[where(*blk)] = (o * _silu(g_ref[where(*blk)])).astype(BF16)


def _attn_c_ctx_call(proj, kc_t, vc_t, li):
    nb = CTX_BATCHES_PER_STEP
    blk = lambda c: pl.BlockSpec((nb * SEQ, W_C), lambda b: (b, c // W_C))
    cache = pl.BlockSpec((nb, None, H_C, DH_C, SEQ), lambda b: (b, li, 0, 0, 0))
    return pl.pallas_call(
        _attn_c_ctx_kernel,
        grid=(BATCH // nb,),
        in_specs=[blk(COL_QC), cache, cache, blk(COL_GC)],
        out_specs=pl.BlockSpec((nb * SEQ, W_C), lambda b: (b, 0)),
        out_shape=jax.ShapeDtypeStruct((BATCH * SEQ, W_C), BF16),
        compiler_params=_params("arbitrary"),
        name="attn_c_ctx",
    )(proj, kc_t, vc_t, proj)


def _rpb_kernel(rpb_ref, o_ref):
    shape = (GRID_W, LANES)
    c = lax.broadcasted_iota(jnp.int32, shape, 0)
    cp = _lane(shape) % GRID_W
    start = jnp.clip(c - NA_KW // 2, 0, GRID_W - NA_KW)
    in_win = (cp >= start) & (cp < start + NA_KW)
    o_ref[0] = jnp.full(shape, NEG_INF, F32)
    for dr in range(2 * NA_KH - 1):
        row = jnp.broadcast_to(rpb_ref[dr:dr + 1, :], shape)
        tile = pltpu.roll(row, LANES - (NA_KW - 1), 1, stride=1, stride_axis=0)
        o_ref[1 + dr] = jnp.where(in_win, tile * LOG2E, NEG_INF)


def _rpb_call(rpb):
    n_dc = 2 * NA_KW - 1
    v = jnp.pad(rpb, ((0, 0), (0, 0), (0, NA_TILES - (2 * NA_KH - 1)), (0, GRID_W - n_dc)))
    v = jnp.concatenate([v] * (LANES // GRID_W), axis=-1)
    return pl.pallas_call(
        _rpb_kernel,
        grid=(DEPTH, H_C),
        in_specs=[pl.BlockSpec((None, None, NA_TILES, LANES), lambda l, h: (l, h, 0, 0))],
        out_specs=pl.BlockSpec((None, None, NA_TILES, GRID_W, LANES), lambda l, h: (l, h, 0, 0, 0)),
        out_shape=jax.ShapeDtypeStruct((DEPTH, H_C, NA_TILES, GRID_W, LANES), F32),
        compiler_params=_params("arbitrary", "arbitrary"),
        name="rpb_tiles",
    )(v)


def _attn_c_lat_kernel(q_ref, k_ref, v_ref, g_ref, ck_ref, cv_ref, tile_ref, o_ref, bias_s):
    tq = q_ref.shape[0]
    nwin = NA_WROWS * GRID_W
    m = pl.program_id(1)
    w0 = jnp.where(m < (GRID_ROWS // NA_QROWS) // 2, 0, GRID_ROWS - NA_WROWS)
    k0 = pl.multiple_of(w0 * GRID_W, GRID_W)
    lo = _lane((GRID_W, LANES)) < HALF
    n_pair = H_C // 2

    def scores(j):
        sl = slice(j * LANES, (j + 1) * LANES)
        for s in range(2):
            for i in range(NA_QROWS):
                r = m * NA_QROWS + i
                start = jnp.clip(r - NA_KH // 2, 0, GRID_ROWS - NA_KH)
                for jp in range(NA_WROWS // 2):
                    idx = []
                    for u in range(2):
                        rk = w0 + 2 * jp + u
                        valid = (rk >= start) & (rk < start + NA_KH)
                        idx.append(jnp.where(valid, rk - r + NA_KH, 0))
                    tile = jnp.where(lo, tile_ref[2 * j + s, idx[0]], tile_ref[2 * j + s, idx[1]])
                    bias_s[(s * NA_QROWS + i) * GRID_W:(s * NA_QROWS + i + 1) * GRID_W,
                           jp * LANES:(jp + 1) * LANES] = tile
        qq = _split_halves(q_ref[:, sl], DH_C ** -0.5)
        kw = k_ref[pl.ds(k0, nwin), sl].astype(BF16)
        s_win = _dot_nt(qq, kw) + bias_s[...]
        ckt = ck_ref[2 * j:2 * j + 2].reshape(LANES, PAST_LEN).astype(BF16)
        return s_win, _dot(qq, ckt)

    pending = [scores(j) for j in range(ATTN_AHEAD)]
    for j in range(n_pair):
        sl = slice(j * LANES, (j + 1) * LANES)
        s_win, s_ctx = pending.pop(0)
        if j + ATTN_AHEAD < n_pair:
            pending.append(scores(j + ATTN_AHEAD))
        vw = v_ref[pl.ds(k0, nwin), sl].astype(BF16)
        mx = jnp.maximum(jnp.max(s_win, axis=-1, keepdims=True), jnp.max(s_ctx, axis=-1, keepdims=True))
        e_win = jnp.exp2(s_win - mx)
        e_ctx = jnp.exp2(s_ctx - mx)
        rs = 1.0 / (jnp.sum(e_win, axis=-1, keepdims=True) + jnp.sum(e_ctx, axis=-1, keepdims=True))
        cvt = cv_ref[2 * j:2 * j + 2].reshape(LANES, PAST_LEN).astype(BF16)
        o = (_dot(e_win.astype(BF16), vw) + _dot_nt(e_ctx.astype(BF16), cvt)) * rs
        o_ref[:, sl] = (_merge_halves(o, tq) * _silu(g_ref[:, sl])).astype(BF16)


def _attn_c_lat_call(proj, kv, cache_k, cache_v, li, tiles):
    tq = NA_QROWS * GRID_W
    nq = DEC_SEQ // tq
    qblk = lambda c: pl.BlockSpec((tq, W_C), lambda b, i: (b * nq + i, c // W_C))
    full = lambda c: pl.BlockSpec((DEC_SEQ, W_C), lambda b, i: (b, c // W_C))
    cache = pl.BlockSpec((None, None, H_C, DH_C, PAST_LEN), lambda b, i: (b, li, 0, 0, 0))
    return pl.pallas_call(
        _attn_c_lat_kernel,
        grid=(DEC_BATCH, nq),
        in_specs=[qblk(COL_QC), full(0), full(W_C), qblk(COL_GC), cache, cache,
                  pl.BlockSpec((None, H_C, NA_TILES, GRID_W, LANES), lambda b, i: (li, 0, 0, 0, 0))],
        out_specs=pl.BlockSpec((tq, W_C), lambda b, i: (b * nq + i, 0)),
        out_shape=jax.ShapeDtypeStruct((DEC_BATCH * DEC_SEQ, W_C), BF16),
        scratch_shapes=[pltpu.VMEM((2 * tq, NA_WROWS * GRID_W), F32)],
        compiler_params=_params("arbitrary", "arbitrary"),
        name="attn_c_lat",
    )(proj, kv, kv, proj, cache_k, cache_v, tiles)


def _ssd_kernel(*refs, seq, use_h0, want_state, has_carry):
    refs = list(refs)
    dt_ref, xs_ref, bc_ref, z_ref = refs[:4]
    pos = 4
    h0_ref = None
    if use_h0:
        h0_ref = refs[pos]
        pos += 1
    cw_ref, cb_ref, dtb_ref, alog_ref, dsk_ref, g_ref = refs[pos:pos + 6]
    pos += 6 + int(has_carry)
    y_ref = refs[pos]
    pos += 1
    hs_ref = None
    if want_state:
        hs_ref = refs[pos]
        pos += 1
    upad_s, xc_s, expo_s, expot_s, dtt_s, tot_s, bmt_s, yf_s, yb_s, st_s = refs[pos:]

    q = SSD_CHUNK
    nc = seq // q
    n_pair = H_B // 2
    n_hd = 2 * H_B
    pad = 8

    upad_s[0:pad, :] = jnp.zeros((pad, CONV_DIM), F32)
    upad_s[pad + seq:2 * pad + seq, :] = jnp.zeros((pad, CONV_DIM), F32)
    upad_s[pad:pad + seq, 0:DI_B] = xs_ref[...]
    upad_s[pad:pad + seq, DI_B:CONV_DIM] = bc_ref[...]

    for c in range(nc):
        for cb_ in range(CONV_DIM // LANES):
            csl = slice(cb_ * LANES, (cb_ + 1) * LANES)
            acc = jnp.zeros((q, LANES), F32) + cb_ref[:, csl]
            for k in range(CONV_K):
                r0 = c * q + pad - CONV_K // 2 + k
                acc = acc + upad_s[r0:r0 + q, csl] * cw_ref[k:k + 1, csl]
            xc_s[c * q:(c + 1) * q, csl] = _silu(acc)

    a_row = -jnp.exp(alog_ref[...]) * LOG2E
    a_col = jnp.broadcast_to(a_row, (LANES, LANES)).T[0:n_hd, 0:1]
    ri = lax.broadcasted_iota(jnp.int32, (q, q), 0)
    ci = lax.broadcasted_iota(jnp.int32, (q, q), 1)
    ltri = (ri >= ci).astype(F32)
    fwd_lane = _lane((q, LANES)) < H_B
    fwd_row = lax.broadcasted_iota(jnp.int32, (n_hd, q), 0) < H_B

    def prep_body(c, carry):
        rows = pl.ds(pl.multiple_of(c * q, q), q)
        xdt = dt_ref[rows, 0:LANES] + dtb_ref[...]
        dtv = jnp.maximum(xdt, 0.0) + jnp.log1p(jnp.exp(-jnp.abs(xdt)))
        la = dtv * a_row
        acum = _dot(ltri, la, HI)
        expo_s[rows, :] = jnp.where(fwd_lane, acum, la - acum)
        acum_t = acum.T[0:n_hd, :]
        dt_t = dtv.T[0:n_hd, :]
        expot_s[c] = jnp.where(fwd_row, acum_t, dt_t * a_col - acum_t)
        dtt_s[c] = dt_t
        tot_s[c] = jnp.broadcast_to(acum_t[:, q - 1:q], (n_hd, q))
        bmt_s[c] = xc_s[rows, DI_B:DI_B + LANES].T
        return carry

    lax.fori_loop(0, nc, prep_body, 0, unroll=2)

    if use_h0:
        st_s[...] = h0_ref[...].reshape(2, n_pair, N_B, LANES)
    else:
        st_s[...] = jnp.zeros_like(st_s)

    lane_q = _lane((q, LANES))
    lo = lane_q < HALF
    lo_st = _lane((N_B, LANES)) < HALF

    def chunk(dirn, c, y_s):
        rows = pl.ds(pl.multiple_of(c * q, q), q)
        tri = (ri >= ci) if dirn == 0 else (ci >= ri)
        bm16 = xc_s[rows, DI_B:DI_B + LANES].astype(BF16)
        cm = xc_s[rows, DI_B + LANES:DI_B + 2 * LANES]
        for g in range(G_B):
            in_g = (lane_q >= g * N_B) & (lane_q < (g + 1) * N_B)
            cmg = jnp.where(in_g, cm, 0.0).astype(BF16)
            cb = _dot_nt(cmg, bm16)
            bmt_g = bmt_s[c, g * N_B:(g + 1) * N_B, :]
            for k in range(g * n_pair // G_B, (g + 1) * n_pair // G_B):
                psl = slice(k * LANES, (k + 1) * LANES)
                x16 = xc_s[rows, psl].astype(BF16)
                mats, lhs, ysc, cdec = [], [], [], []
                for s in range(2):
                    col = dirn * H_B + 2 * k + s
                    e_col = jnp.broadcast_to(expo_s[rows, col:col + 1], (q, q))
                    e_row = expot_s[c, col:col + 1, :]
                    dt_row = dtt_s[c, col:col + 1, :]
                    tot = tot_s[c, col:col + 1, :]
                    dec = jnp.exp2(jnp.where(tri, e_col - e_row, NEG_INF))
                    mats.append((cb * dec * dt_row).astype(BF16))
                    if dirn == 0:
                        ysc.append(jnp.exp2(e_col))
                        w_row = jnp.exp2(tot - e_row)
                    else:
                        ysc.append(jnp.exp2(e_col + tot))
                        w_row = jnp.exp2(-e_row)
                    lhs.append((bmt_g * (w_row * dt_row)).astype(BF16))
                    cdec.append(jnp.exp2(tot[:, 0:LANES]))
                yd = _dot(jnp.concatenate(mats, axis=0), x16)
                st = st_s[dirn, k]
                y_off = _dot(cmg, jnp.concatenate([st, st], axis=0).astype(BF16))
                y_s[rows, psl] = jnp.where(lo, yd[:q] + ysc[0] * y_off, yd[q:] + ysc[1] * y_off)
                ds = _dot(jnp.concatenate(lhs, axis=0), x16)
                st_s[dirn, k] = jnp.where(lo_st, cdec[0] * st + ds[:N_B], cdec[1] * st + ds[N_B:])

    def body(c, carry):
        chunk(0, c, yf_s)
        chunk(1, nc - 1 - c, yb_s)
        return carry

    lax.fori_loop(0, nc, body, 0, unroll=2)

    dsum = dsk_ref[0:1, :] + dsk_ref[1:2, :]

    def out_body(c, carry):
        rows = pl.ds(pl.multiple_of(c * q, q), q)
        y = yf_s[rows, :] + yb_s[rows, :] + xc_s[rows, 0:DI_B] * dsum
        y = y * _silu(z_ref[rows, :])
        y = y * lax.rsqrt(jnp.mean(y * y, axis=-1, keepdims=True) + EPS) * g_ref[...]
        y_ref[rows, :] = y.astype(BF16)
        return carry

    lax.fori_loop(0, nc, out_body, 0)
    if want_state:
        for dirn in range(2):
            for k in range(n_pair):
                st = st_s[dirn, k]
                st_t = jnp.concatenate([st, st], axis=0).T
                for s in range(2):
                    hs_ref[dirn, 2 * k + s] = st_t[s * P_B:(s + 1) * P_B, 0:N_B]


def _ssd_call(proj, h0t, conv_w, conv_b, dtb, alog, dskx, norm_g, *, nb, seq, want_state, li=0, carry=None):
    use_h0 = h0t is not None
    n_pair = H_B // 2
    in_specs = [pl.BlockSpec((seq, DT_PAD), lambda b: (b, COL_DT // DT_PAD)),
                pl.BlockSpec((seq, DI_B), lambda b: (b, COL_XS // DI_B)),
                pl.BlockSpec((seq, BC_DIM), lambda b: (b, COL_BC // BC_DIM)),
                pl.BlockSpec((seq, DI_B), lambda b: (b, COL_Z // DI_B))]
    args = [proj, proj, proj, proj]
    if use_h0:
        in_specs.append(pl.BlockSpec((None, 2, n_pair * N_B, LANES), lambda b: (b, 0, 0, 0)))
        args.append(h0t)
    const = lambda shape: pl.BlockSpec(shape, lambda b: (0,) * len(shape))
    in_specs += [const((CONV_K, CONV_DIM)), const((1, CONV_DIM)), const((1, LANES)), const((1, LANES)),
                 const((2, DI_B)), const((1, DI_B))]
    args += [conv_w, conv_b, dtb, alog, dskx, norm_g]
    out_specs = [pl.BlockSpec((seq, DI_B), lambda b: (b, 0))]
    out_shape = [jax.ShapeDtypeStruct((nb * seq, DI_B), BF16)]
    aliases = {}
    if want_state:
        out_specs.append(pl.BlockSpec((None, None, 2, H_B, P_B, N_B), lambda b: (b, li, 0, 0, 0, 0)))
        out_shape.append(jax.ShapeDtypeStruct((nb, DEPTH, 2, H_B, P_B, N_B), F32))
        if carry is not None:
            aliases = {len(args): 1}
            in_specs.append(pl.BlockSpec(memory_space=pl.ANY))
            args.append(carry)
    nc = seq // SSD_CHUNK
    per_chunk_rows = pltpu.VMEM((nc, 2 * H_B, SSD_CHUNK), F32)
    scratch = [pltpu.VMEM((seq + 16, CONV_DIM), F32), pltpu.VMEM((seq, CONV_DIM), F32),
               pltpu.VMEM((seq, LANES), F32), per_chunk_rows, per_chunk_rows, per_chunk_rows,
               pltpu.VMEM((nc, LANES, SSD_CHUNK), F32),
               pltpu.VMEM((seq, DI_B), F32), pltpu.VMEM((seq, DI_B), F32),
               pltpu.VMEM((2, n_pair, N_B, LANES), F32)]
    return pl.pallas_call(
        functools.partial(_ssd_kernel, seq=seq, use_h0=use_h0, want_state=want_state,
                          has_carry=carry is not None),
        grid=(nb,),
        in_specs=in_specs,
        out_specs=out_specs,
        out_shape=out_shape,
        input_output_aliases=aliases,
        scratch_shapes=scratch,
        compiler_params=_params("arbitrary"),
        name="ssd_ctx" if want_state else "ssd_lat",
    )(*args)


def _post_kernel(x_ref, ya_ref, yb_ref, yc_ref, ada_ref, g_ref, wm_ref, wa_ref, wb_ref, wc_ref, wo_ref, fg_ref,
                 o_ref, *, tm, row_base, tokens_per_row, final):
    row = row_base + (pl.program_id(0) * tm) // tokens_per_row
    gate = ada_ref[pl.ds(row, 1), 2 * D_MODEL:3 * D_MODEL]
    d = D_MODEL
    x = x_ref[...]
    h = _modulated_norm(x, g_ref, ada_ref, row)
    merged = None
    for n, (y_ref, w_ref) in enumerate(((ya_ref, wa_ref), (yb_ref, wb_ref), (yc_ref, wc_ref))):
        logits = _dot_nt(h, wm_ref[n * d:(n + 1) * d, :])
        term = _sigmoid(logits) * _dot(y_ref[...], w_ref[...])
        merged = term if merged is None else merged + term
    x = x + gate * _dot(merged.astype(BF16), wo_ref[...])
    if final:
        x = x * lax.rsqrt(jnp.mean(x * x, axis=-1, keepdims=True) + EPS) * fg_ref[...]
    o_ref[...] = x


def _post_call(x, ya, yb, yc, ada, norm_g, w_merge_t, li, wa, wb, wc, wo, final_g, *, tm, row_base,
               tokens_per_row, final):
    t = x.shape[0]
    tok = lambda w: pl.BlockSpec((tm, w), lambda i: (i, 0))
    const = lambda shape: pl.BlockSpec(shape, lambda i: (0,) * len(shape))
    kern = functools.partial(_post_kernel, tm=tm, row_base=row_base, tokens_per_row=tokens_per_row, final=final)
    return pl.pallas_call(
        kern,
        grid=(t // tm,),
        in_specs=[tok(D_MODEL), tok(W_A), tok(DI_B), tok(W_C),
                  const((8, 3 * D_MODEL)), const((1, D_MODEL)),
                  pl.BlockSpec((None, MERGE_COLS, D_MODEL), lambda i: (li, 0, 0)),
                  const((W_A, D_MODEL)), const((DI_B, D_MODEL)), const((W_C, D_MODEL)),
                  const((D_MODEL, D_MODEL)), const((1, D_MODEL))],
        out_specs=tok(D_MODEL),
        out_shape=jax.ShapeDtypeStruct((t, D_MODEL), F32),
        compiler_params=_params("arbitrary"),
        name="post_final" if final else "post",
    )(x, ya, yb, yc, ada, norm_g.reshape(1, D_MODEL), w_merge_t, wa, wb, wc, wo, final_g)


def _rope_tables():
    pos = np.arange(DEC_SEQ)
    lane = np.arange(LANES)
    l64 = lane % (2 * (DH_A // 2))
    quarter = DH_A // 4
    p = np.where((l64 < DH_A // 2)[None, :], (pos // GRID_W)[:, None], (pos % GRID_W)[:, None])
    inv = ROPE_BASE ** (-np.arange(quarter, dtype=np.float64) / quarter)
    ang = p.astype(np.float64) * inv[l64 % quarter][None, :]
    sign = np.where((lane % (2 * quarter)) < quarter, -1.0, 1.0)
    return jnp.asarray(np.cos(ang), F32), jnp.asarray(np.sin(ang) * sign[None, :], F32)


def _pad_lanes(v, width=LANES):
    v = v.reshape(1, -1).astype(F32)
    return jnp.pad(v, ((0, 0), (0, width - v.shape[1])))


def kernel(x_prompt, x_sample, cache_diff_k, cache_diff_v, cache_na_k, cache_na_v, state_ssd, c, c_ctx,
           norm_g, w_ada, b_ada, w_in, lam_q1, lam_k1, lam_q2, lam_k2, diff_subln_g, conv_w, conv_b,
           dt_bias, a_log, d_skip, ssd_norm_g, na_rpb, w_br_a, w_br_b, w_br_c, w_out, final_g):
    assert x_prompt.shape == (BATCH, SEQ, D_MODEL) and x_sample.shape == (DEC_BATCH, DEC_SEQ, D_MODEL)
    assert w_in.shape == (DEPTH, D_MODEL, _SRC["merge"] + MERGE_COLS)
    w_t = jnp.swapaxes(w_in, 1, 2)
    rows = lambda s0, n: w_t[:, s0:s0 + n, :].astype(BF16)
    w_main = jnp.concatenate([jnp.zeros((DEPTH, n, D_MODEL), BF16) if s0 is None else rows(s0, n)
                              for s0, n in MAIN_SEGMENTS], axis=1)
    assert w_main.shape == (DEPTH, PROJ_COLS, D_MODEL)
    w_kvc = rows(_SRC["kc"], KVC_COLS)
    w_merge = rows(_SRC["merge"], MERGE_COLS)
    wa16, wb16, wc16, wo16 = (w.astype(BF16) for w in (w_br_a, w_br_b, w_br_c, w_out))

    cvecs = jnp.concatenate([c_ctx[None, :], c, jnp.zeros((8 - 1 - DEC_BATCH, D_MODEL), F32)], axis=0)
    ada = _ada_call(cvecs, w_ada, b_ada)
    cos_t, sin_t = _rope_tables()

    ck_a = cache_diff_k.reshape(DEC_BATCH, DEPTH, PAST_LEN, W_A)
    cv_a = cache_diff_v.reshape(DEC_BATCH, DEPTH, PAST_LEN, W_A)
    ck_c = cache_na_k.transpose(0, 1, 3, 4, 2)
    cv_c = cache_na_v.transpose(0, 1, 3, 4, 2)
    na_tiles = _rpb_call(na_rpb)
    h0t = state_ssd.transpose(0, 1, 2, 5, 3, 4).reshape(DEC_BATCH, DEPTH, 2, N_B, DI_B)
    h0t = h0t.reshape(DEC_BATCH, DEPTH, 2, N_B, H_B // 2, LANES).transpose(0, 1, 2, 4, 3, 5)
    h0t = h0t.reshape(DEC_BATCH, DEPTH, 2, (H_B // 2) * N_B, LANES)

    xp = x_prompt.reshape(BATCH * SEQ, D_MODEL)
    xs = x_sample.reshape(DEC_BATCH * DEC_SEQ, D_MODEL)
    fg = final_g.reshape(1, D_MODEL)
    caches = None
    new_ssd = None
    for li in range(DEPTH):
        lam_init = 0.8 - 0.6 * math.exp(-0.3 * li)
        final = li == DEPTH - 1
        lamvec = jnp.concatenate([_pad_lanes(v[li]) for v in (lam_q1, lam_k1, lam_q2, lam_k2)], axis=0)
        subln = diff_subln_g[li].reshape(1, LANES)
        dtb = _pad_lanes(dt_bias[li])
        alog = _pad_lanes(a_log[li])
        dskx = jnp.repeat(d_skip[li], P_B, axis=-1)
        ssd_w = (conv_w[li], conv_b[li].reshape(1, CONV_DIM), dtb, alog, dskx, ssd_norm_g[li].reshape(1, DI_B))
        post_w = (wa16[li], wb16[li], wc16[li], wo16[li], fg)

        proj, *caches = _inproj_call(xp, ada[li], norm_g[li], w_main, w_kvc, tm=1024, row_base=0,
                                     tokens_per_row=BATCH * SEQ, ctx=True, li=li, carry=caches)
        ya = _attn_a_ctx_call(proj, lamvec, subln, lam_init)
        yb, new_ssd = _ssd_call(proj, None, *ssd_w, nb=BATCH, seq=SEQ, want_state=True, li=li, carry=new_ssd)
        yc = _attn_c_ctx_call(proj, caches[2], caches[3], li)
        xp = _post_call(xp, ya, yb, yc, ada[li], norm_g[li], w_merge, li, *post_w, tm=512, row_base=0,
                        tokens_per_row=BATCH * SEQ, final=final)

        proj, kv = _inproj_call(xs, ada[li], norm_g[li], w_main, w_kvc, tm=1024, row_base=1,
                                tokens_per_row=DEC_SEQ, ctx=False, li=li)
        ya = _attn_a_lat_call(proj, ck_a, cv_a, li, cos_t, sin_t, lamvec, subln, lam_init)
        (yb,) = _ssd_call(proj, h0t[:, li], *ssd_w, nb=DEC_BATCH, seq=DEC_SEQ, want_state=False)
        yc = _attn_c_lat_call(proj, kv, ck_c, cv_c, li, na_tiles)
        xs = _post_call(xs, ya, yb, yc, ada[li], norm_g[li], w_merge, li, *post_w, tm=512, row_base=1,
                        tokens_per_row=DEC_SEQ, final=final)

    new_k_a, new_v_a, new_k_c_t, new_v_c_t = caches
    to_token_major = lambda a: a.transpose(0, 1, 4, 2, 3)
    return (xp.reshape(BATCH, SEQ, D_MODEL), xs.reshape(DEC_BATCH, DEC_SEQ, D_MODEL),
            new_k_a, new_v_a, to_token_major(new_k_c_t), to_token_major(new_v_c_t), new_ssd)
```

```python
import functools
import math

import jax
import jax.numpy as jnp
import numpy as np
from jax import lax
from jax.experimental import pallas as pl
from jax.experimental.pallas import tpu as pltpu

D_MODEL = 1024
BATCH = 32
SEQ = 256
DEPTH = 2
DEC_BATCH = 2
DEC_SEQ = 1024
PAST_LEN = 512
GRID_W = 64
GRID_ROWS = DEC_SEQ // GRID_W
H_A = 4
DH_A = 64
W_A = H_A * 2 * DH_A
H_B = 8
P_B = 64
G_B = 2
N_B = 64
DI_B = H_B * P_B
CONV_K = 5
CONV_DIM = DI_B + 2 * G_B * N_B
SSD_CHUNK = 128
H_C = 8
DH_C = 64
W_C = H_C * DH_C
NA_KH = 8
NA_KW = 16
N_BRANCH = 3
ROPE_BASE = 10000.0
EPS = 1e-6

LANES = 128
HALF = LANES // 2
DT_PAD = 256
VMEM_LIMIT = 56 * 1024 * 1024

COL_XS = 0
COL_Z = 512
COL_DT = 1024
COL_BC = 1280
SSD_COLS = 1536
COL_QA = 1536
COL_GA = 2048
COL_QC = 2560
COL_GC = 3072
COL_KA = 3584
COL_VA = 4096
PROJ_COLS = 4608
PROJ_TN = 1536
SIDE_TN = 256
BC_DIM = CONV_DIM - DI_B
KVC_COLS = 2 * W_C
MERGE_COLS = N_BRANCH * D_MODEL
_SRC = dict(qa=0, ka=512, va=1024, ga=1536, z=2048, xs=2560, bc=3072, dt=3328, qc=3344, kc=3856, vc=4368,
            gc=4880, merge=5392)
MAIN_SEGMENTS = ((_SRC["xs"], DI_B), (_SRC["z"], DI_B), (_SRC["dt"], 2 * H_B), (None, DT_PAD - 2 * H_B),
                 (_SRC["bc"], BC_DIM), (_SRC["qa"], W_A), (_SRC["ga"], W_A), (_SRC["qc"], W_C),
                 (_SRC["gc"], W_C), (_SRC["ka"], W_A), (_SRC["va"], W_A))

NA_QROWS = 4
NA_WROWS = 12
NA_TILES = 2 * NA_KH
NEG_INF = float("-inf")
LOG2E = math.log2(math.e)
ATTN_AHEAD = 2
CTX_BATCHES_PER_STEP = 2
HI = lax.Precision.HIGHEST
F32 = jnp.float32
BF16 = jnp.bfloat16


def _dot(a, b, precision=None):
    return jnp.dot(a, b, preferred_element_type=F32, precision=precision)


def _dot_nt(a, b):
    return lax.dot_general(a, b, (((1,), (1,)), ((), ())), preferred_element_type=F32)


def _sigmoid(x):
    return 1.0 / (1.0 + jnp.exp(-x))


def _silu(x):
    return x * _sigmoid(x)


def _lane(shape):
    return lax.broadcasted_iota(jnp.int32, shape, len(shape) - 1)


def _params(*sem):
    return pltpu.CompilerParams(dimension_semantics=sem, vmem_limit_bytes=VMEM_LIMIT)


def _ada_kernel(cv_ref, w_ref, b_ref, o_ref):
    o_ref[...] = _dot(_silu(cv_ref[...]), w_ref[...], HI) + b_ref[...]


def _ada_call(cvecs, w_ada, b_ada):
    tn = 512
    return pl.pallas_call(
        _ada_kernel,
        grid=(DEPTH, 3 * D_MODEL // tn),
        in_specs=[
            pl.BlockSpec((8, D_MODEL), lambda l, j: (0, 0)),
            pl.BlockSpec((None, D_MODEL, tn), lambda l, j: (l, 0, j)),
            pl.BlockSpec((None, 1, tn), lambda l, j: (l, 0, j)),
        ],
        out_specs=pl.BlockSpec((None, 8, tn), lambda l, j: (l, 0, j)),
        out_shape=jax.ShapeDtypeStruct((DEPTH, 8, 3 * D_MODEL), F32),
        compiler_params=_params("arbitrary", "arbitrary"),
        name="ada",
    )(cvecs, w_ada, b_ada.reshape(DEPTH, 1, 3 * D_MODEL))


def _modulated_norm(x, g_ref, ada_ref, row):
    y = x * lax.rsqrt(jnp.mean(x * x, axis=-1, keepdims=True) + EPS) * g_ref[...]
    shift = ada_ref[pl.ds(row, 1), 0:D_MODEL]
    scale = ada_ref[pl.ds(row, 1), D_MODEL:2 * D_MODEL]
    return (y * (1.0 + scale) + shift).astype(BF16)


def _inproj_lat_kernel(x_ref, ada_ref, g_ref, w_ref, wkv_ref, o_ref, kv_ref, h_s):
    b = pl.program_id(0)

    @pl.when(pl.program_id(1) == 0)
    def _():
        h_s[...] = _modulated_norm(x_ref[...], g_ref, ada_ref, 1 + b)
        kv_ref[...] = _dot_nt(h_s[...], wkv_ref[...])

    o_ref[...] = _dot_nt(h_s[...], w_ref[...])


def _inproj_lat_call(x, ada, norm_g, w_main, w_kvc, li):
    tm = DEC_SEQ
    return pl.pallas_call(
        _inproj_lat_kernel,
        grid=(DEC_BATCH, PROJ_COLS // PROJ_TN),
        in_specs=[
            pl.BlockSpec((tm, D_MODEL), lambda i, j: (i, 0)),
            pl.BlockSpec((None, 8, 3 * D_MODEL), lambda i, j: (li, 0, 0)),
            pl.BlockSpec((None, 1, D_MODEL), lambda i, j: (li, 0, 0)),
            pl.BlockSpec((None, PROJ_TN, D_MODEL), lambda i, j: (li, j, 0)),
            pl.BlockSpec((None, KVC_COLS, D_MODEL), lambda i, j: (li, 0, 0)),
        ],
        out_specs=[pl.BlockSpec((tm, PROJ_TN), lambda i, j: (i, j)),
                   pl.BlockSpec((tm, KVC_COLS), lambda i, j: (i, 0))],
        out_shape=[jax.ShapeDtypeStruct((DEC_BATCH * tm, PROJ_COLS), F32),
                   jax.ShapeDtypeStruct((DEC_BATCH * tm, KVC_COLS), F32)],
        scratch_shapes=[pltpu.VMEM((tm, D_MODEL), BF16)],
        compiler_params=_params("arbitrary", "arbitrary"),
        name="inproj_lat",
    )(x, ada, norm_g, w_main, w_kvc)


def _diff_lambda_in_kernel(lam_ref, lam_init):
    v = lam_ref[...]
    l1 = jnp.sum(v[0:1] * v[1:2], axis=-1, keepdims=True)
    l2 = jnp.sum(v[2:3] * v[3:4], axis=-1, keepdims=True)
    return jnp.exp(l1) - jnp.exp(l2) + lam_init


def _split_halves(x, scale):
    lo = _lane(x.shape) < HALF
    xs = x * (scale * LOG2E)
    return jnp.concatenate([jnp.where(lo, xs, 0.0), jnp.where(lo, 0.0, xs)], axis=0).astype(BF16)


def _diff_combine(o2, rsum, lam, t):
    return o2[:t] * rsum[:t] - (lam * rsum[t:]) * o2[t:]


def _diff_head_post(o, subln_g, lam_init, gate):
    o = o * lax.rsqrt(jnp.mean(o * o, axis=-1, keepdims=True) + EPS) * (subln_g * (1.0 - lam_init))
    return (o * _silu(gate)).astype(BF16)


def _attn_a_ctx_kernel(q_ref, k_ref, v_ref, g_ref, lam_ref, sg_ref, o_ref, *, lam_init):
    t = SEQ
    lam = _diff_lambda_in_kernel(lam_ref, lam_init)
    ones = jnp.ones((t, LANES), BF16)
    blocks = [(b, h) for b in range(q_ref.shape[0] // t) for h in range(H_A)]
    where = lambda b, h: (slice(b * t, (b + 1) * t), slice(h * LANES, (h + 1) * LANES))

    def scores(b, h):
        qq = _split_halves(q_ref[where(b, h)], DH_A ** -0.5)
        return _dot_nt(qq, k_ref[where(b, h)].astype(BF16))

    pending = [scores(*blk) for blk in blocks[:ATTN_AHEAD]]
    for n, blk in enumerate(blocks):
        s = pending.pop(0)
        if n + ATTN_AHEAD < len(blocks):
            pending.append(scores(*blocks[n + ATTN_AHEAD]))
        e = jnp.exp2(s - jnp.max(s, axis=-1, keepdims=True)).astype(BF16)
        rsum = 1.0 / _dot(e, ones)
        o = _diff_combine(_dot(e, v_ref[where(*blk)].astype(BF16)), rsum, lam, t)
        o_ref[where(*blk)] = _diff_head_post(o, sg_ref[...], lam_init, g_ref[where(*blk)])


def _attn_a_ctx_call(proj, lamvec, subln_g, lam_init):
    rows = CTX_BATCHES_PER_STEP * SEQ
    blk = lambda c: pl.BlockSpec((rows, W_A), lambda b: (b, (c - SSD_COLS) // W_A))
    return pl.pallas_call(
        functools.partial(_attn_a_ctx_kernel, lam_init=lam_init),
        grid=(BATCH // CTX_BATCHES_PER_STEP,),
        in_specs=[blk(COL_QA), blk(COL_KA), blk(COL_VA), blk(COL_GA),
                  pl.BlockSpec((4, LANES), lambda b: (0, 0)),
                  pl.BlockSpec((1, LANES), lambda b: (0, 0))],
        out_specs=pl.BlockSpec((rows, W_A), lambda b: (b, 0)),
        out_shape=jax.ShapeDtypeStruct((BATCH * SEQ, W_A), BF16),
        compiler_params=_params("arbitrary"),
        name="attn_a_ctx",
    )(proj, proj, proj, proj, lamvec, subln_g)


def _rope(x, cos, sin_signed):
    first = (_lane(x.shape) % 32) < 16
    swapped = jnp.where(first, pltpu.roll(x, LANES - 16, 1), pltpu.roll(x, 16, 1))
    return x * cos + swapped * sin_signed


def _attn_a_lat_kernel(q_ref, k_ref, v_ref, g_ref, ck_ref, cv_ref, cosq_ref, sinq_ref, cosk_ref, sink_ref,
                       lam_ref, sg_ref, o_ref, kr_s, *, lam_init):
    tq = q_ref.shape[0]

    @pl.when(pl.program_id(1) == 0)
    def _():
        for h in range(H_A):
            sl = slice(h * LANES, (h + 1) * LANES)
            kr_s[:, sl] = _rope(k_ref[:, sl], cosk_ref[...], sink_ref[...]).astype(BF16)

    lam = _diff_lambda_in_kernel(lam_ref, lam_init)

    def scores(h):
        sl = slice(h * LANES, (h + 1) * LANES)
        qq = _split_halves(_rope(q_ref[:, sl], cosq_ref[...], sinq_ref[...]), DH_A ** -0.5)
        return _dot_nt(qq, kr_s[:, sl]), _dot_nt(qq, ck_ref[:, sl].astype(BF16))

    pending = [scores(h) for h in range(ATTN_AHEAD)]
    for h in range(H_A):
        sl = slice(h * LANES, (h + 1) * LANES)
        s_lat, s_ctx = pending.pop(0)
        if h + ATTN_AHEAD < H_A:
            pending.append(scores(h + ATTN_AHEAD))
        m = jnp.maximum(jnp.max(s_lat, axis=-1, keepdims=True), jnp.max(s_ctx, axis=-1, keepdims=True))
        e_lat = jnp.exp2(s_lat - m)
        e_ctx = jnp.exp2(s_ctx - m)
        rsum = 1.0 / (jnp.sum(e_lat, axis=-1, keepdims=True) + jnp.sum(e_ctx, axis=-1, keepdims=True))
        o2 = _dot(e_lat.astype(BF16), v_ref[:, sl].astype(BF16)) + _dot(e_ctx.astype(BF16),
                                                                         cv_ref[:, sl].astype(BF16))
        o = _diff_combine(o2, rsum, lam, tq)
        o_ref[:, sl] = _diff_head_post(o, sg_ref[...], lam_init, g_ref[:, sl])


def _attn_a_lat_call(proj, cache_k, cache_v, li, cos_t, sin_t, lamvec, subln_g, lam_init):
    tq = 256
    nq = DEC_SEQ // tq
    qblk = lambda c: pl.BlockSpec((tq, W_A), lambda b, i: (b * nq + i, c // W_A))
    full = lambda c: pl.BlockSpec((DEC_SEQ, W_A), lambda b, i: (b, c // W_A))
    cache = pl.BlockSpec((None, None, PAST_LEN, W_A), lambda b, i: (b, li, 0, 0))
    return pl.pallas_call(
        functools.partial(_attn_a_lat_kernel, lam_init=lam_init),
        grid=(DEC_BATCH, nq),
        in_specs=[qblk(COL_QA), full(COL_KA), full(COL_VA), qblk(COL_GA), cache, cache,
                  pl.BlockSpec((tq, LANES), lambda b, i: (i, 0)),
                  pl.BlockSpec((tq, LANES), lambda b, i: (i, 0)),
                  pl.BlockSpec((DEC_SEQ, LANES), lambda b, i: (0, 0)),
                  pl.BlockSpec((DEC_SEQ, LANES), lambda b, i: (0, 0)),
                  pl.BlockSpec((4, LANES), lambda b, i: (0, 0)),
                  pl.BlockSpec((1, LANES), lambda b, i: (0, 0))],
        out_specs=pl.BlockSpec((tq, W_A), lambda b, i: (b * nq + i, 0)),
        out_shape=jax.ShapeDtypeStruct((DEC_BATCH * DEC_SEQ, W_A), BF16),
        scratch_shapes=[pltpu.VMEM((DEC_SEQ, W_A), BF16)],
        compiler_params=_params("arbitrary", "arbitrary"),
        name="attn_a_lat",
    )(proj, proj, proj, proj, cache_k, cache_v, cos_t, sin_t, cos_t, sin_t, lamvec, subln_g)


def _merge_halves(o, t):
    return jnp.where(_lane((t, LANES)) < HALF, o[:t], o[t:])


def _attn_c_ctx_kernel(q_ref, kt_ref, vt_ref, g_ref, o_ref):
    t = SEQ
    blocks = [(b, j) for b in range(q_ref.shape[0] // t) for j in range(H_C // 2)]
    where = lambda b, j: (slice(b * t, (b + 1) * t), slice(j * LANES, (j + 1) * LANES))
    pair_t = lambda ref, b, j: ref[b, 2 * j:2 * j + 2].reshape(LANES, t).astype(BF16)

    def scores(b, j):
        return _dot(_split_halves(q_ref[where(b, j)], DH_C ** -0.5), pair_t(kt_ref, b, j))

    pending = [scores(*blk) for blk in blocks[:ATTN_AHEAD]]
    for n, blk in enumerate(blocks):
        s = pending.pop(0)
        if n + ATTN_AHEAD < len(blocks):
            pending.append(scores(*blocks[n + ATTN_AHEAD]))
        e = jnp.exp2(s - jnp.max(s, axis=-1, keepdims=True))
        rsum = 1.0 / jnp.sum(e, axis=-1, keepdims=True)
        o = _merge_halves(_dot_nt(e.astype(BF16), pair_t(vt_ref, *blk)) * rsum, t)
        o_ref[where(*blk)] = (o * _silu(g_ref[where(*blk)])).astype(BF16)


def _attn_c_ctx_call(proj, kc_t, vc_t, li):
    nb = CTX_BATCHES_PER_STEP
    blk = lambda c: pl.BlockSpec((nb * SEQ, W_C), lambda b: (b, (c - SSD_COLS) // W_C))
    cache = pl.BlockSpec((nb, None, H_C, DH_C, SEQ), lambda b: (b, li, 0, 0, 0))
    return pl.pallas_call(
        _attn_c_ctx_kernel,
        grid=(BATCH // nb,),
        in_specs=[blk(COL_QC), cache, cache, blk(COL_GC)],
        out_specs=pl.BlockSpec((nb * SEQ, W_C), lambda b: (b, 0)),
        out_shape=jax.ShapeDtypeStruct((BATCH * SEQ, W_C), BF16),
        compiler_params=_params("arbitrary"),
        name="attn_c_ctx",
    )(proj, kc_t, vc_t, proj)


def _rpb_kernel(rpb_ref, o_ref):
    shape = (GRID_W, LANES)
    c = lax.broadcasted_iota(jnp.int32, shape, 0)
    cp = _lane(shape) % GRID_W
    start = jnp.clip(c - NA_KW // 2, 0, GRID_W - NA_KW)
    in_win = (cp >= start) & (cp < start + NA_KW)
    o_ref[0] = jnp.full(shape, NEG_INF, F32)
    for dr in range(2 * NA_KH - 1):
        row = jnp.broadcast_to(rpb_ref[dr:dr + 1, :], shape)
        tile = pltpu.roll(row, LANES - (NA_KW - 1), 1, stride=1, stride_axis=0)
        o_ref[1 + dr] = jnp.where(in_win, tile * LOG2E, NEG_INF)


def _rpb_call(rpb):
    n_dc = 2 * NA_KW - 1
    v = jnp.pad(rpb, ((0, 0), (0, 0), (0, NA_TILES - (2 * NA_KH - 1)), (0, GRID_W - n_dc)))
    v = jnp.concatenate([v] * (LANES // GRID_W), axis=-1)
    return pl.pallas_call(
        _rpb_kernel,
        grid=(DEPTH, H_C),
        in_specs=[pl.BlockSpec((None, None, NA_TILES, LANES), lambda l, h: (l, h, 0, 0))],
        out_specs=pl.BlockSpec((None, None, NA_TILES, GRID_W, LANES), lambda l, h: (l, h, 0, 0, 0)),
        out_shape=jax.ShapeDtypeStruct((DEPTH, H_C, NA_TILES, GRID_W, LANES), F32),
        compiler_params=_params("arbitrary", "arbitrary"),
        name="rpb_tiles",
    )(v)


def _attn_c_lat_kernel(q_ref, k_ref, v_ref, g_ref, ck_ref, cv_ref, tile_ref, o_ref, bias_s):
    tq = q_ref.shape[0]
    nwin = NA_WROWS * GRID_W
    m = pl.program_id(1)
    w0 = jnp.where(m < (GRID_ROWS // NA_QROWS) // 2, 0, GRID_ROWS - NA_WROWS)
    k0 = pl.multiple_of(w0 * GRID_W, GRID_W)
    lo = _lane((GRID_W, LANES)) < HALF
    n_pair = H_C // 2

    def scores(j):
        sl = slice(j * LANES, (j + 1) * LANES)
        for s in range(2):
            for i in range(NA_QROWS):
                r = m * NA_QROWS + i
                start = jnp.clip(r - NA_KH // 2, 0, GRID_ROWS - NA_KH)
                for jp in range(NA_WROWS // 2):
                    idx = []
                    for u in range(2):
                        rk = w0 + 2 * jp + u
                        valid = (rk >= start) & (rk < start + NA_KH)
                        idx.append(jnp.where(valid, rk - r + NA_KH, 0))
                    tile = jnp.where(lo, tile_ref[2 * j + s, idx[0]], tile_ref[2 * j + s, idx[1]])
                    bias_s[(s * NA_QROWS + i) * GRID_W:(s * NA_QROWS + i + 1) * GRID_W,
                           jp * LANES:(jp + 1) * LANES] = tile
        qq = _split_halves(q_ref[:, sl], DH_C ** -0.5)
        kw = k_ref[pl.ds(k0, nwin), sl].astype(BF16)
        s_win = _dot_nt(qq, kw) + bias_s[...]
        ckt = ck_ref[2 * j:2 * j + 2].reshape(LANES, PAST_LEN).astype(BF16)
        return s_win, _dot(qq, ckt)

    pending = [scores(j) for j in range(ATTN_AHEAD)]
    for j in range(n_pair):
        sl = slice(j * LANES, (j + 1) * LANES)
        s_win, s_ctx = pending.pop(0)
        if j + ATTN_AHEAD < n_pair:
            pending.append(scores(j + ATTN_AHEAD))
        vw = v_ref[pl.ds(k0, nwin), sl].astype(BF16)
        mx = jnp.maximum(jnp.max(s_win, axis=-1, keepdims=True), jnp.max(s_ctx, axis=-1, keepdims=True))
        e_win = jnp.exp2(s_win - mx)
        e_ctx = jnp.exp2(s_ctx - mx)
        rs = 1.0 / (jnp.sum(e_win, axis=-1, keepdims=True) + jnp.sum(e_ctx, axis=-1, keepdims=True))
        cvt = cv_ref[2 * j:2 * j + 2].reshape(LANES, PAST_LEN).astype(BF16)
        o = (_dot(e_win.astype(BF16), vw) + _dot_nt(e_ctx.astype(BF16), cvt)) * rs
        o_ref[:, sl] = (_merge_halves(o, tq) * _silu(g_ref[:, sl])).astype(BF16)


def _attn_c_lat_call(proj, kv, cache_k, cache_v, li, tiles):
    tq = NA_QROWS * GRID_W
    nq = DEC_SEQ // tq
    qblk = lambda c: pl.BlockSpec((tq, W_C), lambda b, i: (b * nq + i, c // W_C))
    full = lambda c: pl.BlockSpec((DEC_SEQ, W_C), lambda b, i: (b, c // W_C))
    cache = pl.BlockSpec((None, None, H_C, DH_C, PAST_LEN), lambda b, i: (b, li, 0, 0, 0))
    return pl.pallas_call(
        _attn_c_lat_kernel,
        grid=(DEC_BATCH, nq),
        in_specs=[qblk(COL_QC), full(0), full(W_C), qblk(COL_GC), cache, cache,
                  pl.BlockSpec((None, H_C, NA_TILES, GRID_W, LANES), lambda b, i: (li, 0, 0, 0, 0))],
        out_specs=pl.BlockSpec((tq, W_C), lambda b, i: (b * nq + i, 0)),
        out_shape=jax.ShapeDtypeStruct((DEC_BATCH * DEC_SEQ, W_C), BF16),
        scratch_shapes=[pltpu.VMEM((2 * tq, NA_WROWS * GRID_W), F32)],
        compiler_params=_params("arbitrary", "arbitrary"),
        name="attn_c_lat",
    )(proj, kv, kv, proj, cache_k, cache_v, tiles)


def _ssd_kernel(*refs, seq, use_h0):
    refs = list(refs)
    dt_ref, xs_ref, bc_ref, z_ref = refs[:4]
    pos = 4
    h0_ref = None
    if use_h0:
        h0_ref = refs[pos]
        pos += 1
    _ssd_body(dt_ref, xs_ref, bc_ref, z_ref, h0_ref, refs[pos:pos + 6], refs[pos + 6], None, refs[pos + 7:],
              seq=seq, static_loops=False, side=lambda: None)


def _ssd_body(dt_ref, xs_ref, bc_ref, z_ref, h0_ref, params, y_ref, hs_ref, scratch, *, seq, static_loops, side):
    cw_ref, cb_ref, dtb_ref, alog_ref, dsk_ref, g_ref = params
    upad_s, xc_s, expo_s, expot_s, dtt_s, tot_s, bmt_s, yf_s, yb_s, st_s = scratch
    use_h0 = h0_ref is not None
    want_state = hs_ref is not None

    q = SSD_CHUNK
    nc = seq // q
    n_pair = H_B // 2
    n_hd = 2 * H_B
    pad = 8

    def loop(body, unroll=1):
        if static_loops:
            for c in range(nc):
                body(c, 0)
        else:
            lax.fori_loop(0, nc, body, 0, unroll=unroll)

    def chunk_rows(c):
        return slice(c * q, (c + 1) * q) if isinstance(c, int) else pl.ds(pl.multiple_of(c * q, q), q)

    upad_s[0:pad, :] = jnp.zeros((pad, CONV_DIM), F32)
    upad_s[pad + seq:2 * pad + seq, :] = jnp.zeros((pad, CONV_DIM), F32)
    upad_s[pad:pad + seq, 0:DI_B] = xs_ref[...]
    upad_s[pad:pad + seq, DI_B:CONV_DIM] = bc_ref[...]

    for c in range(nc):
        for cb_ in range(CONV_DIM // LANES):
            csl = slice(cb_ * LANES, (cb_ + 1) * LANES)
            acc = jnp.zeros((q, LANES), F32) + cb_ref[:, csl]
            for k in range(CONV_K):
                r0 = c * q + pad - CONV_K // 2 + k
                acc = acc + upad_s[r0:r0 + q, csl] * cw_ref[k:k + 1, csl]
            xc_s[c * q:(c + 1) * q, csl] = _silu(acc)
            side()

    a_row = -jnp.exp(alog_ref[...]) * LOG2E
    a_col = jnp.broadcast_to(a_row, (LANES, LANES)).T[0:n_hd, 0:1]
    ri = lax.broadcasted_iota(jnp.int32, (q, q), 0)
    ci = lax.broadcasted_iota(jnp.int32, (q, q), 1)
    ltri = (ri >= ci).astype(F32)
    fwd_lane = _lane((q, LANES)) < H_B
    fwd_row = lax.broadcasted_iota(jnp.int32, (n_hd, q), 0) < H_B

    def prep_body(c, carry):
        rows = chunk_rows(c)
        xdt = dt_ref[rows, 0:LANES] + dtb_ref[...]
        dtv = jnp.maximum(xdt, 0.0) + jnp.log1p(jnp.exp(-jnp.abs(xdt)))
        la = dtv * a_row
        acum = _dot(ltri, la, HI)
        expo_s[rows, :] = jnp.where(fwd_lane, acum, la - acum)
        acum_t = acum.T[0:n_hd, :]
        dt_t = dtv.T[0:n_hd, :]
        expot_s[c] = jnp.where(fwd_row, acum_t, dt_t * a_col - acum_t)
        dtt_s[c] = dt_t
        tot_s[c] = jnp.broadcast_to(acum_t[:, q - 1:q], (n_hd, q))
        bmt_s[c] = xc_s[rows, DI_B:DI_B + LANES].T
        side()
        return carry

    loop(prep_body, unroll=2)

    if use_h0:
        st_s[...] = h0_ref[...].reshape(2, n_pair, N_B, LANES)
    else:
        st_s[...] = jnp.zeros_like(st_s)

    lane_q = _lane((q, LANES))
    lo = lane_q < HALF
    lo_st = _lane((N_B, LANES)) < HALF

    def chunk(dirn, c, y_s):
        rows = chunk_rows(c)
        tri = (ri >= ci) if dirn == 0 else (ci >= ri)
        bm16 = xc_s[rows, DI_B:DI_B + LANES].astype(BF16)
        cm = xc_s[rows, DI_B + LANES:DI_B + 2 * LANES]
        for g in range(G_B):
            in_g = (lane_q >= g * N_B) & (lane_q < (g + 1) * N_B)
            cmg = jnp.where(in_g, cm, 0.0).astype(BF16)
            cb = _dot_nt(cmg, bm16)
            bmt_g = bmt_s[c, g * N_B:(g + 1) * N_B, :]
            for k in range(g * n_pair // G_B, (g + 1) * n_pair // G_B):
                psl = slice(k * LANES, (k + 1) * LANES)
                x16 = xc_s[rows, psl].astype(BF16)
                mats, lhs, ysc, cdec = [], [], [], []
                for s in range(2):
                    col = dirn * H_B + 2 * k + s
                    e_col = jnp.broadcast_to(expo_s[rows, col:col + 1], (q, q))
                    e_row = expot_s[c, col:col + 1, :]
                    dt_row = dtt_s[c, col:col + 1, :]
                    tot = tot_s[c, col:col + 1, :]
                    dec = jnp.exp2(jnp.where(tri, e_col - e_row, NEG_INF))
                    mats.append((cb * dec * dt_row).astype(BF16))
                    if dirn == 0:
                        ysc.append(jnp.exp2(e_col))
                        w_row = jnp.exp2(tot - e_row)
                    else:
                        ysc.append(jnp.exp2(e_col + tot))
                        w_row = jnp.exp2(-e_row)
                    lhs.append((bmt_g * (w_row * dt_row)).astype(BF16))
                    cdec.append(jnp.exp2(tot[:, 0:LANES]))
                yd = _dot(jnp.concatenate(mats, axis=0), x16)
                st = st_s[dirn, k]
                y_off = _dot(cmg, jnp.concatenate([st, st], axis=0).astype(BF16))
                y_s[rows, psl] = jnp.where(lo, yd[:q] + ysc[0] * y_off, yd[q:] + ysc[1] * y_off)
                ds = _dot(jnp.concatenate(lhs, axis=0), x16)
                st_s[dirn, k] = jnp.where(lo_st, cdec[0] * st + ds[:N_B], cdec[1] * st + ds[N_B:])
                side()

    def body(c, carry):
        chunk(0, c, yf_s)
        chunk(1, nc - 1 - c, yb_s)
        return carry

    loop(body, unroll=2)

    dsum = dsk_ref[0:1, :] + dsk_ref[1:2, :]

    def out_body(c, carry):
        rows = chunk_rows(c)
        y = yf_s[rows, :] + yb_s[rows, :] + xc_s[rows, 0:DI_B] * dsum
        y = y * _silu(z_ref[rows, :])
        y = y * lax.rsqrt(jnp.mean(y * y, axis=-1, keepdims=True) + EPS) * g_ref[...]
        y_ref[rows, :] = y.astype(BF16)
        side()
        return carry

    loop(out_body)
    if want_state:
        for dirn in range(2):
            for k in range(n_pair):
                st = st_s[dirn, k]
                st_t = jnp.concatenate([st, st], axis=0).T
                for s in range(2):
                    hs_ref[dirn, 2 * k + s] = st_t[s * P_B:(s + 1) * P_B, 0:N_B]


def _ssd_scratch(seq):
    nc = seq // SSD_CHUNK
    per_chunk_rows = pltpu.VMEM((nc, 2 * H_B, SSD_CHUNK), F32)
    return [pltpu.VMEM((seq + 16, CONV_DIM), F32), pltpu.VMEM((seq, CONV_DIM), F32),
            pltpu.VMEM((seq, LANES), F32), per_chunk_rows, per_chunk_rows, per_chunk_rows,
            pltpu.VMEM((nc, LANES, SSD_CHUNK), F32),
            pltpu.VMEM((seq, DI_B), F32), pltpu.VMEM((seq, DI_B), F32),
            pltpu.VMEM((2, H_B // 2, N_B, LANES), F32)]


def _ssd_param_specs(const):
    return [const((CONV_K, CONV_DIM)), const((1, CONV_DIM)), const((1, LANES)), const((1, LANES)),
            const((2, DI_B)), const((1, DI_B))]


def _ssd_lat_call(proj, h0t, li, ssd_w):
    seq = DEC_SEQ
    in_specs = [pl.BlockSpec((seq, DT_PAD), lambda b: (b, COL_DT // DT_PAD)),
                pl.BlockSpec((seq, DI_B), lambda b: (b, COL_XS // DI_B)),
                pl.BlockSpec((seq, BC_DIM), lambda b: (b, COL_BC // BC_DIM)),
                pl.BlockSpec((seq, DI_B), lambda b: (b, COL_Z // DI_B)),
                pl.BlockSpec((None, None, 2, (H_B // 2) * N_B, LANES), lambda b: (b, li, 0, 0, 0))]
    in_specs += _ssd_param_specs(lambda shape: pl.BlockSpec(shape, lambda b: (0,) * len(shape)))
    return pl.pallas_call(
        functools.partial(_ssd_kernel, seq=seq, use_h0=True),
        grid=(DEC_BATCH,),
        in_specs=in_specs,
        out_specs=pl.BlockSpec((seq, DI_B), lambda b: (b, 0)),
        out_shape=jax.ShapeDtypeStruct((DEC_BATCH * seq, DI_B), BF16),
        scratch_shapes=_ssd_scratch(seq),
        compiler_params=_params("arbitrary"),
        name="ssd_lat",
    )(proj, proj, proj, proj, h0t, *ssd_w)


def _ctx_kernel(*refs, nb, n_carry):
    x_ref, ada_ref, g_ref, w_ref, wkv_ref = refs[:5]
    params = refs[5:11]
    outs = refs[11 + n_carry:]
    proj_ref, ka_ref, va_ref, kc_ref, vc_ref, y_ref, hs_ref, h_s, p_s = outs[:9]
    scratch = outs[9:]
    seq = SEQ

    h_s[...] = _modulated_norm(x_ref[...], g_ref, ada_ref, 0)
    p_s[...] = _dot_nt(h_s[...], w_ref[0:SSD_COLS, :])

    work = []
    for c0 in range(SSD_COLS, PROJ_COLS, SIDE_TN):
        def tile(c0=c0):
            proj_ref[:, c0 - SSD_COLS:c0 - SSD_COLS + SIDE_TN] = _dot_nt(h_s[...], w_ref[c0:c0 + SIDE_TN, :])
        work.append(tile)
        for dst, col in ((ka_ref, COL_KA), (va_ref, COL_VA)):
            if col <= c0 < col + W_A and (c0 + SIDE_TN - col) % LANES == 0:
                def store(dst=dst, col=col, c0=c0):
                    for b in range(nb):
                        for h in range((c0 - col) // LANES, (c0 + SIDE_TN - col) // LANES):
                            src = col - SSD_COLS + h * LANES
                            dst[b, :, h, :] = proj_ref[b * seq:(b + 1) * seq, src:src + LANES]
                work.append(store)
    for b in range(nb):
        for r0 in range(0, KVC_COLS, SIDE_TN):
            def tile_t(b=b, r0=r0):
                kv_t = _dot_nt(wkv_ref[r0:r0 + SIDE_TN, :], h_s[b * seq:(b + 1) * seq, :])
                dst, d0 = (kc_ref, r0) if r0 < W_C else (vc_ref, r0 - W_C)
                dst[b, d0 // DH_C:(d0 + SIDE_TN) // DH_C] = kv_t.reshape(SIDE_TN // DH_C, DH_C, seq)
            work.append(tile_t)

    n_slots = nb * (seq // SSD_CHUNK) * (CONV_DIM // LANES + 2 + 2 * (H_B // 2))
    state = dict(slot=0, done=0)

    def side():
        state["slot"] += 1
        target = min(len(work), -(-state["slot"] * len(work) // n_slots))
        while state["done"] < target:
            work[state["done"]]()
            state["done"] += 1

    for b in range(nb):
        rows = pl.ds(b * seq, seq)
        _ssd_body(p_s.at[rows, pl.ds(COL_DT, DT_PAD)], p_s.at[rows, pl.ds(COL_XS, DI_B)],
                  p_s.at[rows, pl.ds(COL_BC, BC_DIM)], p_s.at[rows, pl.ds(COL_Z, DI_B)], None, params,
                  y_ref.at[rows, :], hs_ref.at[b], scratch, seq=seq, static_loops=True, side=side)
    while state["done"] < len(work):
        work[state["done"]]()
        state["done"] += 1


def _ctx_call(x, ada, norm_g, w_main, w_kvc, ssd_w, li, carry):
    nb = CTX_BATCHES_PER_STEP
    tm = nb * SEQ
    n_carry = 0 if carry is None else len(carry)
    const = lambda shape: pl.BlockSpec(shape, lambda i: (0,) * len(shape))
    once = pl.Buffered(1)
    in_specs = [pl.BlockSpec((tm, D_MODEL), lambda i: (i, 0)),
                pl.BlockSpec((None, 8, 3 * D_MODEL), lambda i: (li, 0, 0)),
                pl.BlockSpec((None, 1, D_MODEL), lambda i: (li, 0, 0)),
                pl.BlockSpec((None, PROJ_COLS, D_MODEL), lambda i: (li, 0, 0), pipeline_mode=once),
                pl.BlockSpec((None, KVC_COLS, D_MODEL), lambda i: (li, 0, 0), pipeline_mode=once)]
    in_specs += _ssd_param_specs(const)
    args = [x, ada, norm_g, w_main, w_kvc, *ssd_w]
    out_specs = [pl.BlockSpec((tm, PROJ_COLS - SSD_COLS), lambda i: (i, 0))]
    out_specs += [pl.BlockSpec((nb, None, SEQ, H_A, 2 * DH_A), lambda i: (i, li, 0, 0, 0))] * 2
    out_specs += [pl.BlockSpec((nb, None, H_C, DH_C, SEQ), lambda i: (i, li, 0, 0, 0))] * 2
    out_specs += [pl.BlockSpec((tm, DI_B), lambda i: (i, 0)),
                  pl.BlockSpec((nb, None, 2, H_B, P_B, N_B), lambda i: (i, li, 0, 0, 0, 0))]
    out_shape = [jax.ShapeDtypeStruct((BATCH * SEQ, PROJ_COLS - SSD_COLS), F32)]
    out_shape += [jax.ShapeDtypeStruct((BATCH, DEPTH, SEQ, H_A, 2 * DH_A), F32)] * 2
    out_shape += [jax.ShapeDtypeStruct((BATCH, DEPTH, H_C, DH_C, SEQ), F32)] * 2
    out_shape += [jax.ShapeDtypeStruct((BATCH * SEQ, DI_B), BF16),
                  jax.ShapeDtypeStruct((BATCH, DEPTH, 2, H_B, P_B, N_B), F32)]
    aliases = {}
    if carry is not None:
        in_specs += [pl.BlockSpec(memory_space=pl.ANY)] * n_carry
        aliases = {len(args) + k: (1, 2, 3, 4, 6)[k] for k in range(n_carry)}
        args += list(carry)
    scratch = [pltpu.VMEM((tm, D_MODEL), BF16), pltpu.VMEM((tm, SSD_COLS), F32)] + _ssd_scratch(SEQ)
    return pl.pallas_call(
        functools.partial(_ctx_kernel, nb=nb, n_carry=n_carry),
        grid=(BATCH // nb,),
        in_specs=in_specs,
        out_specs=out_specs,
        out_shape=out_shape,
        input_output_aliases=aliases,
        scratch_shapes=scratch,
        compiler_params=_params("arbitrary"),
        name="ctx_inproj_ssd",
    )(*args)


def _post_kernel(x_ref, ya_ref, yb_ref, yc_ref, ada_ref, g_ref, wm_ref, wa_ref, wb_ref, wc_ref, wo_ref, fg_ref,
                 o_ref, *, tm, row_base, tokens_per_row, final):
    row = row_base + (pl.program_id(0) * tm) // tokens_per_row
    gate = ada_ref[pl.ds(row, 1), 2 * D_MODEL:3 * D_MODEL]
    d = D_MODEL
    x = x_ref[...]
    h = _modulated_norm(x, g_ref, ada_ref, row)
    merged = None
    for n, (y_ref, w_ref) in enumerate(((ya_ref, wa_ref), (yb_ref, wb_ref), (yc_ref, wc_ref))):
        logits = _dot_nt(h, wm_ref[n * d:(n + 1) * d, :])
        term = _sigmoid(logits) * _dot(y_ref[...], w_ref[...])
        merged = term if merged is None else merged + term
    x = x + gate * _dot(merged.astype(BF16), wo_ref[...])
    if final:
        x = x * lax.rsqrt(jnp.mean(x * x, axis=-1, keepdims=True) + EPS) * fg_ref[...]
    o_ref[...] = x


def _post_call(x, ya, yb, yc, ada, norm_g, w_merge_t, li, wa, wb, wc, wo, final_g, *, tm, row_base,
               tokens_per_row, final):
    t = x.shape[0]
    tok = lambda w: pl.BlockSpec((tm, w), lambda i: (i, 0))
    layer = lambda *shape: pl.BlockSpec((None,) + shape, lambda i: (li,) + (0,) * len(shape))
    kern = functools.partial(_post_kernel, tm=tm, row_base=row_base, tokens_per_row=tokens_per_row, final=final)
    return pl.pallas_call(
        kern,
        grid=(t // tm,),
        in_specs=[tok(D_MODEL), tok(W_A), tok(DI_B), tok(W_C),
                  layer(8, 3 * D_MODEL), layer(1, D_MODEL), layer(MERGE_COLS, D_MODEL),
                  layer(W_A, D_MODEL), layer(DI_B, D_MODEL), layer(W_C, D_MODEL), layer(D_MODEL, D_MODEL),
                  pl.BlockSpec((1, D_MODEL), lambda i: (0, 0))],
        out_specs=tok(D_MODEL),
        out_shape=jax.ShapeDtypeStruct((t, D_MODEL), F32),
        compiler_params=_params("arbitrary"),
        name="post_final" if final else "post",
    )(x, ya, yb, yc, ada, norm_g, w_merge_t, wa, wb, wc, wo, final_g)


def _rope_tables():
    pos = np.arange(DEC_SEQ)
    lane = np.arange(LANES)
    l64 = lane % (2 * (DH_A // 2))
    quarter = DH_A // 4
    p = np.where((l64 < DH_A // 2)[None, :], (pos // GRID_W)[:, None], (pos % GRID_W)[:, None])
    inv = ROPE_BASE ** (-np.arange(quarter, dtype=np.float64) / quarter)
    ang = p.astype(np.float64) * inv[l64 % quarter][None, :]
    sign = np.where((lane % (2 * quarter)) < quarter, -1.0, 1.0)
    return jnp.asarray(np.cos(ang), F32), jnp.asarray(np.sin(ang) * sign[None, :], F32)


def _pad_lanes(v, width=LANES):
    v = v.reshape(1, -1).astype(F32)
    return jnp.pad(v, ((0, 0), (0, width - v.shape[1])))


def kernel(x_prompt, x_sample, cache_diff_k, cache_diff_v, cache_na_k, cache_na_v, state_ssd, c, c_ctx,
           norm_g, w_ada, b_ada, w_in, lam_q1, lam_k1, lam_q2, lam_k2, diff_subln_g, conv_w, conv_b,
           dt_bias, a_log, d_skip, ssd_norm_g, na_rpb, w_br_a, w_br_b, w_br_c, w_out, final_g):
    assert x_prompt.shape == (BATCH, SEQ, D_MODEL) and x_sample.shape == (DEC_BATCH, DEC_SEQ, D_MODEL)
    assert w_in.shape == (DEPTH, D_MODEL, _SRC["merge"] + MERGE_COLS)
    w_t = jnp.swapaxes(w_in, 1, 2)
    rows = lambda s0, n: w_t[:, s0:s0 + n, :].astype(BF16)
    w_main = jnp.concatenate([jnp.zeros((DEPTH, n, D_MODEL), BF16) if s0 is None else rows(s0, n)
                              for s0, n in MAIN_SEGMENTS], axis=1)
    assert w_main.shape == (DEPTH, PROJ_COLS, D_MODEL)
    w_kvc = rows(_SRC["kc"], KVC_COLS)
    w_merge = rows(_SRC["merge"], MERGE_COLS)
    wa16, wb16, wc16, wo16 = (w.astype(BF16) for w in (w_br_a, w_br_b, w_br_c, w_out))

    cvecs = jnp.concatenate([c_ctx[None, :], c, jnp.zeros((8 - 1 - DEC_BATCH, D_MODEL), F32)], axis=0)
    ada = _ada_call(cvecs, w_ada, b_ada)
    cos_t, sin_t = _rope_tables()

    ck_a = cache_diff_k.reshape(DEC_BATCH, DEPTH, PAST_LEN, W_A)
    cv_a = cache_diff_v.reshape(DEC_BATCH, DEPTH, PAST_LEN, W_A)
    ck_c = cache_na_k.transpose(0, 1, 3, 4, 2)
    cv_c = cache_na_v.transpose(0, 1, 3, 4, 2)
    na_tiles = _rpb_call(na_rpb)
    h0t = state_ssd.transpose(0, 1, 2, 5, 3, 4).reshape(DEC_BATCH, DEPTH, 2, N_B, DI_B)
    h0t = h0t.reshape(DEC_BATCH, DEPTH, 2, N_B, H_B // 2, LANES).transpose(0, 1, 2, 4, 3, 5)
    h0t = h0t.reshape(DEC_BATCH, DEPTH, 2, (H_B // 2) * N_B, LANES)

    xp = x_prompt.reshape(BATCH * SEQ, D_MODEL)
    xs = x_sample.reshape(DEC_BATCH * DEC_SEQ, D_MODEL)
    fg = final_g.reshape(1, D_MODEL)
    norm_g3 = norm_g.reshape(DEPTH, 1, D_MODEL)
    carry = None
    for li in range(DEPTH):
        lam_init = 0.8 - 0.6 * math.exp(-0.3 * li)
        final = li == DEPTH - 1
        lamvec = jnp.concatenate([_pad_lanes(v[li]) for v in (lam_q1, lam_k1, lam_q2, lam_k2)], axis=0)
        subln = diff_subln_g[li].reshape(1, LANES)
        dtb = _pad_lanes(dt_bias[li])
        alog = _pad_lanes(a_log[li])
        dskx = jnp.repeat(d_skip[li], P_B, axis=-1)
        ssd_w = (conv_w[li], conv_b[li].reshape(1, CONV_DIM), dtb, alog, dskx, ssd_norm_g[li].reshape(1, DI_B))
        post_w = (wa16, wb16, wc16, wo16, fg)

        proj, ka, va, kc_t, vc_t, yb, ssd_state = _ctx_call(xp, ada, norm_g3, w_main, w_kvc, ssd_w, li, carry)
        carry = (ka, va, kc_t, vc_t, ssd_state)
        ya = _attn_a_ctx_call(proj, lamvec, subln, lam_init)
        yc = _attn_c_ctx_call(proj, kc_t, vc_t, li)
        xp = _post_call(xp, ya, yb, yc, ada, norm_g3, w_merge, li, *post_w, tm=512, row_base=0,
                        tokens_per_row=BATCH * SEQ, final=final)

        proj, kv = _inproj_lat_call(xs, ada, norm_g3, w_main, w_kvc, li)
        ya = _attn_a_lat_call(proj, ck_a, cv_a, li, cos_t, sin_t, lamvec, subln, lam_init)
        yb = _ssd_lat_call(proj, h0t, li, ssd_w)
        yc = _attn_c_lat_call(proj, kv, ck_c, cv_c, li, na_tiles)
        xs = _post_call(xs, ya, yb, yc, ada, norm_g3, w_merge, li, *post_w, tm=512, row_base=1,
                        tokens_per_row=DEC_SEQ, final=final)

    new_k_a, new_v_a, new_k_c_t, new_v_c_t, new_ssd = carry
    to_token_major = lambda a: a.transpose(0, 1, 4, 2, 3)
    return (xp.reshape(BATCH, SEQ, D_MODEL), xs.reshape(DEC_BATCH, DEC_SEQ, D_MODEL),
            new_k_a, new_v_a, to_token_major(new_k_c_t), to_token_major(new_v_c_t), new_ssd)
```

```python
import functools
import math

import jax
import jax.numpy as jnp
import numpy as np
from jax import lax
from jax.experimental import pallas as pl
from jax.experimental.pallas import tpu as pltpu

D_MODEL = 1024
BATCH = 32
SEQ = 256
DEPTH = 2
DEC_BATCH = 2
DEC_SEQ = 1024
PAST_LEN = 512
GRID_W = 64
GRID_ROWS = DEC_SEQ // GRID_W
H_A = 4
DH_A = 64
W_A = H_A * 2 * DH_A
H_B = 8
P_B = 64
G_B = 2
N_B = 64
DI_B = H_B * P_B
CONV_K = 5
CONV_DIM = DI_B + 2 * G_B * N_B
SSD_CHUNK = 128
H_C = 8
DH_C = 64
W_C = H_C * DH_C
NA_KH = 8
NA_KW = 16
N_BRANCH = 3
ROPE_BASE = 10000.0
EPS = 1e-6

LANES = 128
HALF = LANES // 2
VMEM_LIMIT = 56 * 1024 * 1024

BC_DIM = CONV_DIM - DI_B
KVC_COLS = 2 * W_C
MERGE_COLS = N_BRANCH * D_MODEL
_SRC = dict(qa=0, ka=512, va=1024, ga=1536, z=2048, xs=2560, bc=3072, dt=3328, qc=3344, kc=3856, vc=4368,
            gc=4880, merge=5392)
COL_QA = 0
COL_GA = 512
COL_QC = 1024
COL_GC = 1536
COL_KA = 2048
COL_VA = 2560
PROJ_COLS = 3072
PROJ_SEGMENTS = ((COL_QA, _SRC["qa"]), (COL_GA, _SRC["ga"]), (COL_QC, _SRC["qc"]), (COL_GC, _SRC["gc"]),
                 (COL_KA, _SRC["ka"]), (COL_VA, _SRC["va"]))
P_Z = 0
P_XS = 512
P_BC = 1024
P_DT = 1280
P_COLS = P_DT + LANES
SIDE_TN = 256

NA_QROWS = 4
NA_WROWS = 12
NA_TILES = 2 * NA_KH
NEG_INF = float("-inf")
LOG2E = math.log2(math.e)
ATTN_AHEAD = 2
CTX_BATCHES_PER_STEP = 2
HI = lax.Precision.HIGHEST
F32 = jnp.float32
BF16 = jnp.bfloat16


def _dot(a, b, precision=None):
    return jnp.dot(a, b, preferred_element_type=F32, precision=precision)


def _dot_nt(a, b):
    return lax.dot_general(a, b, (((1,), (1,)), ((), ())), preferred_element_type=F32)


def _sigmoid(x):
    return 1.0 / (1.0 + jnp.exp(-x))


def _silu(x):
    return x * _sigmoid(x)


def _lane(shape):
    return lax.broadcasted_iota(jnp.int32, shape, len(shape) - 1)


def _params(*sem):
    return pltpu.CompilerParams(dimension_semantics=sem, vmem_limit_bytes=VMEM_LIMIT)


def _ada_kernel(cv_ref, w_ref, b_ref, o_ref):
    o_ref[...] = _dot(_silu(cv_ref[...]), w_ref[...], HI) + b_ref[...]


def _ada_call(cvecs, w_ada, b_ada):
    tn = 512
    return pl.pallas_call(
        _ada_kernel,
        grid=(DEPTH, 3 * D_MODEL // tn),
        in_specs=[
            pl.BlockSpec((8, D_MODEL), lambda l, j: (0, 0)),
            pl.BlockSpec((None, D_MODEL, tn), lambda l, j: (l, 0, j)),
            pl.BlockSpec((None, 1, tn), lambda l, j: (l, 0, j)),
        ],
        out_specs=pl.BlockSpec((None, 8, tn), lambda l, j: (l, 0, j)),
        out_shape=jax.ShapeDtypeStruct((DEPTH, 8, 3 * D_MODEL), F32),
        compiler_params=_params("arbitrary", "arbitrary"),
        name="ada",
    )(cvecs, w_ada, b_ada.reshape(DEPTH, 1, 3 * D_MODEL))


def _modulated_norm(x, g_ref, ada_ref, row):
    y = x * lax.rsqrt(jnp.mean(x * x, axis=-1, keepdims=True) + EPS) * g_ref[...]
    shift = ada_ref[pl.ds(row, 1), 0:D_MODEL]
    scale = ada_ref[pl.ds(row, 1), D_MODEL:2 * D_MODEL]
    return (y * (1.0 + scale) + shift).astype(BF16)


def _diff_lambda_in_kernel(lam_ref, lam_init):
    v = lam_ref[...]
    l1 = jnp.sum(v[0:1] * v[1:2], axis=-1, keepdims=True)
    l2 = jnp.sum(v[2:3] * v[3:4], axis=-1, keepdims=True)
    return jnp.exp(l1) - jnp.exp(l2) + lam_init


def _split_halves(x, scale):
    lo = _lane(x.shape) < HALF
    xs = x * (scale * LOG2E)
    return jnp.concatenate([jnp.where(lo, xs, 0.0), jnp.where(lo, 0.0, xs)], axis=0).astype(BF16)


def _diff_combine(o2, rsum, lam, t):
    return o2[:t] * rsum[:t] - (lam * rsum[t:]) * o2[t:]


def _diff_head_post(o, subln_g, lam_init, gate):
    o = o * lax.rsqrt(jnp.mean(o * o, axis=-1, keepdims=True) + EPS) * (subln_g * (1.0 - lam_init))
    return (o * _silu(gate)).astype(BF16)


def _attn_a_ctx_kernel(q_ref, k_ref, v_ref, g_ref, lam_ref, sg_ref, o_ref, *, lam_init):
    t = SEQ
    lam = _diff_lambda_in_kernel(lam_ref, lam_init)
    ones = jnp.ones((t, LANES), BF16)
    blocks = [(b, h) for b in range(q_ref.shape[0] // t) for h in range(H_A)]
    where = lambda b, h: (slice(b * t, (b + 1) * t), slice(h * LANES, (h + 1) * LANES))

    def scores(b, h):
        qq = _split_halves(q_ref[where(b, h)], DH_A ** -0.5)
        return _dot_nt(qq, k_ref[where(b, h)].astype(BF16))

    pending = [scores(*blk) for blk in blocks[:ATTN_AHEAD]]
    for n, blk in enumerate(blocks):
        s = pending.pop(0)
        if n + ATTN_AHEAD < len(blocks):
            pending.append(scores(*blocks[n + ATTN_AHEAD]))
        e = jnp.exp2(s - jnp.max(s, axis=-1, keepdims=True)).astype(BF16)
        rsum = 1.0 / _dot(e, ones)
        o = _diff_combine(_dot(e, v_ref[where(*blk)].astype(BF16)), rsum, lam, t)
        o_ref[where(*blk)] = _diff_head_post(o, sg_ref[...], lam_init, g_ref[where(*blk)])


def _attn_a_ctx_call(proj, lamvec, subln_g, lam_init):
    rows = CTX_BATCHES_PER_STEP * SEQ
    blk = lambda c: pl.BlockSpec((rows, W_A), lambda b: (b, c // W_A))
    return pl.pallas_call(
        functools.partial(_attn_a_ctx_kernel, lam_init=lam_init),
        grid=(BATCH // CTX_BATCHES_PER_STEP,),
        in_specs=[blk(COL_QA), blk(COL_KA), blk(COL_VA), blk(COL_GA),
                  pl.BlockSpec((4, LANES), lambda b: (0, 0)),
                  pl.BlockSpec((1, LANES), lambda b: (0, 0))],
        out_specs=pl.BlockSpec((rows, W_A), lambda b: (b, 0)),
        out_shape=jax.ShapeDtypeStruct((BATCH * SEQ, W_A), BF16),
        compiler_params=_params("arbitrary"),
        name="attn_a_ctx",
    )(proj, proj, proj, proj, lamvec, subln_g)


def _rope(x, cos, sin_signed):
    first = (_lane(x.shape) % 32) < 16
    swapped = jnp.where(first, pltpu.roll(x, LANES - 16, 1), pltpu.roll(x, 16, 1))
    return x * cos + swapped * sin_signed


def _attn_a_lat_kernel(q_ref, k_ref, v_ref, g_ref, ck_ref, cv_ref, cosq_ref, sinq_ref, cosk_ref, sink_ref,
                       lam_ref, sg_ref, o_ref, kr_s, *, lam_init):
    tq = q_ref.shape[0]

    @pl.when(pl.program_id(1) == 0)
    def _():
        for h in range(H_A):
            sl = slice(h * LANES, (h + 1) * LANES)
            kr_s[:, sl] = _rope(k_ref[:, sl], cosk_ref[...], sink_ref[...]).astype(BF16)

    lam = _diff_lambda_in_kernel(lam_ref, lam_init)

    def scores(h):
        sl = slice(h * LANES, (h + 1) * LANES)
        qq = _split_halves(_rope(q_ref[:, sl], cosq_ref[...], sinq_ref[...]), DH_A ** -0.5)
        return _dot_nt(qq, kr_s[:, sl]), _dot_nt(qq, ck_ref[:, sl].astype(BF16))

    pending = [scores(h) for h in range(ATTN_AHEAD)]
    for h in range(H_A):
        sl = slice(h * LANES, (h + 1) * LANES)
        s_lat, s_ctx = pending.pop(0)
        if h + ATTN_AHEAD < H_A:
            pending.append(scores(h + ATTN_AHEAD))
        m = jnp.maximum(jnp.max(s_lat, axis=-1, keepdims=True), jnp.max(s_ctx, axis=-1, keepdims=True))
        e_lat = jnp.exp2(s_lat - m)
        e_ctx = jnp.exp2(s_ctx - m)
        rsum = 1.0 / (jnp.sum(e_lat, axis=-1, keepdims=True) + jnp.sum(e_ctx, axis=-1, keepdims=True))
        o2 = _dot(e_lat.astype(BF16), v_ref[:, sl].astype(BF16)) + _dot(e_ctx.astype(BF16),
                                                                         cv_ref[:, sl].astype(BF16))
        o = _diff_combine(o2, rsum, lam, tq)
        o_ref[:, sl] = _diff_head_post(o, sg_ref[...], lam_init, g_ref[:, sl])


def _attn_a_lat_call(proj, cache_k, cache_v, li, cos_t, sin_t, lamvec, subln_g, lam_init):
    tq = 256
    nq = DEC_SEQ // tq
    qblk = lambda c: pl.BlockSpec((tq, W_A), lambda b, i: (b * nq + i, c // W_A))
    full = lambda c: pl.BlockSpec((DEC_SEQ, W_A), lambda b, i: (b, c // W_A))
    cache = pl.BlockSpec((None, None, PAST_LEN, W_A), lambda b, i: (b, li, 0, 0))
    return pl.pallas_call(
        functools.partial(_attn_a_lat_kernel, lam_init=lam_init),
        grid=(DEC_BATCH, nq),
        in_specs=[qblk(COL_QA), full(COL_KA), full(COL_VA), qblk(COL_GA), cache, cache,
                  pl.BlockSpec((tq, LANES), lambda b, i: (i, 0)),
                  pl.BlockSpec((tq, LANES), lambda b, i: (i, 0)),
                  pl.BlockSpec((DEC_SEQ, LANES), lambda b, i: (0, 0)),
                  pl.BlockSpec((DEC_SEQ, LANES), lambda b, i: (0, 0)),
                  pl.BlockSpec((4, LANES), lambda b, i: (0, 0)),
                  pl.BlockSpec((1, LANES), lambda b, i: (0, 0))],
        out_specs=pl.BlockSpec((tq, W_A), lambda b, i: (b * nq + i, 0)),
        out_shape=jax.ShapeDtypeStruct((DEC_BATCH * DEC_SEQ, W_A), BF16),
        scratch_shapes=[pltpu.VMEM((DEC_SEQ, W_A), BF16)],
        compiler_params=_params("arbitrary", "arbitrary"),
        name="attn_a_lat",
    )(proj, proj, proj, proj, cache_k, cache_v, cos_t, sin_t, cos_t, sin_t, lamvec, subln_g)


def _merge_halves(o, t):
    return jnp.where(_lane((t, LANES)) < HALF, o[:t], o[t:])


def _attn_c_ctx_kernel(q_ref, kt_ref, vt_ref, g_ref, o_ref):
    t = SEQ
    blocks = [(b, j) for b in range(q_ref.shape[0] // t) for j in range(H_C // 2)]
    where = lambda b, j: (slice(b * t, (b + 1) * t), slice(j * LANES, (j + 1) * LANES))
    pair_t = lambda ref, b, j: ref[b, 2 * j:2 * j + 2].reshape(LANES, t).astype(BF16)

    def scores(b, j):
        return _dot(_split_halves(q_ref[where(b, j)], DH_C ** -0.5), pair_t(kt_ref, b, j))

    pending = [scores(*blk) for blk in blocks[:ATTN_AHEAD]]
    for n, blk in enumerate(blocks):
        s = pending.pop(0)
        if n + ATTN_AHEAD < len(blocks):
            pending.append(scores(*blocks[n + ATTN_AHEAD]))
        e = jnp.exp2(s - jnp.max(s, axis=-1, keepdims=True))
        rsum = 1.0 / jnp.sum(e, axis=-1, keepdims=True)
        o = _merge_halves(_dot_nt(e.astype(BF16), pair_t(vt_ref, *blk)) * rsum, t)
        o_ref[where(*blk)] = (o * _silu(g_ref[where(*blk)])).astype(BF16)


def _attn_c_ctx_call(proj, kc_t, vc_t, li):
    nb = CTX_BATCHES_PER_STEP
    blk = lambda c: pl.BlockSpec((nb * SEQ, W_C), lambda b: (b, c // W_C))
    cache = pl.BlockSpec((nb, None, H_C, DH_C, SEQ), lambda b: (b, li, 0, 0, 0))
    return pl.pallas_call(
        _attn_c_ctx_kernel,
        grid=(BATCH // nb,),
        in_specs=[blk(COL_QC), cache, cache, blk(COL_GC)],
        out_specs=pl.BlockSpec((nb * SEQ, W_C), lambda b: (b, 0)),
        out_shape=jax.ShapeDtypeStruct((BATCH * SEQ, W_C), BF16),
        compiler_params=_params("arbitrary"),
        name="attn_c_ctx",
    )(proj, kc_t, vc_t, proj)


def _rpb_kernel(rpb_ref, o_ref):
    shape = (GRID_W, LANES)
    c = lax.broadcasted_iota(jnp.int32, shape, 0)
    cp = _lane(shape) % GRID_W
    start = jnp.clip(c - NA_KW // 2, 0, GRID_W - NA_KW)
    in_win = (cp >= start) & (cp < start + NA_KW)
    o_ref[0] = jnp.full(shape, NEG_INF, F32)
    for dr in range(2 * NA_KH - 1):
        row = jnp.broadcast_to(rpb_ref[dr:dr + 1, :], shape)
        tile = pltpu.roll(row, LANES - (NA_KW - 1), 1, stride=1, stride_axis=0)
        o_ref[1 + dr] = jnp.where(in_win, tile * LOG2E, NEG_INF)


def _rpb_call(rpb):
    n_dc = 2 * NA_KW - 1
    v = jnp.pad(rpb, ((0, 0), (0, 0), (0, NA_TILES - (2 * NA_KH - 1)), (0, GRID_W - n_dc)))
    v = jnp.concatenate([v] * (LANES // GRID_W), axis=-1)
    return pl.pallas_call(
        _rpb_kernel,
        grid=(DEPTH, H_C),
        in_specs=[pl.BlockSpec((None, None, NA_TILES, LANES), lambda l, h: (l, h, 0, 0))],
        out_specs=pl.BlockSpec((None, None, NA_TILES, GRID_W, LANES), lambda l, h: (l, h, 0, 0, 0)),
        out_shape=jax.ShapeDtypeStruct((DEPTH, H_C, NA_TILES, GRID_W, LANES), F32),
        compiler_params=_params("arbitrary", "arbitrary"),
        name="rpb_tiles",
    )(v)


def _attn_c_lat_kernel(q_ref, k_ref, v_ref, g_ref, ck_ref, cv_ref, tile_ref, o_ref, bias_s):
    tq = q_ref.shape[0]
    nwin = NA_WROWS * GRID_W
    m = pl.program_id(1)
    w0 = jnp.where(m < (GRID_ROWS // NA_QROWS) // 2, 0, GRID_ROWS - NA_WROWS)
    k0 = pl.multiple_of(w0 * GRID_W, GRID_W)
    lo = _lane((GRID_W, LANES)) < HALF
    n_pair = H_C // 2

    def scores(j):
        sl = slice(j * LANES, (j + 1) * LANES)
        for s in range(2):
            for i in range(NA_QROWS):
                r = m * NA_QROWS + i
                start = jnp.clip(r - NA_KH // 2, 0, GRID_ROWS - NA_KH)
                for jp in range(NA_WROWS // 2):
                    idx = []
                    for u in range(2):
                        rk = w0 + 2 * jp + u
                        valid = (rk >= start) & (rk < start + NA_KH)
                        idx.append(jnp.where(valid, rk - r + NA_KH, 0))
                    tile = jnp.where(lo, tile_ref[2 * j + s, idx[0]], tile_ref[2 * j + s, idx[1]])
                    bias_s[(s * NA_QROWS + i) * GRID_W:(s * NA_QROWS + i + 1) * GRID_W,
                           jp * LANES:(jp + 1) * LANES] = tile
        qq = _split_halves(q_ref[:, sl], DH_C ** -0.5)
        kw = k_ref[pl.ds(k0, nwin), sl].astype(BF16)
        s_win = _dot_nt(qq, kw) + bias_s[...]
        ckt = ck_ref[2 * j:2 * j + 2].reshape(LANES, PAST_LEN).astype(BF16)
        return s_win, _dot(qq, ckt)

    pending = [scores(j) for j in range(ATTN_AHEAD)]
    for j in range(n_pair):
        sl = slice(j * LANES, (j + 1) * LANES)
        s_win, s_ctx = pending.pop(0)
        if j + ATTN_AHEAD < n_pair:
            pending.append(scores(j + ATTN_AHEAD))
        vw = v_ref[pl.ds(k0, nwin), sl].astype(BF16)
        mx = jnp.maximum(jnp.max(s_win, axis=-1, keepdims=True), jnp.max(s_ctx, axis=-1, keepdims=True))
        e_win = jnp.exp2(s_win - mx)
        e_ctx = jnp.exp2(s_ctx - mx)
        rs = 1.0 / (jnp.sum(e_win, axis=-1, keepdims=True) + jnp.sum(e_ctx, axis=-1, keepdims=True))
        cvt = cv_ref[2 * j:2 * j + 2].reshape(LANES, PAST_LEN).astype(BF16)
        o = (_dot(e_win.astype(BF16), vw) + _dot_nt(e_ctx.astype(BF16), cvt)) * rs
        o_ref[:, sl] = (_merge_halves(o, tq) * _silu(g_ref[:, sl])).astype(BF16)


def _attn_c_lat_call(proj, kv, cache_k, cache_v, li, tiles):
    tq = NA_QROWS * GRID_W
    nq = DEC_SEQ // tq
    qblk = lambda c: pl.BlockSpec((tq, W_C), lambda b, i: (b * nq + i, c // W_C))
    full = lambda c: pl.BlockSpec((DEC_SEQ, W_C), lambda b, i: (b, c // W_C))
    cache = pl.BlockSpec((None, None, H_C, DH_C, PAST_LEN), lambda b, i: (b, li, 0, 0, 0))
    return pl.pallas_call(
        _attn_c_lat_kernel,
        grid=(DEC_BATCH, nq),
        in_specs=[qblk(COL_QC), full(0), full(W_C), qblk(COL_GC), cache, cache,
                  pl.BlockSpec((None, H_C, NA_TILES, GRID_W, LANES), lambda b, i: (li, 0, 0, 0, 0))],
        out_specs=pl.BlockSpec((tq, W_C), lambda b, i: (b * nq + i, 0)),
        out_shape=jax.ShapeDtypeStruct((DEC_BATCH * DEC_SEQ, W_C), BF16),
        scratch_shapes=[pltpu.VMEM((2 * tq, NA_WROWS * GRID_W), F32)],
        compiler_params=_params("arbitrary", "arbitrary"),
        name="attn_c_lat",
    )(proj, kv, kv, proj, cache_k, cache_v, tiles)


def _ssd_body(dt_ref, xs_ref, bc_ref, z_ref, h0_ref, params, y_ref, hs_ref, scratch, *, seq, static_loops, side):
    cw_ref, cb_ref, dtb_ref, alog_ref, dsk_ref, g_ref = params
    upad_s, xc_s, expo_s, expot_s, dtt_s, tot_s, bmt_s, yf_s, yb_s, st_s = scratch
    use_h0 = h0_ref is not None
    want_state = hs_ref is not None

    q = SSD_CHUNK
    nc = seq // q
    n_pair = H_B // 2
    n_hd = 2 * H_B
    pad = 8

    def loop(body, unroll=1):
        if static_loops:
            for c in range(nc):
                body(c, 0)
        else:
            lax.fori_loop(0, nc, body, 0, unroll=unroll)

    def chunk_rows(c):
        return slice(c * q, (c + 1) * q) if isinstance(c, int) else pl.ds(pl.multiple_of(c * q, q), q)

    upad_s[0:pad, :] = jnp.zeros((pad, CONV_DIM), F32)
    upad_s[pad + seq:2 * pad + seq, :] = jnp.zeros((pad, CONV_DIM), F32)
    upad_s[pad:pad + seq, 0:DI_B] = xs_ref[...]
    upad_s[pad:pad + seq, DI_B:CONV_DIM] = bc_ref[...]

    for c in range(nc):
        for cb_ in range(CONV_DIM // LANES):
            csl = slice(cb_ * LANES, (cb_ + 1) * LANES)
            acc = jnp.zeros((q, LANES), F32) + cb_ref[:, csl]
            for k in range(CONV_K):
                r0 = c * q + pad - CONV_K // 2 + k
                acc = acc + upad_s[r0:r0 + q, csl] * cw_ref[k:k + 1, csl]
            xc_s[c * q:(c + 1) * q, csl] = _silu(acc)
            side()

    a_row = -jnp.exp(alog_ref[...]) * LOG2E
    a_col = jnp.broadcast_to(a_row, (LANES, LANES)).T[0:n_hd, 0:1]
    ri = lax.broadcasted_iota(jnp.int32, (q, q), 0)
    ci = lax.broadcasted_iota(jnp.int32, (q, q), 1)
    ltri = (ri >= ci).astype(F32)
    fwd_lane = _lane((q, LANES)) < H_B
    fwd_row = lax.broadcasted_iota(jnp.int32, (n_hd, q), 0) < H_B

    def prep_body(c, carry):
        rows = chunk_rows(c)
        xdt = dt_ref[rows, 0:LANES] + dtb_ref[...]
        dtv = jnp.maximum(xdt, 0.0) + jnp.log1p(jnp.exp(-jnp.abs(xdt)))
        la = dtv * a_row
        acum = _dot(ltri, la, HI)
        expo_s[rows, :] = jnp.where(fwd_lane, acum, la - acum)
        acum_t = acum.T[0:n_hd, :]
        dt_t = dtv.T[0:n_hd, :]
        expot_s[c] = jnp.where(fwd_row, acum_t, dt_t * a_col - acum_t)
        dtt_s[c] = dt_t
        tot_s[c] = jnp.broadcast_to(acum_t[:, q - 1:q], (n_hd, q))
        bmt_s[c] = xc_s[rows, DI_B:DI_B + LANES].T
        side()
        return carry

    loop(prep_body, unroll=2)

    if use_h0:
        st_s[...] = h0_ref[...].reshape(2, n_pair, N_B, LANES)
    else:
        st_s[...] = jnp.zeros_like(st_s)

    lane_q = _lane((q, LANES))
    lo = lane_q < HALF
    lo_st = _lane((N_B, LANES)) < HALF

    def chunk(dirn, c, y_s):
        rows = chunk_rows(c)
        tri = (ri >= ci) if dirn == 0 else (ci >= ri)
        bm16 = xc_s[rows, DI_B:DI_B + LANES].astype(BF16)
        cm = xc_s[rows, DI_B + LANES:DI_B + 2 * LANES]
        for g in range(G_B):
            in_g = (lane_q >= g * N_B) & (lane_q < (g + 1) * N_B)
            cmg = jnp.where(in_g, cm, 0.0).astype(BF16)
            cb = _dot_nt(cmg, bm16)
            bmt_g = bmt_s[c, g * N_B:(g + 1) * N_B, :]
            for k in range(g * n_pair // G_B, (g + 1) * n_pair // G_B):
                psl = slice(k * LANES, (k + 1) * LANES)
                x16 = xc_s[rows, psl].astype(BF16)
                mats, lhs, ysc, cdec = [], [], [], []
                for s in range(2):
                    col = dirn * H_B + 2 * k + s
                    e_col = jnp.broadcast_to(expo_s[rows, col:col + 1], (q, q))
                    e_row = expot_s[c, col:col + 1, :]
                    dt_row = dtt_s[c, col:col + 1, :]
                    tot = tot_s[c, col:col + 1, :]
                    dec = jnp.exp2(jnp.where(tri, e_col - e_row, NEG_INF))
                    mats.append((cb * dec * dt_row).astype(BF16))
                    if dirn == 0:
                        ysc.append(jnp.exp2(e_col))
                        w_row = jnp.exp2(tot - e_row)
                    else:
                        ysc.append(jnp.exp2(e_col + tot))
                        w_row = jnp.exp2(-e_row)
                    lhs.append((bmt_g * (w_row * dt_row)).astype(BF16))
                    cdec.append(jnp.exp2(tot[:, 0:LANES]))
                yd = _dot(jnp.concatenate(mats, axis=0), x16)
                st = st_s[dirn, k]
                y_off = _dot(cmg, jnp.concatenate([st, st], axis=0).astype(BF16))
                y_s[rows, psl] = jnp.where(lo, yd[:q] + ysc[0] * y_off, yd[q:] + ysc[1] * y_off)
                ds = _dot(jnp.concatenate(lhs, axis=0), x16)
                st_s[dirn, k] = jnp.where(lo_st, cdec[0] * st + ds[:N_B], cdec[1] * st + ds[N_B:])
                side()

    def body(c, carry):
        chunk(0, c, yf_s)
        chunk(1, nc - 1 - c, yb_s)
        return carry

    loop(body, unroll=2)

    dsum = dsk_ref[0:1, :] + dsk_ref[1:2, :]

    def out_body(c, carry):
        rows = chunk_rows(c)
        y = yf_s[rows, :] + yb_s[rows, :] + xc_s[rows, 0:DI_B] * dsum
        y = y * _silu(z_ref[rows, :])
        y = y * lax.rsqrt(jnp.mean(y * y, axis=-1, keepdims=True) + EPS) * g_ref[...]
        y_ref[rows, :] = y.astype(BF16)
        side()
        return carry

    loop(out_body)
    if want_state:
        for dirn in range(2):
            for k in range(n_pair):
                st = st_s[dirn, k]
                st_t = jnp.concatenate([st, st], axis=0).T
                for s in range(2):
                    hs_ref[dirn, 2 * k + s] = st_t[s * P_B:(s + 1) * P_B, 0:N_B]


def _ssd_scratch(seq):
    nc = seq // SSD_CHUNK
    per_chunk_rows = pltpu.VMEM((nc, 2 * H_B, SSD_CHUNK), F32)
    return [pltpu.VMEM((seq + 16, CONV_DIM), F32), pltpu.VMEM((seq, CONV_DIM), F32),
            pltpu.VMEM((seq, LANES), F32), per_chunk_rows, per_chunk_rows, per_chunk_rows,
            pltpu.VMEM((nc, LANES, SSD_CHUNK), F32),
            pltpu.VMEM((seq, DI_B), F32), pltpu.VMEM((seq, DI_B), F32),
            pltpu.VMEM((2, H_B // 2, N_B, LANES), F32)]


def _ssd_param_specs(const):
    return [const((CONV_K, CONV_DIM)), const((1, CONV_DIM)), const((1, LANES)), const((1, LANES)),
            const((2, DI_B)), const((1, DI_B))]


def _proj_ssd_kernel(*refs, nb, seq, ctx, n_carry):
    x_ref, ada_ref, g_ref, w_ref, wdt_ref = refs[:5]
    pos = 5
    h0_ref = None
    if not ctx:
        h0_ref = refs[pos]
        pos += 1
    params = refs[pos:pos + 6]
    outs = refs[pos + 6 + n_carry:]
    if ctx:
        proj_ref, ka_ref, va_ref, kc_ref, vc_ref, y_ref, hs_ref = outs[:7]
        outs = outs[7:]
    else:
        proj_ref, kv_ref, y_ref = outs[:3]
        hs_ref = None
        outs = outs[3:]
    h_s, p_s = outs[:2]
    scratch = outs[2:]

    row = 0 if ctx else 1 + pl.program_id(0)
    for r0 in range(0, nb * seq, SEQ):
        h_s[r0:r0 + SEQ, :] = _modulated_norm(x_ref[r0:r0 + SEQ, :], g_ref, ada_ref, row)
    for c0 in range(P_Z, P_DT, SIDE_TN):
        p_s[:, c0:c0 + SIDE_TN] = _dot_nt(h_s[...], w_ref[_SRC["z"] + c0:_SRC["z"] + c0 + SIDE_TN, :])
    p_s[:, P_DT:P_COLS] = _dot_nt(h_s[...], wdt_ref[...])

    work = []
    for col, src in PROJ_SEGMENTS:
        for off in range(0, W_A, SIDE_TN):
            def tile(d=col + off, s=src + off):
                proj_ref[:, d:d + SIDE_TN] = _dot_nt(h_s[...], w_ref[s:s + SIDE_TN, :])
            work.append(tile)
            if ctx and col in (COL_KA, COL_VA):
                def store(dst=ka_ref if col == COL_KA else va_ref, col=col, off=off):
                    for b in range(nb):
                        for h in range(off // LANES, (off + SIDE_TN) // LANES):
                            dst[b, :, h, :] = proj_ref[b * seq:(b + 1) * seq, col + h * LANES:col + (h + 1) * LANES]
                work.append(store)
    for r0 in range(0, KVC_COLS, SIDE_TN):
        if ctx:
            for b in range(nb):
                def tile_t(b=b, r0=r0):
                    w_rows = w_ref[_SRC["kc"] + r0:_SRC["kc"] + r0 + SIDE_TN, :]
                    kv_t = _dot_nt(w_rows, h_s[b * seq:(b + 1) * seq, :])
                    dst, d0 = (kc_ref, r0) if r0 < W_C else (vc_ref, r0 - W_C)
                    dst[b, d0 // DH_C:(d0 + SIDE_TN) // DH_C] = kv_t.reshape(SIDE_TN // DH_C, DH_C, seq)
                work.append(tile_t)
        else:
            def tile_kv(r0=r0):
                kv_ref[:, r0:r0 + SIDE_TN] = _dot_nt(h_s[...], w_ref[_SRC["kc"] + r0:_SRC["kc"] + r0 + SIDE_TN, :])
            work.append(tile_kv)

    n_slots = nb * (seq // SSD_CHUNK) * (CONV_DIM // LANES + 2 + 2 * (H_B // 2))
    state = dict(slot=0, done=0)

    def side():
        state["slot"] += 1
        target = min(len(work), -(-state["slot"] * len(work) // n_slots))
        while state["done"] < target:
            work[state["done"]]()
            state["done"] += 1

    for b in range(nb):
        rows = pl.ds(b * seq, seq)
        _ssd_body(p_s.at[rows, pl.ds(P_DT, LANES)], p_s.at[rows, pl.ds(P_XS, DI_B)],
                  p_s.at[rows, pl.ds(P_BC, BC_DIM)], p_s.at[rows, pl.ds(P_Z, DI_B)], h0_ref, params,
                  y_ref.at[rows, :], None if hs_ref is None else hs_ref.at[b], scratch,
                  seq=seq, static_loops=True, side=side)
    while state["done"] < len(work):
        work[state["done"]]()
        state["done"] += 1


def _proj_ssd_call(x, ada, norm_g, w16, w_dt, ssd_w, li, *, ctx, carry=None, h0t=None):
    nb, seq = (CTX_BATCHES_PER_STEP, SEQ) if ctx else (1, DEC_SEQ)
    tm = nb * seq
    t = x.shape[0]
    n_carry = 0 if carry is None else len(carry)
    const = lambda shape: pl.BlockSpec(shape, lambda i: (0,) * len(shape))
    once = pl.Buffered(1)
    big = {} if ctx else dict(pipeline_mode=pl.Buffered(1))
    in_specs = [pl.BlockSpec((tm, D_MODEL), lambda i: (i, 0), **big),
                pl.BlockSpec((None, 8, 3 * D_MODEL), lambda i: (li, 0, 0)),
                pl.BlockSpec((None, 1, D_MODEL), lambda i: (li, 0, 0)),
                pl.BlockSpec((None, _SRC["merge"], D_MODEL), lambda i: (li, 0, 0), pipeline_mode=once),
                pl.BlockSpec((None, LANES, D_MODEL), lambda i: (li, 0, 0))]
    args = [x, ada, norm_g, w16, w_dt]
    if not ctx:
        in_specs.append(pl.BlockSpec((None, None, 2, (H_B // 2) * N_B, LANES), lambda i: (i, li, 0, 0, 0)))
        args.append(h0t)
    in_specs += _ssd_param_specs(const)
    args += list(ssd_w)
    out_specs = [pl.BlockSpec((tm, PROJ_COLS), lambda i: (i, 0), **big)]
    out_shape = [jax.ShapeDtypeStruct((t, PROJ_COLS), F32)]
    aliases = {}
    if ctx:
        out_specs += [pl.BlockSpec((nb, None, SEQ, H_A, 2 * DH_A), lambda i: (i, li, 0, 0, 0))] * 2
        out_specs += [pl.BlockSpec((nb, None, H_C, DH_C, SEQ), lambda i: (i, li, 0, 0, 0))] * 2
        out_shape += [jax.ShapeDtypeStruct((BATCH, DEPTH, SEQ, H_A, 2 * DH_A), F32)] * 2
        out_shape += [jax.ShapeDtypeStruct((BATCH, DEPTH, H_C, DH_C, SEQ), F32)] * 2
    else:
        out_specs.append(pl.BlockSpec((tm, KVC_COLS), lambda i: (i, 0), **big))
        out_shape.append(jax.ShapeDtypeStruct((t, KVC_COLS), F32))
    out_specs.append(pl.BlockSpec((tm, DI_B), lambda i: (i, 0)))
    out_shape.append(jax.ShapeDtypeStruct((t, DI_B), BF16))
    if ctx:
        out_specs.append(pl.BlockSpec((nb, None, 2, H_B, P_B, N_B), lambda i: (i, li, 0, 0, 0, 0)))
        out_shape.append(jax.ShapeDtypeStruct((BATCH, DEPTH, 2, H_B, P_B, N_B), F32))
        if carry is not None:
            in_specs += [pl.BlockSpec(memory_space=pl.ANY)] * n_carry
            aliases = {len(args) + k: (1, 2, 3, 4, 6)[k] for k in range(n_carry)}
            args += list(carry)
    scratch = [pltpu.VMEM((tm, D_MODEL), BF16), pltpu.VMEM((tm, P_COLS), F32)] + _ssd_scratch(seq)
    return pl.pallas_call(
        functools.partial(_proj_ssd_kernel, nb=nb, seq=seq, ctx=ctx, n_carry=n_carry),
        grid=(t // tm,),
        in_specs=in_specs,
        out_specs=out_specs,
        out_shape=out_shape,
        input_output_aliases=aliases,
        scratch_shapes=scratch,
        compiler_params=_params("arbitrary"),
        name="ctx_proj_ssd" if ctx else "lat_proj_ssd",
    )(*args)


def _post_kernel(x_ref, ya_ref, yb_ref, yc_ref, ada_ref, g_ref, wm_ref, wa_ref, wb_ref, wc_ref, wo_ref, fg_ref,
                 o_ref, *, tm, row_base, tokens_per_row, final):
    row = row_base + (pl.program_id(0) * tm) // tokens_per_row
    gate = ada_ref[pl.ds(row, 1), 2 * D_MODEL:3 * D_MODEL]
    d = D_MODEL
    x = x_ref[...]
    h = _modulated_norm(x, g_ref, ada_ref, row)
    merged = None
    for n, (y_ref, w_ref) in enumerate(((ya_ref, wa_ref), (yb_ref, wb_ref), (yc_ref, wc_ref))):
        r0 = _SRC["merge"] + n * d
        logits = _dot_nt(h, wm_ref[r0:r0 + d, :])
        term = _sigmoid(logits) * _dot(y_ref[...], w_ref[...])
        merged = term if merged is None else merged + term
    x = x + gate * _dot(merged.astype(BF16), wo_ref[...])
    if final:
        x = x * lax.rsqrt(jnp.mean(x * x, axis=-1, keepdims=True) + EPS) * fg_ref[...]
    o_ref[...] = x


def _post_call(x, ya, yb, yc, ada, norm_g, w_merge_t, li, wa, wb, wc, wo, final_g, *, tm, row_base,
               tokens_per_row, final):
    t = x.shape[0]
    tok = lambda w: pl.BlockSpec((tm, w), lambda i: (i, 0))
    layer = lambda *shape, **kw: pl.BlockSpec((None,) + shape, lambda i: (li,) + (0,) * len(shape), **kw)
    once = dict(pipeline_mode=pl.Buffered(1))
    kern = functools.partial(_post_kernel, tm=tm, row_base=row_base, tokens_per_row=tokens_per_row, final=final)
    return pl.pallas_call(
        kern,
        grid=(t // tm,),
        in_specs=[tok(D_MODEL), tok(W_A), tok(DI_B), tok(W_C),
                  layer(8, 3 * D_MODEL), layer(1, D_MODEL), layer(w_merge_t.shape[1], D_MODEL, **once),
                  layer(W_A, D_MODEL, **once), layer(DI_B, D_MODEL, **once), layer(W_C, D_MODEL, **once),
                  layer(D_MODEL, D_MODEL, **once), pl.BlockSpec((1, D_MODEL), lambda i: (0, 0))],
        out_specs=tok(D_MODEL),
        out_shape=jax.ShapeDtypeStruct((t, D_MODEL), F32),
        compiler_params=_params("arbitrary"),
        name="post_final" if final else "post",
    )(x, ya, yb, yc, ada, norm_g, w_merge_t, wa, wb, wc, wo, final_g)


def _rope_tables():
    pos = np.arange(DEC_SEQ)
    lane = np.arange(LANES)
    l64 = lane % (2 * (DH_A // 2))
    quarter = DH_A // 4
    p = np.where((l64 < DH_A // 2)[None, :], (pos // GRID_W)[:, None], (pos % GRID_W)[:, None])
    inv = ROPE_BASE ** (-np.arange(quarter, dtype=np.float64) / quarter)
    ang = p.astype(np.float64) * inv[l64 % quarter][None, :]
    sign = np.where((lane % (2 * quarter)) < quarter, -1.0, 1.0)
    return jnp.asarray(np.cos(ang), F32), jnp.asarray(np.sin(ang) * sign[None, :], F32)


def _pad_lanes(v, width=LANES):
    v = v.reshape(1, -1).astype(F32)
    return jnp.pad(v, ((0, 0), (0, width - v.shape[1])))


def kernel(x_prompt, x_sample, cache_diff_k, cache_diff_v, cache_na_k, cache_na_v, state_ssd, c, c_ctx,
           norm_g, w_ada, b_ada, w_in, lam_q1, lam_k1, lam_q2, lam_k2, diff_subln_g, conv_w, conv_b,
           dt_bias, a_log, d_skip, ssd_norm_g, na_rpb, w_br_a, w_br_b, w_br_c, w_out, final_g):
    assert x_prompt.shape == (BATCH, SEQ, D_MODEL) and x_sample.shape == (DEC_BATCH, DEC_SEQ, D_MODEL)
    assert w_in.shape == (DEPTH, D_MODEL, _SRC["merge"] + MERGE_COLS)
    w16 = jnp.swapaxes(w_in, 1, 2).astype(BF16)
    w_dt = jnp.pad(w16[:, _SRC["dt"]:_SRC["qc"], :], ((0, 0), (0, LANES - 2 * H_B), (0, 0)))
    wa16, wb16, wc16, wo16 = (w.astype(BF16) for w in (w_br_a, w_br_b, w_br_c, w_out))

    cvecs = jnp.concatenate([c_ctx[None, :], c, jnp.zeros((8 - 1 - DEC_BATCH, D_MODEL), F32)], axis=0)
    ada = _ada_call(cvecs, w_ada, b_ada)
    cos_t, sin_t = _rope_tables()

    ck_a = cache_diff_k.reshape(DEC_BATCH, DEPTH, PAST_LEN, W_A)
    cv_a = cache_diff_v.reshape(DEC_BATCH, DEPTH, PAST_LEN, W_A)
    ck_c = cache_na_k.transpose(0, 1, 3, 4, 2)
    cv_c = cache_na_v.transpose(0, 1, 3, 4, 2)
    na_tiles = _rpb_call(na_rpb)
    h0t = state_ssd.transpose(0, 1, 2, 5, 3, 4).reshape(DEC_BATCH, DEPTH, 2, N_B, DI_B)
    h0t = h0t.reshape(DEC_BATCH, DEPTH, 2, N_B, H_B // 2, LANES).transpose(0, 1, 2, 4, 3, 5)
    h0t = h0t.reshape(DEC_BATCH, DEPTH, 2, (H_B // 2) * N_B, LANES)

    xp = x_prompt.reshape(BATCH * SEQ, D_MODEL)
    xs = x_sample.reshape(DEC_BATCH * DEC_SEQ, D_MODEL)
    fg = final_g.reshape(1, D_MODEL)
    norm_g3 = norm_g.reshape(DEPTH, 1, D_MODEL)
    carry = None
    for li in range(DEPTH):
        lam_init = 0.8 - 0.6 * math.exp(-0.3 * li)
        final = li == DEPTH - 1
        lamvec = jnp.concatenate([_pad_lanes(v[li]) for v in (lam_q1, lam_k1, lam_q2, lam_k2)], axis=0)
        subln = diff_subln_g[li].reshape(1, LANES)
        dtb = _pad_lanes(dt_bias[li])
        alog = _pad_lanes(a_log[li])
        dskx = jnp.repeat(d_skip[li], P_B, axis=-1)
        ssd_w = (conv_w[li], conv_b[li].reshape(1, CONV_DIM), dtb, alog, dskx, ssd_norm_g[li].reshape(1, DI_B))
        post_w = (wa16, wb16, wc16, wo16, fg)

        proj, ka, va, kc_t, vc_t, yb, ssd_state = _proj_ssd_call(xp, ada, norm_g3, w16, w_dt, ssd_w, li,
                                                                 ctx=True, carry=carry)
        carry = (ka, va, kc_t, vc_t, ssd_state)
        ya = _attn_a_ctx_call(proj, lamvec, subln, lam_init)
        yc = _attn_c_ctx_call(proj, kc_t, vc_t, li)
        xp = _post_call(xp, ya, yb, yc, ada, norm_g3, w16, li, *post_w, tm=512, row_base=0,
                        tokens_per_row=BATCH * SEQ, final=final)

        proj, kv, yb = _proj_ssd_call(xs, ada, norm_g3, w16, w_dt, ssd_w, li, ctx=False, h0t=h0t)
        ya = _attn_a_lat_call(proj, ck_a, cv_a, li, cos_t, sin_t, lamvec, subln, lam_init)
        yc = _attn_c_lat_call(proj, kv, ck_c, cv_c, li, na_tiles)
        xs = _post_call(xs, ya, yb, yc, ada, norm_g3, w16, li, *post_w, tm=512, row_base=1,
                        tokens_per_row=DEC_SEQ, final=final)

    new_k_a, new_v_a, new_k_c_t, new_v_c_t, new_ssd = carry
    to_token_major = lambda a: a.transpose(0, 1, 4, 2, 3)
    return (xp.reshape(BATCH, SEQ, D_MODEL), xs.reshape(DEC_BATCH, DEC_SEQ, D_MODEL),
            new_k_a, new_v_a, to_token_major(new_k_c_t), to_token_major(new_v_c_t), new_ssd)
```

```python
import functools
import math

import jax
import jax.numpy as jnp
import numpy as np
from jax import lax
from jax.experimental import pallas as pl
from jax.experimental.pallas import tpu as pltpu

D_MODEL = 1024
BATCH = 32
SEQ = 256
DEPTH = 2
DEC_BATCH = 2
DEC_SEQ = 1024
PAST_LEN = 512
GRID_W = 64
GRID_ROWS = DEC_SEQ // GRID_W
H_A = 4
DH_A = 64
W_A = H_A * 2 * DH_A
H_B = 8
P_B = 64
G_B = 2
N_B = 64
DI_B = H_B * P_B
CONV_K = 5
CONV_DIM = DI_B + 2 * G_B * N_B
SSD_CHUNK = 128
H_C = 8
DH_C = 64
W_C = H_C * DH_C
NA_KH = 8
NA_KW = 16
N_BRANCH = 3
ROPE_BASE = 10000.0
EPS = 1e-6

LANES = 128
HALF = LANES // 2
VMEM_LIMIT = 56 * 1024 * 1024

BC_DIM = CONV_DIM - DI_B
KVC_COLS = 2 * W_C
MERGE_COLS = N_BRANCH * D_MODEL
_SRC = dict(qa=0, ka=512, va=1024, ga=1536, z=2048, xs=2560, bc=3072, dt=3328, qc=3344, kc=3856, vc=4368,
            gc=4880, merge=5392)
COL_QA = 0
COL_GA = 512
COL_QC = 1024
COL_GC = 1536
COL_KA = 2048
COL_VA = 2560
PROJ_COLS = 3072
PROJ_SEGMENTS = ((COL_QA, _SRC["qa"]), (COL_GA, _SRC["ga"]), (COL_QC, _SRC["qc"]), (COL_GC, _SRC["gc"]),
                 (COL_KA, _SRC["ka"]), (COL_VA, _SRC["va"]))
P_Z = 0
P_XS = 512
P_BC = 1024
P_DT = 1280
P_COLS = P_DT + LANES
SIDE_TN = 256

NA_QROWS = 4
NA_WROWS = 12
NA_TILES = 2 * NA_KH
NEG_INF = float("-inf")
LOG2E = math.log2(math.e)
ATTN_AHEAD = 2
CTX_BATCHES_PER_STEP = 2
HI = lax.Precision.HIGHEST
F32 = jnp.float32
BF16 = jnp.bfloat16


def _dot(a, b, precision=None):
    return jnp.dot(a, b, preferred_element_type=F32, precision=precision)


def _dot_nt(a, b):
    return lax.dot_general(a, b, (((1,), (1,)), ((), ())), preferred_element_type=F32)


def _sigmoid(x):
    return 1.0 / (1.0 + jnp.exp(-x))


def _silu(x):
    return x * _sigmoid(x)


def _lane(shape):
    return lax.broadcasted_iota(jnp.int32, shape, len(shape) - 1)


def _params(*sem):
    return pltpu.CompilerParams(dimension_semantics=sem, vmem_limit_bytes=VMEM_LIMIT)


def _ada_kernel(cv_ref, w_ref, b_ref, o_ref):
    o_ref[...] = _dot(_silu(cv_ref[...]), w_ref[...], HI) + b_ref[...]


def _ada_call(cvecs, w_ada, b_ada):
    tn = 512
    return pl.pallas_call(
        _ada_kernel,
        grid=(DEPTH, 3 * D_MODEL // tn),
        in_specs=[
            pl.BlockSpec((8, D_MODEL), lambda l, j: (0, 0)),
            pl.BlockSpec((None, D_MODEL, tn), lambda l, j: (l, 0, j)),
            pl.BlockSpec((None, 1, tn), lambda l, j: (l, 0, j)),
        ],
        out_specs=pl.BlockSpec((None, 8, tn), lambda l, j: (l, 0, j)),
        out_shape=jax.ShapeDtypeStruct((DEPTH, 8, 3 * D_MODEL), F32),
        compiler_params=_params("arbitrary", "arbitrary"),
        name="ada",
    )(cvecs, w_ada, b_ada.reshape(DEPTH, 1, 3 * D_MODEL))


def _modulated_norm(x, g_ref, ada_ref, row):
    y = x * lax.rsqrt(jnp.mean(x * x, axis=-1, keepdims=True) + EPS) * g_ref[...]
    shift = ada_ref[pl.ds(row, 1), 0:D_MODEL]
    scale = ada_ref[pl.ds(row, 1), D_MODEL:2 * D_MODEL]
    return (y * (1.0 + scale) + shift).astype(BF16)


def _diff_lambda_in_kernel(lam_ref, lam_init):
    v = lam_ref[...]
    l1 = jnp.sum(v[0:1] * v[1:2], axis=-1, keepdims=True)
    l2 = jnp.sum(v[2:3] * v[3:4], axis=-1, keepdims=True)
    return jnp.exp(l1) - jnp.exp(l2) + lam_init


def _split_halves(x, scale):
    lo = _lane(x.shape) < HALF
    xs = x * (scale * LOG2E)
    return jnp.concatenate([jnp.where(lo, xs, 0.0), jnp.where(lo, 0.0, xs)], axis=0).astype(BF16)


def _diff_combine(o2, rsum, lam, t):
    return o2[:t] * rsum[:t] - (lam * rsum[t:]) * o2[t:]


def _diff_head_post(o, subln_g, lam_init, gate):
    o = o * lax.rsqrt(jnp.mean(o * o, axis=-1, keepdims=True) + EPS) * (subln_g * (1.0 - lam_init))
    return (o * _silu(gate)).astype(BF16)


def _attn_a_ctx_kernel(q_ref, k_ref, v_ref, g_ref, lam_ref, sg_ref, o_ref, *, lam_init):
    t = SEQ
    lam = _diff_lambda_in_kernel(lam_ref, lam_init)
    ones = jnp.ones((t, LANES), BF16)
    blocks = [(b, h) for b in range(q_ref.shape[0] // t) for h in range(H_A)]
    where = lambda b, h: (slice(b * t, (b + 1) * t), slice(h * LANES, (h + 1) * LANES))

    def scores(b, h):
        qq = _split_halves(q_ref[where(b, h)], DH_A ** -0.5)
        return _dot_nt(qq, k_ref[where(b, h)].astype(BF16))

    pending = [scores(*blk) for blk in blocks[:ATTN_AHEAD]]
    for n, blk in enumerate(blocks):
        s = pending.pop(0)
        if n + ATTN_AHEAD < len(blocks):
            pending.append(scores(*blocks[n + ATTN_AHEAD]))
        e = jnp.exp2(s - jnp.max(s, axis=-1, keepdims=True)).astype(BF16)
        rsum = 1.0 / _dot(e, ones)
        o = _diff_combine(_dot(e, v_ref[where(*blk)].astype(BF16)), rsum, lam, t)
        o_ref[where(*blk)] = _diff_head_post(o, sg_ref[...], lam_init, g_ref[where(*blk)])


def _attn_a_ctx_call(proj, lamvec, subln_g, lam_init):
    rows = CTX_BATCHES_PER_STEP * SEQ
    blk = lambda c: pl.BlockSpec((rows, W_A), lambda b: (b, c // W_A))
    return pl.pallas_call(
        functools.partial(_attn_a_ctx_kernel, lam_init=lam_init),
        grid=(BATCH // CTX_BATCHES_PER_STEP,),
        in_specs=[blk(COL_QA), blk(COL_KA), blk(COL_VA), blk(COL_GA),
                  pl.BlockSpec((4, LANES), lambda b: (0, 0)),
                  pl.BlockSpec((1, LANES), lambda b: (0, 0))],
        out_specs=pl.BlockSpec((rows, W_A), lambda b: (b, 0)),
        out_shape=jax.ShapeDtypeStruct((BATCH * SEQ, W_A), BF16),
        compiler_params=_params("arbitrary"),
        name="attn_a_ctx",
    )(proj, proj, proj, proj, lamvec, subln_g)


def _rope(x, cos, sin_signed):
    first = (_lane(x.shape) % 32) < 16
    swapped = jnp.where(first, pltpu.roll(x, LANES - 16, 1), pltpu.roll(x, 16, 1))
    return x * cos + swapped * sin_signed


def _attn_a_lat_kernel(q_ref, k_ref, v_ref, g_ref, ck_ref, cv_ref, cosq_ref, sinq_ref, cosk_ref, sink_ref,
                       lam_ref, sg_ref, o_ref, kr_s, *, lam_init):
    tq = q_ref.shape[0]

    @pl.when(pl.program_id(1) == 0)
    def _():
        for h in range(H_A):
            sl = slice(h * LANES, (h + 1) * LANES)
            kr_s[:, sl] = _rope(k_ref[:, sl], cosk_ref[...], sink_ref[...]).astype(BF16)

    lam = _diff_lambda_in_kernel(lam_ref, lam_init)

    def scores(h):
        sl = slice(h * LANES, (h + 1) * LANES)
        qq = _split_halves(_rope(q_ref[:, sl], cosq_ref[...], sinq_ref[...]), DH_A ** -0.5)
        return _dot_nt(qq, kr_s[:, sl]), _dot_nt(qq, ck_ref[:, sl].astype(BF16))

    pending = [scores(h) for h in range(ATTN_AHEAD)]
    for h in range(H_A):
        sl = slice(h * LANES, (h + 1) * LANES)
        s_lat, s_ctx = pending.pop(0)
        if h + ATTN_AHEAD < H_A:
            pending.append(scores(h + ATTN_AHEAD))
        m = jnp.maximum(jnp.max(s_lat, axis=-1, keepdims=True), jnp.max(s_ctx, axis=-1, keepdims=True))
        e_lat = jnp.exp2(s_lat - m)
        e_ctx = jnp.exp2(s_ctx - m)
        rsum = 1.0 / (jnp.sum(e_lat, axis=-1, keepdims=True) + jnp.sum(e_ctx, axis=-1, keepdims=True))
        o2 = _dot(e_lat.astype(BF16), v_ref[:, sl].astype(BF16)) + _dot(e_ctx.astype(BF16),
                                                                         cv_ref[:, sl].astype(BF16))
        o = _diff_combine(o2, rsum, lam, tq)
        o_ref[:, sl] = _diff_head_post(o, sg_ref[...], lam_init, g_ref[:, sl])


def _attn_a_lat_call(proj, cache_k, cache_v, li, cos_t, sin_t, lamvec, subln_g, lam_init):
    tq = 256
    nq = DEC_SEQ // tq
    qblk = lambda c: pl.BlockSpec((tq, W_A), lambda b, i: (b * nq + i, c // W_A))
    full = lambda c: pl.BlockSpec((DEC_SEQ, W_A), lambda b, i: (b, c // W_A))
    cache = pl.BlockSpec((None, None, PAST_LEN, W_A), lambda b, i: (b, li, 0, 0))
    return pl.pallas_call(
        functools.partial(_attn_a_lat_kernel, lam_init=lam_init),
        grid=(DEC_BATCH, nq),
        in_specs=[qblk(COL_QA), full(COL_KA), full(COL_VA), qblk(COL_GA), cache, cache,
                  pl.BlockSpec((tq, LANES), lambda b, i: (i, 0)),
                  pl.BlockSpec((tq, LANES), lambda b, i: (i, 0)),
                  pl.BlockSpec((DEC_SEQ, LANES), lambda b, i: (0, 0)),
                  pl.BlockSpec((DEC_SEQ, LANES), lambda b, i: (0, 0)),
                  pl.BlockSpec((4, LANES), lambda b, i: (0, 0)),
                  pl.BlockSpec((1, LANES), lambda b, i: (0, 0))],
        out_specs=pl.BlockSpec((tq, W_A), lambda b, i: (b * nq + i, 0)),
        out_shape=jax.ShapeDtypeStruct((DEC_BATCH * DEC_SEQ, W_A), BF16),
        scratch_shapes=[pltpu.VMEM((DEC_SEQ, W_A), BF16)],
        compiler_params=_params("arbitrary", "arbitrary"),
        name="attn_a_lat",
    )(proj, proj, proj, proj, cache_k, cache_v, cos_t, sin_t, cos_t, sin_t, lamvec, subln_g)


def _merge_halves(o, t):
    return jnp.where(_lane((t, LANES)) < HALF, o[:t], o[t:])


def _attn_c_ctx_kernel(q_ref, kt_ref, vt_ref, g_ref, o_ref):
    t = SEQ
    blocks = [(b, j) for b in range(q_ref.shape[0] // t) for j in range(H_C // 2)]
    where = lambda b, j: (slice(b * t, (b + 1) * t), slice(j * LANES, (j + 1) * LANES))
    pair_t = lambda ref, b, j: ref[b, 2 * j:2 * j + 2].reshape(LANES, t).astype(BF16)

    def scores(b, j):
        return _dot(_split_halves(q_ref[where(b, j)], DH_C ** -0.5), pair_t(kt_ref, b, j))

    pending = [scores(*blk) for blk in blocks[:ATTN_AHEAD]]
    for n, blk in enumerate(blocks):
        s = pending.pop(0)
        if n + ATTN_AHEAD < len(blocks):
            pending.append(scores(*blocks[n + ATTN_AHEAD]))
        e = jnp.exp2(s - jnp.max(s, axis=-1, keepdims=True))
        rsum = 1.0 / jnp.sum(e, axis=-1, keepdims=True)
        o = _merge_halves(_dot_nt(e.astype(BF16), pair_t(vt_ref, *blk)) * rsum, t)
        o_ref[where(*blk)] = (o * _silu(g_ref[where(*blk)])).astype(BF16)


def _attn_c_ctx_call(proj, kc_t, vc_t, li):
    nb = CTX_BATCHES_PER_STEP
    blk = lambda c: pl.BlockSpec((nb * SEQ, W_C), lambda b: (b, c // W_C))
    cache = pl.BlockSpec((nb, None, H_C, DH_C, SEQ), lambda b: (b, li, 0, 0, 0))
    return pl.pallas_call(
        _attn_c_ctx_kernel,
        grid=(BATCH // nb,),
        in_specs=[blk(COL_QC), cache, cache, blk(COL_GC)],
        out_specs=pl.BlockSpec((nb * SEQ, W_C), lambda b: (b, 0)),
        out_shape=jax.ShapeDtypeStruct((BATCH * SEQ, W_C), BF16),
        compiler_params=_params("arbitrary"),
        name="attn_c_ctx",
    )(proj, kc_t, vc_t, proj)


def _rpb_kernel(rpb_ref, o_ref):
    shape = (GRID_W, LANES)
    c = lax.broadcasted_iota(jnp.int32, shape, 0)
    cp = _lane(shape) % GRID_W
    start = jnp.clip(c - NA_KW // 2, 0, GRID_W - NA_KW)
    in_win = (cp >= start) & (cp < start + NA_KW)
    o_ref[0] = jnp.full(shape, NEG_INF, F32)
    for dr in range(2 * NA_KH - 1):
        row = jnp.broadcast_to(rpb_ref[dr:dr + 1, :], shape)
        tile = pltpu.roll(row, LANES - (NA_KW - 1), 1, stride=1, stride_axis=0)
        o_ref[1 + dr] = jnp.where(in_win, tile * LOG2E, NEG_INF)


def _rpb_call(rpb):
    n_dc = 2 * NA_KW - 1
    v = jnp.pad(rpb, ((0, 0), (0, 0), (0, NA_TILES - (2 * NA_KH - 1)), (0, GRID_W - n_dc)))
    v = jnp.concatenate([v] * (LANES // GRID_W), axis=-1)
    return pl.pallas_call(
        _rpb_kernel,
        grid=(DEPTH, H_C),
        in_specs=[pl.BlockSpec((None, None, NA_TILES, LANES), lambda l, h: (l, h, 0, 0))],
        out_specs=pl.BlockSpec((None, None, NA_TILES, GRID_W, LANES), lambda l, h: (l, h, 0, 0, 0)),
        out_shape=jax.ShapeDtypeStruct((DEPTH, H_C, NA_TILES, GRID_W, LANES), F32),
        compiler_params=_params("arbitrary", "arbitrary"),
        name="rpb_tiles",
    )(v)


def _attn_c_lat_kernel(q_ref, k_ref, v_ref, g_ref, ck_ref, cv_ref, tile_ref, o_ref, bias_s):
    tq = q_ref.shape[0]
    nwin = NA_WROWS * GRID_W
    m = pl.program_id(1)
    w0 = jnp.where(m < (GRID_ROWS // NA_QROWS) // 2, 0, GRID_ROWS - NA_WROWS)
    k0 = pl.multiple_of(w0 * GRID_W, GRID_W)
    lo = _lane((GRID_W, LANES)) < HALF
    n_pair = H_C // 2

    def scores(j):
        sl = slice(j * LANES, (j + 1) * LANES)
        for s in range(2):
            for i in range(NA_QROWS):
                r = m * NA_QROWS + i
                start = jnp.clip(r - NA_KH // 2, 0, GRID_ROWS - NA_KH)
                for jp in range(NA_WROWS // 2):
                    idx = []
                    for u in range(2):
                        rk = w0 + 2 * jp + u
                        valid = (rk >= start) & (rk < start + NA_KH)
                        idx.append(jnp.where(valid, rk - r + NA_KH, 0))
                    tile = jnp.where(lo, tile_ref[2 * j + s, idx[0]], tile_ref[2 * j + s, idx[1]])
                    bias_s[(s * NA_QROWS + i) * GRID_W:(s * NA_QROWS + i + 1) * GRID_W,
                           jp * LANES:(jp + 1) * LANES] = tile
        qq = _split_halves(q_ref[:, sl], DH_C ** -0.5)
        kw = k_ref[pl.ds(k0, nwin), sl].astype(BF16)
        s_win = _dot_nt(qq, kw) + bias_s[...]
        ckt = ck_ref[2 * j:2 * j + 2].reshape(LANES, PAST_LEN).astype(BF16)
        return s_win, _dot(qq, ckt)

    pending = [scores(j) for j in range(ATTN_AHEAD)]
    for j in range(n_pair):
        sl = slice(j * LANES, (j + 1) * LANES)
        s_win, s_ctx = pending.pop(0)
        if j + ATTN_AHEAD < n_pair:
            pending.append(scores(j + ATTN_AHEAD))
        vw = v_ref[pl.ds(k0, nwin), sl].astype(BF16)
        mx = jnp.maximum(jnp.max(s_win, axis=-1, keepdims=True), jnp.max(s_ctx, axis=-1, keepdims=True))
        e_win = jnp.exp2(s_win - mx)
        e_ctx = jnp.exp2(s_ctx - mx)
        rs = 1.0 / (jnp.sum(e_win, axis=-1, keepdims=True) + jnp.sum(e_ctx, axis=-1, keepdims=True))
        cvt = cv_ref[2 * j:2 * j + 2].reshape(LANES, PAST_LEN).astype(BF16)
        o = (_dot(e_win.astype(BF16), vw) + _dot_nt(e_ctx.astype(BF16), cvt)) * rs
        o_ref[:, sl] = (_merge_halves(o, tq) * _silu(g_ref[:, sl])).astype(BF16)


def _attn_c_lat_call(proj, kv, cache_k, cache_v, li, tiles):
    tq = NA_QROWS * GRID_W
    nq = DEC_SEQ // tq
    qblk = lambda c: pl.BlockSpec((tq, W_C), lambda b, i: (b * nq + i, c // W_C))
    full = lambda c: pl.BlockSpec((DEC_SEQ, W_C), lambda b, i: (b, c // W_C))
    cache = pl.BlockSpec((None, None, H_C, DH_C, PAST_LEN), lambda b, i: (b, li, 0, 0, 0))
    return pl.pallas_call(
        _attn_c_lat_kernel,
        grid=(DEC_BATCH, nq),
        in_specs=[qblk(COL_QC), full(0), full(W_C), qblk(COL_GC), cache, cache,
                  pl.BlockSpec((None, H_C, NA_TILES, GRID_W, LANES), lambda b, i: (li, 0, 0, 0, 0))],
        out_specs=pl.BlockSpec((tq, W_C), lambda b, i: (b * nq + i, 0)),
        out_shape=jax.ShapeDtypeStruct((DEC_BATCH * DEC_SEQ, W_C), BF16),
        scratch_shapes=[pltpu.VMEM((2 * tq, NA_WROWS * GRID_W), F32)],
        compiler_params=_params("arbitrary", "arbitrary"),
        name="attn_c_lat",
    )(proj, kv, kv, proj, cache_k, cache_v, tiles)


def _ssd_body(dt_ref, xs_ref, bc_ref, z_ref, h0_ref, params, y_ref, hs_ref, scratch, *, seq, static_loops, side):
    cw_ref, cb_ref, dtb_ref, alog_ref, dsk_ref, g_ref = params
    upad_s, xc_s, expo_s, expot_s, dtt_s, tot_s, bmt_s, yf_s, yb_s, st_s = scratch
    use_h0 = h0_ref is not None
    want_state = hs_ref is not None

    q = SSD_CHUNK
    nc = seq // q
    n_pair = H_B // 2
    n_hd = 2 * H_B
    pad = 8

    def loop(body, unroll=1):
        if static_loops:
            for c in range(nc):
                body(c, 0)
        else:
            lax.fori_loop(0, nc, body, 0, unroll=unroll)

    def chunk_rows(c):
        return slice(c * q, (c + 1) * q) if isinstance(c, int) else pl.ds(pl.multiple_of(c * q, q), q)

    upad_s[0:pad, :] = jnp.zeros((pad, CONV_DIM), F32)
    upad_s[pad + seq:2 * pad + seq, :] = jnp.zeros((pad, CONV_DIM), F32)
    upad_s[pad:pad + seq, 0:DI_B] = xs_ref[...]
    upad_s[pad:pad + seq, DI_B:CONV_DIM] = bc_ref[...]

    for c in range(nc):
        for cb_ in range(CONV_DIM // LANES):
            csl = slice(cb_ * LANES, (cb_ + 1) * LANES)
            acc = jnp.zeros((q, LANES), F32) + cb_ref[:, csl]
            for k in range(CONV_K):
                r0 = c * q + pad - CONV_K // 2 + k
                acc = acc + upad_s[r0:r0 + q, csl] * cw_ref[k:k + 1, csl]
            xc_s[c * q:(c + 1) * q, csl] = _silu(acc)
            side()

    a_row = -jnp.exp(alog_ref[...]) * LOG2E
    a_col = jnp.broadcast_to(a_row, (LANES, LANES)).T[0:n_hd, 0:1]
    ri = lax.broadcasted_iota(jnp.int32, (q, q), 0)
    ci = lax.broadcasted_iota(jnp.int32, (q, q), 1)
    ltri = (ri >= ci).astype(F32)
    fwd_lane = _lane((q, LANES)) < H_B
    fwd_row = lax.broadcasted_iota(jnp.int32, (n_hd, q), 0) < H_B

    def prep_body(c, carry):
        rows = chunk_rows(c)
        xdt = dt_ref[rows, 0:LANES] + dtb_ref[...]
        dtv = jnp.maximum(xdt, 0.0) + jnp.log1p(jnp.exp(-jnp.abs(xdt)))
        la = dtv * a_row
        acum = _dot(ltri, la, HI)
        expo_s[rows, :] = jnp.where(fwd_lane, acum, la - acum)
        acum_t = acum.T[0:n_hd, :]
        dt_t = dtv.T[0:n_hd, :]
        expot_s[c] = jnp.where(fwd_row, acum_t, dt_t * a_col - acum_t)
        dtt_s[c] = dt_t
        tot_s[c] = jnp.broadcast_to(acum_t[:, q - 1:q], (n_hd, q))
        bmt_s[c] = xc_s[rows, DI_B:DI_B + LANES].T
        side()
        return carry

    loop(prep_body, unroll=2)

    if use_h0:
        st_s[...] = h0_ref[...].reshape(2, n_pair, N_B, LANES)
    else:
        st_s[...] = jnp.zeros_like(st_s)

    lane_q = _lane((q, LANES))
    lo = lane_q < HALF
    lo_st = _lane((N_B, LANES)) < HALF

    def chunk_pair(c_fwd, c_bwd):
        dirs = ((0, c_fwd, yf_s), (1, c_bwd, yb_s))
        group_of = lambda k: k * G_B // n_pair
        cb, y_off, st_in = {}, {}, {}
        for dirn, c, _ in dirs:
            rows = chunk_rows(c)
            bm16 = xc_s[rows, DI_B:DI_B + LANES].astype(BF16)
            cm = xc_s[rows, DI_B + LANES:DI_B + 2 * LANES]
            for g in range(G_B):
                in_g = (lane_q >= g * N_B) & (lane_q < (g + 1) * N_B)
                cmg = jnp.where(in_g, cm, 0.0).astype(BF16)
                cb[dirn, g] = _dot_nt(cmg, bm16)
                for k in range(g * n_pair // G_B, (g + 1) * n_pair // G_B):
                    st_in[dirn, k] = st_s[dirn, k]
                    st2 = jnp.concatenate([st_in[dirn, k]] * 2, axis=0).astype(BF16)
                    y_off[dirn, k] = _dot(cmg, st2)
        for k in range(n_pair):
            psl = slice(k * LANES, (k + 1) * LANES)
            for dirn, c, y_s in dirs:
                rows = chunk_rows(c)
                tri = (ri >= ci) if dirn == 0 else (ci >= ri)
                bmt_g = bmt_s[c, group_of(k) * N_B:(group_of(k) + 1) * N_B, :]
                x16 = xc_s[rows, psl].astype(BF16)
                mats, lhs, ysc, cdec = [], [], [], []
                for s in range(2):
                    col = dirn * H_B + 2 * k + s
                    e_col = jnp.broadcast_to(expo_s[rows, col:col + 1], (q, q))
                    e_row = expot_s[c, col:col + 1, :]
                    dt_row = dtt_s[c, col:col + 1, :]
                    tot = tot_s[c, col:col + 1, :]
                    dec = jnp.exp2(jnp.where(tri, e_col - e_row, NEG_INF))
                    mats.append((cb[dirn, group_of(k)] * dec * dt_row).astype(BF16))
                    if dirn == 0:
                        ysc.append(jnp.exp2(e_col))
                        w_row = jnp.exp2(tot - e_row)
                    else:
                        ysc.append(jnp.exp2(e_col + tot))
                        w_row = jnp.exp2(-e_row)
                    lhs.append((bmt_g * (w_row * dt_row)).astype(BF16))
                    cdec.append(jnp.exp2(tot[:, 0:LANES]))
                yd = _dot(jnp.concatenate(mats, axis=0), x16)
                ds = _dot(jnp.concatenate(lhs, axis=0), x16)
                yo, st = y_off[dirn, k], st_in[dirn, k]
                y_s[rows, psl] = jnp.where(lo, yd[:q] + ysc[0] * yo, yd[q:] + ysc[1] * yo)
                st_s[dirn, k] = jnp.where(lo_st, cdec[0] * st + ds[:N_B], cdec[1] * st + ds[N_B:])
                side()

    def body(c, carry):
        chunk_pair(c, nc - 1 - c)
        return carry

    loop(body, unroll=2)

    dsum = dsk_ref[0:1, :] + dsk_ref[1:2, :]

    def out_body(c, carry):
        rows = chunk_rows(c)
        y = yf_s[rows, :] + yb_s[rows, :] + xc_s[rows, 0:DI_B] * dsum
        y = y * _silu(z_ref[rows, :])
        y = y * lax.rsqrt(jnp.mean(y * y, axis=-1, keepdims=True) + EPS) * g_ref[...]
        y_ref[rows, :] = y.astype(BF16)
        side()
        return carry

    loop(out_body)
    if want_state:
        for dirn in range(2):
            for k in range(n_pair):
                st = st_s[dirn, k]
                st_t = jnp.concatenate([st, st], axis=0).T
                for s in range(2):
                    hs_ref[dirn, 2 * k + s] = st_t[s * P_B:(s + 1) * P_B, 0:N_B]


def _ssd_scratch(seq):
    nc = seq // SSD_CHUNK
    per_chunk_rows = pltpu.VMEM((nc, 2 * H_B, SSD_CHUNK), F32)
    return [pltpu.VMEM((seq + 16, CONV_DIM), F32), pltpu.VMEM((seq, CONV_DIM), F32),
            pltpu.VMEM((seq, LANES), F32), per_chunk_rows, per_chunk_rows, per_chunk_rows,
            pltpu.VMEM((nc, LANES, SSD_CHUNK), F32),
            pltpu.VMEM((seq, DI_B), F32), pltpu.VMEM((seq, DI_B), F32),
            pltpu.VMEM((2, H_B // 2, N_B, LANES), F32)]


def _ssd_param_specs(const):
    return [const((CONV_K, CONV_DIM)), const((1, CONV_DIM)), const((1, LANES)), const((1, LANES)),
            const((2, DI_B)), const((1, DI_B))]


def _proj_ssd_kernel(*refs, nb, seq, ctx, n_carry):
    x_ref, ada_ref, g_ref, w_ref, wdt_ref = refs[:5]
    pos = 5
    h0_ref = None
    if not ctx:
        h0_ref = refs[pos]
        pos += 1
    params = refs[pos:pos + 6]
    outs = refs[pos + 6 + n_carry:]
    if ctx:
        proj_ref, ka_ref, va_ref, kc_ref, vc_ref, y_ref, hs_ref = outs[:7]
        outs = outs[7:]
    else:
        proj_ref, kv_ref, y_ref = outs[:3]
        hs_ref = None
        outs = outs[3:]
    h_s, p_s = outs[:2]
    scratch = outs[2:]

    row = 0 if ctx else 1 + pl.program_id(0)
    for r0 in range(0, nb * seq, SEQ):
        h_s[r0:r0 + SEQ, :] = _modulated_norm(x_ref[r0:r0 + SEQ, :], g_ref, ada_ref, row)
    for c0 in range(P_Z, P_DT, SIDE_TN):
        p_s[:, c0:c0 + SIDE_TN] = _dot_nt(h_s[...], w_ref[_SRC["z"] + c0:_SRC["z"] + c0 + SIDE_TN, :])
    p_s[:, P_DT:P_COLS] = _dot_nt(h_s[...], wdt_ref[...])

    work = []
    for col, src in PROJ_SEGMENTS:
        for off in range(0, W_A, SIDE_TN):
            def tile(d=col + off, s=src + off):
                proj_ref[:, d:d + SIDE_TN] = _dot_nt(h_s[...], w_ref[s:s + SIDE_TN, :])
            work.append(tile)
            if ctx and col in (COL_KA, COL_VA):
                def store(dst=ka_ref if col == COL_KA else va_ref, col=col, off=off):
                    for b in range(nb):
                        for h in range(off // LANES, (off + SIDE_TN) // LANES):
                            dst[b, :, h, :] = proj_ref[b * seq:(b + 1) * seq, col + h * LANES:col + (h + 1) * LANES]
                work.append(store)
    for r0 in range(0, KVC_COLS, SIDE_TN):
        if ctx:
            for b in range(nb):
                def tile_t(b=b, r0=r0):
                    w_rows = w_ref[_SRC["kc"] + r0:_SRC["kc"] + r0 + SIDE_TN, :]
                    kv_t = _dot_nt(w_rows, h_s[b * seq:(b + 1) * seq, :])
                    dst, d0 = (kc_ref, r0) if r0 < W_C else (vc_ref, r0 - W_C)
                    dst[b, d0 // DH_C:(d0 + SIDE_TN) // DH_C] = kv_t.reshape(SIDE_TN // DH_C, DH_C, seq)
                work.append(tile_t)
        else:
            def tile_kv(r0=r0):
                kv_ref[:, r0:r0 + SIDE_TN] = _dot_nt(h_s[...], w_ref[_SRC["kc"] + r0:_SRC["kc"] + r0 + SIDE_TN, :])
            work.append(tile_kv)

    n_slots = nb * (seq // SSD_CHUNK) * (CONV_DIM // LANES + 2 + 2 * (H_B // 2))
    state = dict(slot=0, done=0)

    def side():
        state["slot"] += 1
        target = min(len(work), -(-state["slot"] * len(work) // n_slots))
        while state["done"] < target:
            work[state["done"]]()
            state["done"] += 1

    for b in range(nb):
        rows = pl.ds(b * seq, seq)
        _ssd_body(p_s.at[rows, pl.ds(P_DT, LANES)], p_s.at[rows, pl.ds(P_XS, DI_B)],
                  p_s.at[rows, pl.ds(P_BC, BC_DIM)], p_s.at[rows, pl.ds(P_Z, DI_B)], h0_ref, params,
                  y_ref.at[rows, :], None if hs_ref is None else hs_ref.at[b], scratch,
                  seq=seq, static_loops=True, side=side)
    while state["done"] < len(work):
        work[state["done"]]()
        state["done"] += 1


def _proj_ssd_call(x, ada, norm_g, w16, w_dt, ssd_w, li, *, ctx, carry=None, h0t=None):
    nb, seq = (CTX_BATCHES_PER_STEP, SEQ) if ctx else (1, DEC_SEQ)
    tm = nb * seq
    t = x.shape[0]
    n_carry = 0 if carry is None else len(carry)
    const = lambda shape: pl.BlockSpec(shape, lambda i: (0,) * len(shape))
    once = pl.Buffered(1)
    big = {} if ctx else dict(pipeline_mode=pl.Buffered(1))
    in_specs = [pl.BlockSpec((tm, D_MODEL), lambda i: (i, 0), **big),
                pl.BlockSpec((None, 8, 3 * D_MODEL), lambda i: (li, 0, 0)),
                pl.BlockSpec((None, 1, D_MODEL), lambda i: (li, 0, 0)),
                pl.BlockSpec((None, _SRC["merge"], D_MODEL), lambda i: (li, 0, 0), pipeline_mode=once),
                pl.BlockSpec((None, LANES, D_MODEL), lambda i: (li, 0, 0))]
    args = [x, ada, norm_g, w16, w_dt]
    if not ctx:
        in_specs.append(pl.BlockSpec((None, None, 2, (H_B // 2) * N_B, LANES), lambda i: (i, li, 0, 0, 0)))
        args.append(h0t)
    in_specs += _ssd_param_specs(const)
    args += list(ssd_w)
    out_specs = [pl.BlockSpec((tm, PROJ_COLS), lambda i: (i, 0), **big)]
    out_shape = [jax.ShapeDtypeStruct((t, PROJ_COLS), F32)]
    aliases = {}
    if ctx:
        out_specs += [pl.BlockSpec((nb, None, SEQ, H_A, 2 * DH_A), lambda i: (i, li, 0, 0, 0))] * 2
        out_specs += [pl.BlockSpec((nb, None, H_C, DH_C, SEQ), lambda i: (i, li, 0, 0, 0))] * 2
        out_shape += [jax.ShapeDtypeStruct((BATCH, DEPTH, SEQ, H_A, 2 * DH_A), F32)] * 2
        out_shape += [jax.ShapeDtypeStruct((BATCH, DEPTH, H_C, DH_C, SEQ), F32)] * 2
    else:
        out_specs.append(pl.BlockSpec((tm, KVC_COLS), lambda i: (i, 0), **big))
        out_shape.append(jax.ShapeDtypeStruct((t, KVC_COLS), F32))
    out_specs.append(pl.BlockSpec((tm, DI_B), lambda i: (i, 0)))
    out_shape.append(jax.ShapeDtypeStruct((t, DI_B), BF16))
    if ctx:
        out_specs.append(pl.BlockSpec((nb, None, 2, H_B, P_B, N_B), lambda i: (i, li, 0, 0, 0, 0)))
        out_shape.append(jax.ShapeDtypeStruct((BATCH, DEPTH, 2, H_B, P_B, N_B), F32))
        if carry is not None:
            in_specs += [pl.BlockSpec(memory_space=pl.ANY)] * n_carry
            aliases = {len(args) + k: (1, 2, 3, 4, 6)[k] for k in range(n_carry)}
            args += list(carry)
    scratch = [pltpu.VMEM((tm, D_MODEL), BF16), pltpu.VMEM((tm, P_COLS), F32)] + _ssd_scratch(seq)
    return pl.pallas_call(
        functools.partial(_proj_ssd_kernel, nb=nb, seq=seq, ctx=ctx, n_carry=n_carry),
        grid=(t // tm,),
        in_specs=in_specs,
        out_specs=out_specs,
        out_shape=out_shape,
        input_output_aliases=aliases,
        scratch_shapes=scratch,
        compiler_params=_params("arbitrary"),
        name="ctx_proj_ssd" if ctx else "lat_proj_ssd",
    )(*args)


def _post_kernel(x_ref, ya_ref, yb_ref, yc_ref, ada_ref, g_ref, wm_ref, wa_ref, wb_ref, wc_ref, wo_ref, fg_ref,
                 o_ref, *, tm, row_base, tokens_per_row, final):
    row = row_base + (pl.program_id(0) * tm) // tokens_per_row
    gate = ada_ref[pl.ds(row, 1), 2 * D_MODEL:3 * D_MODEL]
    d = D_MODEL
    x = x_ref[...]
    h = _modulated_norm(x, g_ref, ada_ref, row)
    merged = None
    for n, (y_ref, w_ref) in enumerate(((ya_ref, wa_ref), (yb_ref, wb_ref), (yc_ref, wc_ref))):
        r0 = _SRC["merge"] + n * d
        logits = _dot_nt(h, wm_ref[r0:r0 + d, :])
        term = _sigmoid(logits) * _dot(y_ref[...], w_ref[...])
        merged = term if merged is None else merged + term
    x = x + gate * _dot(merged.astype(BF16), wo_ref[...])
    if final:
        x = x * lax.rsqrt(jnp.mean(x * x, axis=-1, keepdims=True) + EPS) * fg_ref[...]
    o_ref[...] = x


def _post_call(x, ya, yb, yc, ada, norm_g, w_merge_t, li, wa, wb, wc, wo, final_g, *, tm, row_base,
               tokens_per_row, final):
    t = x.shape[0]
    tok = lambda w: pl.BlockSpec((tm, w), lambda i: (i, 0))
    layer = lambda *shape, **kw: pl.BlockSpec((None,) + shape, lambda i: (li,) + (0,) * len(shape), **kw)
    once = dict(pipeline_mode=pl.Buffered(1))
    kern = functools.partial(_post_kernel, tm=tm, row_base=row_base, tokens_per_row=tokens_per_row, final=final)
    return pl.pallas_call(
        kern,
        grid=(t // tm,),
        in_specs=[tok(D_MODEL), tok(W_A), tok(DI_B), tok(W_C),
                  layer(8, 3 * D_MODEL), layer(1, D_MODEL), layer(w_merge_t.shape[1], D_MODEL, **once),
                  layer(W_A, D_MODEL, **once), layer(DI_B, D_MODEL, **once), layer(W_C, D_MODEL, **once),
                  layer(D_MODEL, D_MODEL, **once), pl.BlockSpec((1, D_MODEL), lambda i: (0, 0))],
        out_specs=tok(D_MODEL),
        out_shape=jax.ShapeDtypeStruct((t, D_MODEL), F32),
        compiler_params=_params("arbitrary"),
        name="post_final" if final else "post",
    )(x, ya, yb, yc, ada, norm_g, w_merge_t, wa, wb, wc, wo, final_g)


def _rope_tables():
    pos = np.arange(DEC_SEQ)
    lane = np.arange(LANES)
    l64 = lane % (2 * (DH_A // 2))
    quarter = DH_A // 4
    p = np.where((l64 < DH_A // 2)[None, :], (pos // GRID_W)[:, None], (pos % GRID_W)[:, None])
    inv = ROPE_BASE ** (-np.arange(quarter, dtype=np.float64) / quarter)
    ang = p.astype(np.float64) * inv[l64 % quarter][None, :]
    sign = np.where((lane % (2 * quarter)) < quarter, -1.0, 1.0)
    return jnp.asarray(np.cos(ang), F32), jnp.asarray(np.sin(ang) * sign[None, :], F32)


def _pad_lanes(v, width=LANES):
    v = v.reshape(1, -1).astype(F32)
    return jnp.pad(v, ((0, 0), (0, width - v.shape[1])))


def kernel(x_prompt, x_sample, cache_diff_k, cache_diff_v, cache_na_k, cache_na_v, state_ssd, c, c_ctx,
           norm_g, w_ada, b_ada, w_in, lam_q1, lam_k1, lam_q2, lam_k2, diff_subln_g, conv_w, conv_b,
           dt_bias, a_log, d_skip, ssd_norm_g, na_rpb, w_br_a, w_br_b, w_br_c, w_out, final_g):
    assert x_prompt.shape == (BATCH, SEQ, D_MODEL) and x_sample.shape == (DEC_BATCH, DEC_SEQ, D_MODEL)
    assert w_in.shape == (DEPTH, D_MODEL, _SRC["merge"] + MERGE_COLS)
    w16 = jnp.swapaxes(w_in, 1, 2).astype(BF16)
    w_dt = jnp.pad(w16[:, _SRC["dt"]:_SRC["qc"], :], ((0, 0), (0, LANES - 2 * H_B), (0, 0)))
    wa16, wb16, wc16, wo16 = (w.astype(BF16) for w in (w_br_a, w_br_b, w_br_c, w_out))

    cvecs = jnp.concatenate([c_ctx[None, :], c, jnp.zeros((8 - 1 - DEC_BATCH, D_MODEL), F32)], axis=0)
    ada = _ada_call(cvecs, w_ada, b_ada)
    cos_t, sin_t = _rope_tables()

    ck_a = cache_diff_k.reshape(DEC_BATCH, DEPTH, PAST_LEN, W_A)
    cv_a = cache_diff_v.reshape(DEC_BATCH, DEPTH, PAST_LEN, W_A)
    ck_c = cache_na_k.transpose(0, 1, 3, 4, 2)
    cv_c = cache_na_v.transpose(0, 1, 3, 4, 2)
    na_tiles = _rpb_call(na_rpb)
    h0t = state_ssd.transpose(0, 1, 2, 5, 3, 4).reshape(DEC_BATCH, DEPTH, 2, N_B, DI_B)
    h0t = h0t.reshape(DEC_BATCH, DEPTH, 2, N_B, H_B // 2, LANES).transpose(0, 1, 2, 4, 3, 5)
    h0t = h0t.reshape(DEC_BATCH, DEPTH, 2, (H_B // 2) * N_B, LANES)

    xp = x_prompt.reshape(BATCH * SEQ, D_MODEL)
    xs = x_sample.reshape(DEC_BATCH * DEC_SEQ, D_MODEL)
    fg = final_g.reshape(1, D_MODEL)
    norm_g3 = norm_g.reshape(DEPTH, 1, D_MODEL)
    carry = None
    for li in range(DEPTH):
        lam_init = 0.8 - 0.6 * math.exp(-0.3 * li)
        final = li == DEPTH - 1
        lamvec = jnp.concatenate([_pad_lanes(v[li]) for v in (lam_q1, lam_k1, lam_q2, lam_k2)], axis=0)
        subln = diff_subln_g[li].reshape(1, LANES)
        dtb = _pad_lanes(dt_bias[li])
        alog = _pad_lanes(a_log[li])
        dskx = jnp.repeat(d_skip[li], P_B, axis=-1)
        ssd_w = (conv_w[li], conv_b[li].reshape(1, CONV_DIM), dtb, alog, dskx, ssd_norm_g[li].reshape(1, DI_B))
        post_w = (wa16, wb16, wc16, wo16, fg)

        proj, ka, va, kc_t, vc_t, yb, ssd_state = _proj_ssd_call(xp, ada, norm_g3, w16, w_dt, ssd_w, li,
                                                                 ctx=True, carry=carry)
        carry = (ka, va, kc_t, vc_t, ssd_state)
        ya = _attn_a_ctx_call(proj, lamvec, subln, lam_init)
        yc = _attn_c_ctx_call(proj, kc_t, vc_t, li)
        xp = _post_call(xp, ya, yb, yc, ada, norm_g3, w16, li, *post_w, tm=512, row_base=0,
                        tokens_per_row=BATCH * SEQ, final=final)

        proj, kv, yb = _proj_ssd_call(xs, ada, norm_g3, w16, w_dt, ssd_w, li, ctx=False, h0t=h0t)
        ya = _attn_a_lat_call(proj, ck_a, cv_a, li, cos_t, sin_t, lamvec, subln, lam_init)
        yc = _attn_c_lat_call(proj, kv, ck_c, cv_c, li, na_tiles)
        xs = _post_call(xs, ya, yb, yc, ada, norm_g3, w16, li, *post_w, tm=512, row_base=1,
                        tokens_per_row=DEC_SEQ, final=final)

    new_k_a, new_v_a, new_k_c_t, new_v_c_t, new_ssd = carry
    to_token_major = lambda a: a.transpose(0, 1, 4, 2, 3)
    return (xp.reshape(BATCH, SEQ, D_MODEL), xs.reshape(DEC_BATCH, DEC_SEQ, D_MODEL),
            new_k_a, new_v_a, to_token_major(new_k_c_t), to_token_major(new_v_c_t), new_ssd)
```

```python
import functools
import math

import jax
import jax.numpy as jnp
import numpy as np
from jax import lax
from jax.experimental import pallas as pl
from jax.experimental.pallas import tpu as pltpu

D_MODEL = 1024
BATCH = 32
SEQ = 256
DEPTH = 2
DEC_BATCH = 2
DEC_SEQ = 1024
PAST_LEN = 512
GRID_W = 64
GRID_ROWS = DEC_SEQ // GRID_W
H_A = 4
DH_A = 64
W_A = H_A * 2 * DH_A
H_B = 8
P_B = 64
G_B = 2
N_B = 64
DI_B = H_B * P_B
CONV_K = 5
CONV_DIM = DI_B + 2 * G_B * N_B
SSD_CHUNK = 128
H_C = 8
DH_C = 64
W_C = H_C * DH_C
NA_KH = 8
NA_KW = 16
N_BRANCH = 3
ROPE_BASE = 10000.0
EPS = 1e-6

LANES = 128
HALF = LANES // 2
VMEM_LIMIT = 56 * 1024 * 1024

BC_DIM = CONV_DIM - DI_B
KVC_COLS = 2 * W_C
MERGE_COLS = N_BRANCH * D_MODEL
_SRC = dict(qa=0, ka=512, va=1024, ga=1536, z=2048, xs=2560, bc=3072, dt=3328, qc=3344, kc=3856, vc=4368,
            gc=4880, merge=5392)
COL_QA = 0
COL_GA = 512
COL_QC = 1024
COL_GC = 1536
COL_KA = 2048
COL_VA = 2560
PROJ_COLS = 3072
PROJ_SEGMENTS = ((COL_QA, _SRC["qa"]), (COL_GA, _SRC["ga"]), (COL_QC, _SRC["qc"]), (COL_GC, _SRC["gc"]),
                 (COL_KA, _SRC["ka"]), (COL_VA, _SRC["va"]))
P_Z = 0
P_XS = 512
P_BC = 1024
P_DT = 1280
P_COLS = P_DT + LANES
SIDE_TN = 256

NA_QROWS = 4
NA_WROWS = 12
NA_TILES = 2 * NA_KH
NEG_INF = float("-inf")
LOG2E = math.log2(math.e)
ATTN_AHEAD = 2
CTX_BATCHES_PER_STEP = 2
ATTN_BATCHES_PER_STEP = 4
HI = lax.Precision.HIGHEST
F32 = jnp.float32
BF16 = jnp.bfloat16


def _dot(a, b, precision=None):
    return jnp.dot(a, b, preferred_element_type=F32, precision=precision)


def _dot_nt(a, b):
    return lax.dot_general(a, b, (((1,), (1,)), ((), ())), preferred_element_type=F32)


def _sigmoid(x):
    return 1.0 / (1.0 + jnp.exp(-x))


def _silu(x):
    return x * _sigmoid(x)


def _lane(shape):
    return lax.broadcasted_iota(jnp.int32, shape, len(shape) - 1)


def _params(*sem):
    return pltpu.CompilerParams(dimension_semantics=sem, vmem_limit_bytes=VMEM_LIMIT)


def _ada_kernel(cvt_ref, w_ref, b_ref, o_ref):
    n_rows = 1 + DEC_BATCH
    s = _silu(cvt_ref[...])
    accs = [jnp.zeros((8, w_ref.shape[1]), F32)] * n_rows
    for k0 in range(0, D_MODEL, 8):
        w = w_ref[k0:k0 + 8, :]
        accs = [acc + w * s[k0:k0 + 8, r:r + 1] for r, acc in enumerate(accs)]
    rows = [jnp.sum(acc, axis=0, keepdims=True) for acc in accs]
    rows.append(jnp.zeros((8 - n_rows, w_ref.shape[1]), F32))
    o_ref[...] = jnp.concatenate(rows, axis=0) + b_ref[...]


def _ada_call(cvecs, w_ada, b_ada):
    tn = 512
    return pl.pallas_call(
        _ada_kernel,
        grid=(DEPTH, 3 * D_MODEL // tn),
        in_specs=[
            pl.BlockSpec((D_MODEL, 8), lambda l, j: (0, 0)),
            pl.BlockSpec((None, D_MODEL, tn), lambda l, j: (l, 0, j)),
            pl.BlockSpec((None, 1, tn), lambda l, j: (l, 0, j)),
        ],
        out_specs=pl.BlockSpec((None, 8, tn), lambda l, j: (l, 0, j)),
        out_shape=jax.ShapeDtypeStruct((DEPTH, 8, 3 * D_MODEL), F32),
        compiler_params=_params("arbitrary", "arbitrary"),
        name="ada",
    )(cvecs, w_ada, b_ada.reshape(DEPTH, 1, 3 * D_MODEL))


def _modulated_norm(x, g_ref, ada_ref, row):
    y = x * lax.rsqrt(jnp.mean(x * x, axis=-1, keepdims=True) + EPS) * g_ref[...]
    shift = ada_ref[pl.ds(row, 1), 0:D_MODEL]
    scale = ada_ref[pl.ds(row, 1), D_MODEL:2 * D_MODEL]
    return (y * (1.0 + scale) + shift).astype(BF16)


def _diff_lambda_in_kernel(lam_ref, lam_init):
    v = lam_ref[...]
    l1 = jnp.sum(v[0:1] * v[1:2], axis=-1, keepdims=True)
    l2 = jnp.sum(v[2:3] * v[3:4], axis=-1, keepdims=True)
    return jnp.exp(l1) - jnp.exp(l2) + lam_init


def _split_halves(x, scale):
    lo = _lane(x.shape) < HALF
    xs = x * (scale * LOG2E)
    return jnp.concatenate([jnp.where(lo, xs, 0.0), jnp.where(lo, 0.0, xs)], axis=0).astype(BF16)


def _diff_combine(o2, rsum, lam, t):
    return o2[:t] * rsum[:t] - (lam * rsum[t:]) * o2[t:]


def _diff_head_post(o, subln_g, lam_init, gate):
    o = o * lax.rsqrt(jnp.mean(o * o, axis=-1, keepdims=True) + EPS) * (subln_g * (1.0 - lam_init))
    return (o * _silu(gate)).astype(BF16)


def _attn_a_ctx_kernel(q_ref, k_ref, v_ref, g_ref, lam_ref, sg_ref, o_ref, *, lam_init):
    t = SEQ
    lam = _diff_lambda_in_kernel(lam_ref, lam_init)
    ones = jnp.ones((t, LANES), BF16)
    blocks = [(b, h) for b in range(q_ref.shape[0] // t) for h in range(H_A)]
    where = lambda b, h: (slice(b * t, (b + 1) * t), slice(h * LANES, (h + 1) * LANES))

    def scores(b, h):
        qq = _split_halves(q_ref[where(b, h)], DH_A ** -0.5)
        return _dot_nt(qq, k_ref[where(b, h)].astype(BF16))

    pending = [scores(*blk) for blk in blocks[:ATTN_AHEAD]]
    for n, blk in enumerate(blocks):
        s = pending.pop(0)
        if n + ATTN_AHEAD < len(blocks):
            pending.append(scores(*blocks[n + ATTN_AHEAD]))
        e = jnp.exp2(s - jnp.max(s, axis=-1, keepdims=True)).astype(BF16)
        rsum = 1.0 / _dot(e, ones)
        o = _diff_combine(_dot(e, v_ref[where(*blk)].astype(BF16)), rsum, lam, t)
        o_ref[where(*blk)] = _diff_head_post(o, sg_ref[...], lam_init, g_ref[where(*blk)])


def _attn_a_ctx_call(proj, lamvec, subln_g, lam_init):
    rows = ATTN_BATCHES_PER_STEP * SEQ
    blk = lambda c: pl.BlockSpec((rows, W_A), lambda b: (b, c // W_A))
    return pl.pallas_call(
        functools.partial(_attn_a_ctx_kernel, lam_init=lam_init),
        grid=(BATCH // ATTN_BATCHES_PER_STEP,),
        in_specs=[blk(COL_QA), blk(COL_KA), blk(COL_VA), blk(COL_GA),
                  pl.BlockSpec((4, LANES), lambda b: (0, 0)),
                  pl.BlockSpec((1, LANES), lambda b: (0, 0))],
        out_specs=pl.BlockSpec((rows, W_A), lambda b: (b, 0)),
        out_shape=jax.ShapeDtypeStruct((BATCH * SEQ, W_A), BF16),
        compiler_params=_params("arbitrary"),
        name="attn_a_ctx",
    )(proj, proj, proj, proj, lamvec, subln_g)


def _rope(x, cos, sin_signed):
    first = (_lane(x.shape) % 32) < 16
    swapped = jnp.where(first, pltpu.roll(x, LANES - 16, 1), pltpu.roll(x, 16, 1))
    return x * cos + swapped * sin_signed


def _attn_a_lat_kernel(q_ref, k_ref, v_ref, g_ref, ck_ref, cv_ref, cosq_ref, sinq_ref, cosk_ref, sink_ref,
                       lam_ref, sg_ref, o_ref, kr_s, *, lam_init):
    tq = q_ref.shape[0]

    @pl.when(pl.program_id(1) == 0)
    def _():
        for h in range(H_A):
            sl = slice(h * LANES, (h + 1) * LANES)
            kr_s[:, sl] = _rope(k_ref[:, sl], cosk_ref[...], sink_ref[...]).astype(BF16)

    lam = _diff_lambda_in_kernel(lam_ref, lam_init)

    def scores(h):
        sl = slice(h * LANES, (h + 1) * LANES)
        qq = _split_halves(_rope(q_ref[:, sl], cosq_ref[...], sinq_ref[...]), DH_A ** -0.5)
        return _dot_nt(qq, kr_s[:, sl]), _dot_nt(qq, ck_ref[:, sl].astype(BF16))

    pending = [scores(h) for h in range(ATTN_AHEAD)]
    for h in range(H_A):
        sl = slice(h * LANES, (h + 1) * LANES)
        s_lat, s_ctx = pending.pop(0)
        if h + ATTN_AHEAD < H_A:
            pending.append(scores(h + ATTN_AHEAD))
        m = jnp.maximum(jnp.max(s_lat, axis=-1, keepdims=True), jnp.max(s_ctx, axis=-1, keepdims=True))
        e_lat = jnp.exp2(s_lat - m)
        e_ctx = jnp.exp2(s_ctx - m)
        rsum = 1.0 / (jnp.sum(e_lat, axis=-1, keepdims=True) + jnp.sum(e_ctx, axis=-1, keepdims=True))
        o2 = _dot(e_lat.astype(BF16), v_ref[:, sl].astype(BF16)) + _dot(e_ctx.astype(BF16),
                                                                         cv_ref[:, sl].astype(BF16))
        o = _diff_combine(o2, rsum, lam, tq)
        o_ref[:, sl] = _diff_head_post(o, sg_ref[...], lam_init, g_ref[:, sl])


def _attn_a_lat_call(proj, cache_k, cache_v, li, cos_t, sin_t, lamvec, subln_g, lam_init):
    tq = 256
    nq = DEC_SEQ // tq
    qblk = lambda c: pl.BlockSpec((tq, W_A), lambda b, i: (b * nq + i, c // W_A))
    full = lambda c: pl.BlockSpec((DEC_SEQ, W_A), lambda b, i: (b, c // W_A))
    cache = pl.BlockSpec((None, None, PAST_LEN, W_A), lambda b, i: (b, li, 0, 0))
    return pl.pallas_call(
        functools.partial(_attn_a_lat_kernel, lam_init=lam_init),
        grid=(DEC_BATCH, nq),
        in_specs=[qblk(COL_QA), full(COL_KA), full(COL_VA), qblk(COL_GA), cache, cache,
                  pl.BlockSpec((tq, LANES), lambda b, i: (i, 0)),
                  pl.BlockSpec((tq, LANES), lambda b, i: (i, 0)),
                  pl.BlockSpec((DEC_SEQ, LANES), lambda b, i: (0, 0)),
                  pl.BlockSpec((DEC_SEQ, LANES), lambda b, i: (0, 0)),
                  pl.BlockSpec((4, LANES), lambda b, i: (0, 0)),
                  pl.BlockSpec((1, LANES), lambda b, i: (0, 0))],
        out_specs=pl.BlockSpec((tq, W_A), lambda b, i: (b * nq + i, 0)),
        out_shape=jax.ShapeDtypeStruct((DEC_BATCH * DEC_SEQ, W_A), BF16),
        scratch_shapes=[pltpu.VMEM((DEC_SEQ, W_A), BF16)],
        compiler_params=_params("arbitrary", "arbitrary"),
        name="attn_a_lat",
    )(proj, proj, proj, proj, cache_k, cache_v, cos_t, sin_t, cos_t, sin_t, lamvec, subln_g)


def _merge_halves(o, t):
    return jnp.where(_lane((t, LANES)) < HALF, o[:t], o[t:])


def _attn_c_ctx_kernel(q_ref, kt_ref, vt_ref, g_ref, o_ref):
    t = SEQ
    blocks = [(b, j) for b in range(q_ref.shape[0] // t) for j in range(H_C // 2)]
    where = lambda b, j: (slice(b * t, (b + 1) * t), slice(j * LANES, (j + 1) * LANES))
    pair_t = lambda ref, b, j: ref[b, 2 * j:2 * j + 2].reshape(LANES, t).astype(BF16)

    def scores(b, j):
        return _dot(_split_halves(q_ref[where(b, j)], DH_C ** -0.5), pair_t(kt_ref, b, j))

    pending = [scores(*blk) for blk in blocks[:ATTN_AHEAD]]
    for n, blk in enumerate(blocks):
        s = pending.pop(0)
        if n + ATTN_AHEAD < len(blocks):
            pending.append(scores(*blocks[n + ATTN_AHEAD]))
        e = jnp.exp2(s - jnp.max(s, axis=-1, keepdims=True))
        rsum = 1.0 / jnp.sum(e, axis=-1, keepdims=True)
        o = _merge_halves(_dot_nt(e.astype(BF16), pair_t(vt_ref, *blk)) * rsum, t)
        o_ref[where(*blk)] = (o * _silu(g_ref[where(*blk)])).astype(BF16)


def _attn_c_ctx_call(proj, kc_t, vc_t, li):
    nb = ATTN_BATCHES_PER_STEP
    blk = lambda c: pl.BlockSpec((nb * SEQ, W_C), lambda b: (b, c // W_C))
    cache = pl.BlockSpec((nb, None, H_C, DH_C, SEQ), lambda b: (b, li, 0, 0, 0))
    return pl.pallas_call(
        _attn_c_ctx_kernel,
        grid=(BATCH // nb,),
        in_specs=[blk(COL_QC), cache, cache, blk(COL_GC)],
        out_specs=pl.BlockSpec((nb * SEQ, W_C), lambda b: (b, 0)),
        out_shape=jax.ShapeDtypeStruct((BATCH * SEQ, W_C), BF16),
        compiler_params=_params("arbitrary"),
        name="attn_c_ctx",
    )(proj, kc_t, vc_t, proj)


def _rpb_kernel(rpb_ref, o_ref):
    shape = (GRID_W, LANES)
    c = lax.broadcasted_iota(jnp.int32, shape, 0)
    cp = _lane(shape) % GRID_W
    start = jnp.clip(c - NA_KW // 2, 0, GRID_W - NA_KW)
    in_win = (cp >= start) & (cp < start + NA_KW)
    for h in range(H_C):
        o_ref[h, 0] = jnp.full(shape, NEG_INF, F32)
        for dr in range(2 * NA_KH - 1):
            row = jnp.broadcast_to(rpb_ref[h, dr:dr + 1, :], shape)
            tile = pltpu.roll(row, LANES - (NA_KW - 1), 1, stride=1, stride_axis=0)
            o_ref[h, 1 + dr] = jnp.where(in_win, tile * LOG2E, NEG_INF)


def _rpb_call(rpb):
    n_dc = 2 * NA_KW - 1
    v = jnp.pad(rpb, ((0, 0), (0, 0), (0, NA_TILES - (2 * NA_KH - 1)), (0, GRID_W - n_dc)))
    v = jnp.concatenate([v] * (LANES // GRID_W), axis=-1)
    return pl.pallas_call(
        _rpb_kernel,
        grid=(DEPTH,),
        in_specs=[pl.BlockSpec((None, H_C, NA_TILES, LANES), lambda l: (l, 0, 0, 0))],
        out_specs=pl.BlockSpec((None, H_C, NA_TILES, GRID_W, LANES), lambda l: (l, 0, 0, 0, 0)),
        out_shape=jax.ShapeDtypeStruct((DEPTH, H_C, NA_TILES, GRID_W, LANES), F32),
        compiler_params=_params("arbitrary"),
        name="rpb_tiles",
    )(v)


def _attn_c_lat_kernel(q_ref, k_ref, v_ref, g_ref, ck_ref, cv_ref, tile_ref, o_ref, bias_s):
    tq = q_ref.shape[0]
    nwin = NA_WROWS * GRID_W
    m = pl.program_id(1)
    w0 = jnp.where(m < (GRID_ROWS // NA_QROWS) // 2, 0, GRID_ROWS - NA_WROWS)
    k0 = pl.multiple_of(w0 * GRID_W, GRID_W)
    lo = _lane((GRID_W, LANES)) < HALF
    n_pair = H_C // 2

    def scores(j):
        sl = slice(j * LANES, (j + 1) * LANES)
        for s in range(2):
            for i in range(NA_QROWS):
                r = m * NA_QROWS + i
                start = jnp.clip(r - NA_KH // 2, 0, GRID_ROWS - NA_KH)
                for jp in range(NA_WROWS // 2):
                    idx = []
                    for u in range(2):
                        rk = w0 + 2 * jp + u
                        valid = (rk >= start) & (rk < start + NA_KH)
                        idx.append(jnp.where(valid, rk - r + NA_KH, 0))
                    tile = jnp.where(lo, tile_ref[2 * j + s, idx[0]], tile_ref[2 * j + s, idx[1]])
                    bias_s[(s * NA_QROWS + i) * GRID_W:(s * NA_QROWS + i + 1) * GRID_W,
                           jp * LANES:(jp + 1) * LANES] = tile
        qq = _split_halves(q_ref[:, sl], DH_C ** -0.5)
        kw = k_ref[pl.ds(k0, nwin), sl].astype(BF16)
        s_win = _dot_nt(qq, kw) + bias_s[...]
        ckt = ck_ref[2 * j:2 * j + 2].reshape(LANES, PAST_LEN).astype(BF16)
        return s_win, _dot(qq, ckt)

    pending = [scores(j) for j in range(ATTN_AHEAD)]
    for j in range(n_pair):
        sl = slice(j * LANES, (j + 1) * LANES)
        s_win, s_ctx = pending.pop(0)
        if j + ATTN_AHEAD < n_pair:
            pending.append(scores(j + ATTN_AHEAD))
        vw = v_ref[pl.ds(k0, nwin), sl].astype(BF16)
        mx = jnp.maximum(jnp.max(s_win, axis=-1, keepdims=True), jnp.max(s_ctx, axis=-1, keepdims=True))
        e_win = jnp.exp2(s_win - mx)
        e_ctx = jnp.exp2(s_ctx - mx)
        rs = 1.0 / (jnp.sum(e_win, axis=-1, keepdims=True) + jnp.sum(e_ctx, axis=-1, keepdims=True))
        cvt = cv_ref[2 * j:2 * j + 2].reshape(LANES, PAST_LEN).astype(BF16)
        o = (_dot(e_win.astype(BF16), vw) + _dot_nt(e_ctx.astype(BF16), cvt)) * rs
        o_ref[:, sl] = (_merge_halves(o, tq) * _silu(g_ref[:, sl])).astype(BF16)


def _attn_c_lat_call(proj, kv, cache_k, cache_v, li, tiles):
    tq = NA_QROWS * GRID_W
    nq = DEC_SEQ // tq
    qblk = lambda c: pl.BlockSpec((tq, W_C), lambda b, i: (b * nq + i, c // W_C))
    full = lambda c: pl.BlockSpec((DEC_SEQ, W_C), lambda b, i: (b, c // W_C))
    cache = pl.BlockSpec((None, None, H_C, DH_C, PAST_LEN), lambda b, i: (b, li, 0, 0, 0))
    return pl.pallas_call(
        _attn_c_lat_kernel,
        grid=(DEC_BATCH, nq),
        in_specs=[qblk(COL_QC), full(0), full(W_C), qblk(COL_GC), cache, cache,
                  pl.BlockSpec((None, H_C, NA_TILES, GRID_W, LANES), lambda b, i: (li, 0, 0, 0, 0))],
        out_specs=pl.BlockSpec((tq, W_C), lambda b, i: (b * nq + i, 0)),
        out_shape=jax.ShapeDtypeStruct((DEC_BATCH * DEC_SEQ, W_C), BF16),
        scratch_shapes=[pltpu.VMEM((2 * tq, NA_WROWS * GRID_W), F32)],
        compiler_params=_params("arbitrary", "arbitrary"),
        name="attn_c_lat",
    )(proj, kv, kv, proj, cache_k, cache_v, tiles)


def _ssd_body(dt_ref, xs_ref, bc_ref, z_ref, h0_ref, params, y_ref, hs_ref, scratch, *, seq, static_loops, side):
    cw_ref, cb_ref, dtb_ref, alog_ref, dsk_ref, g_ref = params
    upad_s, xc_s, expo_s, expot_s, dtt_s, tot_s, bmt_s, yf_s, yb_s, st_s = scratch
    use_h0 = h0_ref is not None
    want_state = hs_ref is not None

    q = SSD_CHUNK
    nc = seq // q
    n_pair = H_B // 2
    n_hd = 2 * H_B
    pad = 8

    def loop(body, unroll=1):
        if static_loops:
            for c in range(nc):
                body(c, 0)
        else:
            lax.fori_loop(0, nc, body, 0, unroll=unroll)

    def chunk_rows(c):
        return slice(c * q, (c + 1) * q) if isinstance(c, int) else pl.ds(pl.multiple_of(c * q, q), q)

    upad_s[0:pad, :] = jnp.zeros((pad, CONV_DIM), F32)
    upad_s[pad + seq:2 * pad + seq, :] = jnp.zeros((pad, CONV_DIM), F32)
    upad_s[pad:pad + seq, 0:DI_B] = xs_ref[...]
    upad_s[pad:pad + seq, DI_B:CONV_DIM] = bc_ref[...]

    for c in range(nc):
        for cb_ in range(CONV_DIM // LANES):
            csl = slice(cb_ * LANES, (cb_ + 1) * LANES)
            acc = jnp.zeros((q, LANES), F32) + cb_ref[:, csl]
            for k in range(CONV_K):
                r0 = c * q + pad - CONV_K // 2 + k
                acc = acc + upad_s[r0:r0 + q, csl] * cw_ref[k:k + 1, csl]
            xc_s[c * q:(c + 1) * q, csl] = _silu(acc)
            side()

    a_row = -jnp.exp(alog_ref[...]) * LOG2E
    a_col = jnp.broadcast_to(a_row, (LANES, LANES)).T[0:n_hd, 0:1]
    ri = lax.broadcasted_iota(jnp.int32, (q, q), 0)
    ci = lax.broadcasted_iota(jnp.int32, (q, q), 1)
    ltri = (ri >= ci).astype(F32)
    fwd_lane = _lane((q, LANES)) < H_B
    fwd_row = lax.broadcasted_iota(jnp.int32, (n_hd, q), 0) < H_B

    def prep_body(c, carry):
        rows = chunk_rows(c)
        xdt = dt_ref[rows, 0:LANES] + dtb_ref[...]
        dtv = jnp.maximum(xdt, 0.0) + jnp.log1p(jnp.exp(-jnp.abs(xdt)))
        la = dtv * a_row
        acum = _dot(ltri, la, HI)
        expo_s[rows, :] = jnp.where(fwd_lane, acum, la - acum)
        acum_t = acum.T[0:n_hd, :]
        dt_t = dtv.T[0:n_hd, :]
        expot_s[c] = jnp.where(fwd_row, acum_t, dt_t * a_col - acum_t)
        dtt_s[c] = dt_t
        tot_s[c] = jnp.broadcast_to(acum_t[:, q - 1:q], (n_hd, q))
        bmt_s[c] = xc_s[rows, DI_B:DI_B + LANES].T
        side()
        return carry

    loop(prep_body, unroll=2)

    if use_h0:
        st_s[...] = h0_ref[...].reshape(2, n_pair, N_B, LANES)
    else:
        st_s[...] = jnp.zeros_like(st_s)

    lane_q = _lane((q, LANES))
    lo = lane_q < HALF
    lo_st = _lane((N_B, LANES)) < HALF

    def chunk_pair(c_fwd, c_bwd):
        dirs = ((0, c_fwd, yf_s), (1, c_bwd, yb_s))
        group_of = lambda k: k * G_B // n_pair
        cb, y_off, st_in = {}, {}, {}
        for dirn, c, _ in dirs:
            rows = chunk_rows(c)
            bm16 = xc_s[rows, DI_B:DI_B + LANES].astype(BF16)
            cm = xc_s[rows, DI_B + LANES:DI_B + 2 * LANES]
            for g in range(G_B):
                in_g = (lane_q >= g * N_B) & (lane_q < (g + 1) * N_B)
                cmg = jnp.where(in_g, cm, 0.0).astype(BF16)
                cb[dirn, g] = _dot_nt(cmg, bm16)
                for k in range(g * n_pair // G_B, (g + 1) * n_pair // G_B):
                    st_in[dirn, k] = st_s[dirn, k]
                    st2 = jnp.concatenate([st_in[dirn, k]] * 2, axis=0).astype(BF16)
                    y_off[dirn, k] = _dot(cmg, st2)
        for k in range(n_pair):
            psl = slice(k * LANES, (k + 1) * LANES)
            for dirn, c, y_s in dirs:
                rows = chunk_rows(c)
                tri = (ri >= ci) if dirn == 0 else (ci >= ri)
                bmt_g = bmt_s[c, group_of(k) * N_B:(group_of(k) + 1) * N_B, :]
                x16 = xc_s[rows, psl].astype(BF16)
                mats, lhs, ysc, cdec = [], [], [], []
                for s in range(2):
                    col = dirn * H_B + 2 * k + s
                    e_col = jnp.broadcast_to(expo_s[rows, col:col + 1], (q, q))
                    e_row = expot_s[c, col:col + 1, :]
                    dt_row = dtt_s[c, col:col + 1, :]
                    tot = tot_s[c, col:col + 1, :]
                    dec = jnp.exp2(jnp.where(tri, e_col - e_row, NEG_INF))
                    mats.append((cb[dirn, group_of(k)] * dec * dt_row).astype(BF16))
                    if dirn == 0:
                        ysc.append(jnp.exp2(e_col))
                        w_row = jnp.exp2(tot - e_row)
                    else:
                        ysc.append(jnp.exp2(e_col + tot))
                        w_row = jnp.exp2(-e_row)
                    lhs.append((bmt_g * (w_row * dt_row)).astype(BF16))
                    cdec.append(jnp.exp2(tot[:, 0:LANES]))
                yd = _dot(jnp.concatenate(mats, axis=0), x16)
                ds = _dot(jnp.concatenate(lhs, axis=0), x16)
                yo, st = y_off[dirn, k], st_in[dirn, k]
                y_s[rows, psl] = jnp.where(lo, yd[:q] + ysc[0] * yo, yd[q:] + ysc[1] * yo)
                st_s[dirn, k] = jnp.where(lo_st, cdec[0] * st + ds[:N_B], cdec[1] * st + ds[N_B:])
                side()

    def body(c, carry):
        chunk_pair(c, nc - 1 - c)
        return carry

    loop(body, unroll=2)

    dsum = dsk_ref[0:1, :] + dsk_ref[1:2, :]

    def out_body(c, carry):
        rows = chunk_rows(c)
        y = yf_s[rows, :] + yb_s[rows, :] + xc_s[rows, 0:DI_B] * dsum
        y = y * _silu(z_ref[rows, :])
        y = y * lax.rsqrt(jnp.mean(y * y, axis=-1, keepdims=True) + EPS) * g_ref[...]
        y_ref[rows, :] = y.astype(BF16)
        side()
        return carry

    loop(out_body)
    if want_state:
        for dirn in range(2):
            for k in range(n_pair):
                st = st_s[dirn, k]
                st_t = jnp.concatenate([st, st], axis=0).T
                for s in range(2):
                    hs_ref[dirn, 2 * k + s] = st_t[s * P_B:(s + 1) * P_B, 0:N_B]


def _ssd_scratch(seq):
    nc = seq // SSD_CHUNK
    per_chunk_rows = pltpu.VMEM((nc, 2 * H_B, SSD_CHUNK), F32)
    return [pltpu.VMEM((seq + 16, CONV_DIM), F32), pltpu.VMEM((seq, CONV_DIM), F32),
            pltpu.VMEM((seq, LANES), F32), per_chunk_rows, per_chunk_rows, per_chunk_rows,
            pltpu.VMEM((nc, LANES, SSD_CHUNK), F32),
            pltpu.VMEM((seq, DI_B), F32), pltpu.VMEM((seq, DI_B), F32),
            pltpu.VMEM((2, H_B // 2, N_B, LANES), F32)]


def _ssd_param_specs(const):
    return [const((CONV_K, CONV_DIM)), const((1, CONV_DIM)), const((1, LANES)), const((1, LANES)),
            const((2, DI_B)), const((1, DI_B))]


def _proj_ssd_kernel(*refs, nb, seq, ctx, n_carry):
    x_ref, ada_ref, g_ref, w_ref, wdt_ref = refs[:5]
    pos = 5
    h0_ref = None
    if not ctx:
        h0_ref = refs[pos]
        pos += 1
    params = refs[pos:pos + 6]
    outs = refs[pos + 6 + n_carry:]
    if ctx:
        proj_ref, ka_ref, va_ref, kc_ref, vc_ref, y_ref, hs_ref = outs[:7]
        outs = outs[7:]
    else:
        proj_ref, kv_ref, y_ref = outs[:3]
        hs_ref = None
        outs = outs[3:]
    h_s, p_s = outs[:2]
    scratch = outs[2:]

    row = 0 if ctx else 1 + pl.program_id(0)
    for r0 in range(0, nb * seq, SEQ):
        h_s[r0:r0 + SEQ, :] = _modulated_norm(x_ref[r0:r0 + SEQ, :], g_ref, ada_ref, row)
    for c0 in range(P_Z, P_DT, SIDE_TN):
        p_s[:, c0:c0 + SIDE_TN] = _dot_nt(h_s[...], w_ref[_SRC["z"] + c0:_SRC["z"] + c0 + SIDE_TN, :])
    p_s[:, P_DT:P_COLS] = _dot_nt(h_s[...], wdt_ref[...])

    work = []
    for col, src in PROJ_SEGMENTS:
        for off in range(0, W_A, SIDE_TN):
            def tile(d=col + off, s=src + off):
                proj_ref[:, d:d + SIDE_TN] = _dot_nt(h_s[...], w_ref[s:s + SIDE_TN, :])
            work.append(tile)
            if ctx and col in (COL_KA, COL_VA):
                def store(dst=ka_ref if col == COL_KA else va_ref, col=col, off=off):
                    for b in range(nb):
                        for h in range(off // LANES, (off + SIDE_TN) // LANES):
                            dst[b, :, h, :] = proj_ref[b * seq:(b + 1) * seq, col + h * LANES:col + (h + 1) * LANES]
                work.append(store)
    for r0 in range(0, KVC_COLS, SIDE_TN):
        if ctx:
            for b in range(nb):
                def tile_t(b=b, r0=r0):
                    w_rows = w_ref[_SRC["kc"] + r0:_SRC["kc"] + r0 + SIDE_TN, :]
                    kv_t = _dot_nt(w_rows, h_s[b * seq:(b + 1) * seq, :])
                    dst, d0 = (kc_ref, r0) if r0 < W_C else (vc_ref, r0 - W_C)
                    dst[b, d0 // DH_C:(d0 + SIDE_TN) // DH_C] = kv_t.reshape(SIDE_TN // DH_C, DH_C, seq)
                work.append(tile_t)
        else:
            def tile_kv(r0=r0):
                kv_ref[:, r0:r0 + SIDE_TN] = _dot_nt(h_s[...], w_ref[_SRC["kc"] + r0:_SRC["kc"] + r0 + SIDE_TN, :])
            work.append(tile_kv)

    n_slots = nb * (seq // SSD_CHUNK) * (CONV_DIM // LANES + 2 + 2 * (H_B // 2))
    state = dict(slot=0, done=0)

    def side():
        state["slot"] += 1
        target = min(len(work), -(-state["slot"] * len(work) // n_slots))
        while state["done"] < target:
            work[state["done"]]()
            state["done"] += 1

    for b in range(nb):
        rows = pl.ds(b * seq, seq)
        _ssd_body(p_s.at[rows, pl.ds(P_DT, LANES)], p_s.at[rows, pl.ds(P_XS, DI_B)],
                  p_s.at[rows, pl.ds(P_BC, BC_DIM)], p_s.at[rows, pl.ds(P_Z, DI_B)], h0_ref, params,
                  y_ref.at[rows, :], None if hs_ref is None else hs_ref.at[b], scratch,
                  seq=seq, static_loops=True, side=side)
    while state["done"] < len(work):
        work[state["done"]]()
        state["done"] += 1


def _proj_ssd_call(x, ada, norm_g, w16, w_dt, ssd_w, li, *, ctx, carry=None, h0t=None):
    nb, seq = (CTX_BATCHES_PER_STEP, SEQ) if ctx else (1, DEC_SEQ)
    tm = nb * seq
    t = x.shape[0]
    n_carry = 0 if carry is None else len(carry)
    const = lambda shape: pl.BlockSpec(shape, lambda i: (0,) * len(shape))
    once = pl.Buffered(1)
    big = {} if ctx else dict(pipeline_mode=pl.Buffered(1))
    in_specs = [pl.BlockSpec((tm, D_MODEL), lambda i: (i, 0), **big),
                pl.BlockSpec((None, 8, 3 * D_MODEL), lambda i: (li, 0, 0)),
                pl.BlockSpec((None, 1, D_MODEL), lambda i: (li, 0, 0)),
                pl.BlockSpec((None, _SRC["merge"], D_MODEL), lambda i: (li, 0, 0), pipeline_mode=once),
                pl.BlockSpec((None, LANES, D_MODEL), lambda i: (li, 0, 0))]
    args = [x, ada, norm_g, w16, w_dt]
    if not ctx:
        in_specs.append(pl.BlockSpec((None, None, 2, (H_B // 2) * N_B, LANES), lambda i: (i, li, 0, 0, 0)))
        args.append(h0t)
    in_specs += _ssd_param_specs(const)
    args += list(ssd_w)
    out_specs = [pl.BlockSpec((tm, PROJ_COLS), lambda i: (i, 0), **big)]
    out_shape = [jax.ShapeDtypeStruct((t, PROJ_COLS), F32)]
    aliases = {}
    if ctx:
        out_specs += [pl.BlockSpec((nb, None, SEQ, H_A, 2 * DH_A), lambda i: (i, li, 0, 0, 0))] * 2
        out_specs += [pl.BlockSpec((nb, None, H_C, DH_C, SEQ), lambda i: (i, li, 0, 0, 0))] * 2
        out_shape += [jax.ShapeDtypeStruct((BATCH, DEPTH, SEQ, H_A, 2 * DH_A), F32)] * 2
        out_shape += [jax.ShapeDtypeStruct((BATCH, DEPTH, H_C, DH_C, SEQ), F32)] * 2
    else:
        out_specs.append(pl.BlockSpec((tm, KVC_COLS), lambda i: (i, 0), **big))
        out_shape.append(jax.ShapeDtypeStruct((t, KVC_COLS), F32))
    out_specs.append(pl.BlockSpec((tm, DI_B), lambda i: (i, 0)))
    out_shape.append(jax.ShapeDtypeStruct((t, DI_B), BF16))
    if ctx:
        out_specs.append(pl.BlockSpec((nb, None, 2, H_B, P_B, N_B), lambda i: (i, li, 0, 0, 0, 0)))
        out_shape.append(jax.ShapeDtypeStruct((BATCH, DEPTH, 2, H_B, P_B, N_B), F32))
        if carry is not None:
            in_specs += [pl.BlockSpec(memory_space=pl.ANY)] * n_carry
            aliases = {len(args) + k: (1, 2, 3, 4, 6)[k] for k in range(n_carry)}
            args += list(carry)
    scratch = [pltpu.VMEM((tm, D_MODEL), BF16), pltpu.VMEM((tm, P_COLS), F32)] + _ssd_scratch(seq)
    return pl.pallas_call(
        functools.partial(_proj_ssd_kernel, nb=nb, seq=seq, ctx=ctx, n_carry=n_carry),
        grid=(t // tm,),
        in_specs=in_specs,
        out_specs=out_specs,
        out_shape=out_shape,
        input_output_aliases=aliases,
        scratch_shapes=scratch,
        compiler_params=_params("arbitrary"),
        name="ctx_proj_ssd" if ctx else "lat_proj_ssd",
    )(*args)


def _post_kernel(x_ref, ya_ref, yb_ref, yc_ref, ada_ref, g_ref, wm_ref, wa_ref, wb_ref, wc_ref, wo_ref, fg_ref,
                 o_ref, *, tm, row_base, tokens_per_row, final):
    row = row_base + (pl.program_id(0) * tm) // tokens_per_row
    gate = ada_ref[pl.ds(row, 1), 2 * D_MODEL:3 * D_MODEL]
    d = D_MODEL
    x = x_ref[...]
    h = _modulated_norm(x, g_ref, ada_ref, row)
    merged = None
    for n, (y_ref, w_ref) in enumerate(((ya_ref, wa_ref), (yb_ref, wb_ref), (yc_ref, wc_ref))):
        r0 = _SRC["merge"] + n * d
        logits = _dot_nt(h, wm_ref[r0:r0 + d, :])
        term = _sigmoid(logits) * _dot(y_ref[...], w_ref[...])
        merged = term if merged is None else merged + term
    x = x + gate * _dot(merged.astype(BF16), wo_ref[...])
    if final:
        x = x * lax.rsqrt(jnp.mean(x * x, axis=-1, keepdims=True) + EPS) * fg_ref[...]
    o_ref[...] = x


def _post_call(x, ya, yb, yc, ada, norm_g, w_merge_t, li, wa, wb, wc, wo, final_g, *, tm, row_base,
               tokens_per_row, final):
    t = x.shape[0]
    tok = lambda w: pl.BlockSpec((tm, w), lambda i: (i, 0))
    layer = lambda *shape, **kw: pl.BlockSpec((None,) + shape, lambda i: (li,) + (0,) * len(shape), **kw)
    once = dict(pipeline_mode=pl.Buffered(1))
    kern = functools.partial(_post_kernel, tm=tm, row_base=row_base, tokens_per_row=tokens_per_row, final=final)
    return pl.pallas_call(
        kern,
        grid=(t // tm,),
        in_specs=[tok(D_MODEL), tok(W_A), tok(DI_B), tok(W_C),
                  layer(8, 3 * D_MODEL), layer(1, D_MODEL), layer(w_merge_t.shape[1], D_MODEL, **once),
                  layer(W_A, D_MODEL, **once), layer(DI_B, D_MODEL, **once), layer(W_C, D_MODEL, **once),
                  layer(D_MODEL, D_MODEL, **once), pl.BlockSpec((1, D_MODEL), lambda i: (0, 0))],
        out_specs=tok(D_MODEL),
        out_shape=jax.ShapeDtypeStruct((t, D_MODEL), F32),
        compiler_params=_params("arbitrary"),
        name="post_final" if final else "post",
    )(x, ya, yb, yc, ada, norm_g, w_merge_t, wa, wb, wc, wo, final_g)


def _rope_tables():
    pos = np.arange(DEC_SEQ)
    lane = np.arange(LANES)
    l64 = lane % (2 * (DH_A // 2))
    quarter = DH_A // 4
    p = np.where((l64 < DH_A // 2)[None, :], (pos // GRID_W)[:, None], (pos % GRID_W)[:, None])
    inv = ROPE_BASE ** (-np.arange(quarter, dtype=np.float64) / quarter)
    ang = p.astype(np.float64) * inv[l64 % quarter][None, :]
    sign = np.where((lane % (2 * quarter)) < quarter, -1.0, 1.0)
    return jnp.asarray(np.cos(ang), F32), jnp.asarray(np.sin(ang) * sign[None, :], F32)


def _pad_lanes(v, width=LANES):
    v = v.reshape(1, -1).astype(F32)
    return jnp.pad(v, ((0, 0), (0, width - v.shape[1])))


def kernel(x_prompt, x_sample, cache_diff_k, cache_diff_v, cache_na_k, cache_na_v, state_ssd, c, c_ctx,
           norm_g, w_ada, b_ada, w_in, lam_q1, lam_k1, lam_q2, lam_k2, diff_subln_g, conv_w, conv_b,
           dt_bias, a_log, d_skip, ssd_norm_g, na_rpb, w_br_a, w_br_b, w_br_c, w_out, final_g):
    assert x_prompt.shape == (BATCH, SEQ, D_MODEL) and x_sample.shape == (DEC_BATCH, DEC_SEQ, D_MODEL)
    assert w_in.shape == (DEPTH, D_MODEL, _SRC["merge"] + MERGE_COLS)
    w16 = jnp.swapaxes(w_in, 1, 2).astype(BF16)
    w_dt = jnp.pad(w16[:, _SRC["dt"]:_SRC["qc"], :], ((0, 0), (0, LANES - 2 * H_B), (0, 0)))
    wa16, wb16, wc16, wo16 = (w.astype(BF16) for w in (w_br_a, w_br_b, w_br_c, w_out))

    cvecs = jnp.concatenate([c_ctx[None, :], c, jnp.zeros((8 - 1 - DEC_BATCH, D_MODEL), F32)], axis=0)
    ada = _ada_call(cvecs.T, w_ada, b_ada)
    cos_t, sin_t = _rope_tables()

    ck_a = cache_diff_k.reshape(DEC_BATCH, DEPTH, PAST_LEN, W_A)
    cv_a = cache_diff_v.reshape(DEC_BATCH, DEPTH, PAST_LEN, W_A)
    ck_c = cache_na_k.transpose(0, 1, 3, 4, 2)
    cv_c = cache_na_v.transpose(0, 1, 3, 4, 2)
    na_tiles = _rpb_call(na_rpb)
    h0t = state_ssd.transpose(0, 1, 2, 5, 3, 4).reshape(DEC_BATCH, DEPTH, 2, N_B, DI_B)
    h0t = h0t.reshape(DEC_BATCH, DEPTH, 2, N_B, H_B // 2, LANES).transpose(0, 1, 2, 4, 3, 5)
    h0t = h0t.reshape(DEC_BATCH, DEPTH, 2, (H_B // 2) * N_B, LANES)

    xp = x_prompt.reshape(BATCH * SEQ, D_MODEL)
    xs = x_sample.reshape(DEC_BATCH * DEC_SEQ, D_MODEL)
    fg = final_g.reshape(1, D_MODEL)
    norm_g3 = norm_g.reshape(DEPTH, 1, D_MODEL)
    carry = None
    for li in range(DEPTH):
        lam_init = 0.8 - 0.6 * math.exp(-0.3 * li)
        final = li == DEPTH - 1
        lamvec = jnp.concatenate([_pad_lanes(v[li]) for v in (lam_q1, lam_k1, lam_q2, lam_k2)], axis=0)
        subln = diff_subln_g[li].reshape(1, LANES)
        dtb = _pad_lanes(dt_bias[li])
        alog = _pad_lanes(a_log[li])
        dskx = jnp.repeat(d_skip[li], P_B, axis=-1)
        ssd_w = (conv_w[li], conv_b[li].reshape(1, CONV_DIM), dtb, alog, dskx, ssd_norm_g[li].reshape(1, DI_B))
        post_w = (wa16, wb16, wc16, wo16, fg)

        proj, ka, va, kc_t, vc_t, yb, ssd_state = _proj_ssd_call(xp, ada, norm_g3, w16, w_dt, ssd_w, li,
                                                                 ctx=True, carry=carry)
        carry = (ka, va, kc_t, vc_t, ssd_state)
        ya = _attn_a_ctx_call(proj, lamvec, subln, lam_init)
        yc = _attn_c_ctx_call(proj, kc_t, vc_t, li)
        xp = _post_call(xp, ya, yb, yc, ada, norm_g3, w16, li, *post_w, tm=512, row_base=0,
                        tokens_per_row=BATCH * SEQ, final=final)

        proj, kv, yb = _proj_ssd_call(xs, ada, norm_g3, w16, w_dt, ssd_w, li, ctx=False, h0t=h0t)
        ya = _attn_a_lat_call(proj, ck_a, cv_a, li, cos_t, sin_t, lamvec, subln, lam_init)
        yc = _attn_c_lat_call(proj, kv, ck_c, cv_c, li, na_tiles)
        xs = _post_call(xs, ya, yb, yc, ada, norm_g3, w16, li, *post_w, tm=512, row_base=1,
                        tokens_per_row=DEC_SEQ, final=final)

    new_k_a, new_v_a, new_k_c_t, new_v_c_t, new_ssd = carry
    to_token_major = lambda a: a.transpose(0, 1, 4, 2, 3)
    return (xp.reshape(BATCH, SEQ, D_MODEL), xs.reshape(DEC_BATCH, DEC_SEQ, D_MODEL),
            new_k_a, new_v_a, to_token_major(new_k_c_t), to_token_major(new_v_c_t), new_ssd)
```

```python
import functools
import math

import jax
import jax.numpy as jnp
import numpy as np
from jax import lax
from jax.experimental import pallas as pl
from jax.experimental.pallas import tpu as pltpu

D_MODEL = 1024
BATCH = 32
SEQ = 256
DEPTH = 2
DEC_BATCH = 2
DEC_SEQ = 1024
PAST_LEN = 512
GRID_W = 64
GRID_ROWS = DEC_SEQ // GRID_W
H_A = 4
DH_A = 64
W_A = H_A * 2 * DH_A
H_B = 8
P_B = 64
G_B = 2
N_B = 64
DI_B = H_B * P_B
CONV_K = 5
CONV_DIM = DI_B + 2 * G_B * N_B
SSD_CHUNK = 128
H_C = 8
DH_C = 64
W_C = H_C * DH_C
NA_KH = 8
NA_KW = 16
N_BRANCH = 3
ROPE_BASE = 10000.0
EPS = 1e-6

LANES = 128
HALF = LANES // 2
VMEM_LIMIT = 56 * 1024 * 1024

BC_DIM = CONV_DIM - DI_B
KVC_COLS = 2 * W_C
MERGE_COLS = N_BRANCH * D_MODEL
_SRC = dict(qa=0, ka=512, va=1024, ga=1536, z=2048, xs=2560, bc=3072, dt=3328, qc=3344, kc=3856, vc=4368,
            gc=4880, merge=5392)
COL_QA = 0
COL_GA = 512
COL_QC = 1024
COL_GC = 1536
COL_KA = 2048
COL_VA = 2560
PROJ_COLS = 3072
PROJ_SEGMENTS = ((COL_QA, _SRC["qa"]), (COL_GA, _SRC["ga"]), (COL_QC, _SRC["qc"]), (COL_GC, _SRC["gc"]),
                 (COL_KA, _SRC["ka"]), (COL_VA, _SRC["va"]))
P_Z = 0
P_XS = 512
P_BC = 1024
P_DT = 1280
P_COLS = P_DT + LANES
SIDE_TN = 256

NA_QROWS = 4
NA_WROWS = 12
NA_TILES = 2 * NA_KH
NEG_INF = float("-inf")
LOG2E = math.log2(math.e)
ATTN_AHEAD = 2
SSD_AHEAD = 2
CTX_BATCHES_PER_STEP = 2
ATTN_BATCHES_PER_STEP = 4
HI = lax.Precision.HIGHEST
F32 = jnp.float32
BF16 = jnp.bfloat16


def _dot(a, b, precision=None):
    return jnp.dot(a, b, preferred_element_type=F32, precision=precision)


def _dot_nt(a, b):
    return lax.dot_general(a, b, (((1,), (1,)), ((), ())), preferred_element_type=F32)


def _sigmoid(x):
    return 1.0 / (1.0 + jnp.exp(-x))


def _silu(x):
    return x * _sigmoid(x)


def _lane(shape):
    return lax.broadcasted_iota(jnp.int32, shape, len(shape) - 1)


def _params(*sem):
    return pltpu.CompilerParams(dimension_semantics=sem, vmem_limit_bytes=VMEM_LIMIT)


def _ada_kernel(cvt_ref, w_ref, b_ref, o_ref):
    n_rows = 1 + DEC_BATCH
    s = _silu(cvt_ref[...])
    accs = [jnp.zeros((8, w_ref.shape[1]), F32)] * n_rows
    for k0 in range(0, D_MODEL, 8):
        w = w_ref[k0:k0 + 8, :]
        accs = [acc + w * s[k0:k0 + 8, r:r + 1] for r, acc in enumerate(accs)]
    rows = [jnp.sum(acc, axis=0, keepdims=True) for acc in accs]
    rows.append(jnp.zeros((8 - n_rows, w_ref.shape[1]), F32))
    o_ref[...] = jnp.concatenate(rows, axis=0) + b_ref[...]


def _ada_call(cvecs, w_ada, b_ada):
    tn = 1536
    return pl.pallas_call(
        _ada_kernel,
        grid=(DEPTH, 3 * D_MODEL // tn),
        in_specs=[
            pl.BlockSpec((D_MODEL, 8), lambda l, j: (0, 0)),
            pl.BlockSpec((None, D_MODEL, tn), lambda l, j: (l, 0, j)),
            pl.BlockSpec((None, 1, tn), lambda l, j: (l, 0, j)),
        ],
        out_specs=pl.BlockSpec((None, 8, tn), lambda l, j: (l, 0, j)),
        out_shape=jax.ShapeDtypeStruct((DEPTH, 8, 3 * D_MODEL), F32),
        compiler_params=_params("arbitrary", "arbitrary"),
        name="ada",
    )(cvecs, w_ada, b_ada.reshape(DEPTH, 1, 3 * D_MODEL))


def _modulated_norm(x, g_ref, ada_ref, row):
    y = x * lax.rsqrt(jnp.mean(x * x, axis=-1, keepdims=True) + EPS) * g_ref[...]
    shift = ada_ref[pl.ds(row, 1), 0:D_MODEL]
    scale = ada_ref[pl.ds(row, 1), D_MODEL:2 * D_MODEL]
    return (y * (1.0 + scale) + shift).astype(BF16)


def _diff_lambda_in_kernel(lam_ref, lam_init):
    v = lam_ref[...]
    l1 = jnp.sum(v[0:1] * v[1:2], axis=-1, keepdims=True)
    l2 = jnp.sum(v[2:3] * v[3:4], axis=-1, keepdims=True)
    return jnp.exp(l1) - jnp.exp(l2) + lam_init


def _split_halves(x, scale):
    lo = _lane(x.shape) < HALF
    xs = x * (scale * LOG2E)
    return jnp.concatenate([jnp.where(lo, xs, 0.0), jnp.where(lo, 0.0, xs)], axis=0).astype(BF16)


def _diff_combine(o2, rsum, lam, t):
    return o2[:t] * rsum[:t] - (lam * rsum[t:]) * o2[t:]


def _diff_head_post(o, subln_g, lam_init, gate):
    o = o * lax.rsqrt(jnp.mean(o * o, axis=-1, keepdims=True) + EPS) * (subln_g * (1.0 - lam_init))
    return (o * _silu(gate)).astype(BF16)


def _attn_a_ctx_kernel(q_ref, k_ref, v_ref, g_ref, lam_ref, sg_ref, o_ref, *, lam_init):
    t = SEQ
    lam = _diff_lambda_in_kernel(lam_ref, lam_init)
    ones = jnp.ones((t, LANES), BF16)
    blocks = [(b, h) for b in range(q_ref.shape[0] // t) for h in range(H_A)]
    where = lambda b, h: (slice(b * t, (b + 1) * t), slice(h * LANES, (h + 1) * LANES))

    def scores(b, h):
        qq = _split_halves(q_ref[where(b, h)], DH_A ** -0.5)
        return _dot_nt(qq, k_ref[where(b, h)].astype(BF16))

    pending = [scores(*blk) for blk in blocks[:ATTN_AHEAD]]
    for n, blk in enumerate(blocks):
        s = pending.pop(0)
        if n + ATTN_AHEAD < len(blocks):
            pending.append(scores(*blocks[n + ATTN_AHEAD]))
        e = jnp.exp2(s - jnp.max(s, axis=-1, keepdims=True)).astype(BF16)
        rsum = 1.0 / _dot(e, ones)
        o = _diff_combine(_dot(e, v_ref[where(*blk)].astype(BF16)), rsum, lam, t)
        o_ref[where(*blk)] = _diff_head_post(o, sg_ref[...], lam_init, g_ref[where(*blk)])


def _attn_a_ctx_call(proj, lamvec, subln_g, lam_init):
    rows = ATTN_BATCHES_PER_STEP * SEQ
    blk = lambda c: pl.BlockSpec((rows, W_A), lambda b: (b, c // W_A))
    return pl.pallas_call(
        functools.partial(_attn_a_ctx_kernel, lam_init=lam_init),
        grid=(BATCH // ATTN_BATCHES_PER_STEP,),
        in_specs=[blk(COL_QA), blk(COL_KA), blk(COL_VA), blk(COL_GA),
                  pl.BlockSpec((4, LANES), lambda b: (0, 0)),
                  pl.BlockSpec((1, LANES), lambda b: (0, 0))],
        out_specs=pl.BlockSpec((rows, W_A), lambda b: (b, 0)),
        out_shape=jax.ShapeDtypeStruct((BATCH * SEQ, W_A), BF16),
        compiler_params=_params("arbitrary"),
        name="attn_a_ctx",
    )(proj, proj, proj, proj, lamvec, subln_g)


def _rope(x, cos, sin_signed):
    first = (_lane(x.shape) % 32) < 16
    swapped = jnp.where(first, pltpu.roll(x, LANES - 16, 1), pltpu.roll(x, 16, 1))
    return x * cos + swapped * sin_signed


def _attn_a_lat_kernel(q_ref, k_ref, v_ref, g_ref, ck_ref, cv_ref, cosq_ref, sinq_ref, cosk_ref, sink_ref,
                       lam_ref, sg_ref, o_ref, kr_s, *, lam_init):
    tq = q_ref.shape[0]

    @pl.when(pl.program_id(1) == 0)
    def _():
        for h in range(H_A):
            sl = slice(h * LANES, (h + 1) * LANES)
            kr_s[:, sl] = _rope(k_ref[:, sl], cosk_ref[...], sink_ref[...]).astype(BF16)

    lam = _diff_lambda_in_kernel(lam_ref, lam_init)

    def scores(h):
        sl = slice(h * LANES, (h + 1) * LANES)
        qq = _split_halves(_rope(q_ref[:, sl], cosq_ref[...], sinq_ref[...]), DH_A ** -0.5)
        return _dot_nt(qq, kr_s[:, sl]), _dot_nt(qq, ck_ref[:, sl].astype(BF16))

    pending = [scores(h) for h in range(ATTN_AHEAD)]
    for h in range(H_A):
        sl = slice(h * LANES, (h + 1) * LANES)
        s_lat, s_ctx = pending.pop(0)
        if h + ATTN_AHEAD < H_A:
            pending.append(scores(h + ATTN_AHEAD))
        m = jnp.maximum(jnp.max(s_lat, axis=-1, keepdims=True), jnp.max(s_ctx, axis=-1, keepdims=True))
        e_lat = jnp.exp2(s_lat - m)
        e_ctx = jnp.exp2(s_ctx - m)
        rsum = 1.0 / (jnp.sum(e_lat, axis=-1, keepdims=True) + jnp.sum(e_ctx, axis=-1, keepdims=True))
        o2 = _dot(e_lat.astype(BF16), v_ref[:, sl].astype(BF16)) + _dot(e_ctx.astype(BF16),
                                                                         cv_ref[:, sl].astype(BF16))
        o = _diff_combine(o2, rsum, lam, tq)
        o_ref[:, sl] = _diff_head_post(o, sg_ref[...], lam_init, g_ref[:, sl])


def _attn_a_lat_call(proj, cache_k, cache_v, li, cos_t, sin_t, lamvec, subln_g, lam_init):
    tq = 256
    nq = DEC_SEQ // tq
    qblk = lambda c: pl.BlockSpec((tq, W_A), lambda b, i: (b * nq + i, c // W_A))
    full = lambda c: pl.BlockSpec((DEC_SEQ, W_A), lambda b, i: (b, c // W_A))
    cache = pl.BlockSpec((None, None, PAST_LEN, W_A), lambda b, i: (b, li, 0, 0))
    return pl.pallas_call(
        functools.partial(_attn_a_lat_kernel, lam_init=lam_init),
        grid=(DEC_BATCH, nq),
        in_specs=[qblk(COL_QA), full(COL_KA), full(COL_VA), qblk(COL_GA), cache, cache,
                  pl.BlockSpec((tq, LANES), lambda b, i: (i, 0)),
                  pl.BlockSpec((tq, LANES), lambda b, i: (i, 0)),
                  pl.BlockSpec((DEC_SEQ, LANES), lambda b, i: (0, 0)),
                  pl.BlockSpec((DEC_SEQ, LANES), lambda b, i: (0, 0)),
                  pl.BlockSpec((4, LANES), lambda b, i: (0, 0)),
                  pl.BlockSpec((1, LANES), lambda b, i: (0, 0))],
        out_specs=pl.BlockSpec((tq, W_A), lambda b, i: (b * nq + i, 0)),
        out_shape=jax.ShapeDtypeStruct((DEC_BATCH * DEC_SEQ, W_A), BF16),
        scratch_shapes=[pltpu.VMEM((DEC_SEQ, W_A), BF16)],
        compiler_params=_params("arbitrary", "arbitrary"),
        name="attn_a_lat",
    )(proj, proj, proj, proj, cache_k, cache_v, cos_t, sin_t, cos_t, sin_t, lamvec, subln_g)


def _merge_halves(o, t):
    return jnp.where(_lane((t, LANES)) < HALF, o[:t], o[t:])


def _attn_c_ctx_kernel(q_ref, kt_ref, vt_ref, g_ref, o_ref):
    t = SEQ
    blocks = [(b, j) for b in range(q_ref.shape[0] // t) for j in range(H_C // 2)]
    where = lambda b, j: (slice(b * t, (b + 1) * t), slice(j * LANES, (j + 1) * LANES))
    pair_t = lambda ref, b, j: ref[b, 2 * j:2 * j + 2].reshape(LANES, t).astype(BF16)

    def scores(b, j):
        return _dot(_split_halves(q_ref[where(b, j)], DH_C ** -0.5), pair_t(kt_ref, b, j))

    pending = [scores(*blk) for blk in blocks[:ATTN_AHEAD]]
    for n, blk in enumerate(blocks):
        s = pending.pop(0)
        if n + ATTN_AHEAD < len(blocks):
            pending.append(scores(*blocks[n + ATTN_AHEAD]))
        e = jnp.exp2(s - jnp.max(s, axis=-1, keepdims=True))
        rsum = 1.0 / jnp.sum(e, axis=-1, keepdims=True)
        o = _merge_halves(_dot_nt(e.astype(BF16), pair_t(vt_ref, *blk)) * rsum, t)
        o_ref[where(*blk)] = (o * _silu(g_ref[where(*blk)])).astype(BF16)


def _attn_c_ctx_call(proj, kc_t, vc_t, li):
    nb = ATTN_BATCHES_PER_STEP
    blk = lambda c: pl.BlockSpec((nb * SEQ, W_C), lambda b: (b, c // W_C))
    cache = pl.BlockSpec((nb, None, H_C, DH_C, SEQ), lambda b: (b, li, 0, 0, 0))
    return pl.pallas_call(
        _attn_c_ctx_kernel,
        grid=(BATCH // nb,),
        in_specs=[blk(COL_QC), cache, cache, blk(COL_GC)],
        out_specs=pl.BlockSpec((nb * SEQ, W_C), lambda b: (b, 0)),
        out_shape=jax.ShapeDtypeStruct((BATCH * SEQ, W_C), BF16),
        compiler_params=_params("arbitrary"),
        name="attn_c_ctx",
    )(proj, kc_t, vc_t, proj)


def _rpb_kernel(rpb_ref, o_ref):
    shape = (GRID_W, LANES)
    c = lax.broadcasted_iota(jnp.int32, shape, 0)
    cp = _lane(shape) % GRID_W
    start = jnp.clip(c - NA_KW // 2, 0, GRID_W - NA_KW)
    in_win = (cp >= start) & (cp < start + NA_KW)
    for h in range(H_C):
        o_ref[h, 0] = jnp.full(shape, NEG_INF, F32)
        for dr in range(2 * NA_KH - 1):
            row = jnp.broadcast_to(rpb_ref[h, dr:dr + 1, :], shape)
            tile = pltpu.roll(row, LANES - (NA_KW - 1), 1, stride=1, stride_axis=0)
            o_ref[h, 1 + dr] = jnp.where(in_win, tile * LOG2E, NEG_INF)


def _rpb_call(rpb):
    n_dc = 2 * NA_KW - 1
    v = jnp.pad(rpb, ((0, 0), (0, 0), (0, NA_TILES - (2 * NA_KH - 1)), (0, GRID_W - n_dc)))
    v = jnp.concatenate([v] * (LANES // GRID_W), axis=-1)
    return pl.pallas_call(
        _rpb_kernel,
        grid=(DEPTH,),
        in_specs=[pl.BlockSpec((None, H_C, NA_TILES, LANES), lambda l: (l, 0, 0, 0))],
        out_specs=pl.BlockSpec((None, H_C, NA_TILES, GRID_W, LANES), lambda l: (l, 0, 0, 0, 0)),
        out_shape=jax.ShapeDtypeStruct((DEPTH, H_C, NA_TILES, GRID_W, LANES), F32),
        compiler_params=_params("arbitrary"),
        name="rpb_tiles",
    )(v)


def _attn_c_lat_kernel(q_ref, k_ref, v_ref, g_ref, ck_ref, cv_ref, tile_ref, o_ref, bias_s):
    tq = q_ref.shape[0]
    nwin = NA_WROWS * GRID_W
    m = pl.program_id(1)
    w0 = jnp.where(m < (GRID_ROWS // NA_QROWS) // 2, 0, GRID_ROWS - NA_WROWS)
    k0 = pl.multiple_of(w0 * GRID_W, GRID_W)
    lo = _lane((GRID_W, LANES)) < HALF
    n_pair = H_C // 2

    def scores(j):
        sl = slice(j * LANES, (j + 1) * LANES)
        for s in range(2):
            for i in range(NA_QROWS):
                r = m * NA_QROWS + i
                start = jnp.clip(r - NA_KH // 2, 0, GRID_ROWS - NA_KH)
                for jp in range(NA_WROWS // 2):
                    idx = []
                    for u in range(2):
                        rk = w0 + 2 * jp + u
                        valid = (rk >= start) & (rk < start + NA_KH)
                        idx.append(jnp.where(valid, rk - r + NA_KH, 0))
                    tile = jnp.where(lo, tile_ref[2 * j + s, idx[0]], tile_ref[2 * j + s, idx[1]])
                    bias_s[(s * NA_QROWS + i) * GRID_W:(s * NA_QROWS + i + 1) * GRID_W,
                           jp * LANES:(jp + 1) * LANES] = tile
        qq = _split_halves(q_ref[:, sl], DH_C ** -0.5)
        kw = k_ref[pl.ds(k0, nwin), sl].astype(BF16)
        s_win = _dot_nt(qq, kw) + bias_s[...]
        ckt = ck_ref[2 * j:2 * j + 2].reshape(LANES, PAST_LEN).astype(BF16)
        return s_win, _dot(qq, ckt)

    pending = [scores(j) for j in range(ATTN_AHEAD)]
    for j in range(n_pair):
        sl = slice(j * LANES, (j + 1) * LANES)
        s_win, s_ctx = pending.pop(0)
        if j + ATTN_AHEAD < n_pair:
            pending.append(scores(j + ATTN_AHEAD))
        vw = v_ref[pl.ds(k0, nwin), sl].astype(BF16)
        mx = jnp.maximum(jnp.max(s_win, axis=-1, keepdims=True), jnp.max(s_ctx, axis=-1, keepdims=True))
        e_win = jnp.exp2(s_win - mx)
        e_ctx = jnp.exp2(s_ctx - mx)
        rs = 1.0 / (jnp.sum(e_win, axis=-1, keepdims=True) + jnp.sum(e_ctx, axis=-1, keepdims=True))
        cvt = cv_ref[2 * j:2 * j + 2].reshape(LANES, PAST_LEN).astype(BF16)
        o = (_dot(e_win.astype(BF16), vw) + _dot_nt(e_ctx.astype(BF16), cvt)) * rs
        o_ref[:, sl] = (_merge_halves(o, tq) * _silu(g_ref[:, sl])).astype(BF16)


def _attn_c_lat_call(proj, kv, cache_k, cache_v, li, tiles):
    tq = NA_QROWS * GRID_W
    nq = DEC_SEQ // tq
    qblk = lambda c: pl.BlockSpec((tq, W_C), lambda b, i: (b * nq + i, c // W_C))
    full = lambda c: pl.BlockSpec((DEC_SEQ, W_C), lambda b, i: (b, c // W_C))
    cache = pl.BlockSpec((None, None, H_C, DH_C, PAST_LEN), lambda b, i: (b, li, 0, 0, 0))
    return pl.pallas_call(
        _attn_c_lat_kernel,
        grid=(DEC_BATCH, nq),
        in_specs=[qblk(COL_QC), full(0), full(W_C), qblk(COL_GC), cache, cache,
                  pl.BlockSpec((None, H_C, NA_TILES, GRID_W, LANES), lambda b, i: (li, 0, 0, 0, 0))],
        out_specs=pl.BlockSpec((tq, W_C), lambda b, i: (b * nq + i, 0)),
        out_shape=jax.ShapeDtypeStruct((DEC_BATCH * DEC_SEQ, W_C), BF16),
        scratch_shapes=[pltpu.VMEM((2 * tq, NA_WROWS * GRID_W), F32)],
        compiler_params=_params("arbitrary", "arbitrary"),
        name="attn_c_lat",
    )(proj, kv, kv, proj, cache_k, cache_v, tiles)


def _ssd_body(dt_ref, xs_ref, bc_ref, z_ref, h0_ref, params, y_ref, hs_ref, scratch, *, seq, static_loops, side):
    cw_ref, cb_ref, dtb_ref, alog_ref, dsk_ref, g_ref = params
    upad_s, xc_s, expo_s, expot_s, dtt_s, tot_s, bmt_s, yf_s, yb_s, st_s = scratch
    use_h0 = h0_ref is not None
    want_state = hs_ref is not None

    q = SSD_CHUNK
    nc = seq // q
    n_pair = H_B // 2
    n_hd = 2 * H_B
    pad = 8

    def loop(body, unroll=1):
        if static_loops:
            for c in range(nc):
                body(c, 0)
        else:
            lax.fori_loop(0, nc, body, 0, unroll=unroll)

    def chunk_rows(c):
        return slice(c * q, (c + 1) * q) if isinstance(c, int) else pl.ds(pl.multiple_of(c * q, q), q)

    upad_s[0:pad, :] = jnp.zeros((pad, CONV_DIM), F32)
    upad_s[pad + seq:2 * pad + seq, :] = jnp.zeros((pad, CONV_DIM), F32)
    upad_s[pad:pad + seq, 0:DI_B] = xs_ref[...]
    upad_s[pad:pad + seq, DI_B:CONV_DIM] = bc_ref[...]

    for c in range(nc):
        for cb_ in range(CONV_DIM // LANES):
            csl = slice(cb_ * LANES, (cb_ + 1) * LANES)
            acc = jnp.zeros((q, LANES), F32) + cb_ref[:, csl]
            for k in range(CONV_K):
                r0 = c * q + pad - CONV_K // 2 + k
                acc = acc + upad_s[r0:r0 + q, csl] * cw_ref[k:k + 1, csl]
            xc_s[c * q:(c + 1) * q, csl] = _silu(acc)
            side()

    a_row = -jnp.exp(alog_ref[...]) * LOG2E
    a_col = jnp.broadcast_to(a_row, (LANES, LANES)).T[0:n_hd, 0:1]
    ri = lax.broadcasted_iota(jnp.int32, (q, q), 0)
    ci = lax.broadcasted_iota(jnp.int32, (q, q), 1)
    ltri = (ri >= ci).astype(F32)
    fwd_lane = _lane((q, LANES)) < H_B
    fwd_row = lax.broadcasted_iota(jnp.int32, (n_hd, q), 0) < H_B

    def prep_body(c, carry):
        rows = chunk_rows(c)
        xdt = dt_ref[rows, 0:LANES] + dtb_ref[...]
        dtv = jnp.maximum(xdt, 0.0) + jnp.log1p(jnp.exp(-jnp.abs(xdt)))
        la = dtv * a_row
        acum = _dot(ltri, la, HI)
        expo_s[rows, :] = jnp.where(fwd_lane, acum, la - acum)
        acum_t = acum.T[0:n_hd, :]
        dt_t = dtv.T[0:n_hd, :]
        expot_s[c] = jnp.where(fwd_row, acum_t, dt_t * a_col - acum_t)
        dtt_s[c] = dt_t
        tot_s[c] = jnp.broadcast_to(acum_t[:, q - 1:q], (n_hd, q))
        bmt_s[c] = xc_s[rows, DI_B:DI_B + LANES].T
        side()
        return carry

    loop(prep_body, unroll=2)

    if use_h0:
        st_s[...] = h0_ref[...].reshape(2, n_pair, N_B, LANES)
    else:
        st_s[...] = jnp.zeros_like(st_s)

    lane_q = _lane((q, LANES))
    lo = lane_q < HALF
    lo_st = _lane((N_B, LANES)) < HALF

    def chunk_pair(c_fwd, c_bwd):
        dirs = ((0, c_fwd, yf_s), (1, c_bwd, yb_s))
        group_of = lambda k: k * G_B // n_pair
        items = [(k, d) for k in range(n_pair) for d in dirs]
        cb, y_off, st_in = {}, {}, {}

        def issue_early(k, d):
            dirn, c, _ = d
            g, rows = group_of(k), chunk_rows(c)
            in_g = (lane_q >= g * N_B) & (lane_q < (g + 1) * N_B)
            cmg = jnp.where(in_g, xc_s[rows, DI_B + LANES:DI_B + 2 * LANES], 0.0).astype(BF16)
            if (dirn, g) not in cb:
                cb[dirn, g] = _dot_nt(cmg, xc_s[rows, DI_B:DI_B + LANES].astype(BF16))
            st_in[dirn, k] = st_s[dirn, k]
            y_off[dirn, k] = _dot(cmg, jnp.concatenate([st_in[dirn, k]] * 2, axis=0).astype(BF16))

        for item in items[:SSD_AHEAD]:
            issue_early(*item)
        for n, (k, (dirn, c, y_s)) in enumerate(items):
            if n + SSD_AHEAD < len(items):
                issue_early(*items[n + SSD_AHEAD])
            psl = slice(k * LANES, (k + 1) * LANES)
            rows = chunk_rows(c)
            tri = (ri >= ci) if dirn == 0 else (ci >= ri)
            bmt_g = bmt_s[c, group_of(k) * N_B:(group_of(k) + 1) * N_B, :]
            x16 = xc_s[rows, psl].astype(BF16)
            mats, lhs, ysc, cdec = [], [], [], []
            for s in range(2):
                col = dirn * H_B + 2 * k + s
                e_col = jnp.broadcast_to(expo_s[rows, col:col + 1], (q, q))
                e_row = expot_s[c, col:col + 1, :]
                dt_row = dtt_s[c, col:col + 1, :]
                tot = tot_s[c, col:col + 1, :]
                dec = jnp.exp2(jnp.where(tri, e_col - e_row, NEG_INF))
                mats.append((cb[dirn, group_of(k)] * dec * dt_row).astype(BF16))
                if dirn == 0:
                    ysc.append(jnp.exp2(e_col))
                    w_row = jnp.exp2(tot - e_row)
                else:
                    ysc.append(jnp.exp2(e_col + tot))
                    w_row = jnp.exp2(-e_row)
                lhs.append((bmt_g * (w_row * dt_row)).astype(BF16))
                cdec.append(jnp.exp2(tot[:, 0:LANES]))
            yd = _dot(jnp.concatenate(mats, axis=0), x16)
            ds = _dot(jnp.concatenate(lhs, axis=0), x16)
            yo, st = y_off.pop((dirn, k)), st_in.pop((dirn, k))
            y_s[rows, psl] = jnp.where(lo, yd[:q] + ysc[0] * yo, yd[q:] + ysc[1] * yo)
            st_s[dirn, k] = jnp.where(lo_st, cdec[0] * st + ds[:N_B], cdec[1] * st + ds[N_B:])
            side()

    def body(c, carry):
        chunk_pair(c, nc - 1 - c)
        return carry

    loop(body, unroll=2)

    dsum = dsk_ref[0:1, :] + dsk_ref[1:2, :]

    def out_body(c, carry):
        rows = chunk_rows(c)
        y = yf_s[rows, :] + yb_s[rows, :] + xc_s[rows, 0:DI_B] * dsum
        y = y * _silu(z_ref[rows, :])
        y = y * lax.rsqrt(jnp.mean(y * y, axis=-1, keepdims=True) + EPS) * g_ref[...]
        y_ref[rows, :] = y.astype(BF16)
        side()
        return carry

    loop(out_body)
    if want_state:
        for dirn in range(2):
            for k in range(n_pair):
                st = st_s[dirn, k]
                st_t = jnp.concatenate([st, st], axis=0).T
                for s in range(2):
                    hs_ref[dirn, 2 * k + s] = st_t[s * P_B:(s + 1) * P_B, 0:N_B]


def _ssd_scratch(seq):
    nc = seq // SSD_CHUNK
    per_chunk_rows = pltpu.VMEM((nc, 2 * H_B, SSD_CHUNK), F32)
    return [pltpu.VMEM((seq + 16, CONV_DIM), F32), pltpu.VMEM((seq, CONV_DIM), F32),
            pltpu.VMEM((seq, LANES), F32), per_chunk_rows, per_chunk_rows, per_chunk_rows,
            pltpu.VMEM((nc, LANES, SSD_CHUNK), F32),
            pltpu.VMEM((seq, DI_B), F32), pltpu.VMEM((seq, DI_B), F32),
            pltpu.VMEM((2, H_B // 2, N_B, LANES), F32)]


def _ssd_param_specs(const):
    return [const((CONV_K, CONV_DIM)), const((1, CONV_DIM)), const((1, LANES)), const((1, LANES)),
            const((2, DI_B)), const((1, DI_B))]


def _proj_ssd_kernel(*refs, nb, seq, ctx, n_carry):
    x_ref, ada_ref, g_ref, w_ref, wdt_ref = refs[:5]
    pos = 5
    h0_ref = None
    if not ctx:
        h0_ref = refs[pos]
        pos += 1
    params = refs[pos:pos + 6]
    outs = refs[pos + 6 + n_carry:]
    if ctx:
        proj_ref, ka_ref, va_ref, kc_ref, vc_ref, y_ref, hs_ref = outs[:7]
        outs = outs[7:]
    else:
        proj_ref, kv_ref, y_ref = outs[:3]
        hs_ref = None
        outs = outs[3:]
    h_s, p_s = outs[:2]
    scratch = outs[2:]

    row = 0 if ctx else 1 + pl.program_id(0)
    for r0 in range(0, nb * seq, SEQ):
        h_s[r0:r0 + SEQ, :] = _modulated_norm(x_ref[r0:r0 + SEQ, :], g_ref, ada_ref, row)
    for c0 in range(P_Z, P_DT, SIDE_TN):
        c1 = min(c0 + SIDE_TN, P_DT)
        p_s[:, c0:c1] = _dot_nt(h_s[...], w_ref[_SRC["z"] + c0:_SRC["z"] + c1, :])
    p_s[:, P_DT:P_COLS] = _dot_nt(h_s[...], wdt_ref[...])

    work = []
    for col, src in PROJ_SEGMENTS:
        for off in range(0, W_A, SIDE_TN):
            def tile(d=col + off, s=src + off):
                proj_ref[:, d:d + SIDE_TN] = _dot_nt(h_s[...], w_ref[s:s + SIDE_TN, :])
            work.append(tile)
            if ctx and col in (COL_KA, COL_VA):
                def store(dst=ka_ref if col == COL_KA else va_ref, col=col, off=off):
                    for b in range(nb):
                        for h in range(off // LANES, (off + SIDE_TN) // LANES):
                            dst[b, :, h, :] = proj_ref[b * seq:(b + 1) * seq, col + h * LANES:col + (h + 1) * LANES]
                work.append(store)
    for r0 in range(0, KVC_COLS, SIDE_TN):
        if ctx:
            for b in range(nb):
                def tile_t(b=b, r0=r0):
                    w_rows = w_ref[_SRC["kc"] + r0:_SRC["kc"] + r0 + SIDE_TN, :]
                    kv_t = _dot_nt(w_rows, h_s[b * seq:(b + 1) * seq, :])
                    dst, d0 = (kc_ref, r0) if r0 < W_C else (vc_ref, r0 - W_C)
                    dst[b, d0 // DH_C:(d0 + SIDE_TN) // DH_C] = kv_t.reshape(SIDE_TN // DH_C, DH_C, seq)
                work.append(tile_t)
        else:
            def tile_kv(r0=r0):
                kv_ref[:, r0:r0 + SIDE_TN] = _dot_nt(h_s[...], w_ref[_SRC["kc"] + r0:_SRC["kc"] + r0 + SIDE_TN, :])
            work.append(tile_kv)

    n_slots = nb * (seq // SSD_CHUNK) * (CONV_DIM // LANES + 2 + 2 * (H_B // 2))
    state = dict(slot=0, done=0)

    def side():
        state["slot"] += 1
        target = min(len(work), -(-state["slot"] * len(work) // n_slots))
        while state["done"] < target:
            work[state["done"]]()
            state["done"] += 1

    for b in range(nb):
        rows = pl.ds(b * seq, seq)
        _ssd_body(p_s.at[rows, pl.ds(P_DT, LANES)], p_s.at[rows, pl.ds(P_XS, DI_B)],
                  p_s.at[rows, pl.ds(P_BC, BC_DIM)], p_s.at[rows, pl.ds(P_Z, DI_B)], h0_ref, params,
                  y_ref.at[rows, :], None if hs_ref is None else hs_ref.at[b], scratch,
                  seq=seq, static_loops=True, side=side)
    while state["done"] < len(work):
        work[state["done"]]()
        state["done"] += 1


def _proj_ssd_call(x, ada, norm_g, w16, w_dt, ssd_w, li, *, ctx, carry=None, h0t=None):
    nb, seq = (CTX_BATCHES_PER_STEP, SEQ) if ctx else (1, DEC_SEQ)
    tm = nb * seq
    t = x.shape[0]
    n_carry = 0 if carry is None else len(carry)
    const = lambda shape: pl.BlockSpec(shape, lambda i: (0,) * len(shape))
    once = pl.Buffered(1)
    big = {} if ctx else dict(pipeline_mode=pl.Buffered(1))
    in_specs = [pl.BlockSpec((tm, D_MODEL), lambda i: (i, 0), **big),
                pl.BlockSpec((None, 8, 3 * D_MODEL), lambda i: (li, 0, 0)),
                pl.BlockSpec((None, 1, D_MODEL), lambda i: (li, 0, 0)),
                pl.BlockSpec((None, _SRC["merge"], D_MODEL), lambda i: (li, 0, 0), pipeline_mode=once),
                pl.BlockSpec((None, LANES, D_MODEL), lambda i: (li, 0, 0))]
    args = [x, ada, norm_g, w16, w_dt]
    if not ctx:
        in_specs.append(pl.BlockSpec((None, None, 2, (H_B // 2) * N_B, LANES), lambda i: (i, li, 0, 0, 0)))
        args.append(h0t)
    in_specs += _ssd_param_specs(const)
    args += list(ssd_w)
    out_specs = [pl.BlockSpec((tm, PROJ_COLS), lambda i: (i, 0), **big)]
    out_shape = [jax.ShapeDtypeStruct((t, PROJ_COLS), F32)]
    aliases = {}
    if ctx:
        out_specs += [pl.BlockSpec((nb, None, SEQ, H_A, 2 * DH_A), lambda i: (i, li, 0, 0, 0))] * 2
        out_specs += [pl.BlockSpec((nb, None, H_C, DH_C, SEQ), lambda i: (i, li, 0, 0, 0))] * 2
        out_shape += [jax.ShapeDtypeStruct((BATCH, DEPTH, SEQ, H_A, 2 * DH_A), F32)] * 2
        out_shape += [jax.ShapeDtypeStruct((BATCH, DEPTH, H_C, DH_C, SEQ), F32)] * 2
    else:
        out_specs.append(pl.BlockSpec((tm, KVC_COLS), lambda i: (i, 0), **big))
        out_shape.append(jax.ShapeDtypeStruct((t, KVC_COLS), F32))
    out_specs.append(pl.BlockSpec((tm, DI_B), lambda i: (i, 0)))
    out_shape.append(jax.ShapeDtypeStruct((t, DI_B), BF16))
    if ctx:
        out_specs.append(pl.BlockSpec((nb, None, 2, H_B, P_B, N_B), lambda i: (i, li, 0, 0, 0, 0)))
        out_shape.append(jax.ShapeDtypeStruct((BATCH, DEPTH, 2, H_B, P_B, N_B), F32))
        if carry is not None:
            in_specs += [pl.BlockSpec(memory_space=pl.ANY)] * n_carry
            aliases = {len(args) + k: (1, 2, 3, 4, 6)[k] for k in range(n_carry)}
            args += list(carry)
    scratch = [pltpu.VMEM((tm, D_MODEL), BF16), pltpu.VMEM((tm, P_COLS), F32)] + _ssd_scratch(seq)
    return pl.pallas_call(
        functools.partial(_proj_ssd_kernel, nb=nb, seq=seq, ctx=ctx, n_carry=n_carry),
        grid=(t // tm,),
        in_specs=in_specs,
        out_specs=out_specs,
        out_shape=out_shape,
        input_output_aliases=aliases,
        scratch_shapes=scratch,
        compiler_params=_params("arbitrary"),
        name="ctx_proj_ssd" if ctx else "lat_proj_ssd",
    )(*args)


def _post_kernel(x_ref, ya_ref, yb_ref, yc_ref, ada_ref, g_ref, wm_ref, wa_ref, wb_ref, wc_ref, wo_ref, fg_ref,
                 o_ref, *, tm, row_base, tokens_per_row, final):
    row = row_base + (pl.program_id(0) * tm) // tokens_per_row
    gate = ada_ref[pl.ds(row, 1), 2 * D_MODEL:3 * D_MODEL]
    d = D_MODEL
    x = x_ref[...]
    h = _modulated_norm(x, g_ref, ada_ref, row)
    merged = None
    for n, (y_ref, w_ref) in enumerate(((ya_ref, wa_ref), (yb_ref, wb_ref), (yc_ref, wc_ref))):
        r0 = _SRC["merge"] + n * d
        logits = _dot_nt(h, wm_ref[r0:r0 + d, :])
        term = _sigmoid(logits) * _dot(y_ref[...], w_ref[...])
        merged = term if merged is None else merged + term
    x = x + gate * _dot(merged.astype(BF16), wo_ref[...])
    if final:
        x = x * lax.rsqrt(jnp.mean(x * x, axis=-1, keepdims=True) + EPS) * fg_ref[...]
    o_ref[...] = x


def _post_call(x, ya, yb, yc, ada, norm_g, w_merge_t, li, wa, wb, wc, wo, final_g, *, tm, row_base,
               tokens_per_row, final):
    t = x.shape[0]
    tok = lambda w: pl.BlockSpec((tm, w), lambda i: (i, 0))
    layer = lambda *shape, **kw: pl.BlockSpec((None,) + shape, lambda i: (li,) + (0,) * len(shape), **kw)
    once = dict(pipeline_mode=pl.Buffered(1))
    kern = functools.partial(_post_kernel, tm=tm, row_base=row_base, tokens_per_row=tokens_per_row, final=final)
    return pl.pallas_call(
        kern,
        grid=(t // tm,),
        in_specs=[tok(D_MODEL), tok(W_A), tok(DI_B), tok(W_C),
                  layer(8, 3 * D_MODEL), layer(1, D_MODEL), layer(w_merge_t.shape[1], D_MODEL, **once),
                  layer(W_A, D_MODEL, **once), layer(DI_B, D_MODEL, **once), layer(W_C, D_MODEL, **once),
                  layer(D_MODEL, D_MODEL, **once), pl.BlockSpec((1, D_MODEL), lambda i: (0, 0))],
        out_specs=tok(D_MODEL),
        out_shape=jax.ShapeDtypeStruct((t, D_MODEL), F32),
        compiler_params=_params("arbitrary"),
        name="post_final" if final else "post",
    )(x, ya, yb, yc, ada, norm_g, w_merge_t, wa, wb, wc, wo, final_g)


def _rope_tables():
    pos = np.arange(DEC_SEQ)
    lane = np.arange(LANES)
    l64 = lane % (2 * (DH_A // 2))
    quarter = DH_A // 4
    p = np.where((l64 < DH_A // 2)[None, :], (pos // GRID_W)[:, None], (pos % GRID_W)[:, None])
    inv = ROPE_BASE ** (-np.arange(quarter, dtype=np.float64) / quarter)
    ang = p.astype(np.float64) * inv[l64 % quarter][None, :]
    sign = np.where((lane % (2 * quarter)) < quarter, -1.0, 1.0)
    return jnp.asarray(np.cos(ang), F32), jnp.asarray(np.sin(ang) * sign[None, :], F32)


def _pad_lanes(v, width=LANES):
    v = v.reshape(1, -1).astype(F32)
    return jnp.pad(v, ((0, 0), (0, width - v.shape[1])))


def kernel(x_prompt, x_sample, cache_diff_k, cache_diff_v, cache_na_k, cache_na_v, state_ssd, c, c_ctx,
           norm_g, w_ada, b_ada, w_in, lam_q1, lam_k1, lam_q2, lam_k2, diff_subln_g, conv_w, conv_b,
           dt_bias, a_log, d_skip, ssd_norm_g, na_rpb, w_br_a, w_br_b, w_br_c, w_out, final_g):
    assert x_prompt.shape == (BATCH, SEQ, D_MODEL) and x_sample.shape == (DEC_BATCH, DEC_SEQ, D_MODEL)
    assert w_in.shape == (DEPTH, D_MODEL, _SRC["merge"] + MERGE_COLS)
    w16 = jnp.swapaxes(w_in, 1, 2).astype(BF16)
    w_dt = jnp.pad(w16[:, _SRC["dt"]:_SRC["qc"], :], ((0, 0), (0, LANES - 2 * H_B), (0, 0)))
    wa16, wb16, wc16, wo16 = (w.astype(BF16) for w in (w_br_a, w_br_b, w_br_c, w_out))

    cvecs = jnp.concatenate([c_ctx[None, :], c, jnp.zeros((8 - 1 - DEC_BATCH, D_MODEL), F32)], axis=0)
    ada = _ada_call(cvecs.T, w_ada, b_ada)
    cos_t, sin_t = _rope_tables()

    ck_a = cache_diff_k.reshape(DEC_BATCH, DEPTH, PAST_LEN, W_A)
    cv_a = cache_diff_v.reshape(DEC_BATCH, DEPTH, PAST_LEN, W_A)
    ck_c = cache_na_k.transpose(0, 1, 3, 4, 2)
    cv_c = cache_na_v.transpose(0, 1, 3, 4, 2)
    na_tiles = _rpb_call(na_rpb)
    h0t = state_ssd.transpose(0, 1, 2, 5, 3, 4).reshape(DEC_BATCH, DEPTH, 2, N_B, DI_B)
    h0t = h0t.reshape(DEC_BATCH, DEPTH, 2, N_B, H_B // 2, LANES).transpose(0, 1, 2, 4, 3, 5)
    h0t = h0t.reshape(DEC_BATCH, DEPTH, 2, (H_B // 2) * N_B, LANES)

    xp = x_prompt.reshape(BATCH * SEQ, D_MODEL)
    xs = x_sample.reshape(DEC_BATCH * DEC_SEQ, D_MODEL)
    fg = final_g.reshape(1, D_MODEL)
    norm_g3 = norm_g.reshape(DEPTH, 1, D_MODEL)
    carry = None
    for li in range(DEPTH):
        lam_init = 0.8 - 0.6 * math.exp(-0.3 * li)
        final = li == DEPTH - 1
        lamvec = jnp.concatenate([_pad_lanes(v[li]) for v in (lam_q1, lam_k1, lam_q2, lam_k2)], axis=0)
        subln = diff_subln_g[li].reshape(1, LANES)
        dtb = _pad_lanes(dt_bias[li])
        alog = _pad_lanes(a_log[li])
        dskx = jnp.repeat(d_skip[li], P_B, axis=-1)
        ssd_w = (conv_w[li], conv_b[li].reshape(1, CONV_DIM), dtb, alog, dskx, ssd_norm_g[li].reshape(1, DI_B))
        post_w = (wa16, wb16, wc16, wo16, fg)

        proj, ka, va, kc_t, vc_t, yb, ssd_state = _proj_ssd_call(xp, ada, norm_g3, w16, w_dt, ssd_w, li,
                                                                 ctx=True, carry=carry)
        carry = (ka, va, kc_t, vc_t, ssd_state)
        ya = _attn_a_ctx_call(proj, lamvec, subln, lam_init)
        yc = _attn_c_ctx_call(proj, kc_t, vc_t, li)
        xp = _post_call(xp, ya, yb, yc, ada, norm_g3, w16, li, *post_w, tm=512, row_base=0,
                        tokens_per_row=BATCH * SEQ, final=final)

        proj, kv, yb = _proj_ssd_call(xs, ada, norm_g3, w16, w_dt, ssd_w, li, ctx=False, h0t=h0t)
        ya = _attn_a_lat_call(proj, ck_a, cv_a, li, cos_t, sin_t, lamvec, subln, lam_init)
        yc = _attn_c_lat_call(proj, kv, ck_c, cv_c, li, na_tiles)
        xs = _post_call(xs, ya, yb, yc, ada, norm_g3, w16, li, *post_w, tm=512, row_base=1,
                        tokens_per_row=DEC_SEQ, final=final)

    new_k_a, new_v_a, new_k_c_t, new_v_c_t, new_ssd = carry
    to_token_major = lambda a: a.transpose(0, 1, 4, 2, 3)
    return (xp.reshape(BATCH, SEQ, D_MODEL), xs.reshape(DEC_BATCH, DEC_SEQ, D_MODEL),
            new_k_a, new_v_a, to_token_major(new_k_c_t), to_token_major(new_v_c_t), new_ssd)
```

```python
import functools
import math

import jax
import jax.numpy as jnp
import numpy as np
from jax import lax
from jax.experimental import pallas as pl
from jax.experimental.pallas import tpu as pltpu

D_MODEL = 1024
BATCH = 32
SEQ = 256
DEPTH = 2
DEC_BATCH = 2
DEC_SEQ = 1024
PAST_LEN = 512
GRID_W = 64
GRID_ROWS = DEC_SEQ // GRID_W
H_A = 4
DH_A = 64
W_A = H_A * 2 * DH_A
H_B = 8
P_B = 64
G_B = 2
N_B = 64
DI_B = H_B * P_B
CONV_K = 5
CONV_DIM = DI_B + 2 * G_B * N_B
SSD_CHUNK = 128
H_C = 8
DH_C = 64
W_C = H_C * DH_C
NA_KH = 8
NA_KW = 16
N_BRANCH = 3
ROPE_BASE = 10000.0
EPS = 1e-6

LANES = 128
HALF = LANES // 2
VMEM_LIMIT = 56 * 1024 * 1024

BC_DIM = CONV_DIM - DI_B
KVC_COLS = 2 * W_C
MERGE_COLS = N_BRANCH * D_MODEL
_SRC = dict(qa=0, ka=512, va=1024, ga=1536, z=2048, xs=2560, bc=3072, dt=3328, qc=3344, kc=3856, vc=4368,
            gc=4880, merge=5392)
COL_QA = 0
COL_GA = 512
COL_QC = 1024
COL_GC = 1536
COL_KA = 2048
COL_VA = 2560
PROJ_COLS = 3072
PROJ_SEGMENTS = ((COL_QA, _SRC["qa"]), (COL_GA, _SRC["ga"]), (COL_QC, _SRC["qc"]), (COL_GC, _SRC["gc"]),
                 (COL_KA, _SRC["ka"]), (COL_VA, _SRC["va"]))
P_Z = 0
P_XS = 512
P_BC = 1024
P_DT = 1280
P_COLS = P_DT + LANES
SIDE_TN = 256

NA_QROWS = 4
NA_WROWS = 12
NA_TILES = 2 * NA_KH
NEG_INF = float("-inf")
LOG2E = math.log2(math.e)
ATTN_AHEAD = 2
SSD_AHEAD = 2
CTX_BATCHES_PER_STEP = 2
ATTN_BATCHES_PER_STEP = 4
HI = lax.Precision.HIGHEST
F32 = jnp.float32
BF16 = jnp.bfloat16


def _dot(a, b, precision=None):
    return jnp.dot(a, b, preferred_element_type=F32, precision=precision)


def _dot_nt(a, b):
    return lax.dot_general(a, b, (((1,), (1,)), ((), ())), preferred_element_type=F32)


def _sigmoid(x):
    return 1.0 / (1.0 + jnp.exp(-x))


def _silu(x):
    return x * _sigmoid(x)


def _lane(shape):
    return lax.broadcasted_iota(jnp.int32, shape, len(shape) - 1)


def _params(*sem):
    return pltpu.CompilerParams(dimension_semantics=sem, vmem_limit_bytes=VMEM_LIMIT)


def _ada_kernel(cvt_ref, w_ref, b_ref, o_ref):
    n_rows = 1 + DEC_BATCH
    s = _silu(cvt_ref[...])
    accs = [jnp.zeros((8, w_ref.shape[1]), F32)] * n_rows
    for k0 in range(0, D_MODEL, 8):
        w = w_ref[k0:k0 + 8, :]
        accs = [acc + w * s[k0:k0 + 8, r:r + 1] for r, acc in enumerate(accs)]
    rows = [jnp.sum(acc, axis=0, keepdims=True) for acc in accs]
    rows.append(jnp.zeros((8 - n_rows, w_ref.shape[1]), F32))
    o_ref[...] = jnp.concatenate(rows, axis=0) + b_ref[...]


def _ada_call(cvecs, w_ada, b_ada):
    tn = 1536
    return pl.pallas_call(
        _ada_kernel,
        grid=(DEPTH, 3 * D_MODEL // tn),
        in_specs=[
            pl.BlockSpec((D_MODEL, 8), lambda l, j: (0, 0)),
            pl.BlockSpec((None, D_MODEL, tn), lambda l, j: (l, 0, j)),
            pl.BlockSpec((None, 1, tn), lambda l, j: (l, 0, j)),
        ],
        out_specs=pl.BlockSpec((None, 8, tn), lambda l, j: (l, 0, j)),
        out_shape=jax.ShapeDtypeStruct((DEPTH, 8, 3 * D_MODEL), F32),
        compiler_params=_params("arbitrary", "arbitrary"),
        name="ada",
    )(cvecs, w_ada, b_ada.reshape(DEPTH, 1, 3 * D_MODEL))


def _modulated_norm(x, g_ref, ada_ref, row):
    y = x * lax.rsqrt(jnp.mean(x * x, axis=-1, keepdims=True) + EPS) * g_ref[...]
    shift = ada_ref[pl.ds(row, 1), 0:D_MODEL]
    scale = ada_ref[pl.ds(row, 1), D_MODEL:2 * D_MODEL]
    return (y * (1.0 + scale) + shift).astype(BF16)


def _diff_lambda_in_kernel(lam_ref, lam_init):
    v = lam_ref[...]
    l1 = jnp.sum(v[0:1] * v[1:2], axis=-1, keepdims=True)
    l2 = jnp.sum(v[2:3] * v[3:4], axis=-1, keepdims=True)
    return jnp.exp(l1) - jnp.exp(l2) + lam_init


def _split_halves(x, scale):
    lo = _lane(x.shape) < HALF
    xs = x * (scale * LOG2E)
    return jnp.concatenate([jnp.where(lo, xs, 0.0), jnp.where(lo, 0.0, xs)], axis=0).astype(BF16)


def _diff_combine(o2, rsum, lam, t):
    return o2[:t] * rsum[:t] - (lam * rsum[t:]) * o2[t:]


def _diff_head_post(o, subln_g, lam_init, gate):
    o = o * lax.rsqrt(jnp.mean(o * o, axis=-1, keepdims=True) + EPS) * (subln_g * (1.0 - lam_init))
    return (o * _silu(gate)).astype(BF16)


def _attn_a_ctx_kernel(q_ref, k_ref, v_ref, g_ref, lam_ref, sg_ref, o_ref, *, lam_init):
    t = SEQ
    lam = _diff_lambda_in_kernel(lam_ref, lam_init)
    ones = jnp.ones((t, LANES), BF16)
    blocks = [(b, h) for b in range(q_ref.shape[0] // t) for h in range(H_A)]
    where = lambda b, h: (slice(b * t, (b + 1) * t), slice(h * LANES, (h + 1) * LANES))

    def scores(b, h):
        qq = _split_halves(q_ref[where(b, h)], DH_A ** -0.5)
        return _dot_nt(qq, k_ref[where(b, h)].astype(BF16))

    pending = [scores(*blk) for blk in blocks[:ATTN_AHEAD]]
    for n, blk in enumerate(blocks):
        s = pending.pop(0)
        if n + ATTN_AHEAD < len(blocks):
            pending.append(scores(*blocks[n + ATTN_AHEAD]))
        e = jnp.exp2(s - jnp.max(s, axis=-1, keepdims=True)).astype(BF16)
        rsum = 1.0 / _dot(e, ones)
        o = _diff_combine(_dot(e, v_ref[where(*blk)].astype(BF16)), rsum, lam, t)
        o_ref[where(*blk)] = _diff_head_post(o, sg_ref[...], lam_init, g_ref[where(*blk)])


def _attn_a_ctx_call(proj, lamvec, subln_g, lam_init):
    rows = ATTN_BATCHES_PER_STEP * SEQ
    blk = lambda c: pl.BlockSpec((rows, W_A), lambda b: (b, c // W_A))
    return pl.pallas_call(
        functools.partial(_attn_a_ctx_kernel, lam_init=lam_init),
        grid=(BATCH // ATTN_BATCHES_PER_STEP,),
        in_specs=[blk(COL_QA), blk(COL_KA), blk(COL_VA), blk(COL_GA),
                  pl.BlockSpec((4, LANES), lambda b: (0, 0)),
                  pl.BlockSpec((1, LANES), lambda b: (0, 0))],
        out_specs=pl.BlockSpec((rows, W_A), lambda b: (b, 0)),
        out_shape=jax.ShapeDtypeStruct((BATCH * SEQ, W_A), BF16),
        compiler_params=_params("arbitrary"),
        name="attn_a_ctx",
    )(proj, proj, proj, proj, lamvec, subln_g)


def _rope(x, cos, sin_signed):
    first = (_lane(x.shape) % 32) < 16
    swapped = jnp.where(first, pltpu.roll(x, LANES - 16, 1), pltpu.roll(x, 16, 1))
    return x * cos + swapped * sin_signed


def _attn_a_lat_kernel(q_ref, k_ref, v_ref, g_ref, ck_ref, cv_ref, cosq_ref, sinq_ref, cosk_ref, sink_ref,
                       lam_ref, sg_ref, o_ref, kr_s, *, lam_init):
    tq = q_ref.shape[0]

    @pl.when(pl.program_id(1) == 0)
    def _():
        for h in range(H_A):
            sl = slice(h * LANES, (h + 1) * LANES)
            kr_s[:, sl] = _rope(k_ref[:, sl], cosk_ref[...], sink_ref[...]).astype(BF16)

    lam = _diff_lambda_in_kernel(lam_ref, lam_init)

    def scores(h):
        sl = slice(h * LANES, (h + 1) * LANES)
        qq = _split_halves(_rope(q_ref[:, sl], cosq_ref[...], sinq_ref[...]), DH_A ** -0.5)
        return _dot_nt(qq, kr_s[:, sl]), _dot_nt(qq, ck_ref[:, sl].astype(BF16))

    pending = [scores(h) for h in range(ATTN_AHEAD)]
    for h in range(H_A):
        sl = slice(h * LANES, (h + 1) * LANES)
        s_lat, s_ctx = pending.pop(0)
        if h + ATTN_AHEAD < H_A:
            pending.append(scores(h + ATTN_AHEAD))
        m = jnp.maximum(jnp.max(s_lat, axis=-1, keepdims=True), jnp.max(s_ctx, axis=-1, keepdims=True))
        e_lat = jnp.exp2(s_lat - m)
        e_ctx = jnp.exp2(s_ctx - m)
        rsum = 1.0 / (jnp.sum(e_lat, axis=-1, keepdims=True) + jnp.sum(e_ctx, axis=-1, keepdims=True))
        o2 = _dot(e_lat.astype(BF16), v_ref[:, sl].astype(BF16)) + _dot(e_ctx.astype(BF16),
                                                                         cv_ref[:, sl].astype(BF16))
        o = _diff_combine(o2, rsum, lam, tq)
        o_ref[:, sl] = _diff_head_post(o, sg_ref[...], lam_init, g_ref[:, sl])


def _attn_a_lat_call(proj, cache_k, cache_v, li, cos_t, sin_t, lamvec, subln_g, lam_init):
    tq = 256
    nq = DEC_SEQ // tq
    qblk = lambda c: pl.BlockSpec((tq, W_A), lambda b, i: (b * nq + i, c // W_A))
    full = lambda c: pl.BlockSpec((DEC_SEQ, W_A), lambda b, i: (b, c // W_A))
    cache = pl.BlockSpec((None, None, PAST_LEN, W_A), lambda b, i: (b, li, 0, 0))
    return pl.pallas_call(
        functools.partial(_attn_a_lat_kernel, lam_init=lam_init),
        grid=(DEC_BATCH, nq),
        in_specs=[qblk(COL_QA), full(COL_KA), full(COL_VA), qblk(COL_GA), cache, cache,
                  pl.BlockSpec((tq, LANES), lambda b, i: (i, 0)),
                  pl.BlockSpec((tq, LANES), lambda b, i: (i, 0)),
                  pl.BlockSpec((DEC_SEQ, LANES), lambda b, i: (0, 0)),
                  pl.BlockSpec((DEC_SEQ, LANES), lambda b, i: (0, 0)),
                  pl.BlockSpec((4, LANES), lambda b, i: (0, 0)),
                  pl.BlockSpec((1, LANES), lambda b, i: (0, 0))],
        out_specs=pl.BlockSpec((tq, W_A), lambda b, i: (b * nq + i, 0)),
        out_shape=jax.ShapeDtypeStruct((DEC_BATCH * DEC_SEQ, W_A), BF16),
        scratch_shapes=[pltpu.VMEM((DEC_SEQ, W_A), BF16)],
        compiler_params=_params("arbitrary", "arbitrary"),
        name="attn_a_lat",
    )(proj, proj, proj, proj, cache_k, cache_v, cos_t, sin_t, cos_t, sin_t, lamvec, subln_g)


def _merge_halves(o, t):
    return jnp.where(_lane((t, LANES)) < HALF, o[:t], o[t:])


def _spread(work, n_slots):
    state = dict(slot=0, done=0)

    def emit_until(target):
        while state["done"] < target:
            work[state["done"]]()
            state["done"] += 1

    def side():
        state["slot"] += 1
        emit_until(min(len(work), -(-state["slot"] * len(work) // n_slots)))

    return side, lambda: emit_until(len(work))


def _attn_c_ctx_body(q_ref, kt_ref, vt_ref, g_ref, o_ref, side):
    t = SEQ
    blocks = [(b, j) for b in range(q_ref.shape[0] // t) for j in range(H_C // 2)]
    where = lambda b, j: (slice(b * t, (b + 1) * t), slice(j * LANES, (j + 1) * LANES))
    pair_t = lambda ref, b, j: ref[b, 2 * j:2 * j + 2].reshape(LANES, t).astype(BF16)

    def scores(b, j):
        return _dot(_split_halves(q_ref[where(b, j)], DH_C ** -0.5), pair_t(kt_ref, b, j))

    pending = [scores(*blk) for blk in blocks[:ATTN_AHEAD]]
    for n, blk in enumerate(blocks):
        s = pending.pop(0)
        if n + ATTN_AHEAD < len(blocks):
            pending.append(scores(*blocks[n + ATTN_AHEAD]))
        e = jnp.exp2(s - jnp.max(s, axis=-1, keepdims=True))
        rsum = 1.0 / jnp.sum(e, axis=-1, keepdims=True)
        o = _merge_halves(_dot_nt(e.astype(BF16), pair_t(vt_ref, *blk)) * rsum, t)
        o_ref[where(*blk)] = (o * _silu(g_ref[where(*blk)])).astype(BF16)
        side()


def _post_ctx_kernel(q_ref, kt_ref, vt_ref, gc_ref, x_ref, ya_ref, yb_ref, ada_ref, g_ref, wm_ref, wa_ref, wb_ref,
                     wc_ref, wo_ref, fg_ref, o_ref, yc_s, h_s, acc_s, sc_s, *, final):
    d = D_MODEL
    gate = ada_ref[0:1, 2 * d:3 * d]
    h_s[...] = _modulated_norm(x_ref[...], g_ref, ada_ref, 0)

    def gate_logits(n, c0):
        return _dot_nt(h_s[...], wm_ref[n * d + c0:n * d + c0 + SIDE_TN, :])

    work = []
    for c0 in range(0, d, SIDE_TN):
        cols = slice(c0, c0 + SIDE_TN)

        def branch_a(c0=c0, cols=cols):
            acc_s[:, cols] = _sigmoid(gate_logits(0, c0)) * _dot(ya_ref[...], wa_ref[:, cols])

        def branch_b(c0=c0, cols=cols):
            acc_s[:, cols] += _sigmoid(gate_logits(1, c0)) * _dot(yb_ref[...], wb_ref[:, cols])

        def gate_c(c0=c0, cols=cols):
            sc_s[:, cols] = _sigmoid(gate_logits(2, c0))

        work += [branch_a, branch_b, gate_c]
    side, flush = _spread(work, (q_ref.shape[0] // SEQ) * (H_C // 2))
    _attn_c_ctx_body(q_ref, kt_ref, vt_ref, gc_ref, yc_s, side)
    flush()
    merged = acc_s[...] + sc_s[...] * _dot(yc_s[...], wc_ref[...])
    x = x_ref[...] + gate * _dot(merged.astype(BF16), wo_ref[...])
    if final:
        x = x * lax.rsqrt(jnp.mean(x * x, axis=-1, keepdims=True) + EPS) * fg_ref[...]
    o_ref[...] = x


def _post_ctx_call(x, proj, kc_t, vc_t, ya, yb, ada, norm_g, w_merge, li, wa, wb, wc, wo, final_g, *, final):
    nb = CTX_BATCHES_PER_STEP
    tm = nb * SEQ
    tok = lambda w: pl.BlockSpec((tm, w), lambda i: (i, 0))
    col = lambda c: pl.BlockSpec((tm, W_C), lambda i: (i, c // W_C))
    cache = pl.BlockSpec((nb, None, H_C, DH_C, SEQ), lambda i: (i, li, 0, 0, 0))
    layer = lambda *shape, **kw: pl.BlockSpec((None,) + shape, lambda i: (li,) + (0,) * len(shape), **kw)
    once = dict(pipeline_mode=pl.Buffered(1))
    return pl.pallas_call(
        functools.partial(_post_ctx_kernel, final=final),
        grid=(BATCH // nb,),
        in_specs=[col(COL_QC), cache, cache, col(COL_GC), tok(D_MODEL), tok(W_A), tok(DI_B),
                  layer(8, 3 * D_MODEL), layer(1, D_MODEL), layer(MERGE_COLS, D_MODEL, **once),
                  layer(W_A, D_MODEL, **once), layer(DI_B, D_MODEL, **once), layer(W_C, D_MODEL, **once),
                  layer(D_MODEL, D_MODEL, **once), pl.BlockSpec((1, D_MODEL), lambda i: (0, 0))],
        out_specs=tok(D_MODEL),
        out_shape=jax.ShapeDtypeStruct((BATCH * SEQ, D_MODEL), F32),
        scratch_shapes=[pltpu.VMEM((tm, W_C), BF16), pltpu.VMEM((tm, D_MODEL), BF16),
                        pltpu.VMEM((tm, D_MODEL), F32), pltpu.VMEM((tm, D_MODEL), F32)],
        compiler_params=_params("arbitrary"),
        name="attn_c_post_ctx_final" if final else "attn_c_post_ctx",
    )(proj, kc_t, vc_t, proj, x, ya, yb, ada, norm_g, w_merge, wa, wb, wc, wo, final_g)


def _rpb_kernel(rpb_ref, o_ref):
    shape = (GRID_W, LANES)
    c = lax.broadcasted_iota(jnp.int32, shape, 0)
    cp = _lane(shape) % GRID_W
    start = jnp.clip(c - NA_KW // 2, 0, GRID_W - NA_KW)
    in_win = (cp >= start) & (cp < start + NA_KW)
    for h in range(H_C):
        o_ref[h, 0] = jnp.full(shape, NEG_INF, F32)
        for dr in range(2 * NA_KH - 1):
            row = jnp.broadcast_to(rpb_ref[h, dr:dr + 1, :], shape)
            tile = pltpu.roll(row, LANES - (NA_KW - 1), 1, stride=1, stride_axis=0)
            o_ref[h, 1 + dr] = jnp.where(in_win, tile * LOG2E, NEG_INF)


def _rpb_call(rpb):
    n_dc = 2 * NA_KW - 1
    v = jnp.pad(rpb, ((0, 0), (0, 0), (0, NA_TILES - (2 * NA_KH - 1)), (0, GRID_W - n_dc)))
    v = jnp.concatenate([v] * (LANES // GRID_W), axis=-1)
    return pl.pallas_call(
        _rpb_kernel,
        grid=(DEPTH,),
        in_specs=[pl.BlockSpec((None, H_C, NA_TILES, LANES), lambda l: (l, 0, 0, 0))],
        out_specs=pl.BlockSpec((None, H_C, NA_TILES, GRID_W, LANES), lambda l: (l, 0, 0, 0, 0)),
        out_shape=jax.ShapeDtypeStruct((DEPTH, H_C, NA_TILES, GRID_W, LANES), F32),
        compiler_params=_params("arbitrary"),
        name="rpb_tiles",
    )(v)


def _attn_c_lat_kernel(q_ref, k_ref, v_ref, g_ref, ck_ref, cv_ref, tile_ref, o_ref, bias_s):
    tq = q_ref.shape[0]
    nwin = NA_WROWS * GRID_W
    m = pl.program_id(1)
    w0 = jnp.where(m < (GRID_ROWS // NA_QROWS) // 2, 0, GRID_ROWS - NA_WROWS)
    k0 = pl.multiple_of(w0 * GRID_W, GRID_W)
    lo = _lane((GRID_W, LANES)) < HALF
    n_pair = H_C // 2

    def scores(j):
        sl = slice(j * LANES, (j + 1) * LANES)
        for s in range(2):
            for i in range(NA_QROWS):
                r = m * NA_QROWS + i
                start = jnp.clip(r - NA_KH // 2, 0, GRID_ROWS - NA_KH)
                for jp in range(NA_WROWS // 2):
                    idx = []
                    for u in range(2):
                        rk = w0 + 2 * jp + u
                        valid = (rk >= start) & (rk < start + NA_KH)
                        idx.append(jnp.where(valid, rk - r + NA_KH, 0))
                    tile = jnp.where(lo, tile_ref[2 * j + s, idx[0]], tile_ref[2 * j + s, idx[1]])
                    bias_s[(s * NA_QROWS + i) * GRID_W:(s * NA_QROWS + i + 1) * GRID_W,
                           jp * LANES:(jp + 1) * LANES] = tile
        qq = _split_halves(q_ref[:, sl], DH_C ** -0.5)
        kw = k_ref[pl.ds(k0, nwin), sl].astype(BF16)
        s_win = _dot_nt(qq, kw) + bias_s[...]
        ckt = ck_ref[2 * j:2 * j + 2].reshape(LANES, PAST_LEN).astype(BF16)
        return s_win, _dot(qq, ckt)

    pending = [scores(j) for j in range(ATTN_AHEAD)]
    for j in range(n_pair):
        sl = slice(j * LANES, (j + 1) * LANES)
        s_win, s_ctx = pending.pop(0)
        if j + ATTN_AHEAD < n_pair:
            pending.append(scores(j + ATTN_AHEAD))
        vw = v_ref[pl.ds(k0, nwin), sl].astype(BF16)
        mx = jnp.maximum(jnp.max(s_win, axis=-1, keepdims=True), jnp.max(s_ctx, axis=-1, keepdims=True))
        e_win = jnp.exp2(s_win - mx)
        e_ctx = jnp.exp2(s_ctx - mx)
        rs = 1.0 / (jnp.sum(e_win, axis=-1, keepdims=True) + jnp.sum(e_ctx, axis=-1, keepdims=True))
        cvt = cv_ref[2 * j:2 * j + 2].reshape(LANES, PAST_LEN).astype(BF16)
        o = (_dot(e_win.astype(BF16), vw) + _dot_nt(e_ctx.astype(BF16), cvt)) * rs
        o_ref[:, sl] = (_merge_halves(o, tq) * _silu(g_ref[:, sl])).astype(BF16)


def _attn_c_lat_call(proj, kv, cache_k, cache_v, li, tiles):
    tq = NA_QROWS * GRID_W
    nq = DEC_SEQ // tq
    qblk = lambda c: pl.BlockSpec((tq, W_C), lambda b, i: (b * nq + i, c // W_C))
    full = lambda c: pl.BlockSpec((DEC_SEQ, W_C), lambda b, i: (b, c // W_C))
    cache = pl.BlockSpec((None, None, H_C, DH_C, PAST_LEN), lambda b, i: (b, li, 0, 0, 0))
    return pl.pallas_call(
        _attn_c_lat_kernel,
        grid=(DEC_BATCH, nq),
        in_specs=[qblk(COL_QC), full(0), full(W_C), qblk(COL_GC), cache, cache,
                  pl.BlockSpec((None, H_C, NA_TILES, GRID_W, LANES), lambda b, i: (li, 0, 0, 0, 0))],
        out_specs=pl.BlockSpec((tq, W_C), lambda b, i: (b * nq + i, 0)),
        out_shape=jax.ShapeDtypeStruct((DEC_BATCH * DEC_SEQ, W_C), BF16),
        scratch_shapes=[pltpu.VMEM((2 * tq, NA_WROWS * GRID_W), F32)],
        compiler_params=_params("arbitrary", "arbitrary"),
        name="attn_c_lat",
    )(proj, kv, kv, proj, cache_k, cache_v, tiles)


def _ssd_body(dt_ref, xs_ref, bc_ref, z_ref, h0_ref, params, y_ref, hs_ref, scratch, *, seq, static_loops, side):
    cw_ref, cb_ref, dtb_ref, alog_ref, dsk_ref, g_ref = params
    upad_s, xc_s, expo_s, expot_s, dtt_s, tot_s, bmt_s, yf_s, yb_s, st_s = scratch
    use_h0 = h0_ref is not None
    want_state = hs_ref is not None

    q = SSD_CHUNK
    nc = seq // q
    n_pair = H_B // 2
    n_hd = 2 * H_B
    pad = 8

    def loop(body, unroll=1):
        if static_loops:
            for c in range(nc):
                body(c, 0)
        else:
            lax.fori_loop(0, nc, body, 0, unroll=unroll)

    def chunk_rows(c):
        return slice(c * q, (c + 1) * q) if isinstance(c, int) else pl.ds(pl.multiple_of(c * q, q), q)

    upad_s[0:pad, :] = jnp.zeros((pad, CONV_DIM), F32)
    upad_s[pad + seq:2 * pad + seq, :] = jnp.zeros((pad, CONV_DIM), F32)
    upad_s[pad:pad + seq, 0:DI_B] = xs_ref[...]
    upad_s[pad:pad + seq, DI_B:CONV_DIM] = bc_ref[...]

    for c in range(nc):
        for cb_ in range(CONV_DIM // LANES):
            csl = slice(cb_ * LANES, (cb_ + 1) * LANES)
            acc = jnp.zeros((q, LANES), F32) + cb_ref[:, csl]
            for k in range(CONV_K):
                r0 = c * q + pad - CONV_K // 2 + k
                acc = acc + upad_s[r0:r0 + q, csl] * cw_ref[k:k + 1, csl]
            xc_s[c * q:(c + 1) * q, csl] = _silu(acc)
            side()

    a_row = -jnp.exp(alog_ref[...]) * LOG2E
    a_col = jnp.broadcast_to(a_row, (LANES, LANES)).T[0:n_hd, 0:1]
    ri = lax.broadcasted_iota(jnp.int32, (q, q), 0)
    ci = lax.broadcasted_iota(jnp.int32, (q, q), 1)
    ltri = (ri >= ci).astype(F32)
    fwd_lane = _lane((q, LANES)) < H_B
    fwd_row = lax.broadcasted_iota(jnp.int32, (n_hd, q), 0) < H_B

    def prep_body(c, carry):
        rows = chunk_rows(c)
        xdt = dt_ref[rows, 0:LANES] + dtb_ref[...]
        dtv = jnp.maximum(xdt, 0.0) + jnp.log1p(jnp.exp(-jnp.abs(xdt)))
        la = dtv * a_row
        acum = _dot(ltri, la, HI)
        expo_s[rows, :] = jnp.where(fwd_lane, acum, la - acum)
        acum_t = acum.T[0:n_hd, :]
        dt_t = dtv.T[0:n_hd, :]
        expot_s[c] = jnp.where(fwd_row, acum_t, dt_t * a_col - acum_t)
        dtt_s[c] = dt_t
        tot_s[c] = jnp.broadcast_to(acum_t[:, q - 1:q], (n_hd, q))
        bmt_s[c] = xc_s[rows, DI_B:DI_B + LANES].T
        side()
        return carry

    loop(prep_body, unroll=2)

    if use_h0:
        st_s[...] = h0_ref[...].reshape(2, n_pair, N_B, LANES)
    else:
        st_s[...] = jnp.zeros_like(st_s)

    lane_q = _lane((q, LANES))
    lo = lane_q < HALF
    lo_st = _lane((N_B, LANES)) < HALF

    def chunk_pair(c_fwd, c_bwd):
        dirs = ((0, c_fwd, yf_s), (1, c_bwd, yb_s))
        group_of = lambda k: k * G_B // n_pair
        items = [(k, d) for k in range(n_pair) for d in dirs]
        cb, y_off, st_in = {}, {}, {}

        def issue_early(k, d):
            dirn, c, _ = d
            g, rows = group_of(k), chunk_rows(c)
            in_g = (lane_q >= g * N_B) & (lane_q < (g + 1) * N_B)
            cmg = jnp.where(in_g, xc_s[rows, DI_B + LANES:DI_B + 2 * LANES], 0.0).astype(BF16)
            if (dirn, g) not in cb:
                cb[dirn, g] = _dot_nt(cmg, xc_s[rows, DI_B:DI_B + LANES].astype(BF16))
            st_in[dirn, k] = st_s[dirn, k]
            y_off[dirn, k] = _dot(cmg, jnp.concatenate([st_in[dirn, k]] * 2, axis=0).astype(BF16))

        for item in items[:SSD_AHEAD]:
            issue_early(*item)
        for n, (k, (dirn, c, y_s)) in enumerate(items):
            if n + SSD_AHEAD < len(items):
                issue_early(*items[n + SSD_AHEAD])
            psl = slice(k * LANES, (k + 1) * LANES)
            rows = chunk_rows(c)
            tri = (ri >= ci) if dirn == 0 else (ci >= ri)
            bmt_g = bmt_s[c, group_of(k) * N_B:(group_of(k) + 1) * N_B, :]
            x16 = xc_s[rows, psl].astype(BF16)
            mats, lhs, ysc, cdec = [], [], [], []
            for s in range(2):
                col = dirn * H_B + 2 * k + s
                e_col = jnp.broadcast_to(expo_s[rows, col:col + 1], (q, q))
                e_row = expot_s[c, col:col + 1, :]
                dt_row = dtt_s[c, col:col + 1, :]
                tot = tot_s[c, col:col + 1, :]
                dec = jnp.exp2(jnp.where(tri, e_col - e_row, NEG_INF))
                mats.append((cb[dirn, group_of(k)] * dec * dt_row).astype(BF16))
                if dirn == 0:
                    ysc.append(jnp.exp2(e_col))
                    w_row = jnp.exp2(tot - e_row)
                else:
                    ysc.append(jnp.exp2(e_col + tot))
                    w_row = jnp.exp2(-e_row)
                lhs.append((bmt_g * (w_row * dt_row)).astype(BF16))
                cdec.append(jnp.exp2(tot[:, 0:LANES]))
            yd = _dot(jnp.concatenate(mats, axis=0), x16)
            ds = _dot(jnp.concatenate(lhs, axis=0), x16)
            yo, st = y_off.pop((dirn, k)), st_in.pop((dirn, k))
            y_s[rows, psl] = jnp.where(lo, yd[:q] + ysc[0] * yo, yd[q:] + ysc[1] * yo)
            st_s[dirn, k] = jnp.where(lo_st, cdec[0] * st + ds[:N_B], cdec[1] * st + ds[N_B:])
            side()

    def body(c, carry):
        chunk_pair(c, nc - 1 - c)
        return carry

    loop(body, unroll=2)

    dsum = dsk_ref[0:1, :] + dsk_ref[1:2, :]

    def out_body(c, carry):
        rows = chunk_rows(c)
        y = yf_s[rows, :] + yb_s[rows, :] + xc_s[rows, 0:DI_B] * dsum
        y = y * _silu(z_ref[rows, :])
        y = y * lax.rsqrt(jnp.mean(y * y, axis=-1, keepdims=True) + EPS) * g_ref[...]
        y_ref[rows, :] = y.astype(BF16)
        side()
        return carry

    loop(out_body)
    if want_state:
        for dirn in range(2):
            for k in range(n_pair):
                st = st_s[dirn, k]
                st_t = jnp.concatenate([st, st], axis=0).T
                for s in range(2):
                    hs_ref[dirn, 2 * k + s] = st_t[s * P_B:(s + 1) * P_B, 0:N_B]


def _ssd_scratch(seq):
    nc = seq // SSD_CHUNK
    per_chunk_rows = pltpu.VMEM((nc, 2 * H_B, SSD_CHUNK), F32)
    return [pltpu.VMEM((seq + 16, CONV_DIM), F32), pltpu.VMEM((seq, CONV_DIM), F32),
            pltpu.VMEM((seq, LANES), F32), per_chunk_rows, per_chunk_rows, per_chunk_rows,
            pltpu.VMEM((nc, LANES, SSD_CHUNK), F32),
            pltpu.VMEM((seq, DI_B), F32), pltpu.VMEM((seq, DI_B), F32),
            pltpu.VMEM((2, H_B // 2, N_B, LANES), F32)]


def _ssd_param_specs(const):
    return [const((CONV_K, CONV_DIM)), const((1, CONV_DIM)), const((1, LANES)), const((1, LANES)),
            const((2, DI_B)), const((1, DI_B))]


def _proj_ssd_kernel(*refs, nb, seq, ctx, n_carry):
    x_ref, ada_ref, g_ref, w_ref, wdt_ref = refs[:5]
    pos = 5
    h0_ref = None
    if not ctx:
        h0_ref = refs[pos]
        pos += 1
    params = refs[pos:pos + 6]
    outs = refs[pos + 6 + n_carry:]
    if ctx:
        proj_ref, ka_ref, va_ref, kc_ref, vc_ref, y_ref, hs_ref = outs[:7]
        outs = outs[7:]
    else:
        proj_ref, kv_ref, y_ref = outs[:3]
        hs_ref = None
        outs = outs[3:]
    h_s, p_s = outs[:2]
    scratch = outs[2:]

    row = 0 if ctx else 1 + pl.program_id(0)
    for r0 in range(0, nb * seq, SEQ):
        h_s[r0:r0 + SEQ, :] = _modulated_norm(x_ref[r0:r0 + SEQ, :], g_ref, ada_ref, row)
    for c0 in range(P_Z, P_DT, SIDE_TN):
        c1 = min(c0 + SIDE_TN, P_DT)
        p_s[:, c0:c1] = _dot_nt(h_s[...], w_ref[_SRC["z"] + c0:_SRC["z"] + c1, :])
    p_s[:, P_DT:P_COLS] = _dot_nt(h_s[...], wdt_ref[...])

    work = []
    for col, src in PROJ_SEGMENTS:
        for off in range(0, W_A, SIDE_TN):
            def tile(d=col + off, s=src + off):
                proj_ref[:, d:d + SIDE_TN] = _dot_nt(h_s[...], w_ref[s:s + SIDE_TN, :])
            work.append(tile)
            if ctx and col in (COL_KA, COL_VA):
                def store(dst=ka_ref if col == COL_KA else va_ref, col=col, off=off):
                    for b in range(nb):
                        for h in range(off // LANES, (off + SIDE_TN) // LANES):
                            dst[b, :, h, :] = proj_ref[b * seq:(b + 1) * seq, col + h * LANES:col + (h + 1) * LANES]
                work.append(store)
    for r0 in range(0, KVC_COLS, SIDE_TN):
        if ctx:
            for b in range(nb):
                def tile_t(b=b, r0=r0):
                    w_rows = w_ref[_SRC["kc"] + r0:_SRC["kc"] + r0 + SIDE_TN, :]
                    kv_t = _dot_nt(w_rows, h_s[b * seq:(b + 1) * seq, :])
                    dst, d0 = (kc_ref, r0) if r0 < W_C else (vc_ref, r0 - W_C)
                    dst[b, d0 // DH_C:(d0 + SIDE_TN) // DH_C] = kv_t.reshape(SIDE_TN // DH_C, DH_C, seq)
                work.append(tile_t)
        else:
            def tile_kv(r0=r0):
                kv_ref[:, r0:r0 + SIDE_TN] = _dot_nt(h_s[...], w_ref[_SRC["kc"] + r0:_SRC["kc"] + r0 + SIDE_TN, :])
            work.append(tile_kv)

    n_slots = nb * (seq // SSD_CHUNK) * (CONV_DIM // LANES + 2 + 2 * (H_B // 2))
    side, flush = _spread(work, n_slots)
    for b in range(nb):
        rows = pl.ds(b * seq, seq)
        _ssd_body(p_s.at[rows, pl.ds(P_DT, LANES)], p_s.at[rows, pl.ds(P_XS, DI_B)],
                  p_s.at[rows, pl.ds(P_BC, BC_DIM)], p_s.at[rows, pl.ds(P_Z, DI_B)], h0_ref, params,
                  y_ref.at[rows, :], None if hs_ref is None else hs_ref.at[b], scratch,
                  seq=seq, static_loops=True, side=side)
    flush()


def _proj_ssd_call(x, ada, norm_g, w16, w_dt, ssd_w, li, *, ctx, carry=None, h0t=None):
    nb, seq = (CTX_BATCHES_PER_STEP, SEQ) if ctx else (1, DEC_SEQ)
    tm = nb * seq
    t = x.shape[0]
    n_carry = 0 if carry is None else len(carry)
    const = lambda shape: pl.BlockSpec(shape, lambda i: (0,) * len(shape))
    once = pl.Buffered(1)
    big = {} if ctx else dict(pipeline_mode=pl.Buffered(1))
    in_specs = [pl.BlockSpec((tm, D_MODEL), lambda i: (i, 0), **big),
                pl.BlockSpec((None, 8, 3 * D_MODEL), lambda i: (li, 0, 0)),
                pl.BlockSpec((None, 1, D_MODEL), lambda i: (li, 0, 0)),
                pl.BlockSpec((None, _SRC["merge"], D_MODEL), lambda i: (li, 0, 0), pipeline_mode=once),
                pl.BlockSpec((None, LANES, D_MODEL), lambda i: (li, 0, 0))]
    args = [x, ada, norm_g, w16, w_dt]
    if not ctx:
        in_specs.append(pl.BlockSpec((None, None, 2, (H_B // 2) * N_B, LANES), lambda i: (i, li, 0, 0, 0)))
        args.append(h0t)
    in_specs += _ssd_param_specs(const)
    args += list(ssd_w)
    out_specs = [pl.BlockSpec((tm, PROJ_COLS), lambda i: (i, 0), **big)]
    out_shape = [jax.ShapeDtypeStruct((t, PROJ_COLS), F32)]
    aliases = {}
    if ctx:
        out_specs += [pl.BlockSpec((nb, None, SEQ, H_A, 2 * DH_A), lambda i: (i, li, 0, 0, 0))] * 2
        out_specs += [pl.BlockSpec((nb, None, H_C, DH_C, SEQ), lambda i: (i, li, 0, 0, 0))] * 2
        out_shape += [jax.ShapeDtypeStruct((BATCH, DEPTH, SEQ, H_A, 2 * DH_A), F32)] * 2
        out_shape += [jax.ShapeDtypeStruct((BATCH, DEPTH, H_C, DH_C, SEQ), F32)] * 2
    else:
        out_specs.append(pl.BlockSpec((tm, KVC_COLS), lambda i: (i, 0), **big))
        out_shape.append(jax.ShapeDtypeStruct((t, KVC_COLS), F32))
    out_specs.append(pl.BlockSpec((tm, DI_B), lambda i: (i, 0)))
    out_shape.append(jax.ShapeDtypeStruct((t, DI_B), BF16))
    if ctx:
        out_specs.append(pl.BlockSpec((nb, None, 2, H_B, P_B, N_B), lambda i: (i, li, 0, 0, 0, 0)))
        out_shape.append(jax.ShapeDtypeStruct((BATCH, DEPTH, 2, H_B, P_B, N_B), F32))
        if carry is not None:
            in_specs += [pl.BlockSpec(memory_space=pl.ANY)] * n_carry
            aliases = {len(args) + k: (1, 2, 3, 4, 6)[k] for k in range(n_carry)}
            args += list(carry)
    scratch = [pltpu.VMEM((tm, D_MODEL), BF16), pltpu.VMEM((tm, P_COLS), F32)] + _ssd_scratch(seq)
    return pl.pallas_call(
        functools.partial(_proj_ssd_kernel, nb=nb, seq=seq, ctx=ctx, n_carry=n_carry),
        grid=(t // tm,),
        in_specs=in_specs,
        out_specs=out_specs,
        out_shape=out_shape,
        input_output_aliases=aliases,
        scratch_shapes=scratch,
        compiler_params=_params("arbitrary"),
        name="ctx_proj_ssd" if ctx else "lat_proj_ssd",
    )(*args)


def _post_kernel(x_ref, ya_ref, yb_ref, yc_ref, ada_ref, g_ref, wm_ref, wa_ref, wb_ref, wc_ref, wo_ref, fg_ref,
                 o_ref, *, tm, row_base, tokens_per_row, final):
    row = row_base + (pl.program_id(0) * tm) // tokens_per_row
    gate = ada_ref[pl.ds(row, 1), 2 * D_MODEL:3 * D_MODEL]
    d = D_MODEL
    x = x_ref[...]
    h = _modulated_norm(x, g_ref, ada_ref, row)
    merged = None
    for n, (y_ref, w_ref) in enumerate(((ya_ref, wa_ref), (yb_ref, wb_ref), (yc_ref, wc_ref))):
        logits = _dot_nt(h, wm_ref[n * d:(n + 1) * d, :])
        term = _sigmoid(logits) * _dot(y_ref[...], w_ref[...])
        merged = term if merged is None else merged + term
    x = x + gate * _dot(merged.astype(BF16), wo_ref[...])
    if final:
        x = x * lax.rsqrt(jnp.mean(x * x, axis=-1, keepdims=True) + EPS) * fg_ref[...]
    o_ref[...] = x


def _post_call(x, ya, yb, yc, ada, norm_g, w_merge_t, li, wa, wb, wc, wo, final_g, *, tm, row_base,
               tokens_per_row, final):
    t = x.shape[0]
    tok = lambda w: pl.BlockSpec((tm, w), lambda i: (i, 0))
    layer = lambda *shape, **kw: pl.BlockSpec((None,) + shape, lambda i: (li,) + (0,) * len(shape), **kw)
    once = dict(pipeline_mode=pl.Buffered(1))
    kern = functools.partial(_post_kernel, tm=tm, row_base=row_base, tokens_per_row=tokens_per_row, final=final)
    return pl.pallas_call(
        kern,
        grid=(t // tm,),
        in_specs=[tok(D_MODEL), tok(W_A), tok(DI_B), tok(W_C),
                  layer(8, 3 * D_MODEL), layer(1, D_MODEL), layer(w_merge_t.shape[1], D_MODEL, **once),
                  layer(W_A, D_MODEL, **once), layer(DI_B, D_MODEL, **once), layer(W_C, D_MODEL, **once),
                  layer(D_MODEL, D_MODEL, **once), pl.BlockSpec((1, D_MODEL), lambda i: (0, 0))],
        out_specs=tok(D_MODEL),
        out_shape=jax.ShapeDtypeStruct((t, D_MODEL), F32),
        compiler_params=_params("arbitrary"),
        name="post_final" if final else "post",
    )(x, ya, yb, yc, ada, norm_g, w_merge_t, wa, wb, wc, wo, final_g)


def _rope_tables():
    pos = np.arange(DEC_SEQ)
    lane = np.arange(LANES)
    l64 = lane % (2 * (DH_A // 2))
    quarter = DH_A // 4
    p = np.where((l64 < DH_A // 2)[None, :], (pos // GRID_W)[:, None], (pos % GRID_W)[:, None])
    inv = ROPE_BASE ** (-np.arange(quarter, dtype=np.float64) / quarter)
    ang = p.astype(np.float64) * inv[l64 % quarter][None, :]
    sign = np.where((lane % (2 * quarter)) < quarter, -1.0, 1.0)
    return jnp.asarray(np.cos(ang), F32), jnp.asarray(np.sin(ang) * sign[None, :], F32)


def _pad_lanes(v, width=LANES):
    v = v.reshape(1, -1).astype(F32)
    return jnp.pad(v, ((0, 0), (0, width - v.shape[1])))


def kernel(x_prompt, x_sample, cache_diff_k, cache_diff_v, cache_na_k, cache_na_v, state_ssd, c, c_ctx,
           norm_g, w_ada, b_ada, w_in, lam_q1, lam_k1, lam_q2, lam_k2, diff_subln_g, conv_w, conv_b,
           dt_bias, a_log, d_skip, ssd_norm_g, na_rpb, w_br_a, w_br_b, w_br_c, w_out, final_g):
    assert x_prompt.shape == (BATCH, SEQ, D_MODEL) and x_sample.shape == (DEC_BATCH, DEC_SEQ, D_MODEL)
    assert w_in.shape == (DEPTH, D_MODEL, _SRC["merge"] + MERGE_COLS)
    w_t = jnp.swapaxes(w_in, 1, 2)
    w16 = w_t[:, :_SRC["merge"], :].astype(BF16)
    w_merge = w_t[:, _SRC["merge"]:, :].astype(BF16)
    w_dt = jnp.pad(w16[:, _SRC["dt"]:_SRC["qc"], :], ((0, 0), (0, LANES - 2 * H_B), (0, 0)))
    wa16, wb16, wc16, wo16 = (w.astype(BF16) for w in (w_br_a, w_br_b, w_br_c, w_out))

    cvecs = jnp.concatenate([c_ctx[None, :], c, jnp.zeros((8 - 1 - DEC_BATCH, D_MODEL), F32)], axis=0)
    ada = _ada_call(cvecs.T, w_ada, b_ada)
    cos_t, sin_t = _rope_tables()

    ck_a = cache_diff_k.reshape(DEC_BATCH, DEPTH, PAST_LEN, W_A)
    cv_a = cache_diff_v.reshape(DEC_BATCH, DEPTH, PAST_LEN, W_A)
    ck_c = cache_na_k.transpose(0, 1, 3, 4, 2)
    cv_c = cache_na_v.transpose(0, 1, 3, 4, 2)
    na_tiles = _rpb_call(na_rpb)
    h0t = state_ssd.transpose(0, 1, 2, 5, 3, 4).reshape(DEC_BATCH, DEPTH, 2, N_B, DI_B)
    h0t = h0t.reshape(DEC_BATCH, DEPTH, 2, N_B, H_B // 2, LANES).transpose(0, 1, 2, 4, 3, 5)
    h0t = h0t.reshape(DEC_BATCH, DEPTH, 2, (H_B // 2) * N_B, LANES)

    xp = x_prompt.reshape(BATCH * SEQ, D_MODEL)
    xs = x_sample.reshape(DEC_BATCH * DEC_SEQ, D_MODEL)
    fg = final_g.reshape(1, D_MODEL)
    norm_g3 = norm_g.reshape(DEPTH, 1, D_MODEL)
    carry = None
    for li in range(DEPTH):
        lam_init = 0.8 - 0.6 * math.exp(-0.3 * li)
        final = li == DEPTH - 1
        lamvec = jnp.concatenate([_pad_lanes(v[li]) for v in (lam_q1, lam_k1, lam_q2, lam_k2)], axis=0)
        subln = diff_subln_g[li].reshape(1, LANES)
        dtb = _pad_lanes(dt_bias[li])
        alog = _pad_lanes(a_log[li])
        dskx = jnp.repeat(d_skip[li], P_B, axis=-1)
        ssd_w = (conv_w[li], conv_b[li].reshape(1, CONV_DIM), dtb, alog, dskx, ssd_norm_g[li].reshape(1, DI_B))
        post_w = (wa16, wb16, wc16, wo16, fg)

        proj, ka, va, kc_t, vc_t, yb, ssd_state = _proj_ssd_call(xp, ada, norm_g3, w16, w_dt, ssd_w, li,
                                                                 ctx=True, carry=carry)
        carry = (ka, va, kc_t, vc_t, ssd_state)
        ya = _attn_a_ctx_call(proj, lamvec, subln, lam_init)
        xp = _post_ctx_call(xp, proj, kc_t, vc_t, ya, yb, ada, norm_g3, w_merge, li, *post_w, final=final)

        proj, kv, yb = _proj_ssd_call(xs, ada, norm_g3, w16, w_dt, ssd_w, li, ctx=False, h0t=h0t)
        ya = _attn_a_lat_call(proj, ck_a, cv_a, li, cos_t, sin_t, lamvec, subln, lam_init)
        yc = _attn_c_lat_call(proj, kv, ck_c, cv_c, li, na_tiles)
        xs = _post_call(xs, ya, yb, yc, ada, norm_g3, w_merge, li, *post_w, tm=512, row_base=1,
                        tokens_per_row=DEC_SEQ, final=final)

    new_k_a, new_v_a, new_k_c_t, new_v_c_t, new_ssd = carry
    to_token_major = lambda a: a.transpose(0, 1, 4, 2, 3)
    return (xp.reshape(BATCH, SEQ, D_MODEL), xs.reshape(DEC_BATCH, DEC_SEQ, D_MODEL),
            new_k_a, new_v_a, to_token_major(new_k_c_t), to_token_major(new_v_c_t), new_ssd)
```

```python
import functools
import math

import jax
import jax.numpy as jnp
import numpy as np
from jax import lax
from jax.experimental import pallas as pl
from jax.experimental.pallas import tpu as pltpu

D_MODEL = 1024
BATCH = 32
SEQ = 256
DEPTH = 2
DEC_BATCH = 2
DEC_SEQ = 1024
PAST_LEN = 512
GRID_W = 64
GRID_ROWS = DEC_SEQ // GRID_W
H_A = 4
DH_A = 64
W_A = H_A * 2 * DH_A
H_B = 8
P_B = 64
G_B = 2
N_B = 64
DI_B = H_B * P_B
CONV_K = 5
CONV_DIM = DI_B + 2 * G_B * N_B
SSD_CHUNK = 128
H_C = 8
DH_C = 64
W_C = H_C * DH_C
NA_KH = 8
NA_KW = 16
N_BRANCH = 3
ROPE_BASE = 10000.0
EPS = 1e-6

LANES = 128
HALF = LANES // 2
VMEM_LIMIT = 56 * 1024 * 1024

BC_DIM = CONV_DIM - DI_B
KVC_COLS = 2 * W_C
MERGE_COLS = N_BRANCH * D_MODEL
_SRC = dict(qa=0, ka=512, va=1024, ga=1536, z=2048, xs=2560, bc=3072, dt=3328, qc=3344, kc=3856, vc=4368,
            gc=4880, merge=5392)
COL_QA = 0
COL_GA = 512
COL_QC = 1024
COL_GC = 1536
COL_KA = 2048
COL_VA = 2560
PROJ_COLS = 3072
PROJ_SEGMENTS = ((COL_QA, _SRC["qa"]), (COL_GA, _SRC["ga"]), (COL_QC, _SRC["qc"]), (COL_GC, _SRC["gc"]),
                 (COL_KA, _SRC["ka"]), (COL_VA, _SRC["va"]))
P_Z = 0
P_XS = 512
P_BC = 1024
P_DT = 1280
P_COLS = P_DT + LANES
SIDE_TN = 256

NA_QROWS = 4
NA_WROWS = 12
NA_TILES = 2 * NA_KH
NEG_INF = float("-inf")
LOG2E = math.log2(math.e)
ATTN_AHEAD = 2
SSD_AHEAD = 2
CTX_BATCHES_PER_STEP = 2
ATTN_BATCHES_PER_STEP = 4
HI = lax.Precision.HIGHEST
F32 = jnp.float32
BF16 = jnp.bfloat16


def _dot(a, b, precision=None):
    return jnp.dot(a, b, preferred_element_type=F32, precision=precision)


def _dot_nt(a, b):
    return lax.dot_general(a, b, (((1,), (1,)), ((), ())), preferred_element_type=F32)


def _sigmoid(x):
    return 1.0 / (1.0 + jnp.exp(-x))


def _silu(x):
    return x * _sigmoid(x)


def _lane(shape):
    return lax.broadcasted_iota(jnp.int32, shape, len(shape) - 1)


def _params(*sem):
    return pltpu.CompilerParams(dimension_semantics=sem, vmem_limit_bytes=VMEM_LIMIT)


def _ada_kernel(cvt_ref, w_ref, b_ref, o_ref):
    n_rows = 1 + DEC_BATCH
    s = _silu(cvt_ref[...])
    accs = [jnp.zeros((8, w_ref.shape[1]), F32)] * n_rows
    for k0 in range(0, D_MODEL, 8):
        w = w_ref[k0:k0 + 8, :]
        accs = [acc + w * s[k0:k0 + 8, r:r + 1] for r, acc in enumerate(accs)]
    rows = [jnp.sum(acc, axis=0, keepdims=True) for acc in accs]
    rows.append(jnp.zeros((8 - n_rows, w_ref.shape[1]), F32))
    o_ref[...] = jnp.concatenate(rows, axis=0) + b_ref[...]


def _ada_call(cvecs, w_ada, b_ada):
    tn = 1536
    return pl.pallas_call(
        _ada_kernel,
        grid=(DEPTH, 3 * D_MODEL // tn),
        in_specs=[
            pl.BlockSpec((D_MODEL, 8), lambda l, j: (0, 0)),
            pl.BlockSpec((None, D_MODEL, tn), lambda l, j: (l, 0, j)),
            pl.BlockSpec((None, 1, tn), lambda l, j: (l, 0, j)),
        ],
        out_specs=pl.BlockSpec((None, 8, tn), lambda l, j: (l, 0, j)),
        out_shape=jax.ShapeDtypeStruct((DEPTH, 8, 3 * D_MODEL), F32),
        compiler_params=_params("arbitrary", "arbitrary"),
        name="ada",
    )(cvecs, w_ada, b_ada.reshape(DEPTH, 1, 3 * D_MODEL))


def _modulated_norm(x, g_ref, ada_ref, row):
    y = x * lax.rsqrt(jnp.mean(x * x, axis=-1, keepdims=True) + EPS) * g_ref[...]
    shift = ada_ref[pl.ds(row, 1), 0:D_MODEL]
    scale = ada_ref[pl.ds(row, 1), D_MODEL:2 * D_MODEL]
    return (y * (1.0 + scale) + shift).astype(BF16)


def _diff_lambda_in_kernel(lam_ref, lam_init):
    v = lam_ref[...]
    l1 = jnp.sum(v[0:1] * v[1:2], axis=-1, keepdims=True)
    l2 = jnp.sum(v[2:3] * v[3:4], axis=-1, keepdims=True)
    return jnp.exp(l1) - jnp.exp(l2) + lam_init


def _split_halves(x, scale):
    lo = _lane(x.shape) < HALF
    xs = x * (scale * LOG2E)
    return jnp.concatenate([jnp.where(lo, xs, 0.0), jnp.where(lo, 0.0, xs)], axis=0).astype(BF16)


def _diff_combine(o2, rsum, lam, t):
    return o2[:t] * rsum[:t] - (lam * rsum[t:]) * o2[t:]


def _diff_head_post(o, subln_g, lam_init, gate):
    o = o * lax.rsqrt(jnp.mean(o * o, axis=-1, keepdims=True) + EPS) * (subln_g * (1.0 - lam_init))
    return (o * _silu(gate)).astype(BF16)


def _attn_a_ctx_kernel(q_ref, k_ref, v_ref, g_ref, lam_ref, sg_ref, o_ref, *, lam_init):
    t = SEQ
    lam = _diff_lambda_in_kernel(lam_ref, lam_init)
    ones = jnp.ones((t, LANES), BF16)
    blocks = [(b, h) for b in range(q_ref.shape[0] // t) for h in range(H_A)]
    where = lambda b, h: (slice(b * t, (b + 1) * t), slice(h * LANES, (h + 1) * LANES))

    def scores(b, h):
        qq = _split_halves(q_ref[where(b, h)], DH_A ** -0.5)
        return _dot_nt(qq, k_ref[where(b, h)].astype(BF16))

    pending = [scores(*blk) for blk in blocks[:ATTN_AHEAD]]
    for n, blk in enumerate(blocks):
        s = pending.pop(0)
        if n + ATTN_AHEAD < len(blocks):
            pending.append(scores(*blocks[n + ATTN_AHEAD]))
        e = jnp.exp2(s - jnp.max(s, axis=-1, keepdims=True)).astype(BF16)
        rsum = 1.0 / _dot(e, ones)
        o = _diff_combine(_dot(e, v_ref[where(*blk)].astype(BF16)), rsum, lam, t)
        o_ref[where(*blk)] = _diff_head_post(o, sg_ref[...], lam_init, g_ref[where(*blk)])


def _attn_a_ctx_call(proj, lamvec, subln_g, lam_init):
    rows = ATTN_BATCHES_PER_STEP * SEQ
    blk = lambda c: pl.BlockSpec((rows, W_A), lambda b: (b, c // W_A))
    return pl.pallas_call(
        functools.partial(_attn_a_ctx_kernel, lam_init=lam_init),
        grid=(BATCH // ATTN_BATCHES_PER_STEP,),
        in_specs=[blk(COL_QA), blk(COL_KA), blk(COL_VA), blk(COL_GA),
                  pl.BlockSpec((4, LANES), lambda b: (0, 0)),
                  pl.BlockSpec((1, LANES), lambda b: (0, 0))],
        out_specs=pl.BlockSpec((rows, W_A), lambda b: (b, 0)),
        out_shape=jax.ShapeDtypeStruct((BATCH * SEQ, W_A), BF16),
        compiler_params=_params("arbitrary"),
        name="attn_a_ctx",
    )(proj, proj, proj, proj, lamvec, subln_g)


def _rope(x, cos, sin_signed):
    first = (_lane(x.shape) % 32) < 16
    swapped = jnp.where(first, pltpu.roll(x, LANES - 16, 1), pltpu.roll(x, 16, 1))
    return x * cos + swapped * sin_signed


def _attn_a_lat_kernel(q_ref, k_ref, v_ref, g_ref, ck_ref, cv_ref, cosq_ref, sinq_ref, cosk_ref, sink_ref,
                       lam_ref, sg_ref, o_ref, kr_s, *, lam_init):
    tq = q_ref.shape[0]

    @pl.when(pl.program_id(1) == 0)
    def _():
        for h in range(H_A):
            sl = slice(h * LANES, (h + 1) * LANES)
            kr_s[:, sl] = _rope(k_ref[:, sl], cosk_ref[...], sink_ref[...]).astype(BF16)

    lam = _diff_lambda_in_kernel(lam_ref, lam_init)

    def scores(h):
        sl = slice(h * LANES, (h + 1) * LANES)
        qq = _split_halves(_rope(q_ref[:, sl], cosq_ref[...], sinq_ref[...]), DH_A ** -0.5)
        return _dot_nt(qq, kr_s[:, sl]), _dot_nt(qq, ck_ref[:, sl].astype(BF16))

    pending = [scores(h) for h in range(ATTN_AHEAD)]
    for h in range(H_A):
        sl = slice(h * LANES, (h + 1) * LANES)
        s_lat, s_ctx = pending.pop(0)
        if h + ATTN_AHEAD < H_A:
            pending.append(scores(h + ATTN_AHEAD))
        m = jnp.maximum(jnp.max(s_lat, axis=-1, keepdims=True), jnp.max(s_ctx, axis=-1, keepdims=True))
        e_lat = jnp.exp2(s_lat - m)
        e_ctx = jnp.exp2(s_ctx - m)
        rsum = 1.0 / (jnp.sum(e_lat, axis=-1, keepdims=True) + jnp.sum(e_ctx, axis=-1, keepdims=True))
        o2 = _dot(e_lat.astype(BF16), v_ref[:, sl].astype(BF16)) + _dot(e_ctx.astype(BF16),
                                                                         cv_ref[:, sl].astype(BF16))
        o = _diff_combine(o2, rsum, lam, tq)
        o_ref[:, sl] = _diff_head_post(o, sg_ref[...], lam_init, g_ref[:, sl])


def _attn_a_lat_call(proj, cache_k, cache_v, li, cos_t, sin_t, lamvec, subln_g, lam_init):
    tq = 256
    nq = DEC_SEQ // tq
    qblk = lambda c: pl.BlockSpec((tq, W_A), lambda b, i: (b * nq + i, c // W_A))
    full = lambda c: pl.BlockSpec((DEC_SEQ, W_A), lambda b, i: (b, c // W_A))
    cache = pl.BlockSpec((None, None, PAST_LEN, W_A), lambda b, i: (b, li, 0, 0))
    return pl.pallas_call(
        functools.partial(_attn_a_lat_kernel, lam_init=lam_init),
        grid=(DEC_BATCH, nq),
        in_specs=[qblk(COL_QA), full(COL_KA), full(COL_VA), qblk(COL_GA), cache, cache,
                  pl.BlockSpec((tq, LANES), lambda b, i: (i, 0)),
                  pl.BlockSpec((tq, LANES), lambda b, i: (i, 0)),
                  pl.BlockSpec((DEC_SEQ, LANES), lambda b, i: (0, 0)),
                  pl.BlockSpec((DEC_SEQ, LANES), lambda b, i: (0, 0)),
                  pl.BlockSpec((4, LANES), lambda b, i: (0, 0)),
                  pl.BlockSpec((1, LANES), lambda b, i: (0, 0))],
        out_specs=pl.BlockSpec((tq, W_A), lambda b, i: (b * nq + i, 0)),
        out_shape=jax.ShapeDtypeStruct((DEC_BATCH * DEC_SEQ, W_A), BF16),
        scratch_shapes=[pltpu.VMEM((DEC_SEQ, W_A), BF16)],
        compiler_params=_params("arbitrary", "arbitrary"),
        name="attn_a_lat",
    )(proj, proj, proj, proj, cache_k, cache_v, cos_t, sin_t, cos_t, sin_t, lamvec, subln_g)


def _merge_halves(o, t):
    return jnp.where(_lane((t, LANES)) < HALF, o[:t], o[t:])


def _spread(work, n_slots):
    state = dict(slot=0, done=0)

    def emit_until(target):
        while state["done"] < target:
            work[state["done"]]()
            state["done"] += 1

    def side():
        state["slot"] += 1
        emit_until(min(len(work), -(-state["slot"] * len(work) // n_slots)))

    return side, lambda: emit_until(len(work))


def _attn_c_ctx_body(q_ref, kt_ref, vt_ref, g_ref, o_ref, side):
    t = SEQ
    blocks = [(b, j) for b in range(q_ref.shape[0] // t) for j in range(H_C // 2)]
    where = lambda b, j: (slice(b * t, (b + 1) * t), slice(j * LANES, (j + 1) * LANES))
    pair_t = lambda ref, b, j: ref[b, 2 * j:2 * j + 2].reshape(LANES, t).astype(BF16)

    def scores(b, j):
        return _dot(_split_halves(q_ref[where(b, j)], DH_C ** -0.5), pair_t(kt_ref, b, j))

    pending = [scores(*blk) for blk in blocks[:ATTN_AHEAD]]
    for n, blk in enumerate(blocks):
        s = pending.pop(0)
        if n + ATTN_AHEAD < len(blocks):
            pending.append(scores(*blocks[n + ATTN_AHEAD]))
        e = jnp.exp2(s - jnp.max(s, axis=-1, keepdims=True))
        rsum = 1.0 / jnp.sum(e, axis=-1, keepdims=True)
        o = _merge_halves(_dot_nt(e.astype(BF16), pair_t(vt_ref, *blk)) * rsum, t)
        o_ref[where(*blk)] = (o * _silu(g_ref[where(*blk)])).astype(BF16)
        side()


def _post_ctx_kernel(q_ref, kt_ref, vt_ref, gc_ref, x_ref, ya_ref, yb_ref, ada_ref, g_ref, wm_ref, wa_ref, wb_ref,
                     wc_ref, wo_ref, fg_ref, o_ref, yc_s, h_s, acc_s, sc_s, *, final):
    d = D_MODEL
    gate = ada_ref[0:1, 2 * d:3 * d]
    h_s[...] = _modulated_norm(x_ref[...], g_ref, ada_ref, 0)

    def gate_logits(n, c0):
        r0 = _SRC["merge"] + n * d + c0
        return _dot_nt(h_s[...], wm_ref[r0:r0 + SIDE_TN, :])

    work = []
    for c0 in range(0, d, SIDE_TN):
        cols = slice(c0, c0 + SIDE_TN)

        def branch_a(c0=c0, cols=cols):
            acc_s[:, cols] = _sigmoid(gate_logits(0, c0)) * _dot(ya_ref[...], wa_ref[:, cols])

        def branch_b(c0=c0, cols=cols):
            acc_s[:, cols] += _sigmoid(gate_logits(1, c0)) * _dot(yb_ref[...], wb_ref[:, cols])

        def gate_c(c0=c0, cols=cols):
            sc_s[:, cols] = _sigmoid(gate_logits(2, c0))

        work += [branch_a, branch_b, gate_c]
    side, flush = _spread(work, (q_ref.shape[0] // SEQ) * (H_C // 2))
    _attn_c_ctx_body(q_ref, kt_ref, vt_ref, gc_ref, yc_s, side)
    flush()
    merged = acc_s[...] + sc_s[...] * _dot(yc_s[...], wc_ref[...])
    x = x_ref[...] + gate * _dot(merged.astype(BF16), wo_ref[...])
    if final:
        x = x * lax.rsqrt(jnp.mean(x * x, axis=-1, keepdims=True) + EPS) * fg_ref[...]
    o_ref[...] = x


def _post_ctx_call(x, proj, kc_t, vc_t, ya, yb, ada, norm_g, w_merge, li, wa, wb, wc, wo, final_g, *, final):
    nb = CTX_BATCHES_PER_STEP
    tm = nb * SEQ
    tok = lambda w: pl.BlockSpec((tm, w), lambda i: (i, 0))
    col = lambda c: pl.BlockSpec((tm, W_C), lambda i: (i, c // W_C))
    cache = pl.BlockSpec((nb, None, H_C, DH_C, SEQ), lambda i: (i, li, 0, 0, 0))
    layer = lambda *shape, **kw: pl.BlockSpec((None,) + shape, lambda i: (li,) + (0,) * len(shape), **kw)
    once = dict(pipeline_mode=pl.Buffered(1))
    return pl.pallas_call(
        functools.partial(_post_ctx_kernel, final=final),
        grid=(BATCH // nb,),
        in_specs=[col(COL_QC), cache, cache, col(COL_GC), tok(D_MODEL), tok(W_A), tok(DI_B),
                  layer(8, 3 * D_MODEL), layer(1, D_MODEL), layer(w_merge.shape[1], D_MODEL, **once),
                  layer(W_A, D_MODEL, **once), layer(DI_B, D_MODEL, **once), layer(W_C, D_MODEL, **once),
                  layer(D_MODEL, D_MODEL, **once), pl.BlockSpec((1, D_MODEL), lambda i: (0, 0))],
        out_specs=tok(D_MODEL),
        out_shape=jax.ShapeDtypeStruct((BATCH * SEQ, D_MODEL), F32),
        scratch_shapes=[pltpu.VMEM((tm, W_C), BF16), pltpu.VMEM((tm, D_MODEL), BF16),
                        pltpu.VMEM((tm, D_MODEL), F32), pltpu.VMEM((tm, D_MODEL), F32)],
        compiler_params=_params("arbitrary"),
        name="attn_c_post_ctx_final" if final else "attn_c_post_ctx",
    )(proj, kc_t, vc_t, proj, x, ya, yb, ada, norm_g, w_merge, wa, wb, wc, wo, final_g)


def _rpb_kernel(rpb_ref, o_ref):
    shape = (GRID_W, LANES)
    c = lax.broadcasted_iota(jnp.int32, shape, 0)
    cp = _lane(shape) % GRID_W
    start = jnp.clip(c - NA_KW // 2, 0, GRID_W - NA_KW)
    in_win = (cp >= start) & (cp < start + NA_KW)
    for h in range(H_C):
        o_ref[h, 0] = jnp.full(shape, NEG_INF, F32)
        for dr in range(2 * NA_KH - 1):
            row = jnp.broadcast_to(rpb_ref[h, dr:dr + 1, :], shape)
            tile = pltpu.roll(row, LANES - (NA_KW - 1), 1, stride=1, stride_axis=0)
            o_ref[h, 1 + dr] = jnp.where(in_win, tile * LOG2E, NEG_INF)


def _rpb_call(rpb):
    n_dc = 2 * NA_KW - 1
    v = jnp.pad(rpb, ((0, 0), (0, 0), (0, NA_TILES - (2 * NA_KH - 1)), (0, GRID_W - n_dc)))
    v = jnp.concatenate([v] * (LANES // GRID_W), axis=-1)
    return pl.pallas_call(
        _rpb_kernel,
        grid=(DEPTH,),
        in_specs=[pl.BlockSpec((None, H_C, NA_TILES, LANES), lambda l: (l, 0, 0, 0))],
        out_specs=pl.BlockSpec((None, H_C, NA_TILES, GRID_W, LANES), lambda l: (l, 0, 0, 0, 0)),
        out_shape=jax.ShapeDtypeStruct((DEPTH, H_C, NA_TILES, GRID_W, LANES), F32),
        compiler_params=_params("arbitrary"),
        name="rpb_tiles",
    )(v)


def _attn_c_lat_kernel(q_ref, k_ref, v_ref, g_ref, ck_ref, cv_ref, tile_ref, o_ref, bias_s):
    tq = q_ref.shape[0]
    nwin = NA_WROWS * GRID_W
    m = pl.program_id(1)
    w0 = jnp.where(m < (GRID_ROWS // NA_QROWS) // 2, 0, GRID_ROWS - NA_WROWS)
    k0 = pl.multiple_of(w0 * GRID_W, GRID_W)
    lo = _lane((GRID_W, LANES)) < HALF
    n_pair = H_C // 2

    def scores(j):
        sl = slice(j * LANES, (j + 1) * LANES)
        for s in range(2):
            for i in range(NA_QROWS):
                r = m * NA_QROWS + i
                start = jnp.clip(r - NA_KH // 2, 0, GRID_ROWS - NA_KH)
                for jp in range(NA_WROWS // 2):
                    idx = []
                    for u in range(2):
                        rk = w0 + 2 * jp + u
                        valid = (rk >= start) & (rk < start + NA_KH)
                        idx.append(jnp.where(valid, rk - r + NA_KH, 0))
                    tile = jnp.where(lo, tile_ref[2 * j + s, idx[0]], tile_ref[2 * j + s, idx[1]])
                    bias_s[(s * NA_QROWS + i) * GRID_W:(s * NA_QROWS + i + 1) * GRID_W,
                           jp * LANES:(jp + 1) * LANES] = tile
        qq = _split_halves(q_ref[:, sl], DH_C ** -0.5)
        kw = k_ref[pl.ds(k0, nwin), sl].astype(BF16)
        s_win = _dot_nt(qq, kw) + bias_s[...]
        ckt = ck_ref[2 * j:2 * j + 2].reshape(LANES, PAST_LEN).astype(BF16)
        return s_win, _dot(qq, ckt)

    pending = [scores(j) for j in range(ATTN_AHEAD)]
    for j in range(n_pair):
        sl = slice(j * LANES, (j + 1) * LANES)
        s_win, s_ctx = pending.pop(0)
        if j + ATTN_AHEAD < n_pair:
            pending.append(scores(j + ATTN_AHEAD))
        vw = v_ref[pl.ds(k0, nwin), sl].astype(BF16)
        mx = jnp.maximum(jnp.max(s_win, axis=-1, keepdims=True), jnp.max(s_ctx, axis=-1, keepdims=True))
        e_win = jnp.exp2(s_win - mx)
        e_ctx = jnp.exp2(s_ctx - mx)
        rs = 1.0 / (jnp.sum(e_win, axis=-1, keepdims=True) + jnp.sum(e_ctx, axis=-1, keepdims=True))
        cvt = cv_ref[2 * j:2 * j + 2].reshape(LANES, PAST_LEN).astype(BF16)
        o = (_dot(e_win.astype(BF16), vw) + _dot_nt(e_ctx.astype(BF16), cvt)) * rs
        o_ref[:, sl] = (_merge_halves(o, tq) * _silu(g_ref[:, sl])).astype(BF16)


def _attn_c_lat_call(proj, kv, cache_k, cache_v, li, tiles):
    tq = NA_QROWS * GRID_W
    nq = DEC_SEQ // tq
    qblk = lambda c: pl.BlockSpec((tq, W_C), lambda b, i: (b * nq + i, c // W_C))
    full = lambda c: pl.BlockSpec((DEC_SEQ, W_C), lambda b, i: (b, c // W_C))
    cache = pl.BlockSpec((None, None, H_C, DH_C, PAST_LEN), lambda b, i: (b, li, 0, 0, 0))
    return pl.pallas_call(
        _attn_c_lat_kernel,
        grid=(DEC_BATCH, nq),
        in_specs=[qblk(COL_QC), full(0), full(W_C), qblk(COL_GC), cache, cache,
                  pl.BlockSpec((None, H_C, NA_TILES, GRID_W, LANES), lambda b, i: (li, 0, 0, 0, 0))],
        out_specs=pl.BlockSpec((tq, W_C), lambda b, i: (b * nq + i, 0)),
        out_shape=jax.ShapeDtypeStruct((DEC_BATCH * DEC_SEQ, W_C), BF16),
        scratch_shapes=[pltpu.VMEM((2 * tq, NA_WROWS * GRID_W), F32)],
        compiler_params=_params("arbitrary", "arbitrary"),
        name="attn_c_lat",
    )(proj, kv, kv, proj, cache_k, cache_v, tiles)


def _ssd_body(dt_ref, xs_ref, bc_ref, z_ref, h0_ref, params, y_ref, hs_ref, scratch, *, seq, static_loops, side):
    cw_ref, cb_ref, dtb_ref, alog_ref, dsk_ref, g_ref = params
    upad_s, xc_s, expo_s, expot_s, dtt_s, tot_s, bmt_s, yf_s, yb_s, st_s = scratch
    use_h0 = h0_ref is not None
    want_state = hs_ref is not None

    q = SSD_CHUNK
    nc = seq // q
    n_pair = H_B // 2
    n_hd = 2 * H_B
    pad = 8

    def loop(body, unroll=1):
        if static_loops:
            for c in range(nc):
                body(c, 0)
        else:
            lax.fori_loop(0, nc, body, 0, unroll=unroll)

    def chunk_rows(c):
        return slice(c * q, (c + 1) * q) if isinstance(c, int) else pl.ds(pl.multiple_of(c * q, q), q)

    upad_s[0:pad, :] = jnp.zeros((pad, CONV_DIM), F32)
    upad_s[pad + seq:2 * pad + seq, :] = jnp.zeros((pad, CONV_DIM), F32)
    upad_s[pad:pad + seq, 0:DI_B] = xs_ref[...]
    upad_s[pad:pad + seq, DI_B:CONV_DIM] = bc_ref[...]

    for c in range(nc):
        for cb_ in range(CONV_DIM // LANES):
            csl = slice(cb_ * LANES, (cb_ + 1) * LANES)
            acc = jnp.zeros((q, LANES), F32) + cb_ref[:, csl]
            for k in range(CONV_K):
                r0 = c * q + pad - CONV_K // 2 + k
                acc = acc + upad_s[r0:r0 + q, csl] * cw_ref[k:k + 1, csl]
            xc_s[c * q:(c + 1) * q, csl] = _silu(acc)
            side()

    a_row = -jnp.exp(alog_ref[...]) * LOG2E
    a_col = jnp.broadcast_to(a_row, (LANES, LANES)).T[0:n_hd, 0:1]
    ri = lax.broadcasted_iota(jnp.int32, (q, q), 0)
    ci = lax.broadcasted_iota(jnp.int32, (q, q), 1)
    ltri = (ri >= ci).astype(F32)
    fwd_lane = _lane((q, LANES)) < H_B
    fwd_row = lax.broadcasted_iota(jnp.int32, (n_hd, q), 0) < H_B

    def prep_body(c, carry):
        rows = chunk_rows(c)
        xdt = dt_ref[rows, 0:LANES] + dtb_ref[...]
        dtv = jnp.maximum(xdt, 0.0) + jnp.log1p(jnp.exp(-jnp.abs(xdt)))
        la = dtv * a_row
        acum = _dot(ltri, la, HI)
        expo_s[rows, :] = jnp.where(fwd_lane, acum, la - acum)
        acum_t = acum.T[0:n_hd, :]
        dt_t = dtv.T[0:n_hd, :]
        expot_s[c] = jnp.where(fwd_row, acum_t, dt_t * a_col - acum_t)
        dtt_s[c] = dt_t
        tot_s[c] = jnp.broadcast_to(acum_t[:, q - 1:q], (n_hd, q))
        bmt_s[c] = xc_s[rows, DI_B:DI_B + LANES].T
        side()
        return carry

    loop(prep_body, unroll=2)

    if use_h0:
        st_s[...] = h0_ref[...].reshape(2, n_pair, N_B, LANES)
    else:
        st_s[...] = jnp.zeros_like(st_s)

    lane_q = _lane((q, LANES))
    lo = lane_q < HALF
    lo_st = _lane((N_B, LANES)) < HALF

    def chunk_pair(c_fwd, c_bwd):
        dirs = ((0, c_fwd, yf_s), (1, c_bwd, yb_s))
        group_of = lambda k: k * G_B // n_pair
        items = [(k, d) for k in range(n_pair) for d in dirs]
        cb, y_off, st_in = {}, {}, {}

        def issue_early(k, d):
            dirn, c, _ = d
            g, rows = group_of(k), chunk_rows(c)
            in_g = (lane_q >= g * N_B) & (lane_q < (g + 1) * N_B)
            cmg = jnp.where(in_g, xc_s[rows, DI_B + LANES:DI_B + 2 * LANES], 0.0).astype(BF16)
            if (dirn, g) not in cb:
                cb[dirn, g] = _dot_nt(cmg, xc_s[rows, DI_B:DI_B + LANES].astype(BF16))
            st_in[dirn, k] = st_s[dirn, k]
            y_off[dirn, k] = _dot(cmg, jnp.concatenate([st_in[dirn, k]] * 2, axis=0).astype(BF16))

        for item in items[:SSD_AHEAD]:
            issue_early(*item)
        for n, (k, (dirn, c, y_s)) in enumerate(items):
            if n + SSD_AHEAD < len(items):
                issue_early(*items[n + SSD_AHEAD])
            psl = slice(k * LANES, (k + 1) * LANES)
            rows = chunk_rows(c)
            tri = (ri >= ci) if dirn == 0 else (ci >= ri)
            bmt_g = bmt_s[c, group_of(k) * N_B:(group_of(k) + 1) * N_B, :]
            x16 = xc_s[rows, psl].astype(BF16)
            mats, lhs, ysc, cdec = [], [], [], []
            for s in range(2):
                col = dirn * H_B + 2 * k + s
                e_col = jnp.broadcast_to(expo_s[rows, col:col + 1], (q, q))
                e_row = expot_s[c, col:col + 1, :]
                dt_row = dtt_s[c, col:col + 1, :]
                tot = tot_s[c, col:col + 1, :]
                dec = jnp.exp2(jnp.where(tri, e_col - e_row, NEG_INF))
                mats.append((cb[dirn, group_of(k)] * dec * dt_row).astype(BF16))
                if dirn == 0:
                    ysc.append(jnp.exp2(e_col))
                    w_row = jnp.exp2(tot - e_row)
                else:
                    ysc.append(jnp.exp2(e_col + tot))
                    w_row = jnp.exp2(-e_row)
                lhs.append((bmt_g * (w_row * dt_row)).astype(BF16))
                cdec.append(jnp.exp2(tot[:, 0:LANES]))
            yd = _dot(jnp.concatenate(mats, axis=0), x16)
            ds = _dot(jnp.concatenate(lhs, axis=0), x16)
            yo, st = y_off.pop((dirn, k)), st_in.pop((dirn, k))
            y_s[rows, psl] = jnp.where(lo, yd[:q] + ysc[0] * yo, yd[q:] + ysc[1] * yo)
            st_s[dirn, k] = jnp.where(lo_st, cdec[0] * st + ds[:N_B], cdec[1] * st + ds[N_B:])
            side()

    def body(c, carry):
        chunk_pair(c, nc - 1 - c)
        return carry

    loop(body, unroll=2)

    dsum = dsk_ref[0:1, :] + dsk_ref[1:2, :]

    def out_body(c, carry):
        rows = chunk_rows(c)
        y = yf_s[rows, :] + yb_s[rows, :] + xc_s[rows, 0:DI_B] * dsum
        y = y * _silu(z_ref[rows, :])
        y = y * lax.rsqrt(jnp.mean(y * y, axis=-1, keepdims=True) + EPS) * g_ref[...]
        y_ref[rows, :] = y.astype(BF16)
        side()
        return carry

    loop(out_body)
    if want_state:
        for dirn in range(2):
            for k in range(n_pair):
                st = st_s[dirn, k]
                st_t = jnp.concatenate([st, st], axis=0).T
                for s in range(2):
                    hs_ref[dirn, 2 * k + s] = st_t[s * P_B:(s + 1) * P_B, 0:N_B]


def _ssd_scratch(seq):
    nc = seq // SSD_CHUNK
    per_chunk_rows = pltpu.VMEM((nc, 2 * H_B, SSD_CHUNK), F32)
    return [pltpu.VMEM((seq + 16, CONV_DIM), F32), pltpu.VMEM((seq, CONV_DIM), F32),
            pltpu.VMEM((seq, LANES), F32), per_chunk_rows, per_chunk_rows, per_chunk_rows,
            pltpu.VMEM((nc, LANES, SSD_CHUNK), F32),
            pltpu.VMEM((seq, DI_B), F32), pltpu.VMEM((seq, DI_B), F32),
            pltpu.VMEM((2, H_B // 2, N_B, LANES), F32)]


def _ssd_param_specs(const):
    return [const((CONV_K, CONV_DIM)), const((1, CONV_DIM)), const((1, LANES)), const((1, LANES)),
            const((2, DI_B)), const((1, DI_B))]


def _proj_ssd_kernel(*refs, nb, seq, ctx, n_carry):
    x_ref, ada_ref, g_ref, w_ref, wdt_ref = refs[:5]
    pos = 5
    h0_ref = None
    if not ctx:
        h0_ref = refs[pos]
        pos += 1
    params = refs[pos:pos + 6]
    outs = refs[pos + 6 + n_carry:]
    if ctx:
        proj_ref, ka_ref, va_ref, kc_ref, vc_ref, y_ref, hs_ref = outs[:7]
        outs = outs[7:]
    else:
        proj_ref, kv_ref, y_ref = outs[:3]
        hs_ref = None
        outs = outs[3:]
    h_s, p_s = outs[:2]
    scratch = outs[2:]

    row = 0 if ctx else 1 + pl.program_id(0)
    for r0 in range(0, nb * seq, SEQ):
        h_s[r0:r0 + SEQ, :] = _modulated_norm(x_ref[r0:r0 + SEQ, :], g_ref, ada_ref, row)
    for c0 in range(P_Z, P_DT, SIDE_TN):
        c1 = min(c0 + SIDE_TN, P_DT)
        p_s[:, c0:c1] = _dot_nt(h_s[...], w_ref[_SRC["z"] + c0:_SRC["z"] + c1, :])
    p_s[:, P_DT:P_COLS] = _dot_nt(h_s[...], wdt_ref[...])

    work = []
    for col, src in PROJ_SEGMENTS:
        for off in range(0, W_A, SIDE_TN):
            def tile(d=col + off, s=src + off):
                proj_ref[:, d:d + SIDE_TN] = _dot_nt(h_s[...], w_ref[s:s + SIDE_TN, :])
            work.append(tile)
            if ctx and col in (COL_KA, COL_VA):
                def store(dst=ka_ref if col == COL_KA else va_ref, col=col, off=off):
                    for b in range(nb):
                        for h in range(off // LANES, (off + SIDE_TN) // LANES):
                            dst[b, :, h, :] = proj_ref[b * seq:(b + 1) * seq, col + h * LANES:col + (h + 1) * LANES]
                work.append(store)
    for r0 in range(0, KVC_COLS, SIDE_TN):
        if ctx:
            for b in range(nb):
                def tile_t(b=b, r0=r0):
                    w_rows = w_ref[_SRC["kc"] + r0:_SRC["kc"] + r0 + SIDE_TN, :]
                    kv_t = _dot_nt(w_rows, h_s[b * seq:(b + 1) * seq, :])
                    dst, d0 = (kc_ref, r0) if r0 < W_C else (vc_ref, r0 - W_C)
                    dst[b, d0 // DH_C:(d0 + SIDE_TN) // DH_C] = kv_t.reshape(SIDE_TN // DH_C, DH_C, seq)
                work.append(tile_t)
        else:
            def tile_kv(r0=r0):
                kv_ref[:, r0:r0 + SIDE_TN] = _dot_nt(h_s[...], w_ref[_SRC["kc"] + r0:_SRC["kc"] + r0 + SIDE_TN, :])
            work.append(tile_kv)

    n_slots = nb * (seq // SSD_CHUNK) * (CONV_DIM // LANES + 2 + 2 * (H_B // 2))
    side, flush = _spread(work, n_slots)
    for b in range(nb):
        rows = pl.ds(b * seq, seq)
        _ssd_body(p_s.at[rows, pl.ds(P_DT, LANES)], p_s.at[rows, pl.ds(P_XS, DI_B)],
                  p_s.at[rows, pl.ds(P_BC, BC_DIM)], p_s.at[rows, pl.ds(P_Z, DI_B)], h0_ref, params,
                  y_ref.at[rows, :], None if hs_ref is None else hs_ref.at[b], scratch,
                  seq=seq, static_loops=True, side=side)
    flush()


def _proj_ssd_call(x, ada, norm_g, w16, w_dt, ssd_w, li, *, ctx, carry=None, h0t=None):
    nb, seq = (CTX_BATCHES_PER_STEP, SEQ) if ctx else (1, DEC_SEQ)
    tm = nb * seq
    t = x.shape[0]
    n_carry = 0 if carry is None else len(carry)
    const = lambda shape: pl.BlockSpec(shape, lambda i: (0,) * len(shape))
    once = pl.Buffered(1)
    big = {} if ctx else dict(pipeline_mode=pl.Buffered(1))
    in_specs = [pl.BlockSpec((tm, D_MODEL), lambda i: (i, 0), **big),
                pl.BlockSpec((None, 8, 3 * D_MODEL), lambda i: (li, 0, 0)),
                pl.BlockSpec((None, 1, D_MODEL), lambda i: (li, 0, 0)),
                pl.BlockSpec((None, _SRC["merge"], D_MODEL), lambda i: (li, 0, 0), pipeline_mode=once),
                pl.BlockSpec((None, LANES, D_MODEL), lambda i: (li, 0, 0))]
    args = [x, ada, norm_g, w16, w_dt]
    if not ctx:
        in_specs.append(pl.BlockSpec((None, None, 2, (H_B // 2) * N_B, LANES), lambda i: (i, li, 0, 0, 0)))
        args.append(h0t)
    in_specs += _ssd_param_specs(const)
    args += list(ssd_w)
    out_specs = [pl.BlockSpec((tm, PROJ_COLS), lambda i: (i, 0), **big)]
    out_shape = [jax.ShapeDtypeStruct((t, PROJ_COLS), F32)]
    aliases = {}
    if ctx:
        out_specs += [pl.BlockSpec((nb, None, SEQ, H_A, 2 * DH_A), lambda i: (i, li, 0, 0, 0))] * 2
        out_specs += [pl.BlockSpec((nb, None, H_C, DH_C, SEQ), lambda i: (i, li, 0, 0, 0))] * 2
        out_shape += [jax.ShapeDtypeStruct((BATCH, DEPTH, SEQ, H_A, 2 * DH_A), F32)] * 2
        out_shape += [jax.ShapeDtypeStruct((BATCH, DEPTH, H_C, DH_C, SEQ), F32)] * 2
    else:
        out_specs.append(pl.BlockSpec((tm, KVC_COLS), lambda i: (i, 0), **big))
        out_shape.append(jax.ShapeDtypeStruct((t, KVC_COLS), F32))
    out_specs.append(pl.BlockSpec((tm, DI_B), lambda i: (i, 0)))
    out_shape.append(jax.ShapeDtypeStruct((t, DI_B), BF16))
    if ctx:
        out_specs.append(pl.BlockSpec((nb, None, 2, H_B, P_B, N_B), lambda i: (i, li, 0, 0, 0, 0)))
        out_shape.append(jax.ShapeDtypeStruct((BATCH, DEPTH, 2, H_B, P_B, N_B), F32))
        if carry is not None:
            in_specs += [pl.BlockSpec(memory_space=pl.ANY)] * n_carry
            aliases = {len(args) + k: (1, 2, 3, 4, 6)[k] for k in range(n_carry)}
            args += list(carry)
    scratch = [pltpu.VMEM((tm, D_MODEL), BF16), pltpu.VMEM((tm, P_COLS), F32)] + _ssd_scratch(seq)
    return pl.pallas_call(
        functools.partial(_proj_ssd_kernel, nb=nb, seq=seq, ctx=ctx, n_carry=n_carry),
        grid=(t // tm,),
        in_specs=in_specs,
        out_specs=out_specs,
        out_shape=out_shape,
        input_output_aliases=aliases,
        scratch_shapes=scratch,
        compiler_params=_params("arbitrary"),
        name="ctx_proj_ssd" if ctx else "lat_proj_ssd",
    )(*args)


def _post_kernel(x_ref, ya_ref, yb_ref, yc_ref, ada_ref, g_ref, wm_ref, wa_ref, wb_ref, wc_ref, wo_ref, fg_ref,
                 o_ref, *, tm, row_base, tokens_per_row, final):
    row = row_base + (pl.program_id(0) * tm) // tokens_per_row
    gate = ada_ref[pl.ds(row, 1), 2 * D_MODEL:3 * D_MODEL]
    d = D_MODEL
    x = x_ref[...]
    h = _modulated_norm(x, g_ref, ada_ref, row)
    merged = None
    for n, (y_ref, w_ref) in enumerate(((ya_ref, wa_ref), (yb_ref, wb_ref), (yc_ref, wc_ref))):
        r0 = _SRC["merge"] + n * d
        logits = _dot_nt(h, wm_ref[r0:r0 + d, :])
        term = _sigmoid(logits) * _dot(y_ref[...], w_ref[...])
        merged = term if merged is None else merged + term
    x = x + gate * _dot(merged.astype(BF16), wo_ref[...])
    if final:
        x = x * lax.rsqrt(jnp.mean(x * x, axis=-1, keepdims=True) + EPS) * fg_ref[...]
    o_ref[...] = x


def _post_call(x, ya, yb, yc, ada, norm_g, w_merge_t, li, wa, wb, wc, wo, final_g, *, tm, row_base,
               tokens_per_row, final):
    t = x.shape[0]
    tok = lambda w: pl.BlockSpec((tm, w), lambda i: (i, 0))
    layer = lambda *shape, **kw: pl.BlockSpec((None,) + shape, lambda i: (li,) + (0,) * len(shape), **kw)
    once = dict(pipeline_mode=pl.Buffered(1))
    kern = functools.partial(_post_kernel, tm=tm, row_base=row_base, tokens_per_row=tokens_per_row, final=final)
    return pl.pallas_call(
        kern,
        grid=(t // tm,),
        in_specs=[tok(D_MODEL), tok(W_A), tok(DI_B), tok(W_C),
                  layer(8, 3 * D_MODEL), layer(1, D_MODEL), layer(w_merge_t.shape[1], D_MODEL, **once),
                  layer(W_A, D_MODEL, **once), layer(DI_B, D_MODEL, **once), layer(W_C, D_MODEL, **once),
                  layer(D_MODEL, D_MODEL, **once), pl.BlockSpec((1, D_MODEL), lambda i: (0, 0))],
        out_specs=tok(D_MODEL),
        out_shape=jax.ShapeDtypeStruct((t, D_MODEL), F32),
        compiler_params=_params("arbitrary"),
        name="post_final" if final else "post",
    )(x, ya, yb, yc, ada, norm_g, w_merge_t, wa, wb, wc, wo, final_g)


def _rope_tables():
    pos = np.arange(DEC_SEQ)
    lane = np.arange(LANES)
    l64 = lane % (2 * (DH_A // 2))
    quarter = DH_A // 4
    p = np.where((l64 < DH_A // 2)[None, :], (pos // GRID_W)[:, None], (pos % GRID_W)[:, None])
    inv = ROPE_BASE ** (-np.arange(quarter, dtype=np.float64) / quarter)
    ang = p.astype(np.float64) * inv[l64 % quarter][None, :]
    sign = np.where((lane % (2 * quarter)) < quarter, -1.0, 1.0)
    return jnp.asarray(np.cos(ang), F32), jnp.asarray(np.sin(ang) * sign[None, :], F32)


def _pad_lanes(v, width=LANES):
    v = v.reshape(1, -1).astype(F32)
    return jnp.pad(v, ((0, 0), (0, width - v.shape[1])))


def kernel(x_prompt, x_sample, cache_diff_k, cache_diff_v, cache_na_k, cache_na_v, state_ssd, c, c_ctx,
           norm_g, w_ada, b_ada, w_in, lam_q1, lam_k1, lam_q2, lam_k2, diff_subln_g, conv_w, conv_b,
           dt_bias, a_log, d_skip, ssd_norm_g, na_rpb, w_br_a, w_br_b, w_br_c, w_out, final_g):
    assert x_prompt.shape == (BATCH, SEQ, D_MODEL) and x_sample.shape == (DEC_BATCH, DEC_SEQ, D_MODEL)
    assert w_in.shape == (DEPTH, D_MODEL, _SRC["merge"] + MERGE_COLS)
    w16 = jnp.swapaxes(w_in, 1, 2).astype(BF16)
    w_dt =jnp.pad(w16[:, _SRC["dt"]:_SRC["qc"], :], ((0, 0), (0, LANES - 2 * H_B), (0, 0)))
    wa16, wb16, wc16, wo16 = (w.astype(BF16) for w in (w_br_a, w_br_b, w_br_c, w_out))

    cvecs = jnp.concatenate([c_ctx[None, :], c, jnp.zeros((8 - 1 - DEC_BATCH, D_MODEL), F32)], axis=0)
    ada = _ada_call(cvecs.T, w_ada, b_ada)
    cos_t, sin_t = _rope_tables()

    ck_a = cache_diff_k.reshape(DEC_BATCH, DEPTH, PAST_LEN, W_A)
    cv_a = cache_diff_v.reshape(DEC_BATCH, DEPTH, PAST_LEN, W_A)
    ck_c = cache_na_k.transpose(0, 1, 3, 4, 2)
    cv_c = cache_na_v.transpose(0, 1, 3, 4, 2)
    na_tiles = _rpb_call(na_rpb)
    h0t = state_ssd.transpose(0, 1, 2, 5, 3, 4).reshape(DEC_BATCH, DEPTH, 2, N_B, DI_B)
    h0t = h0t.reshape(DEC_BATCH, DEPTH, 2, N_B, H_B // 2, LANES).transpose(0, 1, 2, 4, 3, 5)
    h0t = h0t.reshape(DEC_BATCH, DEPTH, 2, (H_B // 2) * N_B, LANES)

    xp = x_prompt.reshape(BATCH * SEQ, D_MODEL)
    xs = x_sample.reshape(DEC_BATCH * DEC_SEQ, D_MODEL)
    fg = final_g.reshape(1, D_MODEL)
    norm_g3 = norm_g.reshape(DEPTH, 1, D_MODEL)
    carry = None
    for li in range(DEPTH):
        lam_init = 0.8 - 0.6 * math.exp(-0.3 * li)
        final = li == DEPTH - 1
        lamvec = jnp.concatenate([_pad_lanes(v[li]) for v in (lam_q1, lam_k1, lam_q2, lam_k2)], axis=0)
        subln = diff_subln_g[li].reshape(1, LANES)
        dtb = _pad_lanes(dt_bias[li])
        alog = _pad_lanes(a_log[li])
        dskx = jnp.repeat(d_skip[li], P_B, axis=-1)
        ssd_w = (conv_w[li], conv_b[li].reshape(1, CONV_DIM), dtb, alog, dskx, ssd_norm_g[li].reshape(1, DI_B))
        post_w = (wa16, wb16, wc16, wo16, fg)

        proj, ka, va, kc_t, vc_t, yb, ssd_state = _proj_ssd_call(xp, ada, norm_g3, w16, w_dt, ssd_w, li,
                                                                 ctx=True, carry=carry)
        carry = (ka, va, kc_t, vc_t, ssd_state)
        ya = _attn_a_ctx_call(proj, lamvec, subln, lam_init)
        xp = _post_ctx_call(xp, proj, kc_t, vc_t, ya, yb, ada, norm_g3, w16, li, *post_w, final=final)

        proj, kv, yb = _proj_ssd_call(xs, ada, norm_g3, w16, w_dt, ssd_w, li, ctx=False, h0t=h0t)
        ya = _attn_a_lat_call(proj, ck_a, cv_a, li, cos_t, sin_t, lamvec, subln, lam_init)
        yc = _attn_c_lat_call(proj, kv, ck_c, cv_c, li, na_tiles)
        xs = _post_call(xs, ya, yb, yc, ada, norm_g3, w16, li, *post_w, tm=512, row_base=1,
                        tokens_per_row=DEC_SEQ, final=final)

    new_k_a, new_v_a, new_k_c_t, new_v_c_t, new_ssd = carry
    to_token_major = lambda a: a.transpose(0, 1, 4, 2, 3)
    return (xp.reshape(BATCH, SEQ, D_MODEL), xs.reshape(DEC_BATCH, DEC_SEQ, D_MODEL),
            new_k_a, new_v_a, to_token_major(new_k_c_t), to_token_major(new_v_c_t), new_ssd)
```

```python
import functools
import math

import jax
import jax.numpy as jnp
import numpy as np
from jax import lax
from jax.experimental import pallas as pl
from jax.experimental.pallas import tpu as pltpu

D_MODEL = 1024
BATCH = 32
SEQ = 256
DEPTH = 2
DEC_BATCH = 2
DEC_SEQ = 1024
PAST_LEN = 512
GRID_W = 64
GRID_ROWS = DEC_SEQ // GRID_W
H_A = 4
DH_A = 64
W_A = H_A * 2 * DH_A
H_B = 8
P_B = 64
G_B = 2
N_B = 64
DI_B = H_B * P_B
CONV_K = 5
CONV_DIM = DI_B + 2 * G_B * N_B
SSD_CHUNK = 128
H_C = 8
DH_C = 64
W_C = H_C * DH_C
NA_KH = 8
NA_KW = 16
N_BRANCH = 3
ROPE_BASE = 10000.0
EPS = 1e-6

LANES = 128
HALF = LANES // 2
VMEM_LIMIT = 56 * 1024 * 1024

BC_DIM = CONV_DIM - DI_B
KVC_COLS = 2 * W_C
MERGE_COLS = N_BRANCH * D_MODEL
_SRC = dict(qa=0, ka=512, va=1024, ga=1536, z=2048, xs=2560, bc=3072, dt=3328, qc=3344, kc=3856, vc=4368,
            gc=4880, merge=5392)
COL_QA = 0
COL_GA = 512
COL_QC = 1024
COL_GC = 1536
COL_KA = 2048
COL_VA = 2560
PROJ_COLS = 3072
PROJ_SEGMENTS = ((COL_QA, _SRC["qa"]), (COL_GA, _SRC["ga"]), (COL_QC, _SRC["qc"]), (COL_GC, _SRC["gc"]),
                 (COL_KA, _SRC["ka"]), (COL_VA, _SRC["va"]))
P_Z = 0
P_XS = 512
P_BC = 1024
P_DT = 1280
P_COLS = P_DT + LANES
SIDE_TN = 256

NA_QROWS = 4
NA_WROWS = 12
NA_TILES = 2 * NA_KH
NEG_INF = float("-inf")
LOG2E = math.log2(math.e)
ATTN_AHEAD = 2
ATTN_AHEAD_CTX = 3
SSD_AHEAD = 2
CTX_BATCHES_PER_STEP = 2
ATTN_BATCHES_PER_STEP = 4
HI = lax.Precision.HIGHEST
F32 = jnp.float32
BF16 = jnp.bfloat16


def _dot(a, b, precision=None):
    return jnp.dot(a, b, preferred_element_type=F32, precision=precision)


def _dot_nt(a, b):
    return lax.dot_general(a, b, (((1,), (1,)), ((), ())), preferred_element_type=F32)


def _sigmoid(x):
    return 1.0 / (1.0 + jnp.exp(-x))


def _silu(x):
    return x * _sigmoid(x)


def _lane(shape):
    return lax.broadcasted_iota(jnp.int32, shape, len(shape) - 1)


def _params(*sem):
    return pltpu.CompilerParams(dimension_semantics=sem, vmem_limit_bytes=VMEM_LIMIT)


def _ada_kernel(cvt_ref, w_ref, b_ref, o_ref):
    n_rows = 1 + DEC_BATCH
    s = _silu(cvt_ref[...])
    accs = [jnp.zeros((8, w_ref.shape[1]), F32)] * n_rows
    for k0 in range(0, D_MODEL, 8):
        w = w_ref[k0:k0 + 8, :]
        accs = [acc + w * s[k0:k0 + 8, r:r + 1] for r, acc in enumerate(accs)]
    rows = [jnp.sum(acc, axis=0, keepdims=True) for acc in accs]
    rows.append(jnp.zeros((8 - n_rows, w_ref.shape[1]), F32))
    o_ref[...] = jnp.concatenate(rows, axis=0) + b_ref[...]


def _ada_call(cvecs, w_ada, b_ada):
    tn = 1536
    return pl.pallas_call(
        _ada_kernel,
        grid=(DEPTH, 3 * D_MODEL // tn),
        in_specs=[
            pl.BlockSpec((D_MODEL, 8), lambda l, j: (0, 0)),
            pl.BlockSpec((None, D_MODEL, tn), lambda l, j: (l, 0, j)),
            pl.BlockSpec((None, 1, tn), lambda l, j: (l, 0, j)),
        ],
        out_specs=pl.BlockSpec((None, 8, tn), lambda l, j: (l, 0, j)),
        out_shape=jax.ShapeDtypeStruct((DEPTH, 8, 3 * D_MODEL), F32),
        compiler_params=_params("arbitrary", "arbitrary"),
        name="ada",
    )(cvecs, w_ada, b_ada.reshape(DEPTH, 1, 3 * D_MODEL))


def _modulated_norm(x, g_ref, ada_ref, row):
    y = x * lax.rsqrt(jnp.mean(x * x, axis=-1, keepdims=True) + EPS) * g_ref[...]
    shift = ada_ref[pl.ds(row, 1), 0:D_MODEL]
    scale = ada_ref[pl.ds(row, 1), D_MODEL:2 * D_MODEL]
    return (y * (1.0 + scale) + shift).astype(BF16)


def _diff_lambda_in_kernel(lam_ref, lam_init):
    v = lam_ref[...]
    l1 = jnp.sum(v[0:1] * v[1:2], axis=-1, keepdims=True)
    l2 = jnp.sum(v[2:3] * v[3:4], axis=-1, keepdims=True)
    return jnp.exp(l1) - jnp.exp(l2) + lam_init


def _split_halves(x, scale):
    lo = _lane(x.shape) < HALF
    xs = x * (scale * LOG2E)
    return jnp.concatenate([jnp.where(lo, xs, 0.0), jnp.where(lo, 0.0, xs)], axis=0).astype(BF16)


def _diff_combine(o2, rsum, lam, t):
    return o2[:t] * rsum[:t] - (lam * rsum[t:]) * o2[t:]


def _diff_head_post(o, subln_g, lam_init, gate):
    o = o * lax.rsqrt(jnp.mean(o * o, axis=-1, keepdims=True) + EPS) * (subln_g * (1.0 - lam_init))
    return (o * _silu(gate)).astype(BF16)


def _attn_a_ctx_kernel(q_ref, k_ref, v_ref, g_ref, lam_ref, sg_ref, o_ref, *, lam_init):
    t = SEQ
    lam = _diff_lambda_in_kernel(lam_ref, lam_init)
    ones = jnp.ones((t, LANES), BF16)
    blocks = [(b, h) for b in range(q_ref.shape[0] // t) for h in range(H_A)]
    where = lambda b, h: (slice(b * t, (b + 1) * t), slice(h * LANES, (h + 1) * LANES))

    def scores(b, h):
        qq = _split_halves(q_ref[where(b, h)], DH_A ** -0.5)
        return _dot_nt(qq, k_ref[where(b, h)].astype(BF16))

    pending = [scores(*blk) for blk in blocks[:ATTN_AHEAD_CTX]]
    for n, blk in enumerate(blocks):
        s = pending.pop(0)
        if n + ATTN_AHEAD_CTX < len(blocks):
            pending.append(scores(*blocks[n + ATTN_AHEAD_CTX]))
        e = jnp.exp2(s - jnp.max(s, axis=-1, keepdims=True)).astype(BF16)
        rsum = 1.0 / _dot(e, ones)
        o = _diff_combine(_dot(e, v_ref[where(*blk)].astype(BF16)), rsum, lam, t)
        o_ref[where(*blk)] = _diff_head_post(o, sg_ref[...], lam_init, g_ref[where(*blk)])


def _attn_a_ctx_call(proj, lamvec, subln_g, lam_init):
    rows = ATTN_BATCHES_PER_STEP * SEQ
    blk = lambda c: pl.BlockSpec((rows, W_A), lambda b: (b, c // W_A))
    return pl.pallas_call(
        functools.partial(_attn_a_ctx_kernel, lam_init=lam_init),
        grid=(BATCH // ATTN_BATCHES_PER_STEP,),
        in_specs=[blk(COL_QA), blk(COL_KA), blk(COL_VA), blk(COL_GA),
                  pl.BlockSpec((4, LANES), lambda b: (0, 0)),
                  pl.BlockSpec((1, LANES), lambda b: (0, 0))],
        out_specs=pl.BlockSpec((rows, W_A), lambda b: (b, 0)),
        out_shape=jax.ShapeDtypeStruct((BATCH * SEQ, W_A), BF16),
        compiler_params=_params("arbitrary"),
        name="attn_a_ctx",
    )(proj, proj, proj, proj, lamvec, subln_g)


def _rope(x, cos, sin_signed):
    first = (_lane(x.shape) % 32) < 16
    swapped = jnp.where(first, pltpu.roll(x, LANES - 16, 1), pltpu.roll(x, 16, 1))
    return x * cos + swapped * sin_signed


def _attn_a_lat_kernel(q_ref, k_ref, v_ref, g_ref, ck_ref, cv_ref, cosq_ref, sinq_ref, cosk_ref, sink_ref,
                       lam_ref, sg_ref, o_ref, kr_s, *, lam_init):
    tq = q_ref.shape[0]

    @pl.when(pl.program_id(1) == 0)
    def _():
        for h in range(H_A):
            sl = slice(h * LANES, (h + 1) * LANES)
            kr_s[:, sl] = _rope(k_ref[:, sl], cosk_ref[...], sink_ref[...]).astype(BF16)

    lam = _diff_lambda_in_kernel(lam_ref, lam_init)

    def scores(h):
        sl = slice(h * LANES, (h + 1) * LANES)
        qq = _split_halves(_rope(q_ref[:, sl], cosq_ref[...], sinq_ref[...]), DH_A ** -0.5)
        return _dot_nt(qq, kr_s[:, sl]), _dot_nt(qq, ck_ref[:, sl].astype(BF16))

    pending = [scores(h) for h in range(ATTN_AHEAD)]
    for h in range(H_A):
        sl = slice(h * LANES, (h + 1) * LANES)
        s_lat, s_ctx = pending.pop(0)
        if h + ATTN_AHEAD < H_A:
            pending.append(scores(h + ATTN_AHEAD))
        m = jnp.maximum(jnp.max(s_lat, axis=-1, keepdims=True), jnp.max(s_ctx, axis=-1, keepdims=True))
        e_lat = jnp.exp2(s_lat - m)
        e_ctx = jnp.exp2(s_ctx - m)
        rsum = 1.0 / (jnp.sum(e_lat, axis=-1, keepdims=True) + jnp.sum(e_ctx, axis=-1, keepdims=True))
        o2 = _dot(e_lat.astype(BF16), v_ref[:, sl].astype(BF16)) + _dot(e_ctx.astype(BF16),
                                                                         cv_ref[:, sl].astype(BF16))
        o = _diff_combine(o2, rsum, lam, tq)
        o_ref[:, sl] = _diff_head_post(o, sg_ref[...], lam_init, g_ref[:, sl])


def _attn_a_lat_call(proj, cache_k, cache_v, li, cos_t, sin_t, lamvec, subln_g, lam_init):
    tq = 256
    nq = DEC_SEQ // tq
    qblk = lambda c: pl.BlockSpec((tq, W_A), lambda b, i: (b * nq + i, c // W_A))
    full = lambda c: pl.BlockSpec((DEC_SEQ, W_A), lambda b, i: (b, c // W_A))
    cache = pl.BlockSpec((None, None, PAST_LEN, W_A), lambda b, i: (b, li, 0, 0))
    return pl.pallas_call(
        functools.partial(_attn_a_lat_kernel, lam_init=lam_init),
        grid=(DEC_BATCH, nq),
        in_specs=[qblk(COL_QA), full(COL_KA), full(COL_VA), qblk(COL_GA), cache, cache,
                  pl.BlockSpec((tq, LANES), lambda b, i: (i, 0)),
                  pl.BlockSpec((tq, LANES), lambda b, i: (i, 0)),
                  pl.BlockSpec((DEC_SEQ, LANES), lambda b, i: (0, 0)),
                  pl.BlockSpec((DEC_SEQ, LANES), lambda b, i: (0, 0)),
                  pl.BlockSpec((4, LANES), lambda b, i: (0, 0)),
                  pl.BlockSpec((1, LANES), lambda b, i: (0, 0))],
        out_specs=pl.BlockSpec((tq, W_A), lambda b, i: (b * nq + i, 0)),
        out_shape=jax.ShapeDtypeStruct((DEC_BATCH * DEC_SEQ, W_A), BF16),
        scratch_shapes=[pltpu.VMEM((DEC_SEQ, W_A), BF16)],
        compiler_params=_params("arbitrary", "arbitrary"),
        name="attn_a_lat",
    )(proj, proj, proj, proj, cache_k, cache_v, cos_t, sin_t, cos_t, sin_t, lamvec, subln_g)


def _merge_halves(o, t):
    return jnp.where(_lane((t, LANES)) < HALF, o[:t], o[t:])


def _spread(work, n_slots):
    state = dict(slot=0, done=0)

    def emit_until(target):
        while state["done"] < target:
            work[state["done"]]()
            state["done"] += 1

    def side():
        state["slot"] += 1
        emit_until(min(len(work), -(-state["slot"] * len(work) // n_slots)))

    return side, lambda: emit_until(len(work))


def _attn_c_ctx_body(q_ref, kt_ref, vt_ref, g_ref, o_ref, side):
    t = SEQ
    blocks = [(b, j) for b in range(q_ref.shape[0] // t) for j in range(H_C // 2)]
    where = lambda b, j: (slice(b * t, (b + 1) * t), slice(j * LANES, (j + 1) * LANES))
    pair_t = lambda ref, b, j: ref[b, 2 * j:2 * j + 2].reshape(LANES, t).astype(BF16)

    def scores(b, j):
        return _dot(_split_halves(q_ref[where(b, j)], DH_C ** -0.5), pair_t(kt_ref, b, j))

    pending = [scores(*blk) for blk in blocks[:ATTN_AHEAD_CTX]]
    for n, blk in enumerate(blocks):
        s = pending.pop(0)
        if n + ATTN_AHEAD_CTX < len(blocks):
            pending.append(scores(*blocks[n + ATTN_AHEAD_CTX]))
        e = jnp.exp2(s - jnp.max(s, axis=-1, keepdims=True))
        rsum = 1.0 / jnp.sum(e, axis=-1, keepdims=True)
        o = _merge_halves(_dot_nt(e.astype(BF16), pair_t(vt_ref, *blk)) * rsum, t)
        o_ref[where(*blk)] = (o * _silu(g_ref[where(*blk)])).astype(BF16)
        side()


def _post_ctx_kernel(q_ref, kt_ref, vt_ref, gc_ref, x_ref, ya_ref, yb_ref, ada_ref, g_ref, wm_ref, wa_ref, wb_ref,
                     wc_ref, wo_ref, fg_ref, o_ref, yc_s, h_s, acc_s, sc_s, *, final):
    d = D_MODEL
    gate = ada_ref[0:1, 2 * d:3 * d]
    h_s[...] = _modulated_norm(x_ref[...], g_ref, ada_ref, 0)

    def gate_logits(n, c0):
        r0 = _SRC["merge"] + n * d + c0
        return _dot_nt(h_s[...], wm_ref[r0:r0 + SIDE_TN, :])

    work = []
    for c0 in range(0, d, SIDE_TN):
        cols = slice(c0, c0 + SIDE_TN)

        def branch_a(c0=c0, cols=cols):
            acc_s[:, cols] = _sigmoid(gate_logits(0, c0)) * _dot(ya_ref[...], wa_ref[:, cols])

        def branch_b(c0=c0, cols=cols):
            acc_s[:, cols] += _sigmoid(gate_logits(1, c0)) * _dot(yb_ref[...], wb_ref[:, cols])

        def gate_c(c0=c0, cols=cols):
            sc_s[:, cols] = _sigmoid(gate_logits(2, c0))

        work += [branch_a, branch_b, gate_c]
    side, flush = _spread(work, (q_ref.shape[0] // SEQ) * (H_C // 2))
    _attn_c_ctx_body(q_ref, kt_ref, vt_ref, gc_ref, yc_s, side)
    flush()
    merged = acc_s[...] + sc_s[...] * _dot(yc_s[...], wc_ref[...])
    x = x_ref[...] + gate * _dot(merged.astype(BF16), wo_ref[...])
    if final:
        x = x * lax.rsqrt(jnp.mean(x * x, axis=-1, keepdims=True) + EPS) * fg_ref[...]
    o_ref[...] = x


def _post_ctx_call(x, proj, kc_t, vc_t, ya, yb, ada, norm_g, w_merge, li, wa, wb, wc, wo, final_g, *, final):
    nb = CTX_BATCHES_PER_STEP
    tm = nb * SEQ
    tok = lambda w: pl.BlockSpec((tm, w), lambda i: (i, 0))
    col = lambda c: pl.BlockSpec((tm, W_C), lambda i: (i, c // W_C))
    cache = pl.BlockSpec((nb, None, H_C, DH_C, SEQ), lambda i: (i, li, 0, 0, 0))
    layer = lambda *shape, **kw: pl.BlockSpec((None,) + shape, lambda i: (li,) + (0,) * len(shape), **kw)
    once = dict(pipeline_mode=pl.Buffered(1))
    return pl.pallas_call(
        functools.partial(_post_ctx_kernel, final=final),
        grid=(BATCH // nb,),
        in_specs=[col(COL_QC), cache, cache, col(COL_GC), tok(D_MODEL), tok(W_A), tok(DI_B),
                  layer(8, 3 * D_MODEL), layer(1, D_MODEL), layer(w_merge.shape[1], D_MODEL, **once),
                  layer(W_A, D_MODEL, **once), layer(DI_B, D_MODEL, **once), layer(W_C, D_MODEL, **once),
                  layer(D_MODEL, D_MODEL, **once), pl.BlockSpec((1, D_MODEL), lambda i: (0, 0))],
        out_specs=tok(D_MODEL),
        out_shape=jax.ShapeDtypeStruct((BATCH * SEQ, D_MODEL), F32),
        scratch_shapes=[pltpu.VMEM((tm, W_C), BF16), pltpu.VMEM((tm, D_MODEL), BF16),
                        pltpu.VMEM((tm, D_MODEL), F32), pltpu.VMEM((tm, D_MODEL), F32)],
        compiler_params=_params("arbitrary"),
        name="attn_c_post_ctx_final" if final else "attn_c_post_ctx",
    )(proj, kc_t, vc_t, proj, x, ya, yb, ada, norm_g, w_merge, wa, wb, wc, wo, final_g)


def _rpb_kernel(rpb_ref, o_ref):
    shape = (GRID_W, LANES)
    c = lax.broadcasted_iota(jnp.int32, shape, 0)
    cp = _lane(shape) % GRID_W
    start = jnp.clip(c - NA_KW // 2, 0, GRID_W - NA_KW)
    in_win = (cp >= start) & (cp < start + NA_KW)
    for h in range(H_C):
        o_ref[h, 0] = jnp.full(shape, NEG_INF, F32)
        for dr in range(2 * NA_KH - 1):
            row = jnp.broadcast_to(rpb_ref[h, dr:dr + 1, :], shape)
            tile = pltpu.roll(row, LANES - (NA_KW - 1), 1, stride=1, stride_axis=0)
            o_ref[h, 1 + dr] = jnp.where(in_win, tile * LOG2E, NEG_INF)


def _rpb_call(rpb):
    n_dc = 2 * NA_KW - 1
    v = jnp.pad(rpb, ((0, 0), (0, 0), (0, NA_TILES - (2 * NA_KH - 1)), (0, GRID_W - n_dc)))
    v = jnp.concatenate([v] * (LANES // GRID_W), axis=-1)
    return pl.pallas_call(
        _rpb_kernel,
        grid=(DEPTH,),
        in_specs=[pl.BlockSpec((None, H_C, NA_TILES, LANES), lambda l: (l, 0, 0, 0))],
        out_specs=pl.BlockSpec((None, H_C, NA_TILES, GRID_W, LANES), lambda l: (l, 0, 0, 0, 0)),
        out_shape=jax.ShapeDtypeStruct((DEPTH, H_C, NA_TILES, GRID_W, LANES), F32),
        compiler_params=_params("arbitrary"),
        name="rpb_tiles",
    )(v)


def _attn_c_lat_kernel(q_ref, k_ref, v_ref, g_ref, ck_ref, cv_ref, tile_ref, o_ref, bias_s):
    tq = q_ref.shape[0]
    m = pl.program_id(1)
    n_blocks = GRID_ROWS // NA_QROWS
    lo = _lane((GRID_W, LANES)) < HALF
    n_pair = H_C // 2

    def run(wrows, w0):
        nwin = wrows * GRID_W
        k0 = pl.multiple_of(w0 * GRID_W, GRID_W)

        def scores(j):
            sl = slice(j * LANES, (j + 1) * LANES)
            for s in range(2):
                for i in range(NA_QROWS):
                    r = m * NA_QROWS + i
                    start = jnp.clip(r - NA_KH // 2, 0, GRID_ROWS - NA_KH)
                    for jp in range(wrows // 2):
                        idx = []
                        for u in range(2):
                            rk = w0 + 2 * jp + u
                            valid = (rk >= start) & (rk < start + NA_KH)
                            idx.append(jnp.where(valid, rk - r + NA_KH, 0))
                        tile = jnp.where(lo, tile_ref[2 * j + s, idx[0]], tile_ref[2 * j + s, idx[1]])
                        bias_s[(s * NA_QROWS + i) * GRID_W:(s * NA_QROWS + i + 1) * GRID_W,
                               jp * LANES:(jp + 1) * LANES] = tile
            qq = _split_halves(q_ref[:, sl], DH_C ** -0.5)
            kw = k_ref[pl.ds(k0, nwin), sl].astype(BF16)
            s_win = _dot_nt(qq, kw) + bias_s[:, 0:nwin]
            ckt = ck_ref[2 * j:2 * j + 2].reshape(LANES, PAST_LEN).astype(BF16)
            return s_win, _dot(qq, ckt)

        pending = [scores(j) for j in range(ATTN_AHEAD)]
        for j in range(n_pair):
            sl = slice(j * LANES, (j + 1) * LANES)
            s_win, s_ctx = pending.pop(0)
            if j + ATTN_AHEAD < n_pair:
                pending.append(scores(j + ATTN_AHEAD))
            vw = v_ref[pl.ds(k0, nwin), sl].astype(BF16)
            mx = jnp.maximum(jnp.max(s_win, axis=-1, keepdims=True), jnp.max(s_ctx, axis=-1, keepdims=True))
            e_win = jnp.exp2(s_win - mx)
            e_ctx = jnp.exp2(s_ctx - mx)
            rs = 1.0 / (jnp.sum(e_win, axis=-1, keepdims=True) + jnp.sum(e_ctx, axis=-1, keepdims=True))
            cvt = cv_ref[2 * j:2 * j + 2].reshape(LANES, PAST_LEN).astype(BF16)
            o = (_dot(e_win.astype(BF16), vw) + _dot_nt(e_ctx.astype(BF16), cvt)) * rs
            o_ref[:, sl] = (_merge_halves(o, tq) * _silu(g_ref[:, sl])).astype(BF16)

    is_edge = (m == 0) | (m == n_blocks - 1)
    pl.when(is_edge)(lambda: run(NA_KH, jnp.where(m == 0, 0, GRID_ROWS - NA_KH)))
    pl.when(jnp.logical_not(is_edge))(
        lambda: run(NA_WROWS, jnp.where(m < n_blocks // 2, 0, GRID_ROWS - NA_WROWS)))


def _attn_c_lat_call(proj, kv, cache_k, cache_v, li, tiles):
    tq = NA_QROWS * GRID_W
    nq = DEC_SEQ // tq
    qblk = lambda c: pl.BlockSpec((tq, W_C), lambda b, i: (b * nq + i, c // W_C))
    full = lambda c: pl.BlockSpec((DEC_SEQ, W_C), lambda b, i: (b, c // W_C))
    cache = pl.BlockSpec((None, None, H_C, DH_C, PAST_LEN), lambda b, i: (b, li, 0, 0, 0))
    return pl.pallas_call(
        _attn_c_lat_kernel,
        grid=(DEC_BATCH, nq),
        in_specs=[qblk(COL_QC), full(0), full(W_C), qblk(COL_GC), cache, cache,
                  pl.BlockSpec((None, H_C, NA_TILES, GRID_W, LANES), lambda b, i: (li, 0, 0, 0, 0))],
        out_specs=pl.BlockSpec((tq, W_C), lambda b, i: (b * nq + i, 0)),
        out_shape=jax.ShapeDtypeStruct((DEC_BATCH * DEC_SEQ, W_C), BF16),
        scratch_shapes=[pltpu.VMEM((2 * tq, NA_WROWS * GRID_W), F32)],
        compiler_params=_params("arbitrary", "arbitrary"),
        name="attn_c_lat",
    )(proj, kv, kv, proj, cache_k, cache_v, tiles)


def _ssd_body(dt_ref, xs_ref, bc_ref, z_ref, h0_ref, params, y_ref, hs_ref, scratch, *, seq, static_loops, side):
    cw_ref, cb_ref, dtb_ref, alog_ref, dsk_ref, g_ref = params
    upad_s, xc_s, expo_s, expot_s, dtt_s, tot_s, bmt_s, yf_s, yb_s, st_s = scratch
    use_h0 = h0_ref is not None
    want_state = hs_ref is not None

    q = SSD_CHUNK
    nc = seq // q
    n_pair = H_B // 2
    n_hd = 2 * H_B
    pad = 8

    def loop(body, unroll=1):
        if static_loops:
            for c in range(nc):
                body(c, 0)
        else:
            lax.fori_loop(0, nc, body, 0, unroll=unroll)

    def chunk_rows(c):
        return slice(c * q, (c + 1) * q) if isinstance(c, int) else pl.ds(pl.multiple_of(c * q, q), q)

    upad_s[0:pad, :] = jnp.zeros((pad, CONV_DIM), F32)
    upad_s[pad + seq:2 * pad + seq, :] = jnp.zeros((pad, CONV_DIM), F32)
    upad_s[pad:pad + seq, 0:DI_B] = xs_ref[...]
    upad_s[pad:pad + seq, DI_B:CONV_DIM] = bc_ref[...]

    for c in range(nc):
        for cb_ in range(CONV_DIM // LANES):
            csl = slice(cb_ * LANES, (cb_ + 1) * LANES)
            acc = jnp.zeros((q, LANES), F32) + cb_ref[:, csl]
            for k in range(CONV_K):
                r0 = c * q + pad - CONV_K // 2 + k
                acc = acc + upad_s[r0:r0 + q, csl] * cw_ref[k:k + 1, csl]
            xc_s[c * q:(c + 1) * q, csl] = _silu(acc)
            side()

    a_row = -jnp.exp(alog_ref[...]) * LOG2E
    a_col = jnp.broadcast_to(a_row, (LANES, LANES)).T[0:n_hd, 0:1]
    ri = lax.broadcasted_iota(jnp.int32, (q, q), 0)
    ci = lax.broadcasted_iota(jnp.int32, (q, q), 1)
    ltri = (ri >= ci).astype(F32)
    fwd_lane = _lane((q, LANES)) < H_B
    fwd_row = lax.broadcasted_iota(jnp.int32, (n_hd, q), 0) < H_B

    def prep_body(c, carry):
        rows = chunk_rows(c)
        xdt = dt_ref[rows, 0:LANES] + dtb_ref[...]
        dtv = jnp.maximum(xdt, 0.0) + jnp.log1p(jnp.exp(-jnp.abs(xdt)))
        la = dtv * a_row
        acum = _dot(ltri, la, HI)
        expo_s[rows, :] = jnp.where(fwd_lane, acum, la - acum)
        acum_t = acum.T[0:n_hd, :]
        dt_t = dtv.T[0:n_hd, :]
        expot_s[c] = jnp.where(fwd_row, acum_t, dt_t * a_col - acum_t)
        dtt_s[c] = dt_t
        tot_s[c] = jnp.broadcast_to(acum_t[:, q - 1:q], (n_hd, q))
        bmt_s[c] = xc_s[rows, DI_B:DI_B + LANES].T
        side()
        return carry

    loop(prep_body, unroll=2)

    if use_h0:
        st_s[...] = h0_ref[...].reshape(2, n_pair, N_B, LANES)
    else:
        st_s[...] = jnp.zeros_like(st_s)

    lane_q = _lane((q, LANES))
    lo = lane_q < HALF
    lo_st = _lane((N_B, LANES)) < HALF

    def chunk_pair(c_fwd, c_bwd):
        dirs = ((0, c_fwd, yf_s), (1, c_bwd, yb_s))
        group_of = lambda k: k * G_B // n_pair
        items = [(k, d) for k in range(n_pair) for d in dirs]
        cb, y_off, st_in = {}, {}, {}

        def issue_early(k, d):
            dirn, c, _ = d
            g, rows = group_of(k), chunk_rows(c)
            in_g = (lane_q >= g * N_B) & (lane_q < (g + 1) * N_B)
            cmg = jnp.where(in_g, xc_s[rows, DI_B + LANES:DI_B + 2 * LANES], 0.0).astype(BF16)
            if (dirn, g) not in cb:
                cb[dirn, g] = _dot_nt(cmg, xc_s[rows, DI_B:DI_B + LANES].astype(BF16))
            st_in[dirn, k] = st_s[dirn, k]
            y_off[dirn, k] = _dot(cmg, jnp.concatenate([st_in[dirn, k]] * 2, axis=0).astype(BF16))

        for item in items[:SSD_AHEAD]:
            issue_early(*item)
        for n, (k, (dirn, c, y_s)) in enumerate(items):
            if n + SSD_AHEAD < len(items):
                issue_early(*items[n + SSD_AHEAD])
            psl = slice(k * LANES, (k + 1) * LANES)
            rows = chunk_rows(c)
            tri = (ri >= ci) if dirn == 0 else (ci >= ri)
            bmt_g = bmt_s[c, group_of(k) * N_B:(group_of(k) + 1) * N_B, :]
            x16 = xc_s[rows, psl].astype(BF16)
            mats, lhs, ysc, cdec = [], [], [], []
            for s in range(2):
                col = dirn * H_B + 2 * k + s
                e_col = jnp.broadcast_to(expo_s[rows, col:col + 1], (q, q))
                e_row = expot_s[c, col:col + 1, :]
                dt_row = dtt_s[c, col:col + 1, :]
                tot = tot_s[c, col:col + 1, :]
                dec = jnp.exp2(jnp.where(tri, e_col - e_row, NEG_INF))
                mats.append((cb[dirn, group_of(k)] * dec * dt_row).astype(BF16))
                if dirn == 0:
                    ysc.append(jnp.exp2(e_col))
                    w_row = jnp.exp2(tot - e_row)
                else:
                    ysc.append(jnp.exp2(e_col + tot))
                    w_row = jnp.exp2(-e_row)
                lhs.append((bmt_g * (w_row * dt_row)).astype(BF16))
                cdec.append(jnp.exp2(tot[:, 0:LANES]))
            yd = _dot(jnp.concatenate(mats, axis=0), x16)
            ds = _dot(jnp.concatenate(lhs, axis=0), x16)
            yo, st = y_off.pop((dirn, k)), st_in.pop((dirn, k))
            y_s[rows, psl] = jnp.where(lo, yd[:q] + ysc[0] * yo, yd[q:] + ysc[1] * yo)
            st_s[dirn, k] = jnp.where(lo_st, cdec[0] * st + ds[:N_B], cdec[1] * st + ds[N_B:])
            side()

    def body(c, carry):
        chunk_pair(c, nc - 1 - c)
        return carry

    loop(body, unroll=2)

    dsum = dsk_ref[0:1, :] + dsk_ref[1:2, :]

    def out_body(c, carry):
        rows = chunk_rows(c)
        y = yf_s[rows, :] + yb_s[rows, :] + xc_s[rows, 0:DI_B] * dsum
        y = y * _silu(z_ref[rows, :])
        y = y * lax.rsqrt(jnp.mean(y * y, axis=-1, keepdims=True) + EPS) * g_ref[...]
        y_ref[rows, :] = y.astype(BF16)
        side()
        return carry

    loop(out_body)
    if want_state:
        for dirn in range(2):
            for k in range(n_pair):
                st = st_s[dirn, k]
                st_t = jnp.concatenate([st, st], axis=0).T
                for s in range(2):
                    hs_ref[dirn, 2 * k + s] = st_t[s * P_B:(s + 1) * P_B, 0:N_B]


def _ssd_scratch(seq):
    nc = seq // SSD_CHUNK
    per_chunk_rows = pltpu.VMEM((nc, 2 * H_B, SSD_CHUNK), F32)
    return [pltpu.VMEM((seq + 16, CONV_DIM), F32), pltpu.VMEM((seq, CONV_DIM), F32),
            pltpu.VMEM((seq, LANES), F32), per_chunk_rows, per_chunk_rows, per_chunk_rows,
            pltpu.VMEM((nc, LANES, SSD_CHUNK), F32),
            pltpu.VMEM((seq, DI_B), F32), pltpu.VMEM((seq, DI_B), F32),
            pltpu.VMEM((2, H_B // 2, N_B, LANES), F32)]


def _ssd_param_specs(const):
    return [const((CONV_K, CONV_DIM)), const((1, CONV_DIM)), const((1, LANES)), const((1, LANES)),
            const((2, DI_B)), const((1, DI_B))]


def _proj_ssd_kernel(*refs, nb, seq, ctx, n_carry):
    x_ref, ada_ref, g_ref, w_ref, wdt_ref = refs[:5]
    pos = 5
    h0_ref = None
    if not ctx:
        h0_ref = refs[pos]
        pos += 1
    params = refs[pos:pos + 6]
    outs = refs[pos + 6 + n_carry:]
    if ctx:
        proj_ref, ka_ref, va_ref, kc_ref, vc_ref, y_ref, hs_ref = outs[:7]
        outs = outs[7:]
    else:
        proj_ref, kv_ref, y_ref = outs[:3]
        hs_ref = None
        outs = outs[3:]
    h_s, p_s = outs[:2]
    scratch = outs[2:]

    row = 0 if ctx else 1 + pl.program_id(0)
    for r0 in range(0, nb * seq, SEQ):
        h_s[r0:r0 + SEQ, :] = _modulated_norm(x_ref[r0:r0 + SEQ, :], g_ref, ada_ref, row)
    for c0 in range(P_Z, P_DT, SIDE_TN):
        c1 = min(c0 + SIDE_TN, P_DT)
        p_s[:, c0:c1] = _dot_nt(h_s[...], w_ref[_SRC["z"] + c0:_SRC["z"] + c1, :])
    p_s[:, P_DT:P_COLS] = _dot_nt(h_s[...], wdt_ref[...])

    work = []
    for col, src in PROJ_SEGMENTS:
        for off in range(0, W_A, SIDE_TN):
            def tile(d=col + off, s=src + off):
                proj_ref[:, d:d + SIDE_TN] = _dot_nt(h_s[...], w_ref[s:s + SIDE_TN, :])
            work.append(tile)
            if ctx and col in (COL_KA, COL_VA):
                def store(dst=ka_ref if col == COL_KA else va_ref, col=col, off=off):
                    for b in range(nb):
                        for h in range(off // LANES, (off + SIDE_TN) // LANES):
                            dst[b, :, h, :] = proj_ref[b * seq:(b + 1) * seq, col + h * LANES:col + (h + 1) * LANES]
                work.append(store)
    for r0 in range(0, KVC_COLS, SIDE_TN):
        if ctx:
            for b in range(nb):
                def tile_t(b=b, r0=r0):
                    w_rows = w_ref[_SRC["kc"] + r0:_SRC["kc"] + r0 + SIDE_TN, :]
                    kv_t = _dot_nt(w_rows, h_s[b * seq:(b + 1) * seq, :])
                    dst, d0 = (kc_ref, r0) if r0 < W_C else (vc_ref, r0 - W_C)
                    dst[b, d0 // DH_C:(d0 + SIDE_TN) // DH_C] = kv_t.reshape(SIDE_TN // DH_C, DH_C, seq)
                work.append(tile_t)
        else:
            def tile_kv(r0=r0):
                kv_ref[:, r0:r0 + SIDE_TN] = _dot_nt(h_s[...], w_ref[_SRC["kc"] + r0:_SRC["kc"] + r0 + SIDE_TN, :])
            work.append(tile_kv)

    n_slots = nb * (seq // SSD_CHUNK) * (CONV_DIM // LANES + 2 + 2 * (H_B // 2))
    side, flush = _spread(work, n_slots)
    for b in range(nb):
        rows = pl.ds(b * seq, seq)
        _ssd_body(p_s.at[rows, pl.ds(P_DT, LANES)], p_s.at[rows, pl.ds(P_XS, DI_B)],
                  p_s.at[rows, pl.ds(P_BC, BC_DIM)], p_s.at[rows, pl.ds(P_Z, DI_B)], h0_ref, params,
                  y_ref.at[rows, :], None if hs_ref is None else hs_ref.at[b], scratch,
                  seq=seq, static_loops=True, side=side)
    flush()


def _proj_ssd_call(x, ada, norm_g, w16, w_dt, ssd_w, li, *, ctx, carry=None, h0t=None):
    nb, seq = (CTX_BATCHES_PER_STEP, SEQ) if ctx else (1, DEC_SEQ)
    tm = nb * seq
    t = x.shape[0]
    n_carry = 0 if carry is None else len(carry)
    const = lambda shape: pl.BlockSpec(shape, lambda i: (0,) * len(shape))
    once = pl.Buffered(1)
    big = {} if ctx else dict(pipeline_mode=pl.Buffered(1))
    in_specs = [pl.BlockSpec((tm, D_MODEL), lambda i: (i, 0), **big),
                pl.BlockSpec((None, 8, 3 * D_MODEL), lambda i: (li, 0, 0)),
                pl.BlockSpec((None, 1, D_MODEL), lambda i: (li, 0, 0)),
                pl.BlockSpec((None, _SRC["merge"], D_MODEL), lambda i: (li, 0, 0), pipeline_mode=once),
                pl.BlockSpec((None, LANES, D_MODEL), lambda i: (li, 0, 0))]
    args = [x, ada, norm_g, w16, w_dt]
    if not ctx:
        in_specs.append(pl.BlockSpec((None, None, 2, (H_B // 2) * N_B, LANES), lambda i: (i, li, 0, 0, 0)))
        args.append(h0t)
    in_specs += _ssd_param_specs(const)
    args += list(ssd_w)
    out_specs = [pl.BlockSpec((tm, PROJ_COLS), lambda i: (i, 0), **big)]
    out_shape = [jax.ShapeDtypeStruct((t, PROJ_COLS), F32)]
    aliases = {}
    if ctx:
        out_specs += [pl.BlockSpec((nb, None, SEQ, H_A, 2 * DH_A), lambda i: (i, li, 0, 0, 0))] * 2
        out_specs += [pl.BlockSpec((nb, None, H_C, DH_C, SEQ), lambda i: (i, li, 0, 0, 0))] * 2
        out_shape += [jax.ShapeDtypeStruct((BATCH, DEPTH, SEQ, H_A, 2 * DH_A), F32)] * 2
        out_shape += [jax.ShapeDtypeStruct((BATCH, DEPTH, H_C, DH_C, SEQ), F32)] * 2
    else:
        out_specs.append(pl.BlockSpec((tm, KVC_COLS), lambda i: (i, 0), **big))
        out_shape.append(jax.ShapeDtypeStruct((t, KVC_COLS), F32))
    out_specs.append(pl.BlockSpec((tm, DI_B), lambda i: (i, 0)))
    out_shape.append(jax.ShapeDtypeStruct((t, DI_B), BF16))
    if ctx:
        out_specs.append(pl.BlockSpec((nb, None, 2, H_B, P_B, N_B), lambda i: (i, li, 0, 0, 0, 0)))
        out_shape.append(jax.ShapeDtypeStruct((BATCH, DEPTH, 2, H_B, P_B, N_B), F32))
        if carry is not None:
            in_specs += [pl.BlockSpec(memory_space=pl.ANY)] * n_carry
            aliases = {len(args) + k: (1, 2, 3, 4, 6)[k] for k in range(n_carry)}
            args += list(carry)
    scratch = [pltpu.VMEM((tm, D_MODEL), BF16), pltpu.VMEM((tm, P_COLS), F32)] + _ssd_scratch(seq)
    return pl.pallas_call(
        functools.partial(_proj_ssd_kernel, nb=nb, seq=seq, ctx=ctx, n_carry=n_carry),
        grid=(t // tm,),
        in_specs=in_specs,
        out_specs=out_specs,
        out_shape=out_shape,
        input_output_aliases=aliases,
        scratch_shapes=scratch,
        compiler_params=_params("arbitrary"),
        name="ctx_proj_ssd" if ctx else "lat_proj_ssd",
    )(*args)


def _post_kernel(x_ref, ya_ref, yb_ref, yc_ref, ada_ref, g_ref, wm_ref, wa_ref, wb_ref, wc_ref, wo_ref, fg_ref,
                 o_ref, *, tm, row_base, tokens_per_row, final):
    row = row_base + (pl.program_id(0) * tm) // tokens_per_row
    gate = ada_ref[pl.ds(row, 1), 2 * D_MODEL:3 * D_MODEL]
    d = D_MODEL
    x = x_ref[...]
    h = _modulated_norm(x, g_ref, ada_ref, row)
    merged = None
    for n, (y_ref, w_ref) in enumerate(((ya_ref, wa_ref), (yb_ref, wb_ref), (yc_ref, wc_ref))):
        r0 = _SRC["merge"] + n * d
        logits = _dot_nt(h, wm_ref[r0:r0 + d, :])
        term = _sigmoid(logits) * _dot(y_ref[...], w_ref[...])
        merged = term if merged is None else merged + term
    x = x + gate * _dot(merged.astype(BF16), wo_ref[...])
    if final:
        x = x * lax.rsqrt(jnp.mean(x * x, axis=-1, keepdims=True) + EPS) * fg_ref[...]
    o_ref[...] = x


def _post_call(x, ya, yb, yc, ada, norm_g, w_merge_t, li, wa, wb, wc, wo, final_g, *, tm, row_base,
               tokens_per_row, final):
    t = x.shape[0]
    tok = lambda w: pl.BlockSpec((tm, w), lambda i: (i, 0))
    layer = lambda *shape, **kw: pl.BlockSpec((None,) + shape, lambda i: (li,) + (0,) * len(shape), **kw)
    once = dict(pipeline_mode=pl.Buffered(1))
    kern = functools.partial(_post_kernel, tm=tm, row_base=row_base, tokens_per_row=tokens_per_row, final=final)
    return pl.pallas_call(
        kern,
        grid=(t // tm,),
        in_specs=[tok(D_MODEL), tok(W_A), tok(DI_B), tok(W_C),
                  layer(8, 3 * D_MODEL), layer(1, D_MODEL), layer(w_merge_t.shape[1], D_MODEL, **once),
                  layer(W_A, D_MODEL, **once), layer(DI_B, D_MODEL, **once), layer(W_C, D_MODEL, **once),
                  layer(D_MODEL, D_MODEL, **once), pl.BlockSpec((1, D_MODEL), lambda i: (0, 0))],
        out_specs=tok(D_MODEL),
        out_shape=jax.ShapeDtypeStruct((t, D_MODEL), F32),
        compiler_params=_params("arbitrary"),
        name="post_final" if final else "post",
    )(x, ya, yb, yc, ada, norm_g, w_merge_t, wa, wb, wc, wo, final_g)


def _rope_tables():
    pos = np.arange(DEC_SEQ)
    lane = np.arange(LANES)
    l64 = lane % (2 * (DH_A // 2))
    quarter = DH_A // 4
    p = np.where((l64 < DH_A // 2)[None, :], (pos // GRID_W)[:, None], (pos % GRID_W)[:, None])
    inv = ROPE_BASE ** (-np.arange(quarter, dtype=np.float64) / quarter)
    ang = p.astype(np.float64) * inv[l64 % quarter][None, :]
    sign = np.where((lane % (2 * quarter)) < quarter, -1.0, 1.0)
    return jnp.asarray(np.cos(ang), F32), jnp.asarray(np.sin(ang) * sign[None, :], F32)


def _pad_lanes(v, width=LANES):
    v = v.reshape(1, -1).astype(F32)
    return jnp.pad(v, ((0, 0), (0, width - v.shape[1])))


def kernel(x_prompt, x_sample, cache_diff_k, cache_diff_v, cache_na_k, cache_na_v, state_ssd, c, c_ctx,
           norm_g, w_ada, b_ada, w_in, lam_q1, lam_k1, lam_q2, lam_k2, diff_subln_g, conv_w, conv_b,
           dt_bias, a_log, d_skip, ssd_norm_g, na_rpb, w_br_a, w_br_b, w_br_c, w_out, final_g):
    assert x_prompt.shape == (BATCH, SEQ, D_MODEL) and x_sample.shape == (DEC_BATCH, DEC_SEQ, D_MODEL)
    assert w_in.shape == (DEPTH, D_MODEL, _SRC["merge"] + MERGE_COLS)
    w16 = jnp.swapaxes(w_in, 1, 2).astype(BF16)
    w_dt =jnp.pad(w16[:, _SRC["dt"]:_SRC["qc"], :], ((0, 0), (0, LANES - 2 * H_B), (0, 0)))
    wa16, wb16, wc16, wo16 = (w.astype(BF16) for w in (w_br_a, w_br_b, w_br_c, w_out))

    cvecs = jnp.concatenate([c_ctx[None, :], c, jnp.zeros((8 - 1 - DEC_BATCH, D_MODEL), F32)], axis=0)
    ada = _ada_call(cvecs.T, w_ada, b_ada)
    cos_t, sin_t = _rope_tables()

    ck_a = cache_diff_k.reshape(DEC_BATCH, DEPTH, PAST_LEN, W_A)
    cv_a = cache_diff_v.reshape(DEC_BATCH, DEPTH, PAST_LEN, W_A)
    ck_c = cache_na_k.transpose(0, 1, 3, 4, 2)
    cv_c = cache_na_v.transpose(0, 1, 3, 4, 2)
    na_tiles = _rpb_call(na_rpb)
    h0t = state_ssd.transpose(0, 1, 2, 5, 3, 4).reshape(DEC_BATCH, DEPTH, 2, N_B, DI_B)
    h0t = h0t.reshape(DEC_BATCH, DEPTH, 2, N_B, H_B // 2, LANES).transpose(0, 1, 2, 4, 3, 5)
    h0t = h0t.reshape(DEC_BATCH, DEPTH, 2, (H_B // 2) * N_B, LANES)

    xp = x_prompt.reshape(BATCH * SEQ, D_MODEL)
    xs = x_sample.reshape(DEC_BATCH * DEC_SEQ, D_MODEL)
    fg = final_g.reshape(1, D_MODEL)
    norm_g3 = norm_g.reshape(DEPTH, 1, D_MODEL)
    carry = None
    for li in range(DEPTH):
        lam_init = 0.8 - 0.6 * math.exp(-0.3 * li)
        final = li == DEPTH - 1
        lamvec = jnp.concatenate([_pad_lanes(v[li]) for v in (lam_q1, lam_k1, lam_q2, lam_k2)], axis=0)
        subln = diff_subln_g[li].reshape(1, LANES)
        dtb = _pad_lanes(dt_bias[li])
        alog = _pad_lanes(a_log[li])
        dskx = jnp.repeat(d_skip[li], P_B, axis=-1)
        ssd_w = (conv_w[li], conv_b[li].reshape(1, CONV_DIM), dtb, alog, dskx, ssd_norm_g[li].reshape(1, DI_B))
        post_w = (wa16, wb16, wc16, wo16, fg)

        proj, ka, va, kc_t, vc_t, yb, ssd_state = _proj_ssd_call(xp, ada, norm_g3, w16, w_dt, ssd_w, li,
                                                                 ctx=True, carry=carry)
        carry = (ka, va, kc_t, vc_t, ssd_state)
        ya = _attn_a_ctx_call(proj, lamvec, subln, lam_init)
        xp = _post_ctx_call(xp, proj, kc_t, vc_t, ya, yb, ada, norm_g3, w16, li, *post_w, final=final)

        proj, kv, yb = _proj_ssd_call(xs, ada, norm_g3, w16, w_dt, ssd_w, li, ctx=False, h0t=h0t)
        ya = _attn_a_lat_call(proj, ck_a, cv_a, li, cos_t, sin_t, lamvec, subln, lam_init)
        yc = _attn_c_lat_call(proj, kv, ck_c, cv_c, li, na_tiles)
        xs = _post_call(xs, ya, yb, yc, ada, norm_g3, w16, li, *post_w, tm=512, row_base=1,
                        tokens_per_row=DEC_SEQ, final=final)

    new_k_a, new_v_a, new_k_c_t, new_v_c_t, new_ssd = carry
    to_token_major = lambda a: a.transpose(0, 1, 4, 2, 3)
    return (xp.reshape(BATCH, SEQ, D_MODEL), xs.reshape(DEC_BATCH, DEC_SEQ, D_MODEL),
            new_k_a, new_v_a, to_token_major(new_k_c_t), to_token_major(new_v_c_t), new_ssd)
```

```python
import functools
import math

import jax
import jax.numpy as jnp
import numpy as np
from jax import lax
from jax.experimental import pallas as pl
from jax.experimental.pallas import tpu as pltpu

D_MODEL = 1024
BATCH = 32
SEQ = 256
DEPTH = 2
DEC_BATCH = 2
DEC_SEQ = 1024
PAST_LEN = 512
GRID_W = 64
GRID_ROWS = DEC_SEQ // GRID_W
H_A = 4
DH_A = 64
W_A = H_A * 2 * DH_A
H_B = 8
P_B = 64
G_B = 2
N_B = 64
DI_B = H_B * P_B
CONV_K = 5
CONV_DIM = DI_B + 2 * G_B * N_B
SSD_CHUNK = 128
H_C = 8
DH_C = 64
W_C = H_C * DH_C
NA_KH = 8
NA_KW = 16
N_BRANCH = 3
ROPE_BASE = 10000.0
EPS = 1e-6

LANES = 128
HALF = LANES // 2
VMEM_LIMIT = 56 * 1024 * 1024

BC_DIM = CONV_DIM - DI_B
KVC_COLS = 2 * W_C
MERGE_COLS = N_BRANCH * D_MODEL
_SRC = dict(qa=0, ka=512, va=1024, ga=1536, z=2048, xs=2560, bc=3072, dt=3328, qc=3344, kc=3856, vc=4368,
            gc=4880, merge=5392)
COL_QA = 0
COL_GA = 512
COL_QC = 1024
COL_GC = 1536
COL_KA = 2048
COL_VA = 2560
PROJ_COLS = 3072
PROJ_SEGMENTS = ((COL_QA, _SRC["qa"]), (COL_GA, _SRC["ga"]), (COL_QC, _SRC["qc"]), (COL_GC, _SRC["gc"]),
                 (COL_KA, _SRC["ka"]), (COL_VA, _SRC["va"]))
P_Z = 0
P_XS = 512
P_BC = 1024
P_DT = 1280
P_COLS = P_DT + LANES
SIDE_TN = 256

NA_QROWS = 4
NA_WROWS = 12
NA_TILES = 2 * NA_KH
NEG_INF = float("-inf")
LOG2E = math.log2(math.e)
ATTN_AHEAD = 2
ATTN_AHEAD_CTX = 3
SSD_AHEAD = 2
CTX_BATCHES_PER_STEP = 2
ATTN_BATCHES_PER_STEP = 4
HI = lax.Precision.HIGHEST
F32 = jnp.float32
BF16 = jnp.bfloat16


def _dot(a, b, precision=None):
    return jnp.dot(a, b, preferred_element_type=F32, precision=precision)


def _dot_nt(a, b):
    return lax.dot_general(a, b, (((1,), (1,)), ((), ())), preferred_element_type=F32)


def _sigmoid(x):
    return 1.0 / (1.0 + jnp.exp(-x))


def _silu(x):
    return x * _sigmoid(x)


def _lane(shape):
    return lax.broadcasted_iota(jnp.int32, shape, len(shape) - 1)


def _params(*sem):
    return pltpu.CompilerParams(dimension_semantics=sem, vmem_limit_bytes=VMEM_LIMIT)


def _ada_kernel(cvt_ref, w_ref, b_ref, o_ref):
    n_rows = 1 + DEC_BATCH
    s = _silu(cvt_ref[...])
    accs = [jnp.zeros((8, w_ref.shape[1]), F32)] * n_rows
    for k0 in range(0, D_MODEL, 8):
        w = w_ref[k0:k0 + 8, :]
        accs = [acc + w * s[k0:k0 + 8, r:r + 1] for r, acc in enumerate(accs)]
    rows = [jnp.sum(acc, axis=0, keepdims=True) for acc in accs]
    rows.append(jnp.zeros((8 - n_rows, w_ref.shape[1]), F32))
    o_ref[...] = jnp.concatenate(rows, axis=0) + b_ref[...]


def _ada_call(cvecs, w_ada, b_ada):
    tn = 1536
    return pl.pallas_call(
        _ada_kernel,
        grid=(DEPTH, 3 * D_MODEL // tn),
        in_specs=[
            pl.BlockSpec((D_MODEL, 8), lambda l, j: (0, 0)),
            pl.BlockSpec((None, D_MODEL, tn), lambda l, j: (l, 0, j)),
            pl.BlockSpec((None, 1, tn), lambda l, j: (l, 0, j)),
        ],
        out_specs=pl.BlockSpec((None, 8, tn), lambda l, j: (l, 0, j)),
        out_shape=jax.ShapeDtypeStruct((DEPTH, 8, 3 * D_MODEL), F32),
        compiler_params=_params("arbitrary", "arbitrary"),
        name="ada",
    )(cvecs, w_ada, b_ada.reshape(DEPTH, 1, 3 * D_MODEL))


def _merge_rows_spec(li):
    return pl.BlockSpec((pl.Element(1), pl.Element(MERGE_COLS), pl.Element(D_MODEL)),
                        lambda i: (li, _SRC["merge"], 0), pipeline_mode=pl.Buffered(1))


def _modulated_norm(x, g_ref, ada_ref, row):
    y = x * lax.rsqrt(jnp.mean(x * x, axis=-1, keepdims=True) + EPS) * g_ref[...]
    shift = ada_ref[pl.ds(row, 1), 0:D_MODEL]
    scale = ada_ref[pl.ds(row, 1), D_MODEL:2 * D_MODEL]
    return (y * (1.0 + scale) + shift).astype(BF16)


def _diff_lambda_in_kernel(lam_ref, lam_init):
    v = lam_ref[...]
    l1 = jnp.sum(v[0:1] * v[1:2], axis=-1, keepdims=True)
    l2 = jnp.sum(v[2:3] * v[3:4], axis=-1, keepdims=True)
    return jnp.exp(l1) - jnp.exp(l2) + lam_init


def _split_halves(x, scale):
    lo = _lane(x.shape) < HALF
    xs = x * (scale * LOG2E)
    return jnp.concatenate([jnp.where(lo, xs, 0.0), jnp.where(lo, 0.0, xs)], axis=0).astype(BF16)


def _diff_combine(o2, rsum, lam, t):
    return o2[:t] * rsum[:t] - (lam * rsum[t:]) * o2[t:]


def _diff_head_post(o, subln_g, lam_init, gate):
    o = o * lax.rsqrt(jnp.mean(o * o, axis=-1, keepdims=True) + EPS) * (subln_g * (1.0 - lam_init))
    return (o * _silu(gate)).astype(BF16)


def _attn_a_ctx_kernel(q_ref, k_ref, v_ref, g_ref, lam_ref, sg_ref, o_ref, *, lam_init):
    t = SEQ
    lam = _diff_lambda_in_kernel(lam_ref, lam_init)
    ones = jnp.ones((t, LANES), BF16)
    blocks = [(b, h) for b in range(q_ref.shape[0] // t) for h in range(H_A)]
    where = lambda b, h: (slice(b * t, (b + 1) * t), slice(h * LANES, (h + 1) * LANES))

    def scores(b, h):
        qq = _split_halves(q_ref[where(b, h)], DH_A ** -0.5)
        return _dot_nt(qq, k_ref[where(b, h)].astype(BF16))

    pending = [scores(*blk) for blk in blocks[:ATTN_AHEAD_CTX]]
    for n, blk in enumerate(blocks):
        s = pending.pop(0)
        if n + ATTN_AHEAD_CTX < len(blocks):
            pending.append(scores(*blocks[n + ATTN_AHEAD_CTX]))
        e = jnp.exp2(s - jnp.max(s, axis=-1, keepdims=True)).astype(BF16)
        rsum = 1.0 / _dot(e, ones)
        o = _diff_combine(_dot(e, v_ref[where(*blk)].astype(BF16)), rsum, lam, t)
        o_ref[where(*blk)] = _diff_head_post(o, sg_ref[...], lam_init, g_ref[where(*blk)])


def _attn_a_ctx_call(proj, lamvec, subln_g, lam_init):
    rows = ATTN_BATCHES_PER_STEP * SEQ
    blk = lambda c: pl.BlockSpec((rows, W_A), lambda b: (b, c // W_A))
    return pl.pallas_call(
        functools.partial(_attn_a_ctx_kernel, lam_init=lam_init),
        grid=(BATCH // ATTN_BATCHES_PER_STEP,),
        in_specs=[blk(COL_QA), blk(COL_KA), blk(COL_VA), blk(COL_GA),
                  pl.BlockSpec((4, LANES), lambda b: (0, 0)),
                  pl.BlockSpec((1, LANES), lambda b: (0, 0))],
        out_specs=pl.BlockSpec((rows, W_A), lambda b: (b, 0)),
        out_shape=jax.ShapeDtypeStruct((BATCH * SEQ, W_A), BF16),
        compiler_params=_params("arbitrary"),
        name="attn_a_ctx",
    )(proj, proj, proj, proj, lamvec, subln_g)


def _rope(x, cos, sin_signed):
    first = (_lane(x.shape) % 32) < 16
    swapped = jnp.where(first, pltpu.roll(x, LANES - 16, 1), pltpu.roll(x, 16, 1))
    return x * cos + swapped * sin_signed


def _attn_a_lat_kernel(q_ref, k_ref, v_ref, g_ref, ck_ref, cv_ref, cosq_ref, sinq_ref, cosk_ref, sink_ref,
                       lam_ref, sg_ref, o_ref, kr_s, *, lam_init):
    tq = q_ref.shape[0]

    @pl.when(pl.program_id(1) == 0)
    def _():
        for h in range(H_A):
            sl = slice(h * LANES, (h + 1) * LANES)
            kr_s[:, sl] = _rope(k_ref[:, sl], cosk_ref[...], sink_ref[...]).astype(BF16)

    lam = _diff_lambda_in_kernel(lam_ref, lam_init)

    def scores(h):
        sl = slice(h * LANES, (h + 1) * LANES)
        qq = _split_halves(_rope(q_ref[:, sl], cosq_ref[...], sinq_ref[...]), DH_A ** -0.5)
        return _dot_nt(qq, kr_s[:, sl]), _dot_nt(qq, ck_ref[:, sl].astype(BF16))

    pending = [scores(h) for h in range(ATTN_AHEAD)]
    for h in range(H_A):
        sl = slice(h * LANES, (h + 1) * LANES)
        s_lat, s_ctx = pending.pop(0)
        if h + ATTN_AHEAD < H_A:
            pending.append(scores(h + ATTN_AHEAD))
        m = jnp.maximum(jnp.max(s_lat, axis=-1, keepdims=True), jnp.max(s_ctx, axis=-1, keepdims=True))
        e_lat = jnp.exp2(s_lat - m)
        e_ctx = jnp.exp2(s_ctx - m)
        rsum = 1.0 / (jnp.sum(e_lat, axis=-1, keepdims=True) + jnp.sum(e_ctx, axis=-1, keepdims=True))
        o2 = _dot(e_lat.astype(BF16), v_ref[:, sl].astype(BF16)) + _dot(e_ctx.astype(BF16),
                                                                         cv_ref[:, sl].astype(BF16))
        o = _diff_combine(o2, rsum, lam, tq)
        o_ref[:, sl] = _diff_head_post(o, sg_ref[...], lam_init, g_ref[:, sl])


def _attn_a_lat_call(proj, cache_k, cache_v, li, cos_t, sin_t, lamvec, subln_g, lam_init):
    tq = 256
    nq = DEC_SEQ // tq
    qblk = lambda c: pl.BlockSpec((tq, W_A), lambda b, i: (b * nq + i, c // W_A))
    full = lambda c: pl.BlockSpec((DEC_SEQ, W_A), lambda b, i: (b, c // W_A))
    cache = pl.BlockSpec((None, None, PAST_LEN, W_A), lambda b, i: (b, li, 0, 0))
    return pl.pallas_call(
        functools.partial(_attn_a_lat_kernel, lam_init=lam_init),
        grid=(DEC_BATCH, nq),
        in_specs=[qblk(COL_QA), full(COL_KA), full(COL_VA), qblk(COL_GA), cache, cache,
                  pl.BlockSpec((tq, LANES), lambda b, i: (i, 0)),
                  pl.BlockSpec((tq, LANES), lambda b, i: (i, 0)),
                  pl.BlockSpec((DEC_SEQ, LANES), lambda b, i: (0, 0)),
                  pl.BlockSpec((DEC_SEQ, LANES), lambda b, i: (0, 0)),
                  pl.BlockSpec((4, LANES), lambda b, i: (0, 0)),
                  pl.BlockSpec((1, LANES), lambda b, i: (0, 0))],
        out_specs=pl.BlockSpec((tq, W_A), lambda b, i: (b * nq + i, 0)),
        out_shape=jax.ShapeDtypeStruct((DEC_BATCH * DEC_SEQ, W_A), BF16),
        scratch_shapes=[pltpu.VMEM((DEC_SEQ, W_A), BF16)],
        compiler_params=_params("arbitrary", "arbitrary"),
        name="attn_a_lat",
    )(proj, proj, proj, proj, cache_k, cache_v, cos_t, sin_t, cos_t, sin_t, lamvec, subln_g)


def _merge_halves(o, t):
    return jnp.where(_lane((t, LANES)) < HALF, o[:t], o[t:])


def _spread(work, n_slots):
    state = dict(slot=0, done=0)

    def emit_until(target):
        while state["done"] < target:
            work[state["done"]]()
            state["done"] += 1

    def side():
        state["slot"] += 1
        emit_until(min(len(work), -(-state["slot"] * len(work) // n_slots)))

    return side, lambda: emit_until(len(work))


def _attn_c_ctx_body(q_ref, kt_ref, vt_ref, g_ref, o_ref, side):
    t = SEQ
    blocks = [(b, j) for b in range(q_ref.shape[0] // t) for j in range(H_C // 2)]
    where = lambda b, j: (slice(b * t, (b + 1) * t), slice(j * LANES, (j + 1) * LANES))
    pair_t = lambda ref, b, j: ref[b, 2 * j:2 * j + 2].reshape(LANES, t).astype(BF16)

    def scores(b, j):
        return _dot(_split_halves(q_ref[where(b, j)], DH_C ** -0.5), pair_t(kt_ref, b, j))

    pending = [scores(*blk) for blk in blocks[:ATTN_AHEAD_CTX]]
    for n, blk in enumerate(blocks):
        s = pending.pop(0)
        if n + ATTN_AHEAD_CTX < len(blocks):
            pending.append(scores(*blocks[n + ATTN_AHEAD_CTX]))
        e = jnp.exp2(s - jnp.max(s, axis=-1, keepdims=True))
        rsum = 1.0 / jnp.sum(e, axis=-1, keepdims=True)
        o = _merge_halves(_dot_nt(e.astype(BF16), pair_t(vt_ref, *blk)) * rsum, t)
        o_ref[where(*blk)] = (o * _silu(g_ref[where(*blk)])).astype(BF16)
        side()


def _post_ctx_kernel(q_ref, kt_ref, vt_ref, gc_ref, x_ref, ya_ref, yb_ref, ada_ref, g_ref, wm_ref, wa_ref, wb_ref,
                     wc_ref, wo_ref, fg_ref, o_ref, yc_s, h_s, acc_s, sc_s, *, final):
    d = D_MODEL
    gate = ada_ref[0:1, 2 * d:3 * d]
    h_s[...] = _modulated_norm(x_ref[...], g_ref, ada_ref, 0)

    def gate_logits(n, c0):
        r0 = n * d + c0
        return _dot_nt(h_s[...], wm_ref[0, r0:r0 + SIDE_TN, :])

    work = []
    for c0 in range(0, d, SIDE_TN):
        cols = slice(c0, c0 + SIDE_TN)

        def branch_a(c0=c0, cols=cols):
            acc_s[:, cols] = _sigmoid(gate_logits(0, c0)) * _dot(ya_ref[...], wa_ref[:, cols])

        def branch_b(c0=c0, cols=cols):
            acc_s[:, cols] += _sigmoid(gate_logits(1, c0)) * _dot(yb_ref[...], wb_ref[:, cols])

        def gate_c(c0=c0, cols=cols):
            sc_s[:, cols] = _sigmoid(gate_logits(2, c0))

        work += [branch_a, branch_b, gate_c]
    side, flush = _spread(work, (q_ref.shape[0] // SEQ) * (H_C // 2))
    _attn_c_ctx_body(q_ref, kt_ref, vt_ref, gc_ref, yc_s, side)
    flush()
    merged = acc_s[...] + sc_s[...] * _dot(yc_s[...], wc_ref[...])
    x = x_ref[...] + gate * _dot(merged.astype(BF16), wo_ref[...])
    if final:
        x = x * lax.rsqrt(jnp.mean(x * x, axis=-1, keepdims=True) + EPS) * fg_ref[...]
    o_ref[...] = x


def _post_ctx_call(x, proj, kc_t, vc_t, ya, yb, ada, norm_g, w_merge, li, wa, wb, wc, wo, final_g, *, final):
    nb = CTX_BATCHES_PER_STEP
    tm = nb * SEQ
    tok = lambda w: pl.BlockSpec((tm, w), lambda i: (i, 0))
    col = lambda c: pl.BlockSpec((tm, W_C), lambda i: (i, c // W_C))
    cache = pl.BlockSpec((nb, None, H_C, DH_C, SEQ), lambda i: (i, li, 0, 0, 0))
    layer = lambda *shape, **kw: pl.BlockSpec((None,) + shape, lambda i: (li,) + (0,) * len(shape), **kw)
    once = dict(pipeline_mode=pl.Buffered(1))
    return pl.pallas_call(
        functools.partial(_post_ctx_kernel, final=final),
        grid=(BATCH // nb,),
        in_specs=[col(COL_QC), cache, cache, col(COL_GC), tok(D_MODEL), tok(W_A), tok(DI_B),
                  layer(8, 3 * D_MODEL), layer(1, D_MODEL), _merge_rows_spec(li),
                  layer(W_A, D_MODEL, **once), layer(DI_B, D_MODEL, **once), layer(W_C, D_MODEL, **once),
                  layer(D_MODEL, D_MODEL, **once), pl.BlockSpec((1, D_MODEL), lambda i: (0, 0))],
        out_specs=tok(D_MODEL),
        out_shape=jax.ShapeDtypeStruct((BATCH * SEQ, D_MODEL), F32),
        scratch_shapes=[pltpu.VMEM((tm, W_C), BF16), pltpu.VMEM((tm, D_MODEL), BF16),
                        pltpu.VMEM((tm, D_MODEL), F32), pltpu.VMEM((tm, D_MODEL), F32)],
        compiler_params=_params("arbitrary"),
        name="attn_c_post_ctx_final" if final else "attn_c_post_ctx",
    )(proj, kc_t, vc_t, proj, x, ya, yb, ada, norm_g, w_merge, wa, wb, wc, wo, final_g)


def _rpb_kernel(rpb_ref, o_ref):
    shape = (GRID_W, LANES)
    c = lax.broadcasted_iota(jnp.int32, shape, 0)
    cp = _lane(shape) % GRID_W
    start = jnp.clip(c - NA_KW // 2, 0, GRID_W - NA_KW)
    in_win = (cp >= start) & (cp < start + NA_KW)
    for h in range(H_C):
        o_ref[h, 0] = jnp.full(shape, NEG_INF, F32)
        for dr in range(2 * NA_KH - 1):
            row = jnp.broadcast_to(rpb_ref[h, dr:dr + 1, :], shape)
            tile = pltpu.roll(row, LANES - (NA_KW - 1), 1, stride=1, stride_axis=0)
            o_ref[h, 1 + dr] = jnp.where(in_win, tile * LOG2E, NEG_INF)


def _rpb_call(rpb):
    n_dc = 2 * NA_KW - 1
    v = jnp.pad(rpb, ((0, 0), (0, 0), (0, NA_TILES - (2 * NA_KH - 1)), (0, GRID_W - n_dc)))
    v = jnp.concatenate([v] * (LANES // GRID_W), axis=-1)
    return pl.pallas_call(
        _rpb_kernel,
        grid=(DEPTH,),
        in_specs=[pl.BlockSpec((None, H_C, NA_TILES, LANES), lambda l: (l, 0, 0, 0))],
        out_specs=pl.BlockSpec((None, H_C, NA_TILES, GRID_W, LANES), lambda l: (l, 0, 0, 0, 0)),
        out_shape=jax.ShapeDtypeStruct((DEPTH, H_C, NA_TILES, GRID_W, LANES), F32),
        compiler_params=_params("arbitrary"),
        name="rpb_tiles",
    )(v)


def _attn_c_lat_kernel(q_ref, k_ref, v_ref, g_ref, ck_ref, cv_ref, tile_ref, o_ref, bias_s):
    tq = q_ref.shape[0]
    m = pl.program_id(1)
    n_blocks = GRID_ROWS // NA_QROWS
    lo = _lane((GRID_W, LANES)) < HALF
    n_pair = H_C // 2

    def run(wrows, w0):
        nwin = wrows * GRID_W
        k0 = pl.multiple_of(w0 * GRID_W, GRID_W)

        def scores(j):
            sl = slice(j * LANES, (j + 1) * LANES)
            for s in range(2):
                for i in range(NA_QROWS):
                    r = m * NA_QROWS + i
                    start = jnp.clip(r - NA_KH // 2, 0, GRID_ROWS - NA_KH)
                    for jp in range(wrows // 2):
                        idx = []
                        for u in range(2):
                            rk = w0 + 2 * jp + u
                            valid = (rk >= start) & (rk < start + NA_KH)
                            idx.append(jnp.where(valid, rk - r + NA_KH, 0))
                        tile = jnp.where(lo, tile_ref[2 * j + s, idx[0]], tile_ref[2 * j + s, idx[1]])
                        bias_s[(s * NA_QROWS + i) * GRID_W:(s * NA_QROWS + i + 1) * GRID_W,
                               jp * LANES:(jp + 1) * LANES] = tile
            qq = _split_halves(q_ref[:, sl], DH_C ** -0.5)
            kw = k_ref[pl.ds(k0, nwin), sl].astype(BF16)
            s_win = _dot_nt(qq, kw) + bias_s[:, 0:nwin]
            ckt = ck_ref[2 * j:2 * j + 2].reshape(LANES, PAST_LEN).astype(BF16)
            return s_win, _dot(qq, ckt)

        pending = [scores(j) for j in range(ATTN_AHEAD)]
        for j in range(n_pair):
            sl = slice(j * LANES, (j + 1) * LANES)
            s_win, s_ctx = pending.pop(0)
            if j + ATTN_AHEAD < n_pair:
                pending.append(scores(j + ATTN_AHEAD))
            vw = v_ref[pl.ds(k0, nwin), sl].astype(BF16)
            mx = jnp.maximum(jnp.max(s_win, axis=-1, keepdims=True), jnp.max(s_ctx, axis=-1, keepdims=True))
            e_win = jnp.exp2(s_win - mx)
            e_ctx = jnp.exp2(s_ctx - mx)
            rs = 1.0 / (jnp.sum(e_win, axis=-1, keepdims=True) + jnp.sum(e_ctx, axis=-1, keepdims=True))
            cvt = cv_ref[2 * j:2 * j + 2].reshape(LANES, PAST_LEN).astype(BF16)
            o = (_dot(e_win.astype(BF16), vw) + _dot_nt(e_ctx.astype(BF16), cvt)) * rs
            o_ref[:, sl] = (_merge_halves(o, tq) * _silu(g_ref[:, sl])).astype(BF16)

    is_edge = (m == 0) | (m == n_blocks - 1)
    pl.when(is_edge)(lambda: run(NA_KH, jnp.where(m == 0, 0, GRID_ROWS - NA_KH)))
    pl.when(jnp.logical_not(is_edge))(
        lambda: run(NA_WROWS, jnp.where(m < n_blocks // 2, 0, GRID_ROWS - NA_WROWS)))


def _attn_c_lat_call(proj, kv, cache_k, cache_v, li, tiles):
    tq = NA_QROWS * GRID_W
    nq = DEC_SEQ // tq
    qblk = lambda c: pl.BlockSpec((tq, W_C), lambda b, i: (b * nq + i, c // W_C))
    full = lambda c: pl.BlockSpec((DEC_SEQ, W_C), lambda b, i: (b, c // W_C))
    cache = pl.BlockSpec((None, None, H_C, DH_C, PAST_LEN), lambda b, i: (b, li, 0, 0, 0))
    return pl.pallas_call(
        _attn_c_lat_kernel,
        grid=(DEC_BATCH, nq),
        in_specs=[qblk(COL_QC), full(0), full(W_C), qblk(COL_GC), cache, cache,
                  pl.BlockSpec((None, H_C, NA_TILES, GRID_W, LANES), lambda b, i: (li, 0, 0, 0, 0))],
        out_specs=pl.BlockSpec((tq, W_C), lambda b, i: (b * nq + i, 0)),
        out_shape=jax.ShapeDtypeStruct((DEC_BATCH * DEC_SEQ, W_C), BF16),
        scratch_shapes=[pltpu.VMEM((2 * tq, NA_WROWS * GRID_W), F32)],
        compiler_params=_params("arbitrary", "arbitrary"),
        name="attn_c_lat",
    )(proj, kv, kv, proj, cache_k, cache_v, tiles)


def _ssd_body(dt_ref, xs_ref, bc_ref, z_ref, h0_ref, params, y_ref, hs_ref, scratch, *, seq, static_loops, side):
    cw_ref, cb_ref, dtb_ref, alog_ref, dsk_ref, g_ref = params
    upad_s, xc_s, expo_s, expot_s, dtt_s, tot_s, bmt_s, yf_s, yb_s, st_s = scratch
    use_h0 = h0_ref is not None
    want_state = hs_ref is not None

    q = SSD_CHUNK
    nc = seq // q
    n_pair = H_B // 2
    n_hd = 2 * H_B
    pad = 8

    def loop(body, unroll=1):
        if static_loops:
            for c in range(nc):
                body(c, 0)
        else:
            lax.fori_loop(0, nc, body, 0, unroll=unroll)

    def chunk_rows(c):
        return slice(c * q, (c + 1) * q) if isinstance(c, int) else pl.ds(pl.multiple_of(c * q, q), q)

    upad_s[0:pad, :] = jnp.zeros((pad, CONV_DIM), F32)
    upad_s[pad + seq:2 * pad + seq, :] = jnp.zeros((pad, CONV_DIM), F32)
    upad_s[pad:pad + seq, 0:DI_B] = xs_ref[...]
    upad_s[pad:pad + seq, DI_B:CONV_DIM] = bc_ref[...]

    for c in range(nc):
        for cb_ in range(CONV_DIM // LANES):
            csl = slice(cb_ * LANES, (cb_ + 1) * LANES)
            acc = jnp.zeros((q, LANES), F32) + cb_ref[:, csl]
            for k in range(CONV_K):
                r0 = c * q + pad - CONV_K // 2 + k
                acc = acc + upad_s[r0:r0 + q, csl] * cw_ref[k:k + 1, csl]
            xc_s[c * q:(c + 1) * q, csl] = _silu(acc)
            side()

    a_row = -jnp.exp(alog_ref[...]) * LOG2E
    a_col = jnp.broadcast_to(a_row, (LANES, LANES)).T[0:n_hd, 0:1]
    ri = lax.broadcasted_iota(jnp.int32, (q, q), 0)
    ci = lax.broadcasted_iota(jnp.int32, (q, q), 1)
    ltri = (ri >= ci).astype(F32)
    fwd_lane = _lane((q, LANES)) < H_B
    fwd_row = lax.broadcasted_iota(jnp.int32, (n_hd, q), 0) < H_B

    def prep_body(c, carry):
        rows = chunk_rows(c)
        xdt = dt_ref[rows, 0:LANES] + dtb_ref[...]
        dtv = jnp.maximum(xdt, 0.0) + jnp.log1p(jnp.exp(-jnp.abs(xdt)))
        la = dtv * a_row
        acum = _dot(ltri, la, HI)
        expo_s[rows, :] = jnp.where(fwd_lane, acum, la - acum)
        acum_t = acum.T[0:n_hd, :]
        dt_t = dtv.T[0:n_hd, :]
        expot_s[c] = jnp.where(fwd_row, acum_t, dt_t * a_col - acum_t)
        dtt_s[c] = dt_t
        tot_s[c] = jnp.broadcast_to(acum_t[:, q - 1:q], (n_hd, q))
        bmt_s[c] = xc_s[rows, DI_B:DI_B + LANES].T
        side()
        return carry

    loop(prep_body, unroll=2)

    if use_h0:
        st_s[...] = h0_ref[...].reshape(2, n_pair, N_B, LANES)
    else:
        st_s[...] = jnp.zeros_like(st_s)

    lane_q = _lane((q, LANES))
    lo = lane_q < HALF
    lo_st = _lane((N_B, LANES)) < HALF

    def chunk_pair(c_fwd, c_bwd):
        dirs = ((0, c_fwd, yf_s), (1, c_bwd, yb_s))
        group_of = lambda k: k * G_B // n_pair
        items = [(k, d) for k in range(n_pair) for d in dirs]
        cb, y_off, st_in = {}, {}, {}

        def issue_early(k, d):
            dirn, c, _ = d
            g, rows = group_of(k), chunk_rows(c)
            in_g = (lane_q >= g * N_B) & (lane_q < (g + 1) * N_B)
            cmg = jnp.where(in_g, xc_s[rows, DI_B + LANES:DI_B + 2 * LANES], 0.0).astype(BF16)
            if (dirn, g) not in cb:
                cb[dirn, g] = _dot_nt(cmg, xc_s[rows, DI_B:DI_B + LANES].astype(BF16))
            st_in[dirn, k] = st_s[dirn, k]
            y_off[dirn, k] = _dot(cmg, jnp.concatenate([st_in[dirn, k]] * 2, axis=0).astype(BF16))

        for item in items[:SSD_AHEAD]:
            issue_early(*item)
        for n, (k, (dirn, c, y_s)) in enumerate(items):
            if n + SSD_AHEAD < len(items):
                issue_early(*items[n + SSD_AHEAD])
            psl = slice(k * LANES, (k + 1) * LANES)
            rows = chunk_rows(c)
            tri = (ri >= ci) if dirn == 0 else (ci >= ri)
            bmt_g = bmt_s[c, group_of(k) * N_B:(group_of(k) + 1) * N_B, :]
            x16 = xc_s[rows, psl].astype(BF16)
            mats, lhs, ysc, cdec = [], [], [], []
            for s in range(2):
                col = dirn * H_B + 2 * k + s
                e_col = jnp.broadcast_to(expo_s[rows, col:col + 1], (q, q))
                e_row = expot_s[c, col:col + 1, :]
                dt_row = dtt_s[c, col:col + 1, :]
                tot = tot_s[c, col:col + 1, :]
                dec = jnp.exp2(jnp.where(tri, e_col - e_row, NEG_INF))
                mats.append((cb[dirn, group_of(k)] * dec * dt_row).astype(BF16))
                if dirn == 0:
                    ysc.append(jnp.exp2(e_col))
                    w_row = jnp.exp2(tot - e_row)
                else:
                    ysc.append(jnp.exp2(e_col + tot))
                    w_row = jnp.exp2(-e_row)
                lhs.append((bmt_g * (w_row * dt_row)).astype(BF16))
                cdec.append(jnp.exp2(tot[:, 0:LANES]))
            yd = _dot(jnp.concatenate(mats, axis=0), x16)
            ds = _dot(jnp.concatenate(lhs, axis=0), x16)
            yo, st = y_off.pop((dirn, k)), st_in.pop((dirn, k))
            y_s[rows, psl] = jnp.where(lo, yd[:q] + ysc[0] * yo, yd[q:] + ysc[1] * yo)
            st_s[dirn, k] = jnp.where(lo_st, cdec[0] * st + ds[:N_B], cdec[1] * st + ds[N_B:])
            side()

    def body(c, carry):
        chunk_pair(c, nc - 1 - c)
        return carry

    loop(body, unroll=2)

    dsum = dsk_ref[0:1, :] + dsk_ref[1:2, :]

    def out_body(c, carry):
        rows = chunk_rows(c)
        y = yf_s[rows, :] + yb_s[rows, :] + xc_s[rows, 0:DI_B] * dsum
        y = y * _silu(z_ref[rows, :])
        y = y * lax.rsqrt(jnp.mean(y * y, axis=-1, keepdims=True) + EPS) * g_ref[...]
        y_ref[rows, :] = y.astype(BF16)
        side()
        return carry

    loop(out_body)
    if want_state:
        for dirn in range(2):
            for k in range(n_pair):
                st = st_s[dirn, k]
                st_t = jnp.concatenate([st, st], axis=0).T
                for s in range(2):
                    hs_ref[dirn, 2 * k + s] = st_t[s * P_B:(s + 1) * P_B, 0:N_B]


def _ssd_scratch(seq):
    nc = seq // SSD_CHUNK
    per_chunk_rows = pltpu.VMEM((nc, 2 * H_B, SSD_CHUNK), F32)
    return [pltpu.VMEM((seq + 16, CONV_DIM), F32), pltpu.VMEM((seq, CONV_DIM), F32),
            pltpu.VMEM((seq, LANES), F32), per_chunk_rows, per_chunk_rows, per_chunk_rows,
            pltpu.VMEM((nc, LANES, SSD_CHUNK), F32),
            pltpu.VMEM((seq, DI_B), F32), pltpu.VMEM((seq, DI_B), F32),
            pltpu.VMEM((2, H_B // 2, N_B, LANES), F32)]


def _ssd_param_specs(const):
    return [const((CONV_K, CONV_DIM)), const((1, CONV_DIM)), const((1, LANES)), const((1, LANES)),
            const((2, DI_B)), const((1, DI_B))]


def _proj_ssd_kernel(*refs, nb, seq, ctx, n_carry):
    x_ref, ada_ref, g_ref, w_ref, wdt_ref = refs[:5]
    pos = 5
    h0_ref = None
    if not ctx:
        h0_ref = refs[pos]
        pos += 1
    params = refs[pos:pos + 6]
    outs = refs[pos + 6 + n_carry:]
    if ctx:
        proj_ref, ka_ref, va_ref, kc_ref, vc_ref, y_ref, hs_ref = outs[:7]
        outs = outs[7:]
    else:
        proj_ref, kv_ref, y_ref = outs[:3]
        hs_ref = None
        outs = outs[3:]
    h_s, p_s = outs[:2]
    scratch = outs[2:]

    row = 0 if ctx else 1 + pl.program_id(0)
    for r0 in range(0, nb * seq, SEQ):
        h_s[r0:r0 + SEQ, :] = _modulated_norm(x_ref[r0:r0 + SEQ, :], g_ref, ada_ref, row)
    for c0 in range(P_Z, P_DT, SIDE_TN):
        c1 = min(c0 + SIDE_TN, P_DT)
        p_s[:, c0:c1] = _dot_nt(h_s[...], w_ref[_SRC["z"] + c0:_SRC["z"] + c1, :])
    p_s[:, P_DT:P_COLS] = _dot_nt(h_s[...], wdt_ref[...])

    work = []
    for col, src in PROJ_SEGMENTS:
        for off in range(0, W_A, SIDE_TN):
            def tile(d=col + off, s=src + off):
                proj_ref[:, d:d + SIDE_TN] = _dot_nt(h_s[...], w_ref[s:s + SIDE_TN, :])
            work.append(tile)
            if ctx and col in (COL_KA, COL_VA):
                def store(dst=ka_ref if col == COL_KA else va_ref, col=col, off=off):
                    for b in range(nb):
                        for h in range(off // LANES, (off + SIDE_TN) // LANES):
                            dst[b, :, h, :] = proj_ref[b * seq:(b + 1) * seq, col + h * LANES:col + (h + 1) * LANES]
                work.append(store)
    for r0 in range(0, KVC_COLS, SIDE_TN):
        if ctx:
            for b in range(nb):
                def tile_t(b=b, r0=r0):
                    w_rows = w_ref[_SRC["kc"] + r0:_SRC["kc"] + r0 + SIDE_TN, :]
                    kv_t = _dot_nt(w_rows, h_s[b * seq:(b + 1) * seq, :])
                    dst, d0 = (kc_ref, r0) if r0 < W_C else (vc_ref, r0 - W_C)
                    dst[b, d0 // DH_C:(d0 + SIDE_TN) // DH_C] = kv_t.reshape(SIDE_TN // DH_C, DH_C, seq)
                work.append(tile_t)
        else:
            def tile_kv(r0=r0):
                kv_ref[:, r0:r0 + SIDE_TN] = _dot_nt(h_s[...], w_ref[_SRC["kc"] + r0:_SRC["kc"] + r0 + SIDE_TN, :])
            work.append(tile_kv)

    n_slots = nb * (seq // SSD_CHUNK) * (CONV_DIM // LANES + 2 + 2 * (H_B // 2))
    side, flush = _spread(work, n_slots)
    for b in range(nb):
        rows = pl.ds(b * seq, seq)
        _ssd_body(p_s.at[rows, pl.ds(P_DT, LANES)], p_s.at[rows, pl.ds(P_XS, DI_B)],
                  p_s.at[rows, pl.ds(P_BC, BC_DIM)], p_s.at[rows, pl.ds(P_Z, DI_B)], h0_ref, params,
                  y_ref.at[rows, :], None if hs_ref is None else hs_ref.at[b], scratch,
                  seq=seq, static_loops=True, side=side)
    flush()


def _proj_ssd_call(x, ada, norm_g, w16, w_dt, ssd_w, li, *, ctx, carry=None, h0t=None):
    nb, seq = (CTX_BATCHES_PER_STEP, SEQ) if ctx else (1, DEC_SEQ)
    tm = nb * seq
    t = x.shape[0]
    n_carry = 0 if carry is None else len(carry)
    const = lambda shape: pl.BlockSpec(shape, lambda i: (0,) * len(shape))
    once = pl.Buffered(1)
    big = {} if ctx else dict(pipeline_mode=pl.Buffered(1))
    in_specs = [pl.BlockSpec((tm, D_MODEL), lambda i: (i, 0), **big),
                pl.BlockSpec((None, 8, 3 * D_MODEL), lambda i: (li, 0, 0)),
                pl.BlockSpec((None, 1, D_MODEL), lambda i: (li, 0, 0)),
                pl.BlockSpec((None, _SRC["merge"], D_MODEL), lambda i: (li, 0, 0), pipeline_mode=once),
                pl.BlockSpec((None, LANES, D_MODEL), lambda i: (li, 0, 0))]
    args = [x, ada, norm_g, w16, w_dt]
    if not ctx:
        in_specs.append(pl.BlockSpec((None, None, 2, (H_B // 2) * N_B, LANES), lambda i: (i, li, 0, 0, 0)))
        args.append(h0t)
    in_specs += _ssd_param_specs(const)
    args += list(ssd_w)
    out_specs = [pl.BlockSpec((tm, PROJ_COLS), lambda i: (i, 0), **big)]
    out_shape = [jax.ShapeDtypeStruct((t, PROJ_COLS), F32)]
    aliases = {}
    if ctx:
        out_specs += [pl.BlockSpec((nb, None, SEQ, H_A, 2 * DH_A), lambda i: (i, li, 0, 0, 0))] * 2
        out_specs += [pl.BlockSpec((nb, None, H_C, DH_C, SEQ), lambda i: (i, li, 0, 0, 0))] * 2
        out_shape += [jax.ShapeDtypeStruct((BATCH, DEPTH, SEQ, H_A, 2 * DH_A), F32)] * 2
        out_shape += [jax.ShapeDtypeStruct((BATCH, DEPTH, H_C, DH_C, SEQ), F32)] * 2
    else:
        out_specs.append(pl.BlockSpec((tm, KVC_COLS), lambda i: (i, 0), **big))
        out_shape.append(jax.ShapeDtypeStruct((t, KVC_COLS), F32))
    out_specs.append(pl.BlockSpec((tm, DI_B), lambda i: (i, 0)))
    out_shape.append(jax.ShapeDtypeStruct((t, DI_B), BF16))
    if ctx:
        out_specs.append(pl.BlockSpec((nb, None, 2, H_B, P_B, N_B), lambda i: (i, li, 0, 0, 0, 0)))
        out_shape.append(jax.ShapeDtypeStruct((BATCH, DEPTH, 2, H_B, P_B, N_B), F32))
        if carry is not None:
            in_specs += [pl.BlockSpec(memory_space=pl.ANY)] * n_carry
            aliases = {len(args) + k: (1, 2, 3, 4, 6)[k] for k in range(n_carry)}
            args += list(carry)
    scratch = [pltpu.VMEM((tm, D_MODEL), BF16), pltpu.VMEM((tm, P_COLS), F32)] + _ssd_scratch(seq)
    return pl.pallas_call(
        functools.partial(_proj_ssd_kernel, nb=nb, seq=seq, ctx=ctx, n_carry=n_carry),
        grid=(t // tm,),
        in_specs=in_specs,
        out_specs=out_specs,
        out_shape=out_shape,
        input_output_aliases=aliases,
        scratch_shapes=scratch,
        compiler_params=_params("arbitrary"),
        name="ctx_proj_ssd" if ctx else "lat_proj_ssd",
    )(*args)


def _post_kernel(x_ref, ya_ref, yb_ref, yc_ref, ada_ref, g_ref, wm_ref, wa_ref, wb_ref, wc_ref, wo_ref, fg_ref,
                 o_ref, *, tm, row_base, tokens_per_row, final):
    row = row_base + (pl.program_id(0) * tm) // tokens_per_row
    gate = ada_ref[pl.ds(row, 1), 2 * D_MODEL:3 * D_MODEL]
    d = D_MODEL
    x = x_ref[...]
    h = _modulated_norm(x, g_ref, ada_ref, row)
    merged = None
    for n, (y_ref, w_ref) in enumerate(((ya_ref, wa_ref), (yb_ref, wb_ref), (yc_ref, wc_ref))):
        logits = _dot_nt(h, wm_ref[0, n * d:(n + 1) * d, :])
        term = _sigmoid(logits) * _dot(y_ref[...], w_ref[...])
        merged = term if merged is None else merged + term
    x = x + gate * _dot(merged.astype(BF16), wo_ref[...])
    if final:
        x = x * lax.rsqrt(jnp.mean(x * x, axis=-1, keepdims=True) + EPS) * fg_ref[...]
    o_ref[...] = x


def _post_call(x, ya, yb, yc, ada, norm_g, w_merge_t, li, wa, wb, wc, wo, final_g, *, tm, row_base,
               tokens_per_row, final):
    t = x.shape[0]
    tok = lambda w: pl.BlockSpec((tm, w), lambda i: (i, 0))
    layer = lambda *shape, **kw: pl.BlockSpec((None,) + shape, lambda i: (li,) + (0,) * len(shape), **kw)
    once = dict(pipeline_mode=pl.Buffered(1))
    kern = functools.partial(_post_kernel, tm=tm, row_base=row_base, tokens_per_row=tokens_per_row, final=final)
    return pl.pallas_call(
        kern,
        grid=(t // tm,),
        in_specs=[tok(D_MODEL), tok(W_A), tok(DI_B), tok(W_C),
                  layer(8, 3 * D_MODEL), layer(1, D_MODEL), _merge_rows_spec(li),
                  layer(W_A, D_MODEL, **once), layer(DI_B, D_MODEL, **once), layer(W_C, D_MODEL, **once),
                  layer(D_MODEL, D_MODEL, **once), pl.BlockSpec((1, D_MODEL), lambda i: (0, 0))],
        out_specs=tok(D_MODEL),
        out_shape=jax.ShapeDtypeStruct((t, D_MODEL), F32),
        compiler_params=_params("arbitrary"),
        name="post_final" if final else "post",
    )(x, ya, yb, yc, ada, norm_g, w_merge_t, wa, wb, wc, wo, final_g)


def _rope_tables():
    pos = np.arange(DEC_SEQ)
    lane = np.arange(LANES)
    l64 = lane % (2 * (DH_A // 2))
    quarter = DH_A // 4
    p = np.where((l64 < DH_A // 2)[None, :], (pos // GRID_W)[:, None], (pos % GRID_W)[:, None])
    inv = ROPE_BASE ** (-np.arange(quarter, dtype=np.float64) / quarter)
    ang = p.astype(np.float64) * inv[l64 % quarter][None, :]
    sign = np.where((lane % (2 * quarter)) < quarter, -1.0, 1.0)
    return jnp.asarray(np.cos(ang), F32), jnp.asarray(np.sin(ang) * sign[None, :], F32)


def _pad_lanes(v, width=LANES):
    v = v.reshape(1, -1).astype(F32)
    return jnp.pad(v, ((0, 0), (0, width - v.shape[1])))


def kernel(x_prompt, x_sample, cache_diff_k, cache_diff_v, cache_na_k, cache_na_v, state_ssd, c, c_ctx,
           norm_g, w_ada, b_ada, w_in, lam_q1, lam_k1, lam_q2, lam_k2, diff_subln_g, conv_w, conv_b,
           dt_bias, a_log, d_skip, ssd_norm_g, na_rpb, w_br_a, w_br_b, w_br_c, w_out, final_g):
    assert x_prompt.shape == (BATCH, SEQ, D_MODEL) and x_sample.shape == (DEC_BATCH, DEC_SEQ, D_MODEL)
    assert w_in.shape == (DEPTH, D_MODEL, _SRC["merge"] + MERGE_COLS)
    w16 = jnp.swapaxes(w_in, 1, 2).astype(BF16)
    w_dt =jnp.pad(w16[:, _SRC["dt"]:_SRC["qc"], :], ((0, 0), (0, LANES - 2 * H_B), (0, 0)))
    wa16, wb16, wc16, wo16 = (w.astype(BF16) for w in (w_br_a, w_br_b, w_br_c, w_out))

    cvecs = jnp.concatenate([c_ctx[None, :], c, jnp.zeros((8 - 1 - DEC_BATCH, D_MODEL), F32)], axis=0)
    ada = _ada_call(cvecs.T, w_ada, b_ada)
    cos_t, sin_t = _rope_tables()

    ck_a = cache_diff_k.reshape(DEC_BATCH, DEPTH, PAST_LEN, W_A)
    cv_a = cache_diff_v.reshape(DEC_BATCH, DEPTH, PAST_LEN, W_A)
    ck_c = cache_na_k.transpose(0, 1, 3, 4, 2)
    cv_c = cache_na_v.transpose(0, 1, 3, 4, 2)
    na_tiles = _rpb_call(na_rpb)
    h0t = state_ssd.transpose(0, 1, 2, 5, 3, 4).reshape(DEC_BATCH, DEPTH, 2, N_B, DI_B)
    h0t = h0t.reshape(DEC_BATCH, DEPTH, 2, N_B, H_B // 2, LANES).transpose(0, 1, 2, 4, 3, 5)
    h0t = h0t.reshape(DEC_BATCH, DEPTH, 2, (H_B // 2) * N_B, LANES)

    xp = x_prompt.reshape(BATCH * SEQ, D_MODEL)
    xs = x_sample.reshape(DEC_BATCH * DEC_SEQ, D_MODEL)
    fg = final_g.reshape(1, D_MODEL)
    norm_g3 = norm_g.reshape(DEPTH, 1, D_MODEL)
    carry = None
    for li in range(DEPTH):
        lam_init = 0.8 - 0.6 * math.exp(-0.3 * li)
        final = li == DEPTH - 1
        lamvec = jnp.concatenate([_pad_lanes(v[li]) for v in (lam_q1, lam_k1, lam_q2, lam_k2)], axis=0)
        subln = diff_subln_g[li].reshape(1, LANES)
        dtb = _pad_lanes(dt_bias[li])
        alog = _pad_lanes(a_log[li])
        dskx = jnp.repeat(d_skip[li], P_B, axis=-1)
        ssd_w = (conv_w[li], conv_b[li].reshape(1, CONV_DIM), dtb, alog, dskx, ssd_norm_g[li].reshape(1, DI_B))
        post_w = (wa16, wb16, wc16, wo16, fg)

        proj, ka, va, kc_t, vc_t, yb, ssd_state = _proj_ssd_call(xp, ada, norm_g3, w16, w_dt, ssd_w, li,
                                                                 ctx=True, carry=carry)
        carry = (ka, va, kc_t, vc_t, ssd_state)
        ya = _attn_a_ctx_call(proj, lamvec, subln, lam_init)
        xp = _post_ctx_call(xp, proj, kc_t, vc_t, ya, yb, ada, norm_g3, w16, li, *post_w, final=final)

        proj, kv, yb = _proj_ssd_call(xs, ada, norm_g3, w16, w_dt, ssd_w, li, ctx=False, h0t=h0t)
        ya = _attn_a_lat_call(proj, ck_a, cv_a, li, cos_t, sin_t, lamvec, subln, lam_init)
        yc = _attn_c_lat_call(proj, kv, ck_c, cv_c, li, na_tiles)
        xs = _post_call(xs, ya, yb, yc, ada, norm_g3, w16, li, *post_w, tm=512, row_base=1,
                        tokens_per_row=DEC_SEQ, final=final)

    new_k_a, new_v_a, new_k_c_t, new_v_c_t, new_ssd = carry
    to_token_major = lambda a: a.transpose(0, 1, 4, 2, 3)
    return (xp.reshape(BATCH, SEQ, D_MODEL), xs.reshape(DEC_BATCH, DEC_SEQ, D_MODEL),
            new_k_a, new_v_a, to_token_major(new_k_c_t), to_token_major(new_v_c_t), new_ssd)
```

```python
import functools
import math

import jax
import jax.numpy as jnp
import numpy as np
from jax import lax
from jax.experimental import pallas as pl
from jax.experimental.pallas import tpu as pltpu

D_MODEL = 1024
BATCH = 32
SEQ = 256
DEPTH = 2
DEC_BATCH = 2
DEC_SEQ = 1024
PAST_LEN = 512
GRID_W = 64
GRID_ROWS = DEC_SEQ // GRID_W
H_A = 4
DH_A = 64
W_A = H_A * 2 * DH_A
H_B = 8
P_B = 64
G_B = 2
N_B = 64
DI_B = H_B * P_B
CONV_K = 5
CONV_DIM = DI_B + 2 * G_B * N_B
SSD_CHUNK = 128
H_C = 8
DH_C = 64
W_C = H_C * DH_C
NA_KH = 8
NA_KW = 16
N_BRANCH = 3
ROPE_BASE = 10000.0
EPS = 1e-6

LANES = 128
HALF = LANES // 2
VMEM_LIMIT = 56 * 1024 * 1024

BC_DIM = CONV_DIM - DI_B
KVC_COLS = 2 * W_C
MERGE_COLS = N_BRANCH * D_MODEL
_SRC = dict(qa=0, ka=512, va=1024, ga=1536, z=2048, xs=2560, bc=3072, dt=3328, qc=3344, kc=3856, vc=4368,
            gc=4880, merge=5392)
COL_QA = 0
COL_GA = 512
COL_QC = 1024
COL_GC = 1536
COL_KA = 2048
COL_VA = 2560
PROJ_COLS = 3072
PROJ_SEGMENTS = ((COL_QA, _SRC["qa"]), (COL_GA, _SRC["ga"]), (COL_QC, _SRC["qc"]), (COL_GC, _SRC["gc"]),
                 (COL_KA, _SRC["ka"]), (COL_VA, _SRC["va"]))
P_Z = 0
P_XS = 512
P_BC = 1024
P_DT = 1280
P_COLS = P_DT + LANES
SIDE_TN = 256
TILE_SLOTS = 8

NA_QROWS = 4
NA_WROWS = 12
NA_TILES = 2 * NA_KH
NEG_INF = float("-inf")
LOG2E = math.log2(math.e)
ATTN_AHEAD = 2
ATTN_AHEAD_CTX = 3
SSD_AHEAD = 2
CTX_BATCHES_PER_STEP = 2
ATTN_BATCHES_PER_STEP = 4
HI = lax.Precision.HIGHEST
F32 = jnp.float32
BF16 = jnp.bfloat16


def _dot(a, b, precision=None):
    return jnp.dot(a, b, preferred_element_type=F32, precision=precision)


def _dot_nt(a, b):
    return lax.dot_general(a, b, (((1,), (1,)), ((), ())), preferred_element_type=F32)


def _sigmoid(x):
    return 1.0 / (1.0 + jnp.exp(-x))


def _silu(x):
    return x * _sigmoid(x)


def _lane(shape):
    return lax.broadcasted_iota(jnp.int32, shape, len(shape) - 1)


def _params(*sem):
    return pltpu.CompilerParams(dimension_semantics=sem, vmem_limit_bytes=VMEM_LIMIT)


def _ada_kernel(cvt_ref, w_ref, b_ref, o_ref):
    n_rows = 1 + DEC_BATCH
    s = _silu(cvt_ref[...])
    accs = [jnp.zeros((8, w_ref.shape[1]), F32)] * n_rows
    for k0 in range(0, D_MODEL, 8):
        w = w_ref[k0:k0 + 8, :]
        accs = [acc + w * s[k0:k0 + 8, r:r + 1] for r, acc in enumerate(accs)]
    rows = [jnp.sum(acc, axis=0, keepdims=True) for acc in accs]
    rows.append(jnp.zeros((8 - n_rows, w_ref.shape[1]), F32))
    o_ref[...] = jnp.concatenate(rows, axis=0) + b_ref[...]


def _ada_call(cvecs, w_ada, b_ada):
    tn = 1536
    return pl.pallas_call(
        _ada_kernel,
        grid=(DEPTH, 3 * D_MODEL // tn),
        in_specs=[
            pl.BlockSpec((D_MODEL, 8), lambda l, j: (0, 0)),
            pl.BlockSpec((None, D_MODEL, tn), lambda l, j: (l, 0, j)),
            pl.BlockSpec((None, 1, tn), lambda l, j: (l, 0, j)),
        ],
        out_specs=pl.BlockSpec((None, 8, tn), lambda l, j: (l, 0, j)),
        out_shape=jax.ShapeDtypeStruct((DEPTH, 8, 3 * D_MODEL), F32),
        compiler_params=_params("arbitrary", "arbitrary"),
        name="ada",
    )(cvecs, w_ada, b_ada.reshape(DEPTH, 1, 3 * D_MODEL))


def _merge_rows_spec(li):
    return pl.BlockSpec((pl.Element(1), pl.Element(MERGE_COLS), pl.Element(D_MODEL)),
                        lambda i: (li, _SRC["merge"], 0), pipeline_mode=pl.Buffered(1))


def _modulated_norm(x, g_ref, ada_ref, row):
    y = x * lax.rsqrt(jnp.mean(x * x, axis=-1, keepdims=True) + EPS) * g_ref[...]
    shift = ada_ref[pl.ds(row, 1), 0:D_MODEL]
    scale = ada_ref[pl.ds(row, 1), D_MODEL:2 * D_MODEL]
    return (y * (1.0 + scale) + shift).astype(BF16)


def _diff_lambda_in_kernel(lam_ref, lam_init):
    v = lam_ref[...]
    l1 = jnp.sum(v[0:1] * v[1:2], axis=-1, keepdims=True)
    l2 = jnp.sum(v[2:3] * v[3:4], axis=-1, keepdims=True)
    return jnp.exp(l1) - jnp.exp(l2) + lam_init


def _split_halves(x, scale):
    lo = _lane(x.shape) < HALF
    xs = x * (scale * LOG2E)
    return jnp.concatenate([jnp.where(lo, xs, 0.0), jnp.where(lo, 0.0, xs)], axis=0).astype(BF16)


def _diff_combine(o2, rsum, lam, t):
    return o2[:t] * rsum[:t] - (lam * rsum[t:]) * o2[t:]


def _diff_head_post(o, subln_g, lam_init, gate):
    o = o * lax.rsqrt(jnp.mean(o * o, axis=-1, keepdims=True) + EPS) * (subln_g * (1.0 - lam_init))
    return (o * _silu(gate)).astype(BF16)


def _attn_a_ctx_kernel(q_ref, k_ref, v_ref, g_ref, lam_ref, sg_ref, o_ref, *, lam_init):
    t = SEQ
    lam = _diff_lambda_in_kernel(lam_ref, lam_init)
    ones = jnp.ones((t, LANES), BF16)
    blocks = [(b, h) for b in range(q_ref.shape[0] // t) for h in range(H_A)]
    where = lambda b, h: (slice(b * t, (b + 1) * t), slice(h * LANES, (h + 1) * LANES))

    def scores(b, h):
        qq = _split_halves(q_ref[where(b, h)], DH_A ** -0.5)
        return _dot_nt(qq, k_ref[where(b, h)].astype(BF16))

    pending = [scores(*blk) for blk in blocks[:ATTN_AHEAD_CTX]]
    for n, blk in enumerate(blocks):
        s = pending.pop(0)
        if n + ATTN_AHEAD_CTX < len(blocks):
            pending.append(scores(*blocks[n + ATTN_AHEAD_CTX]))
        e = jnp.exp2(s - jnp.max(s, axis=-1, keepdims=True)).astype(BF16)
        rsum = 1.0 / _dot(e, ones)
        o = _diff_combine(_dot(e, v_ref[where(*blk)].astype(BF16)), rsum, lam, t)
        o_ref[where(*blk)] = _diff_head_post(o, sg_ref[...], lam_init, g_ref[where(*blk)])


def _attn_a_ctx_call(proj, lamvec, subln_g, lam_init):
    rows = ATTN_BATCHES_PER_STEP * SEQ
    blk = lambda c: pl.BlockSpec((rows, W_A), lambda b: (b, c // W_A))
    return pl.pallas_call(
        functools.partial(_attn_a_ctx_kernel, lam_init=lam_init),
        grid=(BATCH // ATTN_BATCHES_PER_STEP,),
        in_specs=[blk(COL_QA), blk(COL_KA), blk(COL_VA), blk(COL_GA),
                  pl.BlockSpec((4, LANES), lambda b: (0, 0)),
                  pl.BlockSpec((1, LANES), lambda b: (0, 0))],
        out_specs=pl.BlockSpec((rows, W_A), lambda b: (b, 0)),
        out_shape=jax.ShapeDtypeStruct((BATCH * SEQ, W_A), BF16),
        compiler_params=_params("arbitrary"),
        name="attn_a_ctx",
    )(proj, proj, proj, proj, lamvec, subln_g)


def _rope(x, cos, sin_signed):
    first = (_lane(x.shape) % 32) < 16
    swapped = jnp.where(first, pltpu.roll(x, LANES - 16, 1), pltpu.roll(x, 16, 1))
    return x * cos + swapped * sin_signed


def _attn_a_lat_kernel(q_ref, k_ref, v_ref, g_ref, ck_ref, cv_ref, cosq_ref, sinq_ref, cosk_ref, sink_ref,
                       lam_ref, sg_ref, o_ref, kr_s, *, lam_init):
    tq = q_ref.shape[0]

    @pl.when(pl.program_id(1) == 0)
    def _():
        for h in range(H_A):
            sl = slice(h * LANES, (h + 1) * LANES)
            kr_s[:, sl] = _rope(k_ref[:, sl], cosk_ref[...], sink_ref[...]).astype(BF16)

    lam = _diff_lambda_in_kernel(lam_ref, lam_init)

    def scores(h):
        sl = slice(h * LANES, (h + 1) * LANES)
        qq = _split_halves(_rope(q_ref[:, sl], cosq_ref[...], sinq_ref[...]), DH_A ** -0.5)
        return _dot_nt(qq, kr_s[:, sl]), _dot_nt(qq, ck_ref[:, sl].astype(BF16))

    pending = [scores(h) for h in range(ATTN_AHEAD)]
    for h in range(H_A):
        sl = slice(h * LANES, (h + 1) * LANES)
        s_lat, s_ctx = pending.pop(0)
        if h + ATTN_AHEAD < H_A:
            pending.append(scores(h + ATTN_AHEAD))
        m = jnp.maximum(jnp.max(s_lat, axis=-1, keepdims=True), jnp.max(s_ctx, axis=-1, keepdims=True))
        e_lat = jnp.exp2(s_lat - m)
        e_ctx = jnp.exp2(s_ctx - m)
        rsum = 1.0 / (jnp.sum(e_lat, axis=-1, keepdims=True) + jnp.sum(e_ctx, axis=-1, keepdims=True))
        o2 = _dot(e_lat.astype(BF16), v_ref[:, sl].astype(BF16)) + _dot(e_ctx.astype(BF16),
                                                                         cv_ref[:, sl].astype(BF16))
        o = _diff_combine(o2, rsum, lam, tq)
        o_ref[:, sl] = _diff_head_post(o, sg_ref[...], lam_init, g_ref[:, sl])


def _attn_a_lat_call(proj, cache_k, cache_v, li, cos_t, sin_t, lamvec, subln_g, lam_init):
    tq = 256
    nq = DEC_SEQ // tq
    qblk = lambda c: pl.BlockSpec((tq, W_A), lambda b, i: (b * nq + i, c // W_A))
    full = lambda c: pl.BlockSpec((DEC_SEQ, W_A), lambda b, i: (b, c // W_A))
    cache = pl.BlockSpec((None, None, PAST_LEN, W_A), lambda b, i: (b, li, 0, 0))
    return pl.pallas_call(
        functools.partial(_attn_a_lat_kernel, lam_init=lam_init),
        grid=(DEC_BATCH, nq),
        in_specs=[qblk(COL_QA), full(COL_KA), full(COL_VA), qblk(COL_GA), cache, cache,
                  pl.BlockSpec((tq, LANES), lambda b, i: (i, 0)),
                  pl.BlockSpec((tq, LANES), lambda b, i: (i, 0)),
                  pl.BlockSpec((DEC_SEQ, LANES), lambda b, i: (0, 0)),
                  pl.BlockSpec((DEC_SEQ, LANES), lambda b, i: (0, 0)),
                  pl.BlockSpec((4, LANES), lambda b, i: (0, 0)),
                  pl.BlockSpec((1, LANES), lambda b, i: (0, 0))],
        out_specs=pl.BlockSpec((tq, W_A), lambda b, i: (b * nq + i, 0)),
        out_shape=jax.ShapeDtypeStruct((DEC_BATCH * DEC_SEQ, W_A), BF16),
        scratch_shapes=[pltpu.VMEM((DEC_SEQ, W_A), BF16)],
        compiler_params=_params("arbitrary", "arbitrary"),
        name="attn_a_lat",
    )(proj, proj, proj, proj, cache_k, cache_v, cos_t, sin_t, cos_t, sin_t, lamvec, subln_g)


def _merge_halves(o, t):
    return jnp.where(_lane((t, LANES)) < HALF, o[:t], o[t:])


def _spread(work, n_slots):
    state = dict(slot=0, done=0)

    def emit_until(target):
        while state["done"] < target:
            work[state["done"]]()
            state["done"] += 1

    def side():
        state["slot"] += 1
        emit_until(min(len(work), -(-state["slot"] * len(work) // n_slots)))

    return side, lambda: emit_until(len(work))


def _attn_c_ctx_body(q_ref, kt_ref, vt_ref, g_ref, o_ref, side):
    t = SEQ
    blocks = [(b, j) for b in range(q_ref.shape[0] // t) for j in range(H_C // 2)]
    where = lambda b, j: (slice(b * t, (b + 1) * t), slice(j * LANES, (j + 1) * LANES))
    pair_t = lambda ref, b, j: ref[b, 2 * j:2 * j + 2].reshape(LANES, t).astype(BF16)

    def scores(b, j):
        return _dot(_split_halves(q_ref[where(b, j)], DH_C ** -0.5), pair_t(kt_ref, b, j))

    pending = [scores(*blk) for blk in blocks[:ATTN_AHEAD_CTX]]
    for n, blk in enumerate(blocks):
        s = pending.pop(0)
        if n + ATTN_AHEAD_CTX < len(blocks):
            pending.append(scores(*blocks[n + ATTN_AHEAD_CTX]))
        e = jnp.exp2(s - jnp.max(s, axis=-1, keepdims=True))
        rsum = 1.0 / jnp.sum(e, axis=-1, keepdims=True)
        o = _merge_halves(_dot_nt(e.astype(BF16), pair_t(vt_ref, *blk)) * rsum, t)
        o_ref[where(*blk)] = (o * _silu(g_ref[where(*blk)])).astype(BF16)
        side()


def _post_ctx_kernel(q_ref, kt_ref, vt_ref, gc_ref, x_ref, ya_ref, yb_ref, ada_ref, g_ref, wm_ref, wa_ref, wb_ref,
                     wc_ref, wo_ref, fg_ref, o_ref, yc_s, h_s, acc_s, sc_s, *, final):
    d = D_MODEL
    gate = ada_ref[0:1, 2 * d:3 * d]
    h_s[...] = _modulated_norm(x_ref[...], g_ref, ada_ref, 0)

    def gate_logits(n, c0):
        r0 = n * d + c0
        return _dot_nt(h_s[...], wm_ref[0, r0:r0 + SIDE_TN, :])

    work = []
    for c0 in range(0, d, SIDE_TN):
        cols = slice(c0, c0 + SIDE_TN)

        def branch_a(c0=c0, cols=cols):
            acc_s[:, cols] = _sigmoid(gate_logits(0, c0)) * _dot(ya_ref[...], wa_ref[:, cols])

        def branch_b(c0=c0, cols=cols):
            acc_s[:, cols] += _sigmoid(gate_logits(1, c0)) * _dot(yb_ref[...], wb_ref[:, cols])

        def gate_c(c0=c0, cols=cols):
            sc_s[:, cols] = _sigmoid(gate_logits(2, c0))

        work += [branch_a, branch_b, gate_c]
    side, flush = _spread(work, (q_ref.shape[0] // SEQ) * (H_C // 2))
    _attn_c_ctx_body(q_ref, kt_ref, vt_ref, gc_ref, yc_s, side)
    flush()
    merged = acc_s[...] + sc_s[...] * _dot(yc_s[...], wc_ref[...])
    x = x_ref[...] + gate * _dot(merged.astype(BF16), wo_ref[...])
    if final:
        x = x * lax.rsqrt(jnp.mean(x * x, axis=-1, keepdims=True) + EPS) * fg_ref[...]
    o_ref[...] = x


def _post_ctx_call(x, proj, kc_t, vc_t, ya, yb, ada, norm_g, w_merge, li, wa, wb, wc, wo, final_g, *, final):
    nb = CTX_BATCHES_PER_STEP
    tm = nb * SEQ
    tok = lambda w: pl.BlockSpec((tm, w), lambda i: (i, 0))
    col = lambda c: pl.BlockSpec((tm, W_C), lambda i: (i, c // W_C))
    cache = pl.BlockSpec((nb, None, H_C, DH_C, SEQ), lambda i: (i, li, 0, 0, 0))
    layer = lambda *shape, **kw: pl.BlockSpec((None,) + shape, lambda i: (li,) + (0,) * len(shape), **kw)
    once = dict(pipeline_mode=pl.Buffered(1))
    return pl.pallas_call(
        functools.partial(_post_ctx_kernel, final=final),
        grid=(BATCH // nb,),
        in_specs=[col(COL_QC), cache, cache, col(COL_GC), tok(D_MODEL), tok(W_A), tok(DI_B),
                  layer(8, 3 * D_MODEL), layer(1, D_MODEL), _merge_rows_spec(li),
                  layer(W_A, D_MODEL, **once), layer(DI_B, D_MODEL, **once), layer(W_C, D_MODEL, **once),
                  layer(D_MODEL, D_MODEL, **once), pl.BlockSpec((1, D_MODEL), lambda i: (0, 0))],
        out_specs=tok(D_MODEL),
        out_shape=jax.ShapeDtypeStruct((BATCH * SEQ, D_MODEL), F32),
        scratch_shapes=[pltpu.VMEM((tm, W_C), BF16), pltpu.VMEM((tm, D_MODEL), BF16),
                        pltpu.VMEM((tm, D_MODEL), F32), pltpu.VMEM((tm, D_MODEL), F32)],
        compiler_params=_params("arbitrary"),
        name="attn_c_post_ctx_final" if final else "attn_c_post_ctx",
    )(proj, kc_t, vc_t, proj, x, ya, yb, ada, norm_g, w_merge, wa, wb, wc, wo, final_g)


def _rpb_kernel(rpb_ref, o_ref):
    shape = (GRID_W, LANES)
    c = lax.broadcasted_iota(jnp.int32, shape, 0)
    cp = _lane(shape) % GRID_W
    start = jnp.clip(c - NA_KW // 2, 0, GRID_W - NA_KW)
    in_win = (cp >= start) & (cp < start + NA_KW)
    for h in range(H_C):
        o_ref[h, 0] = jnp.full(shape, NEG_INF, F32)
        for dr in range(2 * NA_KH - 1):
            row = jnp.broadcast_to(rpb_ref[h, dr:dr + 1, :], shape)
            tile = pltpu.roll(row, LANES - (NA_KW - 1), 1, stride=1, stride_axis=0)
            o_ref[h, 1 + dr] = jnp.where(in_win, tile * LOG2E, NEG_INF)


def _rpb_call(rpb):
    n_dc = 2 * NA_KW - 1
    v = jnp.pad(rpb, ((0, 0), (0, 0), (0, NA_TILES - (2 * NA_KH - 1)), (0, GRID_W - n_dc)))
    v = jnp.concatenate([v] * (LANES // GRID_W), axis=-1)
    return pl.pallas_call(
        _rpb_kernel,
        grid=(DEPTH,),
        in_specs=[pl.BlockSpec((None, H_C, NA_TILES, LANES), lambda l: (l, 0, 0, 0))],
        out_specs=pl.BlockSpec((None, H_C, NA_TILES, GRID_W, LANES), lambda l: (l, 0, 0, 0, 0)),
        out_shape=jax.ShapeDtypeStruct((DEPTH, H_C, NA_TILES, GRID_W, LANES), F32),
        compiler_params=_params("arbitrary"),
        name="rpb_tiles",
    )(v)


def _attn_c_lat_kernel(q_ref, k_ref, v_ref, g_ref, ck_ref, cv_ref, tile_ref, o_ref, bias_s):
    tq = q_ref.shape[0]
    m = pl.program_id(1)
    n_blocks = GRID_ROWS // NA_QROWS
    lo = _lane((GRID_W, LANES)) < HALF
    n_pair = H_C // 2

    def run(wrows, w0):
        nwin = wrows * GRID_W
        k0 = pl.multiple_of(w0 * GRID_W, GRID_W)

        def scores(j):
            sl = slice(j * LANES, (j + 1) * LANES)
            for s in range(2):
                for i in range(NA_QROWS):
                    r = m * NA_QROWS + i
                    start = jnp.clip(r - NA_KH // 2, 0, GRID_ROWS - NA_KH)
                    for jp in range(wrows // 2):
                        idx = []
                        for u in range(2):
                            rk = w0 + 2 * jp + u
                            valid = (rk >= start) & (rk < start + NA_KH)
                            idx.append(jnp.where(valid, rk - r + NA_KH, 0))
                        tile = jnp.where(lo, tile_ref[2 * j + s, idx[0]], tile_ref[2 * j + s, idx[1]])
                        bias_s[(s * NA_QROWS + i) * GRID_W:(s * NA_QROWS + i + 1) * GRID_W,
                               jp * LANES:(jp + 1) * LANES] = tile
            qq = _split_halves(q_ref[:, sl], DH_C ** -0.5)
            kw = k_ref[pl.ds(k0, nwin), sl].astype(BF16)
            s_win = _dot_nt(qq, kw) + bias_s[:, 0:nwin]
            ckt = ck_ref[2 * j:2 * j + 2].reshape(LANES, PAST_LEN).astype(BF16)
            return s_win, _dot(qq, ckt)

        pending = [scores(j) for j in range(ATTN_AHEAD)]
        for j in range(n_pair):
            sl = slice(j * LANES, (j + 1) * LANES)
            s_win, s_ctx = pending.pop(0)
            if j + ATTN_AHEAD < n_pair:
                pending.append(scores(j + ATTN_AHEAD))
            vw = v_ref[pl.ds(k0, nwin), sl].astype(BF16)
            mx = jnp.maximum(jnp.max(s_win, axis=-1, keepdims=True), jnp.max(s_ctx, axis=-1, keepdims=True))
            e_win = jnp.exp2(s_win - mx)
            e_ctx = jnp.exp2(s_ctx - mx)
            rs = 1.0 / (jnp.sum(e_win, axis=-1, keepdims=True) + jnp.sum(e_ctx, axis=-1, keepdims=True))
            cvt = cv_ref[2 * j:2 * j + 2].reshape(LANES, PAST_LEN).astype(BF16)
            o = (_dot(e_win.astype(BF16), vw) + _dot_nt(e_ctx.astype(BF16), cvt)) * rs
            o_ref[:, sl] = (_merge_halves(o, tq) * _silu(g_ref[:, sl])).astype(BF16)

    is_edge = (m == 0) | (m == n_blocks - 1)
    pl.when(is_edge)(lambda: run(NA_KH, jnp.where(m == 0, 0, GRID_ROWS - NA_KH)))
    pl.when(jnp.logical_not(is_edge))(
        lambda: run(NA_WROWS, jnp.where(m < n_blocks // 2, 0, GRID_ROWS - NA_WROWS)))


def _attn_c_lat_call(proj, kv, cache_k, cache_v, li, tiles):
    tq = NA_QROWS * GRID_W
    nq = DEC_SEQ // tq
    qblk = lambda c: pl.BlockSpec((tq, W_C), lambda b, i: (b * nq + i, c // W_C))
    full = lambda c: pl.BlockSpec((DEC_SEQ, W_C), lambda b, i: (b, c // W_C))
    cache = pl.BlockSpec((None, None, H_C, DH_C, PAST_LEN), lambda b, i: (b, li, 0, 0, 0))
    return pl.pallas_call(
        _attn_c_lat_kernel,
        grid=(DEC_BATCH, nq),
        in_specs=[qblk(COL_QC), full(0), full(W_C), qblk(COL_GC), cache, cache,
                  pl.BlockSpec((None, H_C, NA_TILES, GRID_W, LANES), lambda b, i: (li, 0, 0, 0, 0))],
        out_specs=pl.BlockSpec((tq, W_C), lambda b, i: (b * nq + i, 0)),
        out_shape=jax.ShapeDtypeStruct((DEC_BATCH * DEC_SEQ, W_C), BF16),
        scratch_shapes=[pltpu.VMEM((2 * tq, NA_WROWS * GRID_W), F32)],
        compiler_params=_params("arbitrary", "arbitrary"),
        name="attn_c_lat",
    )(proj, kv, kv, proj, cache_k, cache_v, tiles)


def _ssd_body(dt_ref, xs_ref, bc_ref, z_ref, h0_ref, params, y_ref, hs_ref, scratch, *, seq, static_loops, side):
    cw_ref, cb_ref, dtb_ref, alog_ref, dsk_ref, g_ref = params
    upad_s, xc_s, expo_s, expot_s, dtt_s, tot_s, bmt_s, yf_s, yb_s, st_s = scratch
    use_h0 = h0_ref is not None
    want_state = hs_ref is not None

    q = SSD_CHUNK
    nc = seq // q
    n_pair = H_B // 2
    n_hd = 2 * H_B
    pad = 8

    def loop(body, unroll=1):
        if static_loops:
            for c in range(nc):
                body(c, 0)
        else:
            lax.fori_loop(0, nc, body, 0, unroll=unroll)

    def chunk_rows(c):
        return slice(c * q, (c + 1) * q) if isinstance(c, int) else pl.ds(pl.multiple_of(c * q, q), q)

    upad_s[0:pad, :] = jnp.zeros((pad, CONV_DIM), F32)
    upad_s[pad + seq:2 * pad + seq, :] = jnp.zeros((pad, CONV_DIM), F32)
    upad_s[pad:pad + seq, 0:DI_B] = xs_ref[...]
    upad_s[pad:pad + seq, DI_B:CONV_DIM] = bc_ref[...]

    for c in range(nc):
        for cb_ in range(CONV_DIM // LANES):
            csl = slice(cb_ * LANES, (cb_ + 1) * LANES)
            acc = jnp.zeros((q, LANES), F32) + cb_ref[:, csl]
            for k in range(CONV_K):
                r0 = c * q + pad - CONV_K // 2 + k
                acc = acc + upad_s[r0:r0 + q, csl] * cw_ref[k:k + 1, csl]
            xc_s[c * q:(c + 1) * q, csl] = _silu(acc)
            side()

    a_row = -jnp.exp(alog_ref[...]) * LOG2E
    a_col = jnp.broadcast_to(a_row, (LANES, LANES)).T[0:n_hd, 0:1]
    ri = lax.broadcasted_iota(jnp.int32, (q, q), 0)
    ci = lax.broadcasted_iota(jnp.int32, (q, q), 1)
    ltri = (ri >= ci).astype(F32)
    fwd_lane = _lane((q, LANES)) < H_B
    fwd_row = lax.broadcasted_iota(jnp.int32, (n_hd, q), 0) < H_B

    def prep_body(c, carry):
        rows = chunk_rows(c)
        xdt = dt_ref[rows, 0:LANES] + dtb_ref[...]
        dtv = jnp.maximum(xdt, 0.0) + jnp.log1p(jnp.exp(-jnp.abs(xdt)))
        la = dtv * a_row
        acum = _dot(ltri, la, HI)
        expo_s[rows, :] = jnp.where(fwd_lane, acum, la - acum)
        acum_t = acum.T[0:n_hd, :]
        dt_t = dtv.T[0:n_hd, :]
        expot_s[c] = jnp.where(fwd_row, acum_t, dt_t * a_col - acum_t)
        dtt_s[c] = dt_t
        tot_s[c] = jnp.broadcast_to(acum_t[:, q - 1:q], (n_hd, q))
        bmt_s[c] = xc_s[rows, DI_B:DI_B + LANES].T
        side()
        return carry

    loop(prep_body, unroll=2)

    if use_h0:
        st_s[...] = h0_ref[...].reshape(2, n_pair, N_B, LANES)
    else:
        st_s[...] = jnp.zeros_like(st_s)

    lane_q = _lane((q, LANES))
    lo = lane_q < HALF
    lo_st = _lane((N_B, LANES)) < HALF

    def chunk_pair(c_fwd, c_bwd):
        dirs = ((0, c_fwd, yf_s), (1, c_bwd, yb_s))
        group_of = lambda k: k * G_B // n_pair
        items = [(k, d) for k in range(n_pair) for d in dirs]
        cb, y_off, st_in = {}, {}, {}

        def issue_early(k, d):
            dirn, c, _ = d
            g, rows = group_of(k), chunk_rows(c)
            in_g = (lane_q >= g * N_B) & (lane_q < (g + 1) * N_B)
            cmg = jnp.where(in_g, xc_s[rows, DI_B + LANES:DI_B + 2 * LANES], 0.0).astype(BF16)
            if (dirn, g) not in cb:
                cb[dirn, g] = _dot_nt(cmg, xc_s[rows, DI_B:DI_B + LANES].astype(BF16))
            st_in[dirn, k] = st_s[dirn, k]
            y_off[dirn, k] = _dot(cmg, jnp.concatenate([st_in[dirn, k]] * 2, axis=0).astype(BF16))

        for item in items[:SSD_AHEAD]:
            issue_early(*item)
        for n, (k, (dirn, c, y_s)) in enumerate(items):
            if n + SSD_AHEAD < len(items):
                issue_early(*items[n + SSD_AHEAD])
            psl = slice(k * LANES, (k + 1) * LANES)
            rows = chunk_rows(c)
            tri = (ri >= ci) if dirn == 0 else (ci >= ri)
            bmt_g = bmt_s[c, group_of(k) * N_B:(group_of(k) + 1) * N_B, :]
            x16 = xc_s[rows, psl].astype(BF16)
            mats, lhs, ysc, cdec = [], [], [], []
            for s in range(2):
                col = dirn * H_B + 2 * k + s
                e_col = jnp.broadcast_to(expo_s[rows, col:col + 1], (q, q))
                e_row = expot_s[c, col:col + 1, :]
                dt_row = dtt_s[c, col:col + 1, :]
                tot = tot_s[c, col:col + 1, :]
                dec = jnp.exp2(jnp.where(tri, e_col - e_row, NEG_INF))
                mats.append((cb[dirn, group_of(k)] * dec * dt_row).astype(BF16))
                if dirn == 0:
                    ysc.append(jnp.exp2(e_col))
                    w_row = jnp.exp2(tot - e_row)
                else:
                    ysc.append(jnp.exp2(e_col + tot))
                    w_row = jnp.exp2(-e_row)
                lhs.append((bmt_g * (w_row * dt_row)).astype(BF16))
                cdec.append(jnp.exp2(tot[:, 0:LANES]))
            yd = _dot(jnp.concatenate(mats, axis=0), x16)
            ds = _dot(jnp.concatenate(lhs, axis=0), x16)
            yo, st = y_off.pop((dirn, k)), st_in.pop((dirn, k))
            y_s[rows, psl] = jnp.where(lo, yd[:q] + ysc[0] * yo, yd[q:] + ysc[1] * yo)
            st_s[dirn, k] = jnp.where(lo_st, cdec[0] * st + ds[:N_B], cdec[1] * st + ds[N_B:])
            side()

    def body(c, carry):
        chunk_pair(c, nc - 1 - c)
        return carry

    loop(body, unroll=2)

    dsum = dsk_ref[0:1, :] + dsk_ref[1:2, :]

    def out_body(c, carry):
        rows = chunk_rows(c)
        y = yf_s[rows, :] + yb_s[rows, :] + xc_s[rows, 0:DI_B] * dsum
        y = y * _silu(z_ref[rows, :])
        y = y * lax.rsqrt(jnp.mean(y * y, axis=-1, keepdims=True) + EPS) * g_ref[...]
        y_ref[rows, :] = y.astype(BF16)
        side()
        return carry

    loop(out_body)
    if want_state:
        for dirn in range(2):
            for k in range(n_pair):
                st = st_s[dirn, k]
                st_t = jnp.concatenate([st, st], axis=0).T
                for s in range(2):
                    hs_ref[dirn, 2 * k + s] = st_t[s * P_B:(s + 1) * P_B, 0:N_B]


def _ssd_scratch(seq):
    nc = seq // SSD_CHUNK
    per_chunk_rows = pltpu.VMEM((nc, 2 * H_B, SSD_CHUNK), F32)
    return [pltpu.VMEM((seq + 16, CONV_DIM), F32), pltpu.VMEM((seq, CONV_DIM), F32),
            pltpu.VMEM((seq, LANES), F32), per_chunk_rows, per_chunk_rows, per_chunk_rows,
            pltpu.VMEM((nc, LANES, SSD_CHUNK), F32),
            pltpu.VMEM((seq, DI_B), F32), pltpu.VMEM((seq, DI_B), F32),
            pltpu.VMEM((2, H_B // 2, N_B, LANES), F32)]


def _ssd_param_specs(const):
    return [const((CONV_K, CONV_DIM)), const((1, CONV_DIM)), const((1, LANES)), const((1, LANES)),
            const((2, DI_B)), const((1, DI_B))]


def _proj_ssd_kernel(*refs, nb, seq, ctx, n_carry):
    x_ref, ada_ref, g_ref, w_ref, wdt_ref = refs[:5]
    pos = 5
    h0_ref = None
    if not ctx:
        h0_ref = refs[pos]
        pos += 1
    params = refs[pos:pos + 6]
    outs = refs[pos + 6 + n_carry:]
    if ctx:
        proj_ref, ka_ref, va_ref, kc_ref, vc_ref, y_ref, hs_ref = outs[:7]
        outs = outs[7:]
    else:
        proj_ref, kv_ref, y_ref = outs[:3]
        hs_ref = None
        outs = outs[3:]
    h_s, p_s = outs[:2]
    if ctx:
        scratch = outs[2:]
    else:
        scratch, (stage_s, sem) = outs[2:-2], outs[-2:]
        row0 = pl.multiple_of(pl.program_id(0) * nb * seq, nb * seq)
        copies = []

        def write_tile(dst_hbm, d, tile):
            slot = len(copies) % TILE_SLOTS
            if len(copies) >= TILE_SLOTS:
                copies[len(copies) - TILE_SLOTS].wait()
            stage_s[slot] = tile
            cp = pltpu.make_async_copy(stage_s.at[slot], dst_hbm.at[pl.ds(row0, nb * seq), pl.ds(d, SIDE_TN)],
                                       sem.at[slot])
            cp.start()
            copies.append(cp)

    row = 0 if ctx else 1 + pl.program_id(0)
    for r0 in range(0, nb * seq, SEQ):
        h_s[r0:r0 + SEQ, :] = _modulated_norm(x_ref[r0:r0 + SEQ, :], g_ref, ada_ref, row)
    for c0 in range(P_Z, P_DT, SIDE_TN):
        c1 = min(c0 + SIDE_TN, P_DT)
        p_s[:, c0:c1] = _dot_nt(h_s[...], w_ref[_SRC["z"] + c0:_SRC["z"] + c1, :])
    p_s[:, P_DT:P_COLS] = _dot_nt(h_s[...], wdt_ref[...])

    work = []
    for col, src in PROJ_SEGMENTS:
        for off in range(0, W_A, SIDE_TN):
            def tile(d=col + off, s=src + off):
                t_val = _dot_nt(h_s[...], w_ref[s:s + SIDE_TN, :])
                if ctx:
                    proj_ref[:, d:d + SIDE_TN] = t_val
                else:
                    write_tile(proj_ref, d, t_val)
            work.append(tile)
            if ctx and col in (COL_KA, COL_VA):
                def store(dst=ka_ref if col == COL_KA else va_ref, col=col, off=off):
                    for b in range(nb):
                        for h in range(off // LANES, (off + SIDE_TN) // LANES):
                            dst[b, :, h, :] = proj_ref[b * seq:(b + 1) * seq, col + h * LANES:col + (h + 1) * LANES]
                work.append(store)
    for r0 in range(0, KVC_COLS, SIDE_TN):
        if ctx:
            for b in range(nb):
                def tile_t(b=b, r0=r0):
                    w_rows = w_ref[_SRC["kc"] + r0:_SRC["kc"] + r0 + SIDE_TN, :]
                    kv_t = _dot_nt(w_rows, h_s[b * seq:(b + 1) * seq, :])
                    dst, d0 = (kc_ref, r0) if r0 < W_C else (vc_ref, r0 - W_C)
                    dst[b, d0 // DH_C:(d0 + SIDE_TN) // DH_C] = kv_t.reshape(SIDE_TN // DH_C, DH_C, seq)
                work.append(tile_t)
        else:
            def tile_kv(r0=r0):
                write_tile(kv_ref, r0, _dot_nt(h_s[...], w_ref[_SRC["kc"] + r0:_SRC["kc"] + r0 + SIDE_TN, :]))
            work.append(tile_kv)

    n_slots = nb * (seq // SSD_CHUNK) * (CONV_DIM // LANES + 2 + 2 * (H_B // 2))
    side, flush = _spread(work, n_slots)
    for b in range(nb):
        rows = pl.ds(b * seq, seq)
        _ssd_body(p_s.at[rows, pl.ds(P_DT, LANES)], p_s.at[rows, pl.ds(P_XS, DI_B)],
                  p_s.at[rows, pl.ds(P_BC, BC_DIM)], p_s.at[rows, pl.ds(P_Z, DI_B)], h0_ref, params,
                  y_ref.at[rows, :], None if hs_ref is None else hs_ref.at[b], scratch,
                  seq=seq, static_loops=True, side=side)
    flush()
    if not ctx:
        for cp in copies[-TILE_SLOTS:]:
            cp.wait()


def _proj_ssd_call(x, ada, norm_g, w16, w_dt, ssd_w, li, *, ctx, carry=None, h0t=None):
    nb, seq = (CTX_BATCHES_PER_STEP, SEQ) if ctx else (1, DEC_SEQ)
    tm = nb * seq
    t = x.shape[0]
    n_carry = 0 if carry is None else len(carry)
    const = lambda shape: pl.BlockSpec(shape, lambda i: (0,) * len(shape))
    once = pl.Buffered(1)
    in_specs = [pl.BlockSpec((tm, D_MODEL), lambda i: (i, 0)),
                pl.BlockSpec((None, 8, 3 * D_MODEL), lambda i: (li, 0, 0)),
                pl.BlockSpec((None, 1, D_MODEL), lambda i: (li, 0, 0)),
                pl.BlockSpec((None, _SRC["merge"], D_MODEL), lambda i: (li, 0, 0), pipeline_mode=once),
                pl.BlockSpec((None, LANES, D_MODEL), lambda i: (li, 0, 0))]
    args = [x, ada, norm_g, w16, w_dt]
    if not ctx:
        in_specs.append(pl.BlockSpec((None, None, 2, (H_B // 2) * N_B, LANES), lambda i: (i, li, 0, 0, 0)))
        args.append(h0t)
    in_specs += _ssd_param_specs(const)
    args += list(ssd_w)
    hbm = pl.BlockSpec(memory_space=pl.ANY)
    out_specs = [pl.BlockSpec((tm, PROJ_COLS), lambda i: (i, 0)) if ctx else hbm]
    out_shape = [jax.ShapeDtypeStruct((t, PROJ_COLS), F32)]
    aliases = {}
    if ctx:
        out_specs += [pl.BlockSpec((nb, None, SEQ, H_A, 2 * DH_A), lambda i: (i, li, 0, 0, 0))] * 2
        out_specs += [pl.BlockSpec((nb, None, H_C, DH_C, SEQ), lambda i: (i, li, 0, 0, 0))] * 2
        out_shape += [jax.ShapeDtypeStruct((BATCH, DEPTH, SEQ, H_A, 2 * DH_A), F32)] * 2
        out_shape += [jax.ShapeDtypeStruct((BATCH, DEPTH, H_C, DH_C, SEQ), F32)] * 2
    else:
        out_specs.append(hbm)
        out_shape.append(jax.ShapeDtypeStruct((t, KVC_COLS), F32))
    out_specs.append(pl.BlockSpec((tm, DI_B), lambda i: (i, 0)))
    out_shape.append(jax.ShapeDtypeStruct((t, DI_B), BF16))
    if ctx:
        out_specs.append(pl.BlockSpec((nb, None, 2, H_B, P_B, N_B), lambda i: (i, li, 0, 0, 0, 0)))
        out_shape.append(jax.ShapeDtypeStruct((BATCH, DEPTH, 2, H_B, P_B, N_B), F32))
        if carry is not None:
            in_specs += [pl.BlockSpec(memory_space=pl.ANY)] * n_carry
            aliases = {len(args) + k: (1, 2, 3, 4, 6)[k] for k in range(n_carry)}
            args += list(carry)
    scratch = [pltpu.VMEM((tm, D_MODEL), BF16), pltpu.VMEM((tm, P_COLS), F32)] + _ssd_scratch(seq)
    if not ctx:
        scratch += [pltpu.VMEM((TILE_SLOTS, tm, SIDE_TN), F32), pltpu.SemaphoreType.DMA((TILE_SLOTS,))]
    return pl.pallas_call(
        functools.partial(_proj_ssd_kernel, nb=nb, seq=seq, ctx=ctx, n_carry=n_carry),
        grid=(t // tm,),
        in_specs=in_specs,
        out_specs=out_specs,
        out_shape=out_shape,
        input_output_aliases=aliases,
        scratch_shapes=scratch,
        compiler_params=_params("arbitrary"),
        name="ctx_proj_ssd" if ctx else "lat_proj_ssd",
    )(*args)


def _post_kernel(x_ref, ya_ref, yb_ref, yc_ref, ada_ref, g_ref, wm_ref, wa_ref, wb_ref, wc_ref, wo_ref, fg_ref,
                 o_ref, *, tm, row_base, tokens_per_row, final):
    row = row_base + (pl.program_id(0) * tm) // tokens_per_row
    gate = ada_ref[pl.ds(row, 1), 2 * D_MODEL:3 * D_MODEL]
    d = D_MODEL
    x = x_ref[...]
    h = _modulated_norm(x, g_ref, ada_ref, row)
    merged = None
    for n, (y_ref, w_ref) in enumerate(((ya_ref, wa_ref), (yb_ref, wb_ref), (yc_ref, wc_ref))):
        logits = _dot_nt(h, wm_ref[0, n * d:(n + 1) * d, :])
        term = _sigmoid(logits) * _dot(y_ref[...], w_ref[...])
        merged = term if merged is None else merged + term
    x = x + gate * _dot(merged.astype(BF16), wo_ref[...])
    if final:
        x = x * lax.rsqrt(jnp.mean(x * x, axis=-1, keepdims=True) + EPS) * fg_ref[...]
    o_ref[...] = x


def _post_call(x, ya, yb, yc, ada, norm_g, w_merge_t, li, wa, wb, wc, wo, final_g, *, tm, row_base,
               tokens_per_row, final):
    t = x.shape[0]
    tok = lambda w: pl.BlockSpec((tm, w), lambda i: (i, 0))
    layer = lambda *shape, **kw: pl.BlockSpec((None,) + shape, lambda i: (li,) + (0,) * len(shape), **kw)
    once = dict(pipeline_mode=pl.Buffered(1))
    kern = functools.partial(_post_kernel, tm=tm, row_base=row_base, tokens_per_row=tokens_per_row, final=final)
    return pl.pallas_call(
        kern,
        grid=(t // tm,),
        in_specs=[tok(D_MODEL), tok(W_A), tok(DI_B), tok(W_C),
                  layer(8, 3 * D_MODEL), layer(1, D_MODEL), _merge_rows_spec(li),
                  layer(W_A, D_MODEL, **once), layer(DI_B, D_MODEL, **once), layer(W_C, D_MODEL, **once),
                  layer(D_MODEL, D_MODEL, **once), pl.BlockSpec((1, D_MODEL), lambda i: (0, 0))],
        out_specs=tok(D_MODEL),
        out_shape=jax.ShapeDtypeStruct((t, D_MODEL), F32),
        compiler_params=_params("arbitrary"),
        name="post_final" if final else "post",
    )(x, ya, yb, yc, ada, norm_g, w_merge_t, wa, wb, wc, wo, final_g)


def _rope_tables():
    pos = np.arange(DEC_SEQ)
    lane = np.arange(LANES)
    l64 = lane % (2 * (DH_A // 2))
    quarter = DH_A // 4
    p = np.where((l64 < DH_A // 2)[None, :], (pos // GRID_W)[:, None], (pos % GRID_W)[:, None])
    inv = ROPE_BASE ** (-np.arange(quarter, dtype=np.float64) / quarter)
    ang = p.astype(np.float64) * inv[l64 % quarter][None, :]
    sign = np.where((lane % (2 * quarter)) < quarter, -1.0, 1.0)
    return jnp.asarray(np.cos(ang), F32), jnp.asarray(np.sin(ang) * sign[None, :], F32)


def _pad_lanes(v, width=LANES):
    v = v.reshape(1, -1).astype(F32)
    return jnp.pad(v, ((0, 0), (0, width - v.shape[1])))


def kernel(x_prompt, x_sample, cache_diff_k, cache_diff_v, cache_na_k, cache_na_v, state_ssd, c, c_ctx,
           norm_g, w_ada, b_ada, w_in, lam_q1, lam_k1, lam_q2, lam_k2, diff_subln_g, conv_w, conv_b,
           dt_bias, a_log, d_skip, ssd_norm_g, na_rpb, w_br_a, w_br_b, w_br_c, w_out, final_g):
    assert x_prompt.shape == (BATCH, SEQ, D_MODEL) and x_sample.shape == (DEC_BATCH, DEC_SEQ, D_MODEL)
    assert w_in.shape == (DEPTH, D_MODEL, _SRC["merge"] + MERGE_COLS)
    w16 = jnp.swapaxes(w_in, 1, 2).astype(BF16)
    w_dt =jnp.pad(w16[:, _SRC["dt"]:_SRC["qc"], :], ((0, 0), (0, LANES - 2 * H_B), (0, 0)))
    wa16, wb16, wc16, wo16 = (w.astype(BF16) for w in (w_br_a, w_br_b, w_br_c, w_out))

    cvecs = jnp.concatenate([c_ctx[None, :], c, jnp.zeros((8 - 1 - DEC_BATCH, D_MODEL), F32)], axis=0)
    ada = _ada_call(cvecs.T, w_ada, b_ada)
    cos_t, sin_t = _rope_tables()

    ck_a = cache_diff_k.reshape(DEC_BATCH, DEPTH, PAST_LEN, W_A)
    cv_a = cache_diff_v.reshape(DEC_BATCH, DEPTH, PAST_LEN, W_A)
    ck_c = cache_na_k.transpose(0, 1, 3, 4, 2)
    cv_c = cache_na_v.transpose(0, 1, 3, 4, 2)
    na_tiles = _rpb_call(na_rpb)
    h0t = state_ssd.transpose(0, 1, 2, 5, 3, 4).reshape(DEC_BATCH, DEPTH, 2, N_B, DI_B)
    h0t = h0t.reshape(DEC_BATCH, DEPTH, 2, N_B, H_B // 2, LANES).transpose(0, 1, 2, 4, 3, 5)
    h0t = h0t.reshape(DEC_BATCH, DEPTH, 2, (H_B // 2) * N_B, LANES)

    xp = x_prompt.reshape(BATCH * SEQ, D_MODEL)
    xs = x_sample.reshape(DEC_BATCH * DEC_SEQ, D_MODEL)
    fg = final_g.reshape(1, D_MODEL)
    norm_g3 = norm_g.reshape(DEPTH, 1, D_MODEL)
    carry = None
    for li in range(DEPTH):
        lam_init = 0.8 - 0.6 * math.exp(-0.3 * li)
        final = li == DEPTH - 1
        lamvec = jnp.concatenate([_pad_lanes(v[li]) for v in (lam_q1, lam_k1, lam_q2, lam_k2)], axis=0)
        subln = diff_subln_g[li].reshape(1, LANES)
        dtb = _pad_lanes(dt_bias[li])
        alog = _pad_lanes(a_log[li])
        dskx = jnp.repeat(d_skip[li], P_B, axis=-1)
        ssd_w = (conv_w[li], conv_b[li].reshape(1, CONV_DIM), dtb, alog, dskx, ssd_norm_g[li].reshape(1, DI_B))
        post_w = (wa16, wb16, wc16, wo16, fg)

        proj, ka, va, kc_t, vc_t, yb, ssd_state = _proj_ssd_call(xp, ada, norm_g3, w16, w_dt, ssd_w, li,
                                                                 ctx=True, carry=carry)
        carry = (ka, va, kc_t, vc_t, ssd_state)
        ya = _attn_a_ctx_call(proj, lamvec, subln, lam_init)
        xp = _post_ctx_call(xp, proj, kc_t, vc_t, ya, yb, ada, norm_g3, w16, li, *post_w, final=final)

        proj, kv, yb = _proj_ssd_call(xs, ada, norm_g3, w16, w_dt, ssd_w, li, ctx=False, h0t=h0t)
        ya = _attn_a_lat_call(proj, ck_a, cv_a, li, cos_t, sin_t, lamvec, subln, lam_init)
        yc = _attn_c_lat_call(proj, kv, ck_c, cv_c, li, na_tiles)
        xs = _post_call(xs, ya, yb, yc, ada, norm_g3, w16, li, *post_w, tm=512, row_base=1,
                        tokens_per_row=DEC_SEQ, final=final)

    new_k_a, new_v_a, new_k_c_t, new_v_c_t, new_ssd = carry
    to_token_major = lambda a: a.transpose(0, 1, 4, 2, 3)
    return (xp.reshape(BATCH, SEQ, D_MODEL), xs.reshape(DEC_BATCH, DEC_SEQ, D_MODEL),
            new_k_a, new_v_a, to_token_major(new_k_c_t), to_token_major(new_v_c_t), new_ssd)
```

```python
import functools
import math

import jax
import jax.numpy as jnp
import numpy as np
from jax import lax
from jax.experimental import pallas as pl
from jax.experimental.pallas import tpu as pltpu

D_MODEL = 1024
BATCH = 32
SEQ = 256
DEPTH = 2
DEC_BATCH = 2
DEC_SEQ = 1024
PAST_LEN = 512
GRID_W = 64
GRID_ROWS = DEC_SEQ // GRID_W
H_A = 4
DH_A = 64
W_A = H_A * 2 * DH_A
H_B = 8
P_B = 64
G_B = 2
N_B = 64
DI_B = H_B * P_B
CONV_K = 5
CONV_DIM = DI_B + 2 * G_B * N_B
SSD_CHUNK = 128
H_C = 8
DH_C = 64
W_C = H_C * DH_C
NA_KH = 8
NA_KW = 16
N_BRANCH = 3
ROPE_BASE = 10000.0
EPS = 1e-6

LANES = 128
HALF = LANES // 2
VMEM_LIMIT = 56 * 1024 * 1024

BC_DIM = CONV_DIM - DI_B
KVC_COLS = 2 * W_C
MERGE_COLS = N_BRANCH * D_MODEL
_SRC = dict(qa=0, ka=512, va=1024, ga=1536, z=2048, xs=2560, bc=3072, dt=3328, qc=3344, kc=3856, vc=4368,
            gc=4880, merge=5392)
COL_QA = 0
COL_GA = 512
COL_QC = 1024
COL_GC = 1536
COL_KA = 2048
COL_VA = 2560
PROJ_COLS = 3072
PROJ_SEGMENTS = ((COL_QA, _SRC["qa"]), (COL_GA, _SRC["ga"]), (COL_QC, _SRC["qc"]), (COL_GC, _SRC["gc"]),
                 (COL_KA, _SRC["ka"]), (COL_VA, _SRC["va"]))
P_Z = 0
P_XS = 512
P_BC = 1024
P_DT = 1280
P_COLS = P_DT + LANES
SIDE_TN = 256

NA_QROWS = 4
NA_WROWS = 12
NA_TILES = 2 * NA_KH
NEG_INF = float("-inf")
LOG2E = math.log2(math.e)
ATTN_AHEAD = 2
ATTN_AHEAD_CTX = 3
SSD_AHEAD = 2
CTX_BATCHES_PER_STEP = 2
ATTN_BATCHES_PER_STEP = 4
HI = lax.Precision.HIGHEST
F32 = jnp.float32
BF16 = jnp.bfloat16


def _dot(a, b, precision=None):
    return jnp.dot(a, b, preferred_element_type=F32, precision=precision)


def _dot_nt(a, b):
    return lax.dot_general(a, b, (((1,), (1,)), ((), ())), preferred_element_type=F32)


def _sigmoid(x):
    return 1.0 / (1.0 + jnp.exp(-x))


def _silu(x):
    return x * _sigmoid(x)


def _lane(shape):
    return lax.broadcasted_iota(jnp.int32, shape, len(shape) - 1)


def _params(*sem):
    return pltpu.CompilerParams(dimension_semantics=sem, vmem_limit_bytes=VMEM_LIMIT)


def _ada_kernel(cvt_ref, w_ref, b_ref, o_ref):
    n_rows = 1 + DEC_BATCH
    s = _silu(cvt_ref[...])
    accs = [jnp.zeros((8, w_ref.shape[1]), F32)] * n_rows
    for k0 in range(0, D_MODEL, 8):
        w = w_ref[k0:k0 + 8, :]
        accs = [acc + w * s[k0:k0 + 8, r:r + 1] for r, acc in enumerate(accs)]
    rows = [jnp.sum(acc, axis=0, keepdims=True) for acc in accs]
    rows.append(jnp.zeros((8 - n_rows, w_ref.shape[1]), F32))
    o_ref[...] = jnp.concatenate(rows, axis=0) + b_ref[...]


def _ada_call(cvecs, w_ada, b_ada):
    tn = 1536
    return pl.pallas_call(
        _ada_kernel,
        grid=(DEPTH, 3 * D_MODEL // tn),
        in_specs=[
            pl.BlockSpec((D_MODEL, 8), lambda l, j: (0, 0)),
            pl.BlockSpec((None, D_MODEL, tn), lambda l, j: (l, 0, j)),
            pl.BlockSpec((None, 1, tn), lambda l, j: (l, 0, j)),
        ],
        out_specs=pl.BlockSpec((None, 8, tn), lambda l, j: (l, 0, j)),
        out_shape=jax.ShapeDtypeStruct((DEPTH, 8, 3 * D_MODEL), F32),
        compiler_params=_params("arbitrary", "arbitrary"),
        name="ada",
    )(cvecs, w_ada, b_ada.reshape(DEPTH, 1, 3 * D_MODEL))


def _merge_rows_spec(li):
    return pl.BlockSpec((pl.Element(1), pl.Element(MERGE_COLS), pl.Element(D_MODEL)),
                        lambda i: (li, _SRC["merge"], 0), pipeline_mode=pl.Buffered(1))


def _modulated_norm(x, g_ref, ada_ref, row):
    y = x * lax.rsqrt(jnp.mean(x * x, axis=-1, keepdims=True) + EPS) * g_ref[...]
    shift = ada_ref[pl.ds(row, 1), 0:D_MODEL]
    scale = ada_ref[pl.ds(row, 1), D_MODEL:2 * D_MODEL]
    return (y * (1.0 + scale) + shift).astype(BF16)


def _diff_lambda_in_kernel(lam_ref, lam_init):
    v = lam_ref[...]
    l1 = jnp.sum(v[0:1] * v[1:2], axis=-1, keepdims=True)
    l2 = jnp.sum(v[2:3] * v[3:4], axis=-1, keepdims=True)
    return jnp.exp(l1) - jnp.exp(l2) + lam_init


def _split_halves(x, scale):
    lo = _lane(x.shape) < HALF
    xs = x * (scale * LOG2E)
    return jnp.concatenate([jnp.where(lo, xs, 0.0), jnp.where(lo, 0.0, xs)], axis=0).astype(BF16)


def _diff_combine(o2, rsum, lam, t):
    return o2[:t] * rsum[:t] - (lam * rsum[t:]) * o2[t:]


def _diff_head_post(o, subln_g, lam_init, gate):
    o = o * lax.rsqrt(jnp.mean(o * o, axis=-1, keepdims=True) + EPS) * (subln_g * (1.0 - lam_init))
    return (o * _silu(gate)).astype(BF16)


def _attn_a_ctx_kernel(q_ref, k_ref, v_ref, g_ref, lam_ref, sg_ref, o_ref, *, lam_init):
    t = SEQ
    lam = _diff_lambda_in_kernel(lam_ref, lam_init)
    ones = jnp.ones((t, LANES), BF16)
    blocks = [(b, h) for b in range(q_ref.shape[0] // t) for h in range(H_A)]
    where = lambda b, h: (slice(b * t, (b + 1) * t), slice(h * LANES, (h + 1) * LANES))

    def scores(b, h):
        qq = _split_halves(q_ref[where(b, h)], DH_A ** -0.5)
        return _dot_nt(qq, k_ref[where(b, h)].astype(BF16))

    pending = [scores(*blk) for blk in blocks[:ATTN_AHEAD_CTX]]
    for n, blk in enumerate(blocks):
        s = pending.pop(0)
        if n + ATTN_AHEAD_CTX < len(blocks):
            pending.append(scores(*blocks[n + ATTN_AHEAD_CTX]))
        e = jnp.exp2(s - jnp.max(s, axis=-1, keepdims=True)).astype(BF16)
        rsum = 1.0 / _dot(e, ones)
        o = _diff_combine(_dot(e, v_ref[where(*blk)].astype(BF16)), rsum, lam, t)
        o_ref[where(*blk)] = _diff_head_post(o, sg_ref[...], lam_init, g_ref[where(*blk)])


def _attn_a_ctx_call(proj, lamvec, subln_g, lam_init):
    rows = ATTN_BATCHES_PER_STEP * SEQ
    blk = lambda c: pl.BlockSpec((rows, W_A), lambda b: (b, c // W_A))
    return pl.pallas_call(
        functools.partial(_attn_a_ctx_kernel, lam_init=lam_init),
        grid=(BATCH // ATTN_BATCHES_PER_STEP,),
        in_specs=[blk(COL_QA), blk(COL_KA), blk(COL_VA), blk(COL_GA),
                  pl.BlockSpec((4, LANES), lambda b: (0, 0)),
                  pl.BlockSpec((1, LANES), lambda b: (0, 0))],
        out_specs=pl.BlockSpec((rows, W_A), lambda b: (b, 0)),
        out_shape=jax.ShapeDtypeStruct((BATCH * SEQ, W_A), BF16),
        compiler_params=_params("arbitrary"),
        name="attn_a_ctx",
    )(proj, proj, proj, proj, lamvec, subln_g)


def _rope(x, cos, sin_signed):
    first = (_lane(x.shape) % 32) < 16
    swapped = jnp.where(first, pltpu.roll(x, LANES - 16, 1), pltpu.roll(x, 16, 1))
    return x * cos + swapped * sin_signed


def _attn_a_lat_kernel(q_ref, k_ref, v_ref, g_ref, ck_ref, cv_ref, cosq_ref, sinq_ref, cosk_ref, sink_ref,
                       lam_ref, sg_ref, o_ref, kr_s, *, lam_init):
    tq = q_ref.shape[0]

    @pl.when(pl.program_id(1) == 0)
    def _():
        for h in range(H_A):
            sl = slice(h * LANES, (h + 1) * LANES)
            kr_s[:, sl] = _rope(k_ref[:, sl], cosk_ref[...], sink_ref[...]).astype(BF16)

    lam = _diff_lambda_in_kernel(lam_ref, lam_init)

    def scores(h):
        sl = slice(h * LANES, (h + 1) * LANES)
        qq = _split_halves(_rope(q_ref[:, sl], cosq_ref[...], sinq_ref[...]), DH_A ** -0.5)
        return _dot_nt(qq, kr_s[:, sl]), _dot_nt(qq, ck_ref[:, sl].astype(BF16))

    pending = [scores(h) for h in range(ATTN_AHEAD)]
    for h in range(H_A):
        sl = slice(h * LANES, (h + 1) * LANES)
        s_lat, s_ctx = pending.pop(0)
        if h + ATTN_AHEAD < H_A:
            pending.append(scores(h + ATTN_AHEAD))
        m = jnp.maximum(jnp.max(s_lat, axis=-1, keepdims=True), jnp.max(s_ctx, axis=-1, keepdims=True))
        e_lat = jnp.exp2(s_lat - m)
        e_ctx = jnp.exp2(s_ctx - m)
        rsum = 1.0 / (jnp.sum(e_lat, axis=-1, keepdims=True) + jnp.sum(e_ctx, axis=-1, keepdims=True))
        o2 = _dot(e_lat.astype(BF16), v_ref[:, sl].astype(BF16)) + _dot(e_ctx.astype(BF16),
                                                                         cv_ref[:, sl].astype(BF16))
        o = _diff_combine(o2, rsum, lam, tq)
        o_ref[:, sl] = _diff_head_post(o, sg_ref[...], lam_init, g_ref[:, sl])


def _attn_a_lat_call(proj, cache_k, cache_v, li, cos_t, sin_t, lamvec, subln_g, lam_init):
    tq = 128
    nq = DEC_SEQ // tq
    qblk = lambda c: pl.BlockSpec((tq, W_A), lambda b, i: (b * nq + i, c // W_A))
    full = lambda c: pl.BlockSpec((DEC_SEQ, W_A), lambda b, i: (b, c // W_A))
    cache = pl.BlockSpec((None, None, PAST_LEN, W_A), lambda b, i: (b, li, 0, 0))
    return pl.pallas_call(
        functools.partial(_attn_a_lat_kernel, lam_init=lam_init),
        grid=(DEC_BATCH, nq),
        in_specs=[qblk(COL_QA), full(COL_KA), full(COL_VA), qblk(COL_GA), cache, cache,
                  pl.BlockSpec((tq, LANES), lambda b, i: (i, 0)),
                  pl.BlockSpec((tq, LANES), lambda b, i: (i, 0)),
                  pl.BlockSpec((DEC_SEQ, LANES), lambda b, i: (0, 0)),
                  pl.BlockSpec((DEC_SEQ, LANES), lambda b, i: (0, 0)),
                  pl.BlockSpec((4, LANES), lambda b, i: (0, 0)),
                  pl.BlockSpec((1, LANES), lambda b, i: (0, 0))],
        out_specs=pl.BlockSpec((tq, W_A), lambda b, i: (b * nq + i, 0)),
        out_shape=jax.ShapeDtypeStruct((DEC_BATCH * DEC_SEQ, W_A), BF16),
        scratch_shapes=[pltpu.VMEM((DEC_SEQ, W_A), BF16)],
        compiler_params=_params("arbitrary", "arbitrary"),
        name="attn_a_lat",
    )(proj, proj, proj, proj, cache_k, cache_v, cos_t, sin_t, cos_t, sin_t, lamvec, subln_g)


def _merge_halves(o, t):
    return jnp.where(_lane((t, LANES)) < HALF, o[:t], o[t:])


def _spread(work, n_slots):
    state = dict(slot=0, done=0)

    def emit_until(target):
        while state["done"] < target:
            work[state["done"]]()
            state["done"] += 1

    def side():
        state["slot"] += 1
        emit_until(min(len(work), -(-state["slot"] * len(work) // n_slots)))

    return side, lambda: emit_until(len(work))


def _attn_c_ctx_body(q_ref, kt_ref, vt_ref, g_ref, o_ref, side):
    t = SEQ
    blocks = [(b, j) for b in range(q_ref.shape[0] // t) for j in range(H_C // 2)]
    where = lambda b, j: (slice(b * t, (b + 1) * t), slice(j * LANES, (j + 1) * LANES))
    pair_t = lambda ref, b, j: ref[b, 2 * j:2 * j + 2].reshape(LANES, t).astype(BF16)

    def scores(b, j):
        return _dot(_split_halves(q_ref[where(b, j)], DH_C ** -0.5), pair_t(kt_ref, b, j))

    pending = [scores(*blk) for blk in blocks[:ATTN_AHEAD_CTX]]
    for n, blk in enumerate(blocks):
        s = pending.pop(0)
        if n + ATTN_AHEAD_CTX < len(blocks):
            pending.append(scores(*blocks[n + ATTN_AHEAD_CTX]))
        e = jnp.exp2(s - jnp.max(s, axis=-1, keepdims=True))
        rsum = 1.0 / jnp.sum(e, axis=-1, keepdims=True)
        o = _merge_halves(_dot_nt(e.astype(BF16), pair_t(vt_ref, *blk)) * rsum, t)
        o_ref[where(*blk)] = (o * _silu(g_ref[where(*blk)])).astype(BF16)
        side()


def _post_ctx_kernel(q_ref, kt_ref, vt_ref, gc_ref, x_ref, ya_ref, yb_ref, ada_ref, g_ref, wm_ref, wa_ref, wb_ref,
                     wc_ref, wo_ref, fg_ref, o_ref, yc_s, h_s, acc_s, sc_s, *, final):
    d = D_MODEL
    gate = ada_ref[0:1, 2 * d:3 * d]
    h_s[...] = _modulated_norm(x_ref[...], g_ref, ada_ref, 0)

    def gate_logits(n, c0):
        r0 = n * d + c0
        return _dot_nt(h_s[...], wm_ref[0, r0:r0 + SIDE_TN, :])

    work = []
    for c0 in range(0, d, SIDE_TN):
        cols = slice(c0, c0 + SIDE_TN)

        def branch_a(c0=c0, cols=cols):
            acc_s[:, cols] = _sigmoid(gate_logits(0, c0)) * _dot(ya_ref[...], wa_ref[:, cols])

        def branch_b(c0=c0, cols=cols):
            acc_s[:, cols] += _sigmoid(gate_logits(1, c0)) * _dot(yb_ref[...], wb_ref[:, cols])

        def gate_c(c0=c0, cols=cols):
            sc_s[:, cols] = _sigmoid(gate_logits(2, c0))

        work += [branch_a, branch_b, gate_c]
    side, flush = _spread(work, (q_ref.shape[0] // SEQ) * (H_C // 2))
    _attn_c_ctx_body(q_ref, kt_ref, vt_ref, gc_ref, yc_s, side)
    flush()
    merged = acc_s[...] + sc_s[...] * _dot(yc_s[...], wc_ref[...])
    x = x_ref[...] + gate * _dot(merged.astype(BF16), wo_ref[...])
    if final:
        x = x * lax.rsqrt(jnp.mean(x * x, axis=-1, keepdims=True) + EPS) * fg_ref[...]
    o_ref[...] = x


def _post_ctx_call(x, proj, kc_t, vc_t, ya, yb, ada, norm_g, w_merge, li, wa, wb, wc, wo, final_g, *, final):
    nb = CTX_BATCHES_PER_STEP
    tm = nb * SEQ
    tok = lambda w: pl.BlockSpec((tm, w), lambda i: (i, 0))
    col = lambda c: pl.BlockSpec((tm, W_C), lambda i: (i, c // W_C))
    cache = pl.BlockSpec((nb, None, H_C, DH_C, SEQ), lambda i: (i, li, 0, 0, 0))
    layer = lambda *shape, **kw: pl.BlockSpec((None,) + shape, lambda i: (li,) + (0,) * len(shape), **kw)
    once = dict(pipeline_mode=pl.Buffered(1))
    return pl.pallas_call(
        functools.partial(_post_ctx_kernel, final=final),
        grid=(BATCH // nb,),
        in_specs=[col(COL_QC), cache, cache, col(COL_GC), tok(D_MODEL), tok(W_A), tok(DI_B),
                  layer(8, 3 * D_MODEL), layer(1, D_MODEL), _merge_rows_spec(li),
                  layer(W_A, D_MODEL, **once), layer(DI_B, D_MODEL, **once), layer(W_C, D_MODEL, **once),
                  layer(D_MODEL, D_MODEL, **once), pl.BlockSpec((1, D_MODEL), lambda i: (0, 0))],
        out_specs=tok(D_MODEL),
        out_shape=jax.ShapeDtypeStruct((BATCH * SEQ, D_MODEL), F32),
        scratch_shapes=[pltpu.VMEM((tm, W_C), BF16), pltpu.VMEM((tm, D_MODEL), BF16),
                        pltpu.VMEM((tm, D_MODEL), F32), pltpu.VMEM((tm, D_MODEL), F32)],
        compiler_params=_params("arbitrary"),
        name="attn_c_post_ctx_final" if final else "attn_c_post_ctx",
    )(proj, kc_t, vc_t, proj, x, ya, yb, ada, norm_g, w_merge, wa, wb, wc, wo, final_g)


def _rpb_kernel(rpb_ref, o_ref):
    shape = (GRID_W, LANES)
    c = lax.broadcasted_iota(jnp.int32, shape, 0)
    cp = _lane(shape) % GRID_W
    start = jnp.clip(c - NA_KW // 2, 0, GRID_W - NA_KW)
    in_win = (cp >= start) & (cp < start + NA_KW)
    for h in range(H_C):
        o_ref[h, 0] = jnp.full(shape, NEG_INF, F32)
        for dr in range(2 * NA_KH - 1):
            row = jnp.broadcast_to(rpb_ref[h, dr:dr + 1, :], shape)
            tile = pltpu.roll(row, LANES - (NA_KW - 1), 1, stride=1, stride_axis=0)
            o_ref[h, 1 + dr] = jnp.where(in_win, tile * LOG2E, NEG_INF)


def _rpb_call(rpb):
    n_dc = 2 * NA_KW - 1
    v = jnp.pad(rpb, ((0, 0), (0, 0), (0, NA_TILES - (2 * NA_KH - 1)), (0, GRID_W - n_dc)))
    v = jnp.concatenate([v] * (LANES // GRID_W), axis=-1)
    return pl.pallas_call(
        _rpb_kernel,
        grid=(DEPTH,),
        in_specs=[pl.BlockSpec((None, H_C, NA_TILES, LANES), lambda l: (l, 0, 0, 0))],
        out_specs=pl.BlockSpec((None, H_C, NA_TILES, GRID_W, LANES), lambda l: (l, 0, 0, 0, 0)),
        out_shape=jax.ShapeDtypeStruct((DEPTH, H_C, NA_TILES, GRID_W, LANES), F32),
        compiler_params=_params("arbitrary"),
        name="rpb_tiles",
    )(v)


def _attn_c_lat_kernel(q_ref, k_ref, v_ref, g_ref, ck_ref, cv_ref, tile_ref, o_ref, bias_s):
    tq = q_ref.shape[0]
    m = pl.program_id(1)
    n_blocks = GRID_ROWS // NA_QROWS
    lo = _lane((GRID_W, LANES)) < HALF
    n_pair = H_C // 2

    def run(wrows, w0):
        nwin = wrows * GRID_W
        k0 = pl.multiple_of(w0 * GRID_W, GRID_W)

        def scores(j):
            sl = slice(j * LANES, (j + 1) * LANES)
            for s in range(2):
                for i in range(NA_QROWS):
                    r = m * NA_QROWS + i
                    start = jnp.clip(r - NA_KH // 2, 0, GRID_ROWS - NA_KH)
                    for jp in range(wrows // 2):
                        idx = []
                        for u in range(2):
                            rk = w0 + 2 * jp + u
                            valid = (rk >= start) & (rk < start + NA_KH)
                            idx.append(jnp.where(valid, rk - r + NA_KH, 0))
                        tile = jnp.where(lo, tile_ref[2 * j + s, idx[0]], tile_ref[2 * j + s, idx[1]])
                        bias_s[(s * NA_QROWS + i) * GRID_W:(s * NA_QROWS + i + 1) * GRID_W,
                               jp * LANES:(jp + 1) * LANES] = tile
            qq = _split_halves(q_ref[:, sl], DH_C ** -0.5)
            kw = k_ref[pl.ds(k0, nwin), sl].astype(BF16)
            s_win = _dot_nt(qq, kw) + bias_s[:, 0:nwin]
            ckt = ck_ref[2 * j:2 * j + 2].reshape(LANES, PAST_LEN).astype(BF16)
            return s_win, _dot(qq, ckt)

        pending = [scores(j) for j in range(ATTN_AHEAD)]
        for j in range(n_pair):
            sl = slice(j * LANES, (j + 1) * LANES)
            s_win, s_ctx = pending.pop(0)
            if j + ATTN_AHEAD < n_pair:
                pending.append(scores(j + ATTN_AHEAD))
            vw = v_ref[pl.ds(k0, nwin), sl].astype(BF16)
            mx = jnp.maximum(jnp.max(s_win, axis=-1, keepdims=True), jnp.max(s_ctx, axis=-1, keepdims=True))
            e_win = jnp.exp2(s_win - mx)
            e_ctx = jnp.exp2(s_ctx - mx)
            rs = 1.0 / (jnp.sum(e_win, axis=-1, keepdims=True) + jnp.sum(e_ctx, axis=-1, keepdims=True))
            cvt = cv_ref[2 * j:2 * j + 2].reshape(LANES, PAST_LEN).astype(BF16)
            o = (_dot(e_win.astype(BF16), vw) + _dot_nt(e_ctx.astype(BF16), cvt)) * rs
            o_ref[:, sl] = (_merge_halves(o, tq) * _silu(g_ref[:, sl])).astype(BF16)

    is_edge = (m == 0) | (m == n_blocks - 1)
    pl.when(is_edge)(lambda: run(NA_KH, jnp.where(m == 0, 0, GRID_ROWS - NA_KH)))
    pl.when(jnp.logical_not(is_edge))(
        lambda: run(NA_WROWS, jnp.where(m < n_blocks // 2, 0, GRID_ROWS - NA_WROWS)))


def _attn_c_lat_call(proj, kv, cache_k, cache_v, li, tiles):
    tq = NA_QROWS * GRID_W
    nq = DEC_SEQ // tq
    qblk = lambda c: pl.BlockSpec((tq, W_C), lambda b, i: (b * nq + i, c // W_C))
    full = lambda c: pl.BlockSpec((DEC_SEQ, W_C), lambda b, i: (b, c // W_C))
    cache = pl.BlockSpec((None, None, H_C, DH_C, PAST_LEN), lambda b, i: (b, li, 0, 0, 0))
    return pl.pallas_call(
        _attn_c_lat_kernel,
        grid=(DEC_BATCH, nq),
        in_specs=[qblk(COL_QC), full(0), full(W_C), qblk(COL_GC), cache, cache,
                  pl.BlockSpec((None, H_C, NA_TILES, GRID_W, LANES), lambda b, i: (li, 0, 0, 0, 0))],
        out_specs=pl.BlockSpec((tq, W_C), lambda b, i: (b * nq + i, 0)),
        out_shape=jax.ShapeDtypeStruct((DEC_BATCH * DEC_SEQ, W_C), BF16),
        scratch_shapes=[pltpu.VMEM((2 * tq, NA_WROWS * GRID_W), F32)],
        compiler_params=_params("arbitrary", "arbitrary"),
        name="attn_c_lat",
    )(proj, kv, kv, proj, cache_k, cache_v, tiles)


def _ssd_body(dt_ref, xs_ref, bc_ref, z_ref, h0_ref, params, y_ref, hs_ref, scratch, *, seq, static_loops, side):
    cw_ref, cb_ref, dtb_ref, alog_ref, dsk_ref, g_ref = params
    upad_s, xc_s, expo_s, expot_s, dtt_s, tot_s, bmt_s, yf_s, yb_s, st_s = scratch
    use_h0 = h0_ref is not None
    want_state = hs_ref is not None

    q = SSD_CHUNK
    nc = seq // q
    n_pair = H_B // 2
    n_hd = 2 * H_B
    pad = 8

    def loop(body, unroll=1):
        if static_loops:
            for c in range(nc):
                body(c, 0)
        else:
            lax.fori_loop(0, nc, body, 0, unroll=unroll)

    def chunk_rows(c):
        return slice(c * q, (c + 1) * q) if isinstance(c, int) else pl.ds(pl.multiple_of(c * q, q), q)

    upad_s[0:pad, :] = jnp.zeros((pad, CONV_DIM), F32)
    upad_s[pad + seq:2 * pad + seq, :] = jnp.zeros((pad, CONV_DIM), F32)
    upad_s[pad:pad + seq, 0:DI_B] = xs_ref[...]
    upad_s[pad:pad + seq, DI_B:CONV_DIM] = bc_ref[...]

    for c in range(nc):
        for cb_ in range(CONV_DIM // LANES):
            csl = slice(cb_ * LANES, (cb_ + 1) * LANES)
            acc = jnp.zeros((q, LANES), F32) + cb_ref[:, csl]
            for k in range(CONV_K):
                r0 = c * q + pad - CONV_K // 2 + k
                acc = acc + upad_s[r0:r0 + q, csl] * cw_ref[k:k + 1, csl]
            xc_s[c * q:(c + 1) * q, csl] = _silu(acc)
            side()

    a_row = -jnp.exp(alog_ref[...]) * LOG2E
    a_col = jnp.broadcast_to(a_row, (LANES, LANES)).T[0:n_hd, 0:1]
    ri = lax.broadcasted_iota(jnp.int32, (q, q), 0)
    ci = lax.broadcasted_iota(jnp.int32, (q, q), 1)
    ltri = (ri >= ci).astype(F32)
    fwd_lane = _lane((q, LANES)) < H_B
    fwd_row = lax.broadcasted_iota(jnp.int32, (n_hd, q), 0) < H_B

    def prep_body(c, carry):
        rows = chunk_rows(c)
        xdt = dt_ref[rows, 0:LANES] + dtb_ref[...]
        dtv = jnp.maximum(xdt, 0.0) + jnp.log1p(jnp.exp(-jnp.abs(xdt)))
        la = dtv * a_row
        acum = _dot(ltri, la, HI)
        expo_s[rows, :] = jnp.where(fwd_lane, acum, la - acum)
        acum_t = acum.T[0:n_hd, :]
        dt_t = dtv.T[0:n_hd, :]
        expot_s[c] = jnp.where(fwd_row, acum_t, dt_t * a_col - acum_t)
        dtt_s[c] = dt_t
        tot_s[c] = jnp.broadcast_to(acum_t[:, q - 1:q], (n_hd, q))
        bmt_s[c] = xc_s[rows, DI_B:DI_B + LANES].T
        side()
        return carry

    loop(prep_body, unroll=2)

    if use_h0:
        st_s[...] = h0_ref[...].reshape(2, n_pair, N_B, LANES)
    else:
        st_s[...] = jnp.zeros_like(st_s)

    lane_q = _lane((q, LANES))
    lo = lane_q < HALF
    lo_st = _lane((N_B, LANES)) < HALF

    def chunk_pair(c_fwd, c_bwd):
        dirs = ((0, c_fwd, yf_s), (1, c_bwd, yb_s))
        group_of = lambda k: k * G_B // n_pair
        items = [(k, d) for k in range(n_pair) for d in dirs]
        cb, y_off, st_in = {}, {}, {}

        def issue_early(k, d):
            dirn, c, _ = d
            g, rows = group_of(k), chunk_rows(c)
            in_g = (lane_q >= g * N_B) & (lane_q < (g + 1) * N_B)
            cmg = jnp.where(in_g, xc_s[rows, DI_B + LANES:DI_B + 2 * LANES], 0.0).astype(BF16)
            if (dirn, g) not in cb:
                cb[dirn, g] = _dot_nt(cmg, xc_s[rows, DI_B:DI_B + LANES].astype(BF16))
            st_in[dirn, k] = st_s[dirn, k]
            y_off[dirn, k] = _dot(cmg, jnp.concatenate([st_in[dirn, k]] * 2, axis=0).astype(BF16))

        for item in items[:SSD_AHEAD]:
            issue_early(*item)
        for n, (k, (dirn, c, y_s)) in enumerate(items):
            if n + SSD_AHEAD < len(items):
                issue_early(*items[n + SSD_AHEAD])
            psl = slice(k * LANES, (k + 1) * LANES)
            rows = chunk_rows(c)
            tri = (ri >= ci) if dirn == 0 else (ci >= ri)
            bmt_g = bmt_s[c, group_of(k) * N_B:(group_of(k) + 1) * N_B, :]
            x16 = xc_s[rows, psl].astype(BF16)
            mats, lhs, ysc, cdec = [], [], [], []
            for s in range(2):
                col = dirn * H_B + 2 * k + s
                e_col = jnp.broadcast_to(expo_s[rows, col:col + 1], (q, q))
                e_row = expot_s[c, col:col + 1, :]
                dt_row = dtt_s[c, col:col + 1, :]
                tot = tot_s[c, col:col + 1, :]
                dec = jnp.exp2(jnp.where(tri, e_col - e_row, NEG_INF))
                mats.append((cb[dirn, group_of(k)] * dec * dt_row).astype(BF16))
                if dirn == 0:
                    ysc.append(jnp.exp2(e_col))
                    w_row = jnp.exp2(tot - e_row)
                else:
                    ysc.append(jnp.exp2(e_col + tot))
                    w_row = jnp.exp2(-e_row)
                lhs.append((bmt_g * (w_row * dt_row)).astype(BF16))
                cdec.append(jnp.exp2(tot[:, 0:LANES]))
            yd = _dot(jnp.concatenate(mats, axis=0), x16)
            ds = _dot(jnp.concatenate(lhs, axis=0), x16)
            yo, st = y_off.pop((dirn, k)), st_in.pop((dirn, k))
            y_s[rows, psl] = jnp.where(lo, yd[:q] + ysc[0] * yo, yd[q:] + ysc[1] * yo)
            st_s[dirn, k] = jnp.where(lo_st, cdec[0] * st + ds[:N_B], cdec[1] * st + ds[N_B:])
            side()

    def body(c, carry):
        chunk_pair(c, nc - 1 - c)
        return carry

    loop(body, unroll=2)

    dsum = dsk_ref[0:1, :] + dsk_ref[1:2, :]

    def out_body(c, carry):
        rows = chunk_rows(c)
        y = yf_s[rows, :] + yb_s[rows, :] + xc_s[rows, 0:DI_B] * dsum
        y = y * _silu(z_ref[rows, :])
        y = y * lax.rsqrt(jnp.mean(y * y, axis=-1, keepdims=True) + EPS) * g_ref[...]
        y_ref[rows, :] = y.astype(BF16)
        side()
        return carry

    loop(out_body)
    if want_state:
        for dirn in range(2):
            for k in range(n_pair):
                st = st_s[dirn, k]
                st_t = jnp.concatenate([st, st], axis=0).T
                for s in range(2):
                    hs_ref[dirn, 2 * k + s] = st_t[s * P_B:(s + 1) * P_B, 0:N_B]


def _ssd_scratch(seq):
    nc = seq // SSD_CHUNK
    per_chunk_rows = pltpu.VMEM((nc, 2 * H_B, SSD_CHUNK), F32)
    return [pltpu.VMEM((seq + 16, CONV_DIM), F32), pltpu.VMEM((seq, CONV_DIM), F32),
            pltpu.VMEM((seq, LANES), F32), per_chunk_rows, per_chunk_rows, per_chunk_rows,
            pltpu.VMEM((nc, LANES, SSD_CHUNK), F32),
            pltpu.VMEM((seq, DI_B), F32), pltpu.VMEM((seq, DI_B), F32),
            pltpu.VMEM((2, H_B // 2, N_B, LANES), F32)]


def _ssd_param_specs(const):
    return [const((CONV_K, CONV_DIM)), const((1, CONV_DIM)), const((1, LANES)), const((1, LANES)),
            const((2, DI_B)), const((1, DI_B))]


def _proj_ssd_kernel(*refs, nb, seq, ctx, n_carry):
    x_ref, ada_ref, g_ref, w_ref, wdt_ref = refs[:5]
    pos = 5
    h0_ref = None
    if not ctx:
        h0_ref = refs[pos]
        pos += 1
    params = refs[pos:pos + 6]
    outs = refs[pos + 6 + n_carry:]
    if ctx:
        proj_ref, ka_ref, va_ref, kc_ref, vc_ref, y_ref, hs_ref = outs[:7]
        outs = outs[7:]
    else:
        proj_ref, kv_ref, y_ref = outs[:3]
        hs_ref = None
        outs = outs[3:]
    h_s, p_s = outs[:2]
    scratch = outs[2:]

    row = 0 if ctx else 1 + pl.program_id(0)
    for r0 in range(0, nb * seq, SEQ):
        h_s[r0:r0 + SEQ, :] = _modulated_norm(x_ref[r0:r0 + SEQ, :], g_ref, ada_ref, row)
    for c0 in range(P_Z, P_DT, SIDE_TN):
        c1 = min(c0 + SIDE_TN, P_DT)
        p_s[:, c0:c1] = _dot_nt(h_s[...], w_ref[_SRC["z"] + c0:_SRC["z"] + c1, :])
    p_s[:, P_DT:P_COLS] = _dot_nt(h_s[...], wdt_ref[...])

    work = []
    for col, src in PROJ_SEGMENTS:
        for off in range(0, W_A, SIDE_TN):
            def tile(d=col + off, s=src + off):
                proj_ref[:, d:d + SIDE_TN] = _dot_nt(h_s[...], w_ref[s:s + SIDE_TN, :])
            work.append(tile)
            if ctx and col in (COL_KA, COL_VA):
                def store(dst=ka_ref if col == COL_KA else va_ref, col=col, off=off):
                    for b in range(nb):
                        for h in range(off // LANES, (off + SIDE_TN) // LANES):
                            dst[b, :, h, :] = proj_ref[b * seq:(b + 1) * seq, col + h * LANES:col + (h + 1) * LANES]
                work.append(store)
    for r0 in range(0, KVC_COLS, SIDE_TN):
        if ctx:
            for b in range(nb):
                def tile_t(b=b, r0=r0):
                    w_rows = w_ref[_SRC["kc"] + r0:_SRC["kc"] + r0 + SIDE_TN, :]
                    kv_t = _dot_nt(w_rows, h_s[b * seq:(b + 1) * seq, :])
                    dst, d0 = (kc_ref, r0) if r0 < W_C else (vc_ref, r0 - W_C)
                    dst[b, d0 // DH_C:(d0 + SIDE_TN) // DH_C] = kv_t.reshape(SIDE_TN // DH_C, DH_C, seq)
                work.append(tile_t)
        else:
            def tile_kv(r0=r0):
                kv_ref[:, r0:r0 + SIDE_TN] = _dot_nt(h_s[...], w_ref[_SRC["kc"] + r0:_SRC["kc"] + r0 + SIDE_TN, :])
            work.append(tile_kv)

    n_slots = nb * (seq // SSD_CHUNK) * (CONV_DIM // LANES + 2 + 2 * (H_B // 2))
    side, flush = _spread(work, n_slots)
    for b in range(nb):
        rows = pl.ds(b * seq, seq)
        _ssd_body(p_s.at[rows, pl.ds(P_DT, LANES)], p_s.at[rows, pl.ds(P_XS, DI_B)],
                  p_s.at[rows, pl.ds(P_BC, BC_DIM)], p_s.at[rows, pl.ds(P_Z, DI_B)], h0_ref, params,
                  y_ref.at[rows, :], None if hs_ref is None else hs_ref.at[b], scratch,
                  seq=seq, static_loops=True, side=side)
    flush()


def _proj_ssd_call(x, ada, norm_g, w16, w_dt, ssd_w, li, *, ctx, carry=None, h0t=None):
    nb, seq = (CTX_BATCHES_PER_STEP, SEQ) if ctx else (1, DEC_SEQ)
    tm = nb * seq
    t = x.shape[0]
    n_carry = 0 if carry is None else len(carry)
    const = lambda shape: pl.BlockSpec(shape, lambda i: (0,) * len(shape))
    once = pl.Buffered(1)
    big = {} if ctx else dict(pipeline_mode=pl.Buffered(1))
    in_specs = [pl.BlockSpec((tm, D_MODEL), lambda i: (i, 0), **big),
                pl.BlockSpec((None, 8, 3 * D_MODEL), lambda i: (li, 0, 0)),
                pl.BlockSpec((None, 1, D_MODEL), lambda i: (li, 0, 0)),
                pl.BlockSpec((None, _SRC["merge"], D_MODEL), lambda i: (li, 0, 0), pipeline_mode=once),
                pl.BlockSpec((None, LANES, D_MODEL), lambda i: (li, 0, 0))]
    args = [x, ada, norm_g, w16, w_dt]
    if not ctx:
        in_specs.append(pl.BlockSpec((None, None, 2, (H_B // 2) * N_B, LANES), lambda i: (i, li, 0, 0, 0)))
        args.append(h0t)
    in_specs += _ssd_param_specs(const)
    args += list(ssd_w)
    out_specs = [pl.BlockSpec((tm, PROJ_COLS), lambda i: (i, 0), **big)]
    out_shape = [jax.ShapeDtypeStruct((t, PROJ_COLS), F32)]
    aliases = {}
    if ctx:
        out_specs += [pl.BlockSpec((nb, None, SEQ, H_A, 2 * DH_A), lambda i: (i, li, 0, 0, 0))] * 2
        out_specs += [pl.BlockSpec((nb, None, H_C, DH_C, SEQ), lambda i: (i, li, 0, 0, 0))] * 2
        out_shape += [jax.ShapeDtypeStruct((BATCH, DEPTH, SEQ, H_A, 2 * DH_A), F32)] * 2
        out_shape += [jax.ShapeDtypeStruct((BATCH, DEPTH, H_C, DH_C, SEQ), F32)] * 2
    else:
        out_specs.append(pl.BlockSpec((tm, KVC_COLS), lambda i: (i, 0), **big))
        out_shape.append(jax.ShapeDtypeStruct((t, KVC_COLS), F32))
    out_specs.append(pl.BlockSpec((tm, DI_B), lambda i: (i, 0)))
    out_shape.append(jax.ShapeDtypeStruct((t, DI_B), BF16))
    if ctx:
        out_specs.append(pl.BlockSpec((nb, None, 2, H_B, P_B, N_B), lambda i: (i, li, 0, 0, 0, 0)))
        out_shape.append(jax.ShapeDtypeStruct((BATCH, DEPTH, 2, H_B, P_B, N_B), F32))
        if carry is not None:
            in_specs += [pl.BlockSpec(memory_space=pl.ANY)] * n_carry
            aliases = {len(args) + k: (1, 2, 3, 4, 6)[k] for k in range(n_carry)}
            args += list(carry)
    scratch = [pltpu.VMEM((tm, D_MODEL), BF16), pltpu.VMEM((tm, P_COLS), F32)] + _ssd_scratch(seq)
    return pl.pallas_call(
        functools.partial(_proj_ssd_kernel, nb=nb, seq=seq, ctx=ctx, n_carry=n_carry),
        grid=(t // tm,),
        in_specs=in_specs,
        out_specs=out_specs,
        out_shape=out_shape,
        input_output_aliases=aliases,
        scratch_shapes=scratch,
        compiler_params=_params("arbitrary"),
        name="ctx_proj_ssd" if ctx else "lat_proj_ssd",
    )(*args)


def _post_kernel(x_ref, ya_ref, yb_ref, yc_ref, ada_ref, g_ref, wm_ref, wa_ref, wb_ref, wc_ref, wo_ref, fg_ref,
                 o_ref, *, tm, row_base, tokens_per_row, final):
    row = row_base + (pl.program_id(0) * tm) // tokens_per_row
    gate = ada_ref[pl.ds(row, 1), 2 * D_MODEL:3 * D_MODEL]
    d = D_MODEL
    x = x_ref[...]
    h = _modulated_norm(x, g_ref, ada_ref, row)
    merged = None
    for n, (y_ref, w_ref) in enumerate(((ya_ref, wa_ref), (yb_ref, wb_ref), (yc_ref, wc_ref))):
        logits = _dot_nt(h, wm_ref[0, n * d:(n + 1) * d, :])
        term = _sigmoid(logits) * _dot(y_ref[...], w_ref[...])
        merged = term if merged is None else merged + term
    x = x + gate * _dot(merged.astype(BF16), wo_ref[...])
    if final:
        x = x * lax.rsqrt(jnp.mean(x * x, axis=-1, keepdims=True) + EPS) * fg_ref[...]
    o_ref[...] = x


def _post_call(x, ya, yb, yc, ada, norm_g, w_merge_t, li, wa, wb, wc, wo, final_g, *, tm, row_base,
               tokens_per_row, final):
    t = x.shape[0]
    tok = lambda w: pl.BlockSpec((tm, w), lambda i: (i, 0))
    layer = lambda *shape, **kw: pl.BlockSpec((None,) + shape, lambda i: (li,) + (0,) * len(shape), **kw)
    once = dict(pipeline_mode=pl.Buffered(1))
    kern = functools.partial(_post_kernel, tm=tm, row_base=row_base, tokens_per_row=tokens_per_row, final=final)
    return pl.pallas_call(
        kern,
        grid=(t // tm,),
        in_specs=[tok(D_MODEL), tok(W_A), tok(DI_B), tok(W_C),
                  layer(8, 3 * D_MODEL), layer(1, D_MODEL), _merge_rows_spec(li),
                  layer(W_A, D_MODEL, **once), layer(DI_B, D_MODEL, **once), layer(W_C, D_MODEL, **once),
                  layer(D_MODEL, D_MODEL, **once), pl.BlockSpec((1, D_MODEL), lambda i: (0, 0))],
        out_specs=tok(D_MODEL),
        out_shape=jax.ShapeDtypeStruct((t, D_MODEL), F32),
        compiler_params=_params("arbitrary"),
        name="post_final" if final else "post",
    )(x, ya, yb, yc, ada, norm_g, w_merge_t, wa, wb, wc, wo, final_g)


def _rope_tables():
    pos = np.arange(DEC_SEQ)
    lane = np.arange(LANES)
    l64 = lane % (2 * (DH_A // 2))
    quarter = DH_A // 4
    p = np.where((l64 < DH_A // 2)[None, :], (pos // GRID_W)[:, None], (pos % GRID_W)[:, None])
    inv = ROPE_BASE ** (-np.arange(quarter, dtype=np.float64) / quarter)
    ang = p.astype(np.float64) * inv[l64 % quarter][None, :]
    sign = np.where((lane % (2 * quarter)) < quarter, -1.0, 1.0)
    return jnp.asarray(np.cos(ang), F32), jnp.asarray(np.sin(ang) * sign[None, :], F32)


def _pad_lanes(v, width=LANES):
    v = v.reshape(1, -1).astype(F32)
    return jnp.pad(v, ((0, 0), (0, width - v.shape[1])))


def kernel(x_prompt, x_sample, cache_diff_k, cache_diff_v, cache_na_k, cache_na_v, state_ssd, c, c_ctx,
           norm_g, w_ada, b_ada, w_in, lam_q1, lam_k1, lam_q2, lam_k2, diff_subln_g, conv_w, conv_b,
           dt_bias, a_log, d_skip, ssd_norm_g, na_rpb, w_br_a, w_br_b, w_br_c, w_out, final_g):
    assert x_prompt.shape == (BATCH, SEQ, D_MODEL) and x_sample.shape == (DEC_BATCH, DEC_SEQ, D_MODEL)
    assert w_in.shape == (DEPTH, D_MODEL, _SRC["merge"] + MERGE_COLS)
    w16 = jnp.swapaxes(w_in, 1, 2).astype(BF16)
    w_dt =jnp.pad(w16[:, _SRC["dt"]:_SRC["qc"], :], ((0, 0), (0, LANES - 2 * H_B), (0, 0)))
    wa16, wb16, wc16, wo16 = (w.astype(BF16) for w in (w_br_a, w_br_b, w_br_c, w_out))

    cvecs = jnp.concatenate([c_ctx[None, :], c, jnp.zeros((8 - 1 - DEC_BATCH, D_MODEL), F32)], axis=0)
    ada = _ada_call(cvecs.T, w_ada, b_ada)
    cos_t, sin_t = _rope_tables()

    ck_a = cache_diff_k.reshape(DEC_BATCH, DEPTH, PAST_LEN, W_A)
    cv_a = cache_diff_v.reshape(DEC_BATCH, DEPTH, PAST_LEN, W_A)
    ck_c = cache_na_k.transpose(0, 1, 3, 4, 2)
    cv_c = cache_na_v.transpose(0, 1, 3, 4, 2)
    na_tiles = _rpb_call(na_rpb)
    h0t = state_ssd.transpose(0, 1, 2, 5, 3, 4).reshape(DEC_BATCH, DEPTH, 2, N_B, DI_B)
    h0t = h0t.reshape(DEC_BATCH, DEPTH, 2, N_B, H_B // 2, LANES).transpose(0, 1, 2, 4, 3, 5)
    h0t = h0t.reshape(DEC_BATCH, DEPTH, 2, (H_B // 2) * N_B, LANES)

    xp = x_prompt.reshape(BATCH * SEQ, D_MODEL)
    xs = x_sample.reshape(DEC_BATCH * DEC_SEQ, D_MODEL)
    fg = final_g.reshape(1, D_MODEL)
    norm_g3 = norm_g.reshape(DEPTH, 1, D_MODEL)
    carry = None
    for li in range(DEPTH):
        lam_init = 0.8 - 0.6 * math.exp(-0.3 * li)
        final = li == DEPTH - 1
        lamvec = jnp.concatenate([_pad_lanes(v[li]) for v in (lam_q1, lam_k1, lam_q2, lam_k2)], axis=0)
        subln = diff_subln_g[li].reshape(1, LANES)
        dtb = _pad_lanes(dt_bias[li])
        alog = _pad_lanes(a_log[li])
        dskx = jnp.repeat(d_skip[li], P_B, axis=-1)
        ssd_w = (conv_w[li], conv_b[li].reshape(1, CONV_DIM), dtb, alog, dskx, ssd_norm_g[li].reshape(1, DI_B))
        post_w = (wa16, wb16, wc16, wo16, fg)

        proj, ka, va, kc_t, vc_t, yb, ssd_state = _proj_ssd_call(xp, ada, norm_g3, w16, w_dt, ssd_w, li,
                                                                 ctx=True, carry=carry)
        carry = (ka, va, kc_t, vc_t, ssd_state)
        ya = _attn_a_ctx_call(proj, lamvec, subln, lam_init)
        xp = _post_ctx_call(xp, proj, kc_t, vc_t, ya, yb, ada, norm_g3, w16, li, *post_w, final=final)

        proj, kv, yb = _proj_ssd_call(xs, ada, norm_g3, w16, w_dt, ssd_w, li, ctx=False, h0t=h0t)
        ya = _attn_a_lat_call(proj, ck_a, cv_a, li, cos_t, sin_t, lamvec, subln, lam_init)
        yc = _attn_c_lat_call(proj, kv, ck_c, cv_c, li, na_tiles)
        xs = _post_call(xs, ya, yb, yc, ada, norm_g3, w16, li, *post_w, tm=512, row_base=1,
                        tokens_per_row=DEC_SEQ, final=final)

    new_k_a, new_v_a, new_k_c_t, new_v_c_t, new_ssd = carry
    to_token_major = lambda a: a.transpose(0, 1, 4, 2, 3)
    return (xp.reshape(BATCH, SEQ, D_MODEL), xs.reshape(DEC_BATCH, DEC_SEQ, D_MODEL),
            new_k_a, new_v_a, to_token_major(new_k_c_t), to_token_major(new_v_c_t), new_ssd)
```
